```python
import math
import jax, jax.numpy as jnp
from jax import lax
import numpy as np

D_MODEL = 2048
BATCH = 8
SEQ = 8192
DEPTH = 4

N_MEM = 256
MIX_WIDTH = D_MODEL
ATTN_WIDTH = MIX_WIDTH // 2
HG_WIDTH = MIX_WIDTH - ATTN_WIDTH
ATTN_HEAD_DIM = 128
ATTN_HEADS = ATTN_WIDTH // ATTN_HEAD_DIM
HG_EXPAND = 128
HG_HEADS = HG_WIDTH // HG_EXPAND
HG_VDIM = HG_WIDTH // HG_HEADS
HG_CHUNK = 64
DILATED_BRANCHES = ((128, 1), (512, 4), (2048, 16))
Q_BLOCK = 128
REL_BUCKETS = 32
REL_MAX_DIST = 2048
CROSS_HEADS = 4
CROSS_HEAD_DIM = D_MODEL // CROSS_HEADS
D_FF = ((8 * D_MODEL + 3 * 256 - 1) // (3 * 256)) * 256
IN_SIZES = (ATTN_WIDTH, ATTN_WIDTH, ATTN_WIDTH, HG_WIDTH, HG_WIDTH, HG_WIDTH, HG_WIDTH)
IN_COLS = sum(IN_SIZES)
N_NORMS = 7
RMS_EPS = 1e-6
NEG_INF = -1e30

kernel_name = "hybrid_dilated_attn_hgrn2_trunk"


def rmsnorm(x, g):
    xf = x.astype(jnp.float32)
    y = xf * lax.rsqrt(jnp.mean(xf * xf, axis=-1, keepdims=True) + RMS_EPS)
    return (y * g.astype(jnp.float32)).astype(x.dtype)


def rel_bucket(dist):
    max_exact = REL_BUCKETS // 2
    d_f = jnp.maximum(dist, 1).astype(jnp.float32)
    large = max_exact + (jnp.log(d_f / max_exact) / math.log(REL_MAX_DIST / max_exact)
                         * (REL_BUCKETS - max_exact)).astype(jnp.int32)
    large = jnp.minimum(large, REL_BUCKETS - 1)
    return jnp.where(dist < max_exact, dist, large)


def dilated_branch(q, k, v, rel_bias, window, dilation):
    B, S, H, Dh = q.shape
    band = window // dilation
    assert band <= Q_BLOCK
    M = S // dilation
    Mp = -(-M // Q_BLOCK) * Q_BLOCK
    nb = Mp // Q_BLOCK

    def split(t):
        t = t.reshape(B, M, dilation, H, Dh)
        return jnp.pad(t, ((0, 0), (0, Mp - M), (0, 0), (0, 0), (0, 0)))

    def kwin(t):
        tp = jnp.pad(split(t), ((0, 0), (Q_BLOCK, 0), (0, 0), (0, 0), (0, 0)))
        tp = tp.reshape(B, nb + 1, Q_BLOCK, dilation, H, Dh)
        return jnp.concatenate([tp[:, :-1], tp[:, 1:]], axis=2)

    qs = split(q).reshape(B, nb, Q_BLOCK, dilation, H, Dh)
    ks, vs = kwin(k), kwin(v)
    scores = jnp.einsum('bnqrhd,bnkrhd->bnrhqk', qs, ks).astype(jnp.float32) * (Dh ** -0.5)

    qi = jnp.arange(Q_BLOCK)[:, None]
    kj = jnp.arange(2 * Q_BLOCK)[None, :]
    m = qi - kj + Q_BLOCK
    bucket = rel_bucket(jnp.maximum(m, 0) * dilation)
    bias = jnp.transpose(rel_bias[bucket].astype(jnp.float32), (2, 0, 1))
    band_ok = (m >= 0) & (m <= band)
    blk = jnp.arange(nb)[:, None, None]
    start_ok = (blk * Q_BLOCK + kj[None] - Q_BLOCK) >= 0
    valid = band_ok[None] & start_ok
    logits = jnp.where(valid[None, :, None, None], scores + bias, NEG_INF)
    lse = jax.nn.logsumexp(logits, axis=-1)
    p = jnp.exp(logits - lse[..., None])
    o = jnp.einsum('bnrhqk,bnkrhd->bnqrhd', p.astype(vs.dtype), vs)
    o = o.reshape(B, Mp, dilation, H, Dh)[:, :M].reshape(B, S, H, Dh)
    lse = jnp.transpose(lse, (0, 1, 4, 2, 3)).reshape(B, Mp, dilation, H)[:, :M].reshape(B, S, H)
    return o, lse


def dilated_attention(q, k, v, rel_bias):
    outs, lses = [], []
    for window, dilation in DILATED_BRANCHES:
        o, l = dilated_branch(q, k, v, rel_bias, window, dilation)
        outs.append(o)
        lses.append(l)
    w = jax.nn.softmax(jnp.stack(lses), axis=0)
    o = jnp.einsum('gbsh,gbshd->bshd', w.astype(q.dtype), jnp.stack(outs))
    return o


def hgrn2(fz, iv, qz, gz, lb, hg_gain):
    B, S, _ = fz.shape
    nc = S // HG_CHUNK
    f = lb + (1.0 - lb) * jax.nn.sigmoid(fz.astype(jnp.float32))
    log_f = jnp.log(f)
    kk = 1.0 - f
    qq = jax.nn.silu(qz.astype(jnp.float32))
    vv = iv.astype(jnp.float32)

    def chunks(t, dim):
        return jnp.transpose(t.reshape(B, nc, HG_CHUNK, HG_HEADS, dim), (1, 0, 3, 2, 4))

    xs = (chunks(qq, HG_EXPAND), chunks(kk, HG_EXPAND), chunks(vv, HG_VDIM), chunks(log_f, HG_EXPAND))
    tri = jnp.tril(jnp.ones((HG_CHUNK, HG_CHUNK), dtype=bool))

    def step(state, inp):
        qc, kc, vc, lfc = inp
        b = jnp.cumsum(lfc, axis=-2)
        o_inter = jnp.einsum('bhck,bhkv->bhcv', qc * jnp.exp(b), state)
        diff = b[:, :, :, None, :] - b[:, :, None, :, :]
        decay = jnp.exp(jnp.where(tri[:, :, None], diff, -jnp.inf))
        A = jnp.einsum('bhtk,bhtsk,bhsk->bhts', qc, decay, kc)
        o = o_inter + jnp.einsum('bhts,bhsv->bhtv', A, vc)
        b_last = b[:, :, -1:, :]
        new_state = jnp.exp(b_last[:, :, 0, :])[..., None] * state + \
            jnp.einsum('bhsk,bhsv->bhkv', kc * jnp.exp(b_last - b), vc)
        return new_state, o

    s0 = jnp.zeros((B, HG_HEADS, HG_EXPAND, HG_VDIM), jnp.float32)
    _, o = lax.scan(step, s0, xs)
    o = jnp.transpose(o, (1, 0, 3, 2, 4)).reshape(B, S, HG_HEADS, HG_VDIM)
    o = o * lax.rsqrt(jnp.mean(o * o, axis=-1, keepdims=True) + RMS_EPS)
    o = o * hg_gain.astype(jnp.float32).reshape(HG_HEADS, HG_VDIM)
    o = o.reshape(B, S, HG_WIDTH) * jax.nn.silu(gz.astype(jnp.float32))
    return o.astype(fz.dtype)


def _fwd_setup_inputs(seed: int = 0) -> dict:
    key = jax.random.key(seed)
    ks = jax.random.split(key, 14)
    f32 = jnp.float32
    nrm = lambda k, shape, scale: jax.random.normal(k, shape, f32) * scale
    return {
        "x": nrm(ks[0], (BATCH, SEQ, D_MODEL), 1.0),
        "mem": nrm(ks[1], (BATCH, N_MEM, D_MODEL), 1.0),
        "rel_bias": nrm(ks[2], (REL_BUCKETS, ATTN_HEADS), 0.5),
        "lb_logits": nrm(ks[3], (DEPTH, HG_WIDTH), 0.5),
        "norm_gains": 1.0 + nrm(ks[4], (DEPTH, N_NORMS, D_MODEL), 0.1),
        "w_in": nrm(ks[5], (DEPTH, D_MODEL, IN_COLS), D_MODEL ** -0.5),
        "hg_norm": 1.0 + nrm(ks[6], (DEPTH, HG_WIDTH), 0.1),
        "w_out": nrm(ks[7], (DEPTH, MIX_WIDTH, D_MODEL), MIX_WIDTH ** -0.5),
        "w_cq": nrm(ks[8], (DEPTH, D_MODEL, D_MODEL), D_MODEL ** -0.5),
        "w_ckv": nrm(ks[9], (DEPTH, D_MODEL, 2 * D_MODEL), D_MODEL ** -0.5),
        "w_co": nrm(ks[10], (DEPTH, D_MODEL, D_MODEL), D_MODEL ** -0.5),
        "w_gate_up": nrm(ks[11], (DEPTH, D_MODEL, 2 * D_FF), D_MODEL ** -0.5),
        "w_down": nrm(ks[12], (DEPTH, D_FF, D_MODEL), D_FF ** -0.5),
    }


def _fwd_reference(x, mem, rel_bias, lb_logits, norm_gains, w_in, hg_norm, w_out,
              w_cq, w_ckv, w_co, w_gate_up, w_down):
    B, S, D = x.shape
    n_mem = mem.shape[1]
    P = jax.nn.softmax(lb_logits.astype(jnp.float32), axis=0)
    lb_all = jnp.cumsum(P, axis=0) - P
    split_idx = [int(i) for i in np.cumsum(IN_SIZES)[:-1]]

    for l in range(DEPTH):
        g = norm_gains[l]
        h = rmsnorm(x, g[0])
        proj = h @ w_in[l]
        aq, ak, av, fz, iv, qz, gz = jnp.split(proj, split_idx, axis=-1)
        shp = (B, S, ATTN_HEADS, ATTN_HEAD_DIM)
        attn = dilated_attention(aq.reshape(shp), ak.reshape(shp), av.reshape(shp), rel_bias)
        attn = attn.reshape(B, S, ATTN_WIDTH)
        hg = hgrn2(fz, iv, qz, gz, lb_all[l], hg_norm[l])
        mix = jnp.concatenate([attn, hg], axis=-1) @ w_out[l]
        x = x + rmsnorm(mix, g[1])
        hc = rmsnorm(x, g[2])
        mn = rmsnorm(mem, g[3])
        cq = (hc @ w_cq[l]).reshape(B, S, CROSS_HEADS, CROSS_HEAD_DIM)
        ck, cv = jnp.split(mn @ w_ckv[l], 2, axis=-1)
        ck = ck.reshape(B, n_mem, CROSS_HEADS, CROSS_HEAD_DIM)
        cv = cv.reshape(B, n_mem, CROSS_HEADS, CROSS_HEAD_DIM)
        s = jnp.einsum('bshd,bmhd->bhsm', cq, ck).astype(jnp.float32) * (CROSS_HEAD_DIM ** -0.5)
        p = jax.nn.softmax(s, axis=-1).astype(cv.dtype)
        co = jnp.einsum('bhsm,bmhd->bshd', p, cv).reshape(B, S, D) @ w_co[l]
        x = x + rmsnorm(co, g[4])
        hf = rmsnorm(x, g[5])
        gate, up = jnp.split(hf @ w_gate_up[l], 2, axis=-1)
        y = (jax.nn.silu(gate) * up) @ w_down[l]
        x = x + rmsnorm(y, g[6])
    return x


import jax as _jax
import jax.numpy as _jnp

TWIN_FORMAT = 'train_step'
FWD_PARAMS = ['x', 'mem', 'rel_bias', 'lb_logits', 'norm_gains', 'w_in', 'hg_norm', 'w_out', 'w_cq', 'w_ckv', 'w_co', 'w_gate_up', 'w_down']
TWIN_WEIGHTS = ['rel_bias', 'lb_logits', 'norm_gains', 'w_in', 'hg_norm', 'w_out', 'w_cq', 'w_ckv', 'w_co', 'w_gate_up', 'w_down']
TWIN_DIFF_INPUT = 'x'
TWIN_INPUTS = ['x', 'mem', 'rel_bias', 'lb_logits', 'norm_gains', 'w_in', 'hg_norm', 'w_out', 'w_cq', 'w_ckv', 'w_co', 'w_gate_up', 'w_down', 'loss_target', 'm_rel_bias', 'm_lb_logits', 'm_norm_gains', 'm_w_in', 'm_hg_norm', 'm_w_out', 'm_w_cq', 'm_w_ckv', 'm_w_co', 'm_w_gate_up', 'm_w_down', 'v_rel_bias', 'v_lb_logits', 'v_norm_gains', 'v_w_in', 'v_hg_norm', 'v_w_out', 'v_w_cq', 'v_w_ckv', 'v_w_co', 'v_w_gate_up', 'v_w_down']
TWIN_OUTPUTS = ['loss', 'grad_x', 'grad_rel_bias', 'grad_lb_logits', 'grad_norm_gains', 'grad_w_in', 'grad_hg_norm', 'grad_w_out', 'grad_w_cq', 'grad_w_ckv', 'grad_w_co', 'grad_w_gate_up', 'grad_w_down', 'delta_rel_bias', 'delta_lb_logits', 'delta_norm_gains', 'delta_w_in', 'delta_hg_norm', 'delta_w_out', 'delta_w_cq', 'delta_w_ckv', 'delta_w_co', 'delta_w_gate_up', 'delta_w_down', 'new_m_rel_bias', 'new_m_lb_logits', 'new_m_norm_gains', 'new_m_w_in', 'new_m_hg_norm', 'new_m_w_out', 'new_m_w_cq', 'new_m_w_ckv', 'new_m_w_co', 'new_m_w_gate_up', 'new_m_w_down', 'new_v_rel_bias', 'new_v_lb_logits', 'new_v_norm_gains', 'new_v_w_in', 'new_v_hg_norm', 'new_v_w_out', 'new_v_w_cq', 'new_v_w_ckv', 'new_v_w_co', 'new_v_w_gate_up', 'new_v_w_down']
TWIN_LEAF_KINDS = {'loss': 'loss', 'grad_x': 'grad_x', 'grad_rel_bias': 'grad_w', 'grad_lb_logits': 'grad_w', 'grad_norm_gains': 'grad_w', 'grad_w_in': 'grad_w', 'grad_hg_norm': 'grad_w', 'grad_w_out': 'grad_w', 'grad_w_cq': 'grad_w', 'grad_w_ckv': 'grad_w', 'grad_w_co': 'grad_w', 'grad_w_gate_up': 'grad_w', 'grad_w_down': 'grad_w', 'delta_rel_bias': 'delta_w', 'delta_lb_logits': 'delta_w', 'delta_norm_gains': 'delta_w', 'delta_w_in': 'delta_w', 'delta_hg_norm': 'delta_w', 'delta_w_out': 'delta_w', 'delta_w_cq': 'delta_w', 'delta_w_ckv': 'delta_w', 'delta_w_co': 'delta_w', 'delta_w_gate_up': 'delta_w', 'delta_w_down': 'delta_w', 'new_m_rel_bias': 'new_m', 'new_m_lb_logits': 'new_m', 'new_m_norm_gains': 'new_m', 'new_m_w_in': 'new_m', 'new_m_hg_norm': 'new_m', 'new_m_w_out': 'new_m', 'new_m_w_cq': 'new_m', 'new_m_w_ckv': 'new_m', 'new_m_w_co': 'new_m', 'new_m_w_gate_up': 'new_m', 'new_m_w_down': 'new_m', 'new_v_rel_bias': 'new_v', 'new_v_lb_logits': 'new_v', 'new_v_norm_gains': 'new_v', 'new_v_w_in': 'new_v', 'new_v_hg_norm': 'new_v', 'new_v_w_out': 'new_v', 'new_v_w_cq': 'new_v', 'new_v_w_ckv': 'new_v', 'new_v_w_co': 'new_v', 'new_v_w_gate_up': 'new_v', 'new_v_w_down': 'new_v'}


def _forward(args):
    return _fwd_reference(*[args[k] for k in FWD_PARAMS])


def _output_shape():
    def fwd():
        inp = _fwd_setup_inputs(0)
        return _fwd_reference(*[inp[k] for k in FWD_PARAMS])
    out = _jax.eval_shape(fwd)
    return out.shape, out.dtype

N_MICROBATCH = 1
ADAM_LR = 0.001
ADAM_B1 = 0.9
ADAM_B2 = 0.999
ADAM_EPS = 1e-08
ADAM_WD = 0.01
ADAM_STEP = 10
PER_EXAMPLE_BATCH_AXIS = {'x': 0, 'mem': 0, 'loss_target': 0}
SHARED_INPUTS = []
_WEIGHT_DTYPES = {'rel_bias': _jnp.float32, 'lb_logits': _jnp.float32, 'norm_gains': _jnp.float32, 'w_in': _jnp.float32, 'hg_norm': _jnp.float32, 'w_out': _jnp.float32, 'w_cq': _jnp.float32, 'w_ckv': _jnp.float32, 'w_co': _jnp.float32, 'w_gate_up': _jnp.float32, 'w_down': _jnp.float32}
MOMENT_SCALE = {'rel_bias': 1.855263e+00, 'lb_logits': 8.735656e-02, 'norm_gains': 2.387070e+01, 'w_in': 5.614190e+00, 'hg_norm': 6.357556e+00, 'w_out': 1.054400e+01, 'w_cq': 4.353194e+00, 'w_ckv': 1.220018e+01, 'w_co': 1.709858e+01, 'w_gate_up': 2.412640e+00, 'w_down': 4.718071e+00}


def _to_microbatches(a, axis):
    t = _jnp.moveaxis(a, axis, 0)
    t = t.reshape((N_MICROBATCH, t.shape[0] // N_MICROBATCH) + t.shape[1:])
    return _jnp.moveaxis(t, 1, axis + 1)


def setup_inputs(seed: int = 0) -> dict:
    inp = _fwd_setup_inputs(seed)
    key = _jax.random.fold_in(_jax.random.key(seed), 7919)
    shape, _ = _output_shape()
    out = dict(inp)
    out["loss_target"] = _jax.random.normal(_jax.random.fold_in(key, 0), shape, _jnp.float32)
    for i, name in enumerate(TWIN_WEIGHTS):
        w = inp[name].astype(_jnp.float32)
        if MOMENT_SCALE is None:
            s = _jnp.sqrt(_jnp.mean(_jnp.square(w)) + 1e-30)
        else:
            s = MOMENT_SCALE[name]
        km, kv = _jax.random.split(_jax.random.fold_in(key, i + 1))
        out[name] = w
        out["m_" + name] = s * _jax.random.normal(km, w.shape, _jnp.float32)
        out["v_" + name] = (s * s) * _jax.random.uniform(kv, w.shape, _jnp.float32, 0.5, 1.5)
    if N_MICROBATCH > 1:
        for name, axis in PER_EXAMPLE_BATCH_AXIS.items():
            out[name] = _to_microbatches(out[name], axis)
    return {'x': out['x'], 'mem': out['mem'], 'rel_bias': out['rel_bias'], 'lb_logits': out['lb_logits'], 'norm_gains': out['norm_gains'], 'w_in': out['w_in'], 'hg_norm': out['hg_norm'], 'w_out': out['w_out'], 'w_cq': out['w_cq'], 'w_ckv': out['w_ckv'], 'w_co': out['w_co'], 'w_gate_up': out['w_gate_up'], 'w_down': out['w_down'], 'loss_target': out['loss_target'], 'm_rel_bias': out['m_rel_bias'], 'm_lb_logits': out['m_lb_logits'], 'm_norm_gains': out['m_norm_gains'], 'm_w_in': out['m_w_in'], 'm_hg_norm': out['m_hg_norm'], 'm_w_out': out['m_w_out'], 'm_w_cq': out['m_w_cq'], 'm_w_ckv': out['m_w_ckv'], 'm_w_co': out['m_w_co'], 'm_w_gate_up': out['m_w_gate_up'], 'm_w_down': out['m_w_down'], 'v_rel_bias': out['v_rel_bias'], 'v_lb_logits': out['v_lb_logits'], 'v_norm_gains': out['v_norm_gains'], 'v_w_in': out['v_w_in'], 'v_hg_norm': out['v_hg_norm'], 'v_w_out': out['v_w_out'], 'v_w_cq': out['v_w_cq'], 'v_w_ckv': out['v_w_ckv'], 'v_w_co': out['v_w_co'], 'v_w_gate_up': out['v_w_gate_up'], 'v_w_down': out['v_w_down']}


def _loss(weights, diff, rest, loss_target):
    with _jax.named_scope("forward"):
        args = {**rest, TWIN_DIFF_INPUT: diff, **{k: w.astype(_WEIGHT_DTYPES[k]) for k, w in weights.items()}}
        y = _forward(args)
    with _jax.named_scope("loss_head"):
        err = _jnp.square(y.astype(_jnp.float32) - loss_target)
        return 0.5 * _jnp.sum(_jnp.mean(err, axis=-1)) if err.ndim else 0.5 * err


def _adamw(w, g, m, v):
    m = ADAM_B1 * m + (1.0 - ADAM_B1) * g
    v = ADAM_B2 * v + (1.0 - ADAM_B2) * _jnp.square(g)
    m_hat = m / (1.0 - ADAM_B1 ** ADAM_STEP)
    v_hat = v / (1.0 - ADAM_B2 ** ADAM_STEP)
    delta = -ADAM_LR * (m_hat / (_jnp.sqrt(v_hat) + ADAM_EPS) + ADAM_WD * w)
    return delta, m, v


def reference(x, mem, rel_bias, lb_logits, norm_gains, w_in, hg_norm, w_out, w_cq, w_ckv, w_co, w_gate_up, w_down, loss_target, m_rel_bias, m_lb_logits, m_norm_gains, m_w_in, m_hg_norm, m_w_out, m_w_cq, m_w_ckv, m_w_co, m_w_gate_up, m_w_down, v_rel_bias, v_lb_logits, v_norm_gains, v_w_in, v_hg_norm, v_w_out, v_w_cq, v_w_ckv, v_w_co, v_w_gate_up, v_w_down):
    given = dict(x=x, mem=mem, rel_bias=rel_bias, lb_logits=lb_logits, norm_gains=norm_gains, w_in=w_in, hg_norm=hg_norm, w_out=w_out, w_cq=w_cq, w_ckv=w_ckv, w_co=w_co, w_gate_up=w_gate_up, w_down=w_down, loss_target=loss_target, m_rel_bias=m_rel_bias, m_lb_logits=m_lb_logits, m_norm_gains=m_norm_gains, m_w_in=m_w_in, m_hg_norm=m_hg_norm, m_w_out=m_w_out, m_w_cq=m_w_cq, m_w_ckv=m_w_ckv, m_w_co=m_w_co, m_w_gate_up=m_w_gate_up, m_w_down=m_w_down, v_rel_bias=v_rel_bias, v_lb_logits=v_lb_logits, v_norm_gains=v_norm_gains, v_w_in=v_w_in, v_hg_norm=v_hg_norm, v_w_out=v_w_out, v_w_cq=v_w_cq, v_w_ckv=v_w_ckv, v_w_co=v_w_co, v_w_gate_up=v_w_gate_up, v_w_down=v_w_down)
    weights = {n: given[n] for n in TWIN_WEIGHTS}
    shared = {n: given[n] for n in SHARED_INPUTS}
    per_example = {n: given[n] for n in ['x', 'mem']}
    grad_fn = _jax.value_and_grad(_loss, argnums=(0, 1))

    def one_microbatch(ex, loss_target):
        ex = dict(ex)
        diff = ex.pop(TWIN_DIFF_INPUT)
        return grad_fn(weights, diff, {**shared, **ex}, loss_target)

    if N_MICROBATCH == 1:
        loss, (grad_w, grad_x) = one_microbatch(per_example, given["loss_target"])
    else:
        def body(carry, xs):
            loss_sum, grad_sum = carry
            l_k, (gw_k, gx_k) = one_microbatch(xs[0], xs[1])
            with _jax.named_scope("update"):
                return (loss_sum + l_k, _jax.tree.map(_jnp.add, grad_sum, gw_k)), gx_k

        init = (_jnp.zeros((), _jnp.float32), _jax.tree.map(_jnp.zeros_like, weights))
        (loss, grad_w), grad_x = _jax.lax.scan(body, init, (per_example, given["loss_target"]))
    with _jax.named_scope("update"):
        delta_w, new_m, new_v = {}, {}, {}
        for n in TWIN_WEIGHTS:
            delta_w[n], new_m[n], new_v[n] = _adamw(weights[n], grad_w[n], given["m_" + n], given["v_" + n])
    return (loss, grad_x, *[grad_w[n] for n in TWIN_WEIGHTS], *[delta_w[n] for n in TWIN_WEIGHTS],
            *[new_m[n] for n in TWIN_WEIGHTS], *[new_v[n] for n in TWIN_WEIGHTS])
```

```python
import functools
import math

import jax
import jax.numpy as jnp
from jax import lax
from jax.experimental import pallas as pl
from jax.experimental.pallas import tpu as pltpu

F32 = jnp.float32
BF16 = jnp.bfloat16
MESH = pl.DeviceIdType.MESH

N_DEV = 8
HEAD = 128
CROSS_HEADS = 4
Q_BLOCK = 128
BRANCHES = ((128, 1), (512, 4), (2048, 16))
REL_BUCKETS = 32
REL_MAX_DIST = 2048
HG_CHUNK = 16
RMS_EPS = 1e-6
NEG_INF = -1e30
ADAM_LR, ADAM_B1, ADAM_B2, ADAM_EPS, ADAM_WD, ADAM_STEP = 0.001, 0.9, 0.999, 1e-08, 0.01, 10
VMEM_LIMIT = 48 * 1024 * 1024


def _tile(n, target, mult):
    best = None
    t = mult
    while t <= min(n, target):
        if n % t == 0:
            best = t
        t += mult
    return best if best is not None else n


def _cp(sem, vmem=VMEM_LIMIT):
    return pltpu.CompilerParams(dimension_semantics=sem, vmem_limit_bytes=vmem)


def _mm_nn(a, b3, out_dtype, name):
    M, K = a.shape
    J, K2, Nb = b3.shape
    assert K == K2
    tm, tn, tk = _tile(M, 1024, 8), _tile(Nb, 1408, 128), _tile(K, 2048, 128)
    nn, nk = Nb // tn, K // tk

    def body(a_ref, b_ref, o_ref, *acc):
        prod = jnp.dot(a_ref[...].astype(BF16), b_ref[...].astype(BF16), preferred_element_type=F32)
        if nk == 1:
            o_ref[...] = prod.astype(o_ref.dtype)
        else:
            k = pl.program_id(2)

            @pl.when(k == 0)
            def _():
                acc[0][...] = prod

            @pl.when(k > 0)
            def _():
                acc[0][...] += prod

            @pl.when(k == nk - 1)
            def _():
                o_ref[...] = acc[0][...].astype(o_ref.dtype)

    return pl.pallas_call(
        body, name=name, grid=(M // tm, J * nn, nk),
        in_specs=[pl.BlockSpec((tm, tk), lambda i, n, k: (i, k)),
                  pl.BlockSpec((None, tk, tn), lambda i, n, k: (n // nn, k, n % nn))],
        out_specs=pl.BlockSpec((tm, tn), lambda i, n, k: (i, n)),
        out_shape=jax.ShapeDtypeStruct((M, J * Nb), out_dtype),
        scratch_shapes=[] if nk == 1 else [pltpu.VMEM((tm, tn), F32)],
        compiler_params=_cp(("parallel", "parallel", "arbitrary")),
    )(a, b3)


def _mm_nt(a, b3, out_dtype, name):
    M, Kt = a.shape
    J, N, Kb = b3.shape
    assert Kt == J * Kb
    tm, tn, tk = _tile(M, 1024, 8), _tile(N, 1408, 128), _tile(Kb, 2048, 128)
    nkb = Kb // tk
    nk = J * nkb

    def body(a_ref, b_ref, o_ref, *acc):
        prod = lax.dot_general(a_ref[...].astype(BF16), b_ref[...].astype(BF16), (((1,), (1,)), ((), ())),
                               preferred_element_type=F32)
        if nk == 1:
            o_ref[...] = prod.astype(o_ref.dtype)
        else:
            k = pl.program_id(2)

            @pl.when(k == 0)
            def _():
                acc[0][...] = prod

            @pl.when(k > 0)
            def _():
                acc[0][...] += prod

            @pl.when(k == nk - 1)
            def _():
                o_ref[...] = acc[0][...].astype(o_ref.dtype)

    return pl.pallas_call(
        body, name=name, grid=(M // tm, N // tn, nk),
        in_specs=[pl.BlockSpec((tm, tk), lambda i, n, k: (i, k)),
                  pl.BlockSpec((None, tn, tk), lambda i, n, k: (k // nkb, n, k % nkb))],
        out_specs=pl.BlockSpec((tm, tn), lambda i, n, k: (i, n)),
        out_shape=jax.ShapeDtypeStruct((M, N), out_dtype),
        scratch_shapes=[] if nk == 1 else [pltpu.VMEM((tm, tn), F32)],
        compiler_params=_cp(("parallel", "parallel", "arbitrary")),
    )(a, b3)


def _mm_tn(a, b, J, out_dtype, name):
    S, M = a.shape
    S2, Nt = b.shape
    assert S == S2 and Nt % J == 0
    Nb = Nt // J
    tm, tn, tk = _tile(M, 1408, 128), _tile(Nb, 1408, 128), _tile(S, 512, 8)
    nn, nk = Nb // tn, S // tk

    def body(a_ref, b_ref, o_ref, *acc):
        prod = lax.dot_general(a_ref[...].astype(BF16), b_ref[...].astype(BF16), (((0,), (0,)), ((), ())),
                               preferred_element_type=F32)
        if nk == 1:
            o_ref[...] = prod.astype(o_ref.dtype)
        else:
            k = pl.program_id(2)

            @pl.when(k == 0)
            def _():
                acc[0][...] = prod

            @pl.when(k > 0)
            def _():
                acc[0][...] += prod

            @pl.when(k == nk - 1)
            def _():
                o_ref[...] = acc[0][...].astype(o_ref.dtype)

    return pl.pallas_call(
        body, name=name, grid=(M // tm, J * nn, nk),
        in_specs=[pl.BlockSpec((tk, tm), lambda i, n, k: (k, i)),
                  pl.BlockSpec((tk, tn), lambda i, n, k: (k, n))],
        out_specs=pl.BlockSpec((None, tm, tn), lambda i, n, k: (n // nn, i, n % nn)),
        out_shape=jax.ShapeDtypeStruct((J, M, Nb), out_dtype),
        scratch_shapes=[] if nk == 1 else [pltpu.VMEM((tm, tn), F32)],
        compiler_params=_cp(("parallel", "parallel", "arbitrary")),
    )(a, b)


def _rms_scale(v):
    return lax.rsqrt(jnp.mean(v * v, axis=-1, keepdims=True) + RMS_EPS)


def _rms_bwd(dy, v, g):
    r = _rms_scale(v)
    vh = v * r
    u = dy * g
    dv = r * (u - vh * jnp.mean(u * vh, axis=-1, keepdims=True))
    return dv, dy * vh


def _fold8(t):
    R, D = t.shape
    return jnp.sum(t.reshape(R // 8, 8, D), axis=0)


def _norm_fwd(x, z, g_post, g_pre, name):
    S, D = x.shape
    tr = _tile(S, 256, 8)
    has_z, has_pre = z is not None, g_pre is not None

    def body(*refs):
        it = iter(refs)
        x_ref = next(it)
        z_ref = next(it) if has_z else None
        gpost_ref = next(it) if has_z else None
        gpre_ref = next(it) if has_pre else None
        xn_ref = next(it) if has_z else None
        h_ref = next(it) if has_pre else None
        xn = x_ref[...]
        if has_z:
            zz = z_ref[...]
            xn = xn + zz * _rms_scale(zz) * gpost_ref[...]
            xn_ref[...] = xn
        if has_pre:
            h_ref[...] = (xn * _rms_scale(xn) * gpre_ref[...]).astype(BF16)

    row = pl.BlockSpec((tr, D), lambda i: (i, 0))
    gain = pl.BlockSpec((1, D), lambda i: (0, 0))
    ins, specs, outs, ospecs = [x], [row], [], []
    if has_z:
        ins += [z, g_post.reshape(1, D)]
        specs += [row, gain]
        outs.append(jax.ShapeDtypeStruct((S, D), F32))
        ospecs.append(row)
    if has_pre:
        ins.append(g_pre.reshape(1, D))
        specs.append(gain)
        outs.append(jax.ShapeDtypeStruct((S, D), BF16))
        ospecs.append(row)
    res = pl.pallas_call(body, name=name, grid=(S // tr,), in_specs=specs, out_specs=ospecs, out_shape=outs,
                         compiler_params=_cp(("parallel",)))(*ins)
    res = list(res)
    xn = res.pop(0) if has_z else x
    h = res.pop(0) if has_pre else None
    return xn, h


def _norm_bwd(dres, dh, xn, z, g_pre, g_post, name):
    S, D = xn.shape
    tr = _tile(S, 256, 8)
    has_res, has_pre, has_z = dres is not None, dh is not None, z is not None

    def body(*refs):
        it = iter(refs)
        dres_ref = next(it) if has_res else None
        dh_ref = next(it) if has_pre else None
        xn_ref = next(it) if has_pre else None
        gpre_ref = next(it) if has_pre else None
        z_ref = next(it) if has_z else None
        gpost_ref = next(it) if has_z else None
        t_ref = next(it)
        dz_ref = next(it) if has_z else None
        dgpre_ref = next(it) if has_pre else None
        dgpost_ref = next(it) if has_z else None
        first = pl.program_id(0) == 0
        t = dres_ref[...] if has_res else None
        if has_pre:
            dv, dg = _rms_bwd(dh_ref[...].astype(F32), xn_ref[...], gpre_ref[...])
            t = dv if t is None else t + dv
            part = _fold8(dg)

            @pl.when(first)
            def _():
                dgpre_ref[...] = part

            @pl.when(jnp.logical_not(first))
            def _():
                dgpre_ref[...] += part
        t_ref[...] = t
        if has_z:
            dv, dg = _rms_bwd(t, z_ref[...], gpost_ref[...])
            dz_ref[...] = dv.astype(BF16)
            part2 = _fold8(dg)

            @pl.when(first)
            def _():
                dgpost_ref[...] = part2

            @pl.when(jnp.logical_not(first))
            def _():
                dgpost_ref[...] += part2

    row = pl.BlockSpec((tr, D), lambda i: (i, 0))
    gain = pl.BlockSpec((1, D), lambda i: (0, 0))
    accs = pl.BlockSpec((8, D), lambda i: (0, 0))
    ins, specs = [], []
    if has_res:
        ins.append(dres)
        specs.append(row)
    if has_pre:
        ins += [dh, xn, g_pre.reshape(1, D)]
        specs += [row, row, gain]
    if has_z:
        ins += [z, g_post.reshape(1, D)]
        specs += [row, gain]
    outs, ospecs = [jax.ShapeDtypeStruct((S, D), F32)], [row]
    if has_z:
        outs.append(jax.ShapeDtypeStruct((S, D), BF16))
        ospecs.append(row)
    if has_pre:
        outs.append(jax.ShapeDtypeStruct((8, D), F32))
        ospecs.append(accs)
    if has_z:
        outs.append(jax.ShapeDtypeStruct((8, D), F32))
        ospecs.append(accs)
    res = list(pl.pallas_call(body, name=name, grid=(S // tr,), in_specs=specs, out_specs=ospecs, out_shape=outs,
                              compiler_params=_cp(("arbitrary",)))(*ins))
    t = res.pop(0)
    dz = res.pop(0) if has_z else None
    dgpre = jnp.sum(res.pop(0), axis=0) if has_pre else None
    dgpost = jnp.sum(res.pop(0), axis=0) if has_z else None
    return t, dz, dgpre, dgpost


def _loss_and_grad(y, target):
    S, D = y.shape
    tr = _tile(S, 256, 8)

    def body(y_ref, t_ref, dy_ref, l_ref):
        err = y_ref[...] - t_ref[...]
        dy_ref[...] = err * (1.0 / D)
        part = 0.5 * jnp.sum(jnp.mean(err * err, axis=-1, keepdims=True), axis=0, keepdims=True)
        first = pl.program_id(0) == 0

        @pl.when(first)
        def _():
            l_ref[...] = jnp.broadcast_to(part, l_ref.shape)

        @pl.when(jnp.logical_not(first))
        def _():
            l_ref[...] += jnp.broadcast_to(part, l_ref.shape)

    row = pl.BlockSpec((tr, D), lambda i: (i, 0))
    dy, l = pl.pallas_call(
        body, name="loss_grad", grid=(S // tr,), in_specs=[row, row],
        out_specs=[row, pl.BlockSpec((8, 128), lambda i: (0, 0))],
        out_shape=[jax.ShapeDtypeStruct((S, D), F32), jax.ShapeDtypeStruct((8, 128), F32)],
        compiler_params=_cp(("arbitrary",)))(y, target)
    return l[0, 0], dy


def _swiglu_fwd(gu, name):
    S, F2 = gu.shape
    F = F2 // 2
    tr, tc = _tile(S, 512, 8), _tile(F, 1536, 128)
    nf = F // tc

    def body(g_ref, u_ref, o_ref):
        g = g_ref[...]
        o_ref[...] = (g * jax.nn.sigmoid(g) * u_ref[...]).astype(BF16)

    return pl.pallas_call(
        body, name=name, grid=(S // tr, nf),
        in_specs=[pl.BlockSpec((tr, tc), lambda i, j: (i, j)), pl.BlockSpec((tr, tc), lambda i, j: (i, j + nf))],
        out_specs=pl.BlockSpec((tr, tc), lambda i, j: (i, j)),
        out_shape=jax.ShapeDtypeStruct((S, F), BF16), compiler_params=_cp(("parallel", "parallel")))(gu, gu)


def _swiglu_bwd(gu, dact, name):
    S, F2 = gu.shape
    F = F2 // 2
    tr, tc = _tile(S, 512, 8), _tile(F, 1536, 128)
    nf = F // tc

    def body(g_ref, u_ref, d_ref, o_ref):
        g = g_ref[...]
        d = d_ref[...].astype(F32)
        sg = jax.nn.sigmoid(g)

        @pl.when(pl.program_id(1) < nf)
        def _():
            o_ref[...] = (d * u_ref[...] * (sg * (1.0 + g * (1.0 - sg)))).astype(BF16)

        @pl.when(pl.program_id(1) >= nf)
        def _():
            o_ref[...] = (d * g * sg).astype(BF16)

    return pl.pallas_call(
        body, name=name, grid=(S // tr, 2 * nf),
        in_specs=[pl.BlockSpec((tr, tc), lambda i, j: (i, j % nf)), pl.BlockSpec((tr, tc), lambda i, j: (i, j % nf + nf)),
                  pl.BlockSpec((tr, tc), lambda i, j: (i, j % nf))],
        out_specs=pl.BlockSpec((tr, tc), lambda i, j: (i, j)),
        out_shape=jax.ShapeDtypeStruct((S, F2), BF16), compiler_params=_cp(("parallel", "parallel")))(gu, gu, dact)


def _cross_scores(q, k, scale):
    s = lax.dot_general(q, k, (((1,), (1,)), ((), ())), preferred_element_type=F32) * scale
    m = jnp.max(s, axis=-1, keepdims=True)
    e = jnp.exp(s - m)
    return e / jnp.sum(e, axis=-1, keepdims=True)


def _cross_fwd(cq, ckv, name):
    S, D = cq.shape
    Nm = ckv.shape[0]
    dh = D // CROSS_HEADS
    tq = _tile(S, 512, 8)
    scale = dh ** -0.5

    def body(q_ref, kv_ref, o_ref):
        for h in range(CROSS_HEADS):
            cols = slice(h * dh, (h + 1) * dh)
            p = _cross_scores(q_ref[:, cols], kv_ref[:, cols], scale)
            o_ref[:, cols] = jnp.dot(p.astype(BF16), kv_ref[:, D + h * dh:D + (h + 1) * dh],
                                     preferred_element_type=F32).astype(BF16)

    return pl.pallas_call(
        body, name=name, grid=(S // tq,),
        in_specs=[pl.BlockSpec((tq, D), lambda i: (i, 0)), pl.BlockSpec((Nm, 2 * D), lambda i: (0, 0))],
        out_specs=pl.BlockSpec((tq, D), lambda i: (i, 0)),
        out_shape=jax.ShapeDtypeStruct((S, D), BF16), compiler_params=_cp(("parallel",)))(cq, ckv)


def _cross_bwd(cq, ckv, do, name):
    S, D = cq.shape
    Nm = ckv.shape[0]
    dh = D // CROSS_HEADS
    tq = _tile(S, 512, 8)
    scale = dh ** -0.5

    def body(q_ref, kv_ref, do_ref, dq_ref, dkv_ref):
        first = pl.program_id(0) == 0

        @pl.when(first)
        def _():
            dkv_ref[...] = jnp.zeros_like(dkv_ref)

        for h in range(CROSS_HEADS):
            cols = slice(h * dh, (h + 1) * dh)
            vcols = slice(D + h * dh, D + (h + 1) * dh)
            q, k, v = q_ref[:, cols], kv_ref[:, cols], kv_ref[:, vcols]
            doh = do_ref[:, cols].astype(BF16)
            p = _cross_scores(q, k, scale)
            dp = lax.dot_general(doh, v, (((1,), (1,)), ((), ())), preferred_element_type=F32)
            ds = (p * (dp - jnp.sum(p * dp, axis=-1, keepdims=True)) * scale).astype(BF16)
            dq_ref[:, cols] = jnp.dot(ds, k, preferred_element_type=F32).astype(BF16)
            dkv_ref[:, cols] += lax.dot_general(ds, q, (((0,), (0,)), ((), ())), preferred_element_type=F32)
            dkv_ref[:, vcols] += lax.dot_general(p.astype(BF16), doh, (((0,), (0,)), ((), ())),
                                                 preferred_element_type=F32)

    return pl.pallas_call(
        body, name=name, grid=(S // tq,),
        in_specs=[pl.BlockSpec((tq, D), lambda i: (i, 0)), pl.BlockSpec((Nm, 2 * D), lambda i: (0, 0)),
                  pl.BlockSpec((tq, D), lambda i: (i, 0))],
        out_specs=[pl.BlockSpec((tq, D), lambda i: (i, 0)), pl.BlockSpec((Nm, 2 * D), lambda i: (0, 0))],
        out_shape=[jax.ShapeDtypeStruct((S, D), BF16), jax.ShapeDtypeStruct((Nm, 2 * D), F32)],
        compiler_params=_cp(("arbitrary",)))(cq, ckv, do)


def _rel_bucket(dist):
    max_exact = REL_BUCKETS // 2
    d_f = jnp.maximum(dist, 1).astype(F32)
    large = max_exact + (jnp.log(d_f / max_exact) / math.log(REL_MAX_DIST / max_exact)
                         * (REL_BUCKETS - max_exact)).astype(jnp.int32)
    large = jnp.minimum(large, REL_BUCKETS - 1)
    return jnp.where(dist < max_exact, dist, large)


def _branch_tables(dilation):
    qi = jnp.arange(Q_BLOCK)[:, None]
    kj = jnp.arange(2 * Q_BLOCK)[None, :]
    m = qi - kj + Q_BLOCK
    return _rel_bucket(jnp.maximum(m, 0) * dilation), (m >= 0) & (m <= Q_BLOCK)


def _attn_logits(q, kcat, bias, first_block, scale):
    s = lax.dot_general(q, kcat, (((1,), (1,)), ((), ())), preferred_element_type=F32) * scale + bias
    col = lax.broadcasted_iota(jnp.int32, s.shape, 1)
    return jnp.where(jnp.logical_and(first_block, col < Q_BLOCK), NEG_INF, s)


def _attn_branch_fwd(proj, bias, dilation, H, name):
    S, NC = proj.shape
    AW = H * HEAD
    r = dilation
    M = S // r
    nb = M // Q_BLOCK
    ncb = NC // AW
    view = proj.reshape(M, r * NC)
    scale = HEAD ** -0.5

    def body(q_ref, kp_ref, kc_ref, vp_ref, vc_ref, b_ref, o_ref, l_ref):
        first_block = pl.program_id(1) == 0
        for h in range(H):
            cols = slice(h * HEAD, (h + 1) * HEAD)
            q = q_ref[:, cols].astype(BF16)
            kcat = jnp.concatenate([kp_ref[:, cols], kc_ref[:, cols]], axis=0).astype(BF16)
            vcat = jnp.concatenate([vp_ref[:, cols], vc_ref[:, cols]], axis=0).astype(BF16)
            s = _attn_logits(q, kcat, b_ref[h], first_block, scale)
            m = jnp.max(s, axis=-1, keepdims=True)
            e = jnp.exp(s - m)
            den = jnp.sum(e, axis=-1, keepdims=True)
            o_ref[:, cols] = jnp.dot(e.astype(BF16), vcat, preferred_element_type=F32) / den
            l_ref[:, cols] = jnp.broadcast_to(m + jnp.log(den), (Q_BLOCK, HEAD))

    def blk(col, prev):
        if prev:
            return pl.BlockSpec((Q_BLOCK, AW), lambda p, n: (jnp.maximum(n - 1, 0), p * ncb + col))
        return pl.BlockSpec((Q_BLOCK, AW), lambda p, n: (n, p * ncb + col))

    out = pl.BlockSpec((Q_BLOCK, AW), lambda p, n: (n, p))
    o, l = pl.pallas_call(
        body, name=name, grid=(r, nb),
        in_specs=[blk(0, False), blk(1, True), blk(1, False), blk(2, True), blk(2, False),
                  pl.BlockSpec((H, Q_BLOCK, 2 * Q_BLOCK), lambda p, n: (0, 0, 0))],
        out_specs=[out, out],
        out_shape=[jax.ShapeDtypeStruct((M, r * AW), F32)] * 2,
        compiler_params=_cp(("parallel", "parallel")))(view, view, view, view, view, bias)
    return o.reshape(S, AW), l.reshape(S, AW)


def _attn_merge(outs, lses, width_out, name):
    S, AW = outs[0].shape
    tr = _tile(S, 256, 8)

    def body(o0, o1, o2, l0, l1, l2, cat_ref, om_ref, lt_ref):
        a, b, c = l0[...], l1[...], l2[...]
        m = jnp.maximum(jnp.maximum(a, b), c)
        ea, eb, ec = jnp.exp(a - m), jnp.exp(b - m), jnp.exp(c - m)
        tot = ea + eb + ec
        o = (ea * o0[...] + eb * o1[...] + ec * o2[...]) / tot
        om_ref[...] = o
        cat_ref[...] = o.astype(BF16)
        lt_ref[...] = m + jnp.log(tot)

    row = pl.BlockSpec((tr, AW), lambda i: (i, 0))
    return pl.pallas_call(
        body, name=name, grid=(S // tr,), in_specs=[row] * 6, out_specs=[row, row, row],
        out_shape=[jax.ShapeDtypeStruct((S, width_out), BF16), jax.ShapeDtypeStruct((S, AW), F32),
                   jax.ShapeDtypeStruct((S, AW), F32)],
        compiler_params=_cp(("parallel",)))(*outs, *lses)


def _attn_delta(dcat, o_attn, H, name):
    S, AW = o_attn.shape
    tr = _tile(S, 256, 8)

    def body(d_ref, o_ref, out_ref):
        for h in range(H):
            cols = slice(h * HEAD, (h + 1) * HEAD)
            out_ref[:, cols] = jnp.broadcast_to(
                jnp.sum(d_ref[:, cols] * o_ref[:, cols], axis=-1, keepdims=True), (tr, HEAD))

    row = pl.BlockSpec((tr, AW), lambda i: (i, 0))
    return pl.pallas_call(body, name=name, grid=(S // tr,), in_specs=[row, row], out_specs=row,
                          out_shape=jax.ShapeDtypeStruct((S, AW), F32),
                          compiler_params=_cp(("parallel",)))(dcat, o_attn)


def _attn_branch_bwd(proj, dcat, lse_tot, delta, bias, dilation, H, name):
    S, NC = proj.shape
    AW = H * HEAD
    r = dilation
    M = S // r
    nb = M // Q_BLOCK
    ncb = NC // AW
    dcb = dcat.shape[1] // AW
    pv = proj.reshape(M, r * NC)
    dv_ = dcat.reshape(M, r * dcat.shape[1])
    lv = lse_tot.reshape(M, r * AW)
    tv = delta.reshape(M, r * AW)
    scale = HEAD ** -0.5

    def body(q_ref, kp_ref, kc_ref, vp_ref, vc_ref, do_ref, l_ref, t_ref, b_ref,
             dq_ref, dk_ref, dv_ref, db_ref, ck_ref, cv_ref):
        p_id, n = pl.program_id(0), pl.program_id(1)

        @pl.when(jnp.logical_and(p_id == 0, n == 0))
        def _():
            db_ref[...] = jnp.zeros_like(db_ref)

        @pl.when(n == 0)
        def _():
            ck_ref[...] = jnp.zeros_like(ck_ref)
            cv_ref[...] = jnp.zeros_like(cv_ref)

        @pl.when(n < nb)
        def _():
            for h in range(H):
                cols = slice(h * HEAD, (h + 1) * HEAD)
                q = q_ref[:, cols].astype(BF16)
                kcat = jnp.concatenate([kp_ref[:, cols], kc_ref[:, cols]], axis=0).astype(BF16)
                vcat = jnp.concatenate([vp_ref[:, cols], vc_ref[:, cols]], axis=0).astype(BF16)
                doh = do_ref[:, cols].astype(BF16)
                s = _attn_logits(q, kcat, b_ref[h], n == 0, scale)
                lt = l_ref[:, cols]
                dl = t_ref[:, cols]
                p = jnp.exp(s - jnp.concatenate([lt, lt], axis=1))
                dp = lax.dot_general(doh, vcat, (((1,), (1,)), ((), ())), preferred_element_type=F32)
                ds = p * (dp - jnp.concatenate([dl, dl], axis=1))
                db_ref[h] += ds
                dsb = (ds * scale).astype(BF16)
                dq_ref[:, cols] = jnp.dot(dsb, kcat, preferred_element_type=F32)
                dkc = lax.dot_general(dsb, q, (((0,), (0,)), ((), ())), preferred_element_type=F32)
                dvc = lax.dot_general(p.astype(BF16), doh, (((0,), (0,)), ((), ())), preferred_element_type=F32)
                dk_ref[:, cols] = ck_ref[:, cols] + dkc[:Q_BLOCK]
                dv_ref[:, cols] = cv_ref[:, cols] + dvc[:Q_BLOCK]
                ck_ref[:, cols] = dkc[Q_BLOCK:]
                cv_ref[:, cols] = dvc[Q_BLOCK:]

        @pl.when(n == nb)
        def _():
            dk_ref[...] = ck_ref[...]
            dv_ref[...] = cv_ref[...]

    last = nb - 1

    def cur(colfn):
        return pl.BlockSpec((Q_BLOCK, AW), lambda p, n: (jnp.minimum(n, last), colfn(p)))

    def prev(colfn):
        return pl.BlockSpec((Q_BLOCK, AW), lambda p, n: (jnp.clip(n - 1, 0, last), colfn(p)))

    dq, dk, dv, db = pl.pallas_call(
        body, name=name, grid=(r, nb + 1),
        in_specs=[cur(lambda p: p * ncb), prev(lambda p: p * ncb + 1), cur(lambda p: p * ncb + 1),
                  prev(lambda p: p * ncb + 2), cur(lambda p: p * ncb + 2), cur(lambda p: p * dcb),
                  cur(lambda p: p), cur(lambda p: p),
                  pl.BlockSpec((H, Q_BLOCK, 2 * Q_BLOCK), lambda p, n: (0, 0, 0))],
        out_specs=[cur(lambda p: p), prev(lambda p: p), prev(lambda p: p),
                   pl.BlockSpec((H, Q_BLOCK, 2 * Q_BLOCK), lambda p, n: (0, 0, 0))],
        out_shape=[jax.ShapeDtypeStruct((M, r * AW), F32)] * 3 + [jax.ShapeDtypeStruct((H, Q_BLOCK, 2 * Q_BLOCK), F32)],
        scratch_shapes=[pltpu.VMEM((Q_BLOCK, AW), F32), pltpu.VMEM((Q_BLOCK, AW), F32)],
        compiler_params=_cp(("arbitrary", "arbitrary")))(pv, pv, pv, pv, pv, dv_, lv, tv, bias)
    return dq.reshape(S, AW), dk.reshape(S, AW), dv.reshape(S, AW), db


def _bias_grad(dbs, onehots, H):
    def body(d0, d1, d2, e0, e1, e2, o_ref):
        acc = jnp.zeros((H, REL_BUCKETS), F32)
        for d, e in ((d0, e0), (d1, e1), (d2, e2)):
            acc = acc + lax.dot_general(d[...], e[...], (((1,), (1,)), ((), ())), preferred_element_type=F32,
                                        precision=lax.Precision.HIGHEST)
        o_ref[...] = acc

    flat = [d.reshape(H, 2 * Q_BLOCK * Q_BLOCK) for d in dbs]
    return pl.pallas_call(body, name="bias_grad", out_shape=jax.ShapeDtypeStruct((H, REL_BUCKETS), F32),
                          compiler_params=pltpu.CompilerParams(vmem_limit_bytes=VMEM_LIMIT))(*flat, *onehots)


def _assemble_dproj(dqkv, dhg, name):
    S, AW = dqkv[0][0].shape
    tr = _tile(S, 256, 8)
    ins = [dqkv[g][c] for g in range(3) for c in range(3)] + list(dhg)

    def body(*refs):
        o_ref = refs[-1]
        j = pl.program_id(1)
        for c in range(3):
            @pl.when(j == c)
            def _(c=c):
                o_ref[...] = (refs[c][...] + refs[3 + c][...] + refs[6 + c][...]).astype(BF16)

        for k in range(4):
            @pl.when(j == 3 + k)
            def _(k=k):
                o_ref[...] = refs[9 + k][...]

    row = pl.BlockSpec((tr, AW), lambda i, j: (i, 0))
    return pl.pallas_call(
        body, name=name, grid=(S // tr, 7), in_specs=[row] * 13,
        out_specs=pl.BlockSpec((tr, AW), lambda i, j: (i, j)),
        out_shape=jax.ShapeDtypeStruct((S, 7 * AW), BF16),
        compiler_params=_cp(("parallel", "arbitrary")))(*ins)


def _tri(n, upper):
    r = lax.broadcasted_iota(jnp.int32, (n, n), 0)
    c = lax.broadcasted_iota(jnp.int32, (n, n), 1)
    return jnp.where(c >= r if upper else c <= r, 1.0, 0.0).astype(F32)


def _hg_gates(fz, qz, lb):
    sig = jax.nn.sigmoid(fz)
    f = lb + (1.0 - lb) * sig
    sq = jax.nn.sigmoid(qz)
    return sig, f, jnp.log(f), 1.0 - f, qz * sq, sq


def _hg_fwd(proj, cat, lb, gain, HH, name):
    S, NC = proj.shape
    W = HH * HEAD
    C = HG_CHUNK
    tb = _tile(S, 256, C)
    ncb = tb // C
    base = 3 * W // HEAD
    cat_cols = cat.shape[1] // HEAD

    def body(fz_ref, iv_ref, qz_ref, gz_ref, lb_ref, g_ref, cat_in, cat_ref, o_ref, st_ref, state):
        del cat_in

        @pl.when(pl.program_id(1) == 0)
        def _():
            state[...] = jnp.zeros_like(state)

        lbv, gv = lb_ref[...], g_ref[...]
        tri = _tri(C, False)
        row = lax.broadcasted_iota(jnp.int32, (C, 1), 0)

        def chunk(c, carry):
            rows = pl.ds(pl.multiple_of(c * C, C), C)
            fz, vv, qz, gz = fz_ref[rows, :], iv_ref[rows, :], qz_ref[rows, :], gz_ref[rows, :]
            _, f, lf, kk, qq, _ = _hg_gates(fz, qz, lbv)
            b = jnp.dot(tri, lf, preferred_element_type=F32, precision=lax.Precision.HIGHEST)
            bl = b[C - 1:C, :]
            st = state[...]
            stb = st.astype(BF16)
            st_ref[c] = stb
            o = lax.dot_general((qq * jnp.exp(b)).astype(BF16), stb, (((1,), (1,)), ((), ())),
                                preferred_element_type=F32)
            for s in range(C):
                dec = jnp.exp(jnp.minimum(b - b[s:s + 1, :], 0.0))
                a = jnp.sum(qq * dec * kk[s:s + 1, :], axis=-1, keepdims=True)
                o = o + jnp.where(row >= s, a, 0.0) * vv[s:s + 1, :]
            kh = (kk * jnp.exp(bl - b)).astype(BF16)
            upd = lax.dot_general(vv.astype(BF16), kh, (((0,), (0,)), ((), ())), preferred_element_type=F32)
            state[...] = st * jnp.exp(bl) + upd
            o_ref[rows, :] = o
            n = o * lax.rsqrt(jnp.mean(o * o, axis=-1, keepdims=True) + RMS_EPS)
            cat_ref[rows, :] = (n * gv * (gz * jax.nn.sigmoid(gz))).astype(BF16)
            return carry

        lax.fori_loop(0, ncb, chunk, 0)

    def col(k):
        return pl.BlockSpec((tb, HEAD), lambda h, i: (i, base + k * HH + h))

    vec = pl.BlockSpec((1, HEAD), lambda h, i: (0, h))
    cat_out, o_raw, states = pl.pallas_call(
        body, name=name, grid=(HH, S // tb),
        in_specs=[col(0), col(1), col(2), col(3), vec, vec, pl.BlockSpec(memory_space=pl.ANY)],
        out_specs=[pl.BlockSpec((tb, HEAD), lambda h, i: (i, cat_cols - HH + h)),
                   pl.BlockSpec((tb, HEAD), lambda h, i: (i, h)),
                   pl.BlockSpec((None, ncb, HEAD, HEAD), lambda h, i: (h, i, 0, 0))],
        out_shape=[jax.ShapeDtypeStruct(cat.shape, BF16), jax.ShapeDtypeStruct((S, W), F32),
                   jax.ShapeDtypeStruct((HH, S // C, HEAD, HEAD), BF16)],
        scratch_shapes=[pltpu.VMEM((HEAD, HEAD), F32)],
        input_output_aliases={6: 0},
        compiler_params=_cp(("parallel", "arbitrary")))(proj, proj, proj, proj, lb.reshape(1, W), gain.reshape(1, W), cat)
    return cat_out, o_raw, states


def _hg_bwd(proj, dcat, o_raw, states, lb, gain, HH, name):
    S, NC = proj.shape
    W = HH * HEAD
    C = HG_CHUNK
    tb = _tile(S, 256, C)
    ncb = tb // C
    nt = S // tb
    base = 3 * W // HEAD
    dcat_cols = dcat.shape[1] // HEAD

    def body(fz_ref, iv_ref, qz_ref, gz_ref, dc_ref, o_ref, st_ref, lb_ref, g_ref,
             dfz_ref, div_ref, dqz_ref, dgz_ref, dlb_ref, dg_ref, dstate):
        @pl.when(pl.program_id(1) == 0)
        def _():
            dstate[...] = jnp.zeros_like(dstate)
            dlb_ref[...] = jnp.zeros_like(dlb_ref)
            dg_ref[...] = jnp.zeros_like(dg_ref)

        lbv, gv = lb_ref[...], g_ref[...]
        tri_lo, tri_up = _tri(C, False), _tri(C, True)
        row = lax.broadcasted_iota(jnp.int32, (C, 1), 0)

        def chunk(cc, carry):
            c = ncb - 1 - cc
            rows = pl.ds(pl.multiple_of(c * C, C), C)
            fz, vv, qz, gz = fz_ref[rows, :], iv_ref[rows, :], qz_ref[rows, :], gz_ref[rows, :]
            sig, f, lf, kk, qq, sq = _hg_gates(fz, qz, lbv)
            b = jnp.dot(tri_lo, lf, preferred_element_type=F32, precision=lax.Precision.HIGHEST)
            bl = b[C - 1:C, :]
            eb, ebl = jnp.exp(b), jnp.exp(bl)
            ekb = jnp.exp(bl - b)
            qh, kh = qq * eb, kk * ekb
            o = o_ref[rows, :]
            dhg = dc_ref[rows, :]
            sgz = jax.nn.sigmoid(gz)
            silu_g = gz * sgz
            rr = lax.rsqrt(jnp.mean(o * o, axis=-1, keepdims=True) + RMS_EPS)
            nrm = o * rr
            dpre = dhg * silu_g
            dgz_ref[rows, :] = (dhg * nrm * gv * (sgz * (1.0 + gz * (1.0 - sgz)))).astype(BF16)
            dg_ref[...] += jnp.sum(dpre * nrm, axis=0, keepdims=True)
            dn = dpre * gv
            do = rr * (dn - nrm * jnp.mean(dn * nrm, axis=-1, keepdims=True))
            dob = do.astype(BF16)
            st0 = st_ref[c]
            dst1 = dstate[...]
            dst1b = dst1.astype(BF16)
            dqh = jnp.dot(dob, st0, preferred_element_type=F32)
            dv = lax.dot_general(kh.astype(BF16), dst1b, (((1,), (1,)), ((), ())), preferred_element_type=F32)
            dkh = jnp.dot(vv.astype(BF16), dst1b, preferred_element_type=F32)
            dstate[...] = dst1 * ebl + lax.dot_general(dob, qh.astype(BF16), (((0,), (0,)), ((), ())),
                                                       preferred_element_type=F32)
            extra = jnp.sum(dkh * kh, axis=0, keepdims=True) + ebl * jnp.sum(st0.astype(F32) * dst1, axis=0, keepdims=True)
            dq = dqh * eb
            dk = dkh * ekb
            dk_rows, dv_rows = [], []
            for s in range(C):
                keep = row >= s
                dec = jnp.exp(jnp.minimum(b - b[s:s + 1, :], 0.0))
                ks, vs = kk[s:s + 1, :], vv[s:s + 1, :]
                da = jnp.where(keep, jnp.sum(do * vs, axis=-1, keepdims=True), 0.0)
                a = jnp.where(keep, jnp.sum(qq * dec * ks, axis=-1, keepdims=True), 0.0)
                gd = da * dec
                dq = dq + gd * ks
                dk_rows.append(jnp.sum(gd * qq, axis=0, keepdims=True))
                dv_rows.append(jnp.sum(a * do, axis=0, keepdims=True))
            dk = dk + jnp.concatenate(dk_rows, axis=0)
            dv = dv + jnp.concatenate(dv_rows, axis=0)
            db = qq * dq - kk * dk + jnp.where(row == C - 1, extra, 0.0)
            dlf = jnp.dot(tri_up, db, preferred_element_type=F32, precision=lax.Precision.HIGHEST)
            df = dlf / f - dk
            dfz_ref[rows, :] = (df * (1.0 - lbv) * sig * (1.0 - sig)).astype(BF16)
            dlb_ref[...] += jnp.sum(df * (1.0 - sig), axis=0, keepdims=True)
            dqz_ref[rows, :] = (dq * (sq * (1.0 + qz * (1.0 - sq)))).astype(BF16)
            div_ref[rows, :] = dv.astype(BF16)
            return carry

        lax.fori_loop(0, ncb, chunk, 0)

    def col(k):
        return pl.BlockSpec((tb, HEAD), lambda h, i: (nt - 1 - i, base + k * HH + h))

    ocol = pl.BlockSpec((tb, HEAD), lambda h, i: (nt - 1 - i, h))
    vec = pl.BlockSpec((1, HEAD), lambda h, i: (0, h))
    outs = pl.pallas_call(
        body, name=name, grid=(HH, nt),
        in_specs=[col(0), col(1), col(2), col(3),
                  pl.BlockSpec((tb, HEAD), lambda h, i: (nt - 1 - i, dcat_cols - HH + h)),
                  pl.BlockSpec((tb, HEAD), lambda h, i: (nt - 1 - i, h)),
                  pl.BlockSpec((None, ncb, HEAD, HEAD), lambda h, i: (h, nt - 1 - i, 0, 0)), vec, vec],
        out_specs=[ocol, ocol, ocol, ocol, vec, vec],
        out_shape=[jax.ShapeDtypeStruct((S, W), BF16)] * 4 + [jax.ShapeDtypeStruct((1, W), F32)] * 2,
        scratch_shapes=[pltpu.VMEM((HEAD, HEAD), F32)],
        compiler_params=_cp(("parallel", "arbitrary")))(proj, proj, proj, proj, dcat, o_raw, states,
                                                        lb.reshape(1, W), gain.reshape(1, W))
    return outs


def _lb_all(lb_logits):
    p = jax.nn.softmax(lb_logits.astype(F32), axis=0)
    return jnp.cumsum(p, axis=0) - p


def _local_step(x, mem, target, rel_bias, lb_logits, gains, hg_norm, weights):
    S, D = x.shape
    L = len(weights)
    H = (D // 2) // HEAD
    lb = _lb_all(lb_logits)
    tables = [_branch_tables(r) for _, r in BRANCHES]
    bias = [jnp.where(ok[None], jnp.transpose(rel_bias[bk].astype(F32), (2, 0, 1)), NEG_INF) for bk, ok in tables]

    saved = []
    _, h = _norm_fwd(x, None, None, gains[0, 0], "norm_first")
    xl = x
    for l in range(L):
        g, w = gains[l], weights[l]
        proj = _mm_nn(h, w["in"], F32, "mm_in")
        outs, lses = [], []
        for gi, (_, r) in enumerate(BRANCHES):
            o, ls = _attn_branch_fwd(proj, bias[gi], r, H, f"attn_fwd_d{r}")
            outs.append(o)
            lses.append(ls)
        cat, o_attn, lse_tot = _attn_merge(outs, lses, D, "attn_merge")
        cat, o_raw, states = _hg_fwd(proj, cat, lb[l], hg_norm[l], H, "hgrn_fwd")
        mix = _mm_nn(cat, w["out"], F32, "mm_out")
        x1, hc = _norm_fwd(xl, mix, g[1], g[2], "norm_post_pre")
        _, mn = _norm_fwd(mem, None, None, g[3], "norm_mem")
        cq = _mm_nn(hc, w["cq"], BF16, "mm_cq")
        ckv = _mm_nn(mn, w["ckv"], BF16, "mm_ckv")
        co = _cross_fwd(cq, ckv, "cross_fwd")
        cop = _mm_nn(co, w["co"], F32, "mm_co")
        x2, hf = _norm_fwd(x1, cop, g[4], g[5], "norm_post_pre")
        gu = _mm_nn(hf, w["gu"], F32, "mm_gu")
        act = _swiglu_fwd(gu, "swiglu_fwd")
        y = _mm_nn(act, w["down"], F32, "mm_down")
        g_next = gains[l + 1, 0] if l + 1 < L else None
        x3, h_next = _norm_fwd(x2, y, g[6], g_next, "norm_post_pre" if l + 1 < L else "norm_post")
        saved.append(dict(x0=xl, h0=h, proj=proj, cat=cat, o_attn=o_attn, lse_tot=lse_tot, o_raw=o_raw, states=states,
                          mix=mix, x1=x1, hc=hc, mn=mn, cq=cq, ckv=ckv, co=co, cop=cop, x2=x2, hf=hf, gu=gu, act=act,
                          y=y, x3=x3))
        xl, h = x3, h_next

    loss, dres = _loss_and_grad(xl, target)

    dh_next = None
    wgrads = [None] * L
    d_gains = [[None] * 7 for _ in range(L)]
    d_lb = [None] * L
    d_hg = [None] * L
    d_bias = [jnp.zeros((H, Q_BLOCK, 2 * Q_BLOCK), F32) for _ in BRANCHES]
    for l in reversed(range(L)):
        g, w, sv = gains[l], weights[l], saved[l]
        g_next = gains[l + 1, 0] if l + 1 < L else None
        t3, dy, dg_next, dg6 = _norm_bwd(dres, dh_next, sv["x3"], sv["y"], g_next, g[6], "norm_bwd")
        if l + 1 < L:
            d_gains[l + 1][0] = dg_next
        d_gains[l][6] = dg6
        gw = {}
        gw["down"] = _mm_tn(sv["act"], dy, 1, BF16, "mm_dw_down")
        dact = _mm_nt(dy, w["down"], F32, "mm_dx_down")
        dgu = _swiglu_bwd(sv["gu"], dact, "swiglu_bwd")
        gw["gu"] = _mm_tn(sv["hf"], dgu, w["gu"].shape[0], BF16, "mm_dw_gu")
        dhf = _mm_nt(dgu, w["gu"], F32, "mm_dx_gu")
        t2, dcop, d_gains[l][5], d_gains[l][4] = _norm_bwd(t3, dhf, sv["x2"], sv["cop"], g[5], g[4], "norm_bwd")
        gw["co"] = _mm_tn(sv["co"], dcop, 1, BF16, "mm_dw_sq")
        dco = _mm_nt(dcop, w["co"], F32, "mm_dx_sq")
        dcq, dckv = _cross_bwd(sv["cq"], sv["ckv"], dco, "cross_bwd")
        gw["cq"] = _mm_tn(sv["hc"], dcq, 1, BF16, "mm_dw_sq")
        dhc = _mm_nt(dcq, w["cq"], F32, "mm_dx_sq")
        gw["ckv"] = _mm_tn(sv["mn"], dckv, w["ckv"].shape[0], BF16, "mm_dw_ckv")
        dmn = _mm_nt(dckv, w["ckv"], F32, "mm_dx_ckv")
        _, _, d_gains[l][3], _ = _norm_bwd(None, dmn, mem, None, g[3], None, "norm_bwd_mem")
        t1, dmix, d_gains[l][2], d_gains[l][1] = _norm_bwd(t2, dhc, sv["x1"], sv["mix"], g[2], g[1], "norm_bwd")
        gw["out"] = _mm_tn(sv["cat"], dmix, 1, BF16, "mm_dw_sq")
        dcat = _mm_nt(dmix, w["out"], F32, "mm_dx_sq")
        delta = _attn_delta(dcat, sv["o_attn"], H, "attn_delta")
        dqkv = []
        for gi, (_, r) in enumerate(BRANCHES):
            dq, dk, dv, db = _attn_branch_bwd(sv["proj"], dcat, sv["lse_tot"], delta, bias[gi], r, H, f"attn_bwd_d{r}")
            dqkv.append((dq, dk, dv))
            d_bias[gi] = d_bias[gi] + db
        dfz, div, dqz, dgz, dlb_l, dhg_l = _hg_bwd(sv["proj"], dcat, sv["o_raw"], sv["states"], lb[l], hg_norm[l], H,
                                                   "hgrn_bwd")
        d_lb[l], d_hg[l] = dlb_l[0], dhg_l[0]
        dproj = _assemble_dproj(dqkv, [dfz, div, dqz, dgz], "assemble_dproj")
        gw["in"] = _mm_tn(sv["h0"], dproj, w["in"].shape[0], BF16, "mm_dw_in")
        dh_next = _mm_nt(dproj, w["in"], F32, "mm_dx_in")
        dres = t1
        wgrads[l] = gw
    grad_x, _, d_gains[0][0], _ = _norm_bwd(dres, dh_next, x, None, gains[0, 0], None, "norm_bwd_first")

    onehots = [jnp.where(ok[None], jax.nn.one_hot(bk, REL_BUCKETS, dtype=F32, axis=0), 0.0)
               .reshape(REL_BUCKETS, 2 * Q_BLOCK * Q_BLOCK) for bk, ok in tables]
    d_rel_bias = _bias_grad(d_bias, onehots, H).T
    d_gains = jnp.stack([jnp.stack(row) for row in d_gains])
    return loss, grad_x, wgrads, d_rel_bias, jnp.stack(d_lb), jnp.stack(d_hg), d_gains


def _place():
    return lax.axis_index("x"), lax.axis_index("y"), lax.axis_index("c")


def _all_gather(shards, name):
    n = len(shards)
    HBM = pl.BlockSpec(memory_space=pltpu.HBM)

    def body(*refs):
        ins, outs = refs[:n], refs[n:2 * n]
        send_sems, recv_sems, local_sems = refs[2 * n:]
        x, y, c = _place()
        me, sibling = (x, y, c), (x, y, 1 - c)
        chips = [(1 - x, y), (x, 1 - y), (1 - x, 1 - y)]

        def copy(a, k, block, to, own=False):
            dst = outs[a].at[4 * block[0] + 2 * block[1] + block[2]]
            return pltpu.make_async_remote_copy(
                src_ref=ins[a] if own else dst, dst_ref=dst, send_sem=send_sems.at[7 * a + k],
                recv_sem=recv_sems.at[7 * a + k], device_id=to, device_id_type=MESH)

        mine = [pltpu.make_async_copy(ins[a], outs[a].at[4 * x + 2 * y + c], local_sems.at[a]) for a in range(n)]
        for cp in mine:
            cp.start()
        first = []
        for a in range(n):
            first.append(copy(a, 0, me, sibling, own=True))
            first += [copy(a, 1 + j, me, (*chip, c), own=True) for j, chip in enumerate(chips)]
        for cp in first:
            cp.start()
        passed = []
        for j, chip in enumerate(chips):
            for a in range(n):
                copy(a, 1 + j, (*chip, c), me).wait_recv()
                fwd = copy(a, 4 + j, (*chip, c), sibling)
                fwd.start()
                passed.append(fwd)
        for a in range(n):
            copy(a, 0, sibling, me).wait_recv()
            for j, chip in enumerate(chips):
                copy(a, 4 + j, (*chip, 1 - c), me).wait_recv()
        for cp in first + passed:
            cp.wait_send()
        for cp in mine:
            cp.wait()

    return pl.pallas_call(
        body, name=name, in_specs=[HBM] * n, out_specs=[HBM] * n,
        out_shape=[jax.ShapeDtypeStruct((N_DEV,) + s.shape, s.dtype) for s in shards],
        scratch_shapes=[pltpu.SemaphoreType.DMA((7 * n,)), pltpu.SemaphoreType.DMA((7 * n,)),
                        pltpu.SemaphoreType.DMA((n,))],
    )(*shards)


def _exchange_sibling(grads, name):
    n = len(grads)
    HBM = pl.BlockSpec(memory_space=pltpu.HBM)

    def body(*refs):
        ins, outs = refs[:n], refs[n:2 * n]
        send_sems, recv_sems = refs[2 * n:]
        x, y, c = _place()
        sibling = (x, y, 1 - c)
        for a in range(n):
            for q in range(4):
                pltpu.make_async_remote_copy(
                    src_ref=ins[a].at[2 * q + 1 - c], dst_ref=outs[a].at[q], send_sem=send_sems.at[a],
                    recv_sem=recv_sems.at[a], device_id=sibling, device_id_type=MESH).start()
        for a in range(n):
            pltpu.make_async_remote_copy(
                src_ref=ins[a].at[pl.ds(0, 4)], dst_ref=outs[a], send_sem=send_sems.at[a], recv_sem=recv_sems.at[a],
                device_id=sibling, device_id_type=MESH).wait()

    return pl.pallas_call(
        body, name=name, in_specs=[HBM] * n, out_specs=[HBM] * n,
        out_shape=[jax.ShapeDtypeStruct((4,) + g.shape[1:], g.dtype) for g in grads],
        scratch_shapes=[pltpu.SemaphoreType.DMA((n,)), pltpu.SemaphoreType.DMA((n,))],
    )(*grads)


def _exchange_chips(parts, name):
    n = len(parts)
    HBM = pl.BlockSpec(memory_space=pltpu.HBM)

    def body(*refs):
        ins, outs = refs[:n], refs[n:2 * n]
        send_sems, recv_sems = refs[2 * n:]
        x, y, c = _place()
        chips = [(1 - x, y), (x, 1 - y), (1 - x, 1 - y)]
        copies = []
        for a in range(n):
            for j, chip in enumerate(chips):
                cp = pltpu.make_async_remote_copy(
                    src_ref=ins[a].at[2 * chip[0] + chip[1]], dst_ref=outs[a].at[j], send_sem=send_sems.at[3 * a + j],
                    recv_sem=recv_sems.at[3 * a + j], device_id=(*chip, c), device_id_type=MESH)
                cp.start()
                copies.append(cp)
        for cp in copies:
            cp.wait()

    return pl.pallas_call(
        body, name=name, in_specs=[HBM] * n, out_specs=[HBM] * n,
        out_shape=[jax.ShapeDtypeStruct((3,) + p.shape[1:], p.dtype) for p in parts],
        scratch_shapes=[pltpu.SemaphoreType.DMA((3 * n,)), pltpu.SemaphoreType.DMA((3 * n,))],
    )(*parts)


def _pair_sum(grad, recv, c_idx, name):
    _, R, C = grad.shape
    tr = _tile(R, 512, 16)

    def body(c_ref, g_ref, p_ref, o_ref):
        del c_ref
        o_ref[...] = (g_ref[...].astype(F32) + p_ref[...].astype(F32)).astype(BF16)

    spec = pltpu.PrefetchScalarGridSpec(
        num_scalar_prefetch=1, grid=(4, R // tr),
        in_specs=[pl.BlockSpec((None, tr, C), lambda q, i, c: (2 * q + c[0], i, 0)),
                  pl.BlockSpec((None, tr, C), lambda q, i, c: (q, i, 0))],
        out_specs=pl.BlockSpec((None, tr, C), lambda q, i, c: (q, i, 0)))
    return pl.pallas_call(body, name=name, grid_spec=spec, out_shape=jax.ShapeDtypeStruct((4, R, C), BF16),
                          compiler_params=_cp(("parallel", "parallel")))(c_idx, grad, recv)


def _adam(w, g, m, v):
    m2 = ADAM_B1 * m + (1.0 - ADAM_B1) * g
    v2 = ADAM_B2 * v + (1.0 - ADAM_B2) * (g * g)
    m_hat = m2 / (1.0 - ADAM_B1 ** ADAM_STEP)
    v_hat = v2 / (1.0 - ADAM_B2 ** ADAM_STEP)
    return -ADAM_LR * (m_hat / (jnp.sqrt(v_hat) + ADAM_EPS) + ADAM_WD * w), m2, v2


def _reduce_update(part, recv, chip_idx, w, m, v, layer, prev, name):
    L, R, C = w.shape
    tr = _tile(R, 128, 16)
    has_prev = prev is not None

    def body(idx_ref, q_ref, r0_ref, r1_ref, r2_ref, w_ref, m_ref, v_ref, *rest):
        del idx_ref
        g_ref, d_ref, m2_ref, v2_ref = rest[-4:]
        g = ((q_ref[...].astype(F32) + r0_ref[...].astype(F32)) + r1_ref[...].astype(F32)) + r2_ref[...].astype(F32)
        d, m2, v2 = _adam(w_ref[...], g, m_ref[...], v_ref[...])
        g_ref[...] = g
        d_ref[...] = d
        m2_ref[...] = m2
        v2_ref[...] = v2

    def rblk(j):
        return pl.BlockSpec((None, tr, C), lambda i, k: (j, i, 0))

    lay = pl.BlockSpec((None, tr, C), lambda i, k: (layer, i, 0))
    in_specs = [pl.BlockSpec((None, tr, C), lambda i, k: (k[0], i, 0)), rblk(0), rblk(1), rblk(2), lay, lay, lay]
    ins = [part, recv, recv, recv, w, m, v]
    aliases = {}
    if has_prev:
        in_specs += [pl.BlockSpec(memory_space=pl.ANY)] * 4
        ins += list(prev)
        aliases = {8 + t: t for t in range(4)}
    spec = pltpu.PrefetchScalarGridSpec(num_scalar_prefetch=1, grid=(R // tr,), in_specs=in_specs,
                                        out_specs=[lay, lay, lay, lay])
    return pl.pallas_call(body, name=name, grid_spec=spec, out_shape=[jax.ShapeDtypeStruct((L, R, C), F32)] * 4,
                          input_output_aliases=aliases, compiler_params=_cp(("parallel",)))(chip_idx, *ins)


def _sum_devices(gathered):
    n, R, C = gathered.shape

    def body(g_ref, o_ref):
        acc = g_ref[0]
        for d in range(1, n):
            acc = acc + g_ref[d]
        o_ref[...] = acc

    return pl.pallas_call(body, name="sum_devices", out_shape=jax.ShapeDtypeStruct((R, C), F32))(gathered)


def _adam_small(w, g, m, v, name):
    shape = w.shape
    two = (-1, shape[-1])

    def body(w_ref, g_ref, m_ref, v_ref, d_ref, m2_ref, v2_ref):
        d, m2, v2 = _adam(w_ref[...], g_ref[...], m_ref[...], v_ref[...])
        d_ref[...] = d
        m2_ref[...] = m2
        v2_ref[...] = v2

    outs = pl.pallas_call(body, name=name, out_shape=[jax.ShapeDtypeStruct(w.reshape(two).shape, F32)] * 3)(
        w.reshape(two), g.reshape(two), m.reshape(two), v.reshape(two))
    return [o.reshape(shape) for o in outs]


_SHARDED = ("in", "out", "cq", "ckv", "co", "gu", "down")
_COLUMN_SHARDED = ("in", "ckv", "gu")


def kernel(x, mem, rel_bias, lb_logits, norm_gains, w_in, hg_norm, w_out, w_cq, w_ckv, w_co, w_gate_up, w_down, loss_target, m_rel_bias, m_lb_logits, m_norm_gains, m_w_in, m_hg_norm, m_w_out, m_w_cq, m_w_ckv, m_w_co, m_w_gate_up, m_w_down, v_rel_bias, v_lb_logits, v_norm_gains, v_w_in, v_hg_norm, v_w_out, v_w_cq, v_w_ckv, v_w_co, v_w_gate_up, v_w_down):
    L = w_in.shape[0]
    D = x.shape[-1]
    ws = dict(zip(_SHARDED, (w_in, w_out, w_cq, w_ckv, w_co, w_gate_up, w_down)))
    ms = dict(zip(_SHARDED, (m_w_in, m_w_out, m_w_cq, m_w_ckv, m_w_co, m_w_gate_up, m_w_down)))
    vs = dict(zip(_SHARDED, (v_w_in, v_w_out, v_w_cq, v_w_ckv, v_w_co, v_w_gate_up, v_w_down)))
    px, py, pc = _place()
    dev = 4 * px + 2 * py + pc
    c_idx = jnp.reshape(pc, (1,)).astype(jnp.int32)
    chip_idx = jnp.reshape(2 * px + py, (1,)).astype(jnp.int32)

    weights = []
    for l in range(L):
        got = _all_gather([ws[k][l].astype(BF16) for k in _SHARDED], "gather_weights")
        wl = {}
        for k, g in zip(_SHARDED, got):
            wl[k] = g if k in _COLUMN_SHARDED else g.reshape(1, N_DEV * g.shape[1], g.shape[2])
        weights.append(wl)
    gains = _all_gather([norm_gains], "gather_gains")[0]
    gains = jnp.transpose(gains, (1, 2, 0, 3)).reshape(L, 7, D)

    loss, grad_x, wgrads, d_rel_bias, d_lb, d_hg, d_gains = _local_step(
        x[0], mem[0], loss_target[0], rel_bias, lb_logits, gains, hg_norm, weights)

    results = {k: None for k in _SHARDED}
    for l in range(L):
        full = [wgrads[l][k].reshape((N_DEV,) + ws[k].shape[1:]) for k in _SHARDED]
        from_sibling = _exchange_sibling(full, "scatter_sibling")
        parts = [_pair_sum(g, r, c_idx, "pair_sum") for g, r in zip(full, from_sibling)]
        from_chips = _exchange_chips(parts, "scatter_chips")
        for k, part, recv in zip(_SHARDED, parts, from_chips):
            results[k] = _reduce_update(part, recv, chip_idx, ws[k], ms[k], vs[k], l, results[k], "reduce_update")

    pieces = [jnp.reshape(loss, (1,)), d_rel_bias.reshape(-1), d_lb.reshape(-1), d_hg.reshape(-1), d_gains.reshape(-1)]
    sizes = [p.shape[0] for p in pieces]
    total = sum(sizes)
    rows = -(-total // 1024) * 8
    pack = jnp.pad(jnp.concatenate(pieces), (0, rows * 128 - total)).reshape(rows, 128)
    summed = _sum_devices(_all_gather([pack], "gather_small")[0]).reshape(-1)
    offs = [sum(sizes[:i]) for i in range(len(sizes))]
    loss = summed[0]
    g_rel_bias = summed[offs[1]:offs[1] + sizes[1]].reshape(rel_bias.shape)
    g_lb_all = summed[offs[2]:offs[2] + sizes[2]].reshape(lb_logits.shape)
    g_hg = summed[offs[3]:offs[3] + sizes[3]].reshape(hg_norm.shape)
    g_gains_full = summed[offs[4]:offs[4] + sizes[4]].reshape(L, 7, D)
    g_gains = lax.dynamic_slice_in_dim(g_gains_full, dev * (D // N_DEV), D // N_DEV, axis=2)
    g_lb = jax.vjp(_lb_all, lb_logits)[1](g_lb_all)[0]

    small = [(rel_bias, g_rel_bias, m_rel_bias, v_rel_bias), (lb_logits, g_lb, m_lb_logits, v_lb_logits),
             (norm_gains, g_gains, m_norm_gains, v_norm_gains), (hg_norm, g_hg, m_hg_norm, v_hg_norm)]
    upd = [_adam_small(w, g, m, v, "adam_small") for w, g, m, v in small]

    def ordered(small_vals, big_vals):
        return [small_vals[0], small_vals[1], small_vals[2], big_vals["in"], small_vals[3], big_vals["out"],
                big_vals["cq"], big_vals["ckv"], big_vals["co"], big_vals["gu"], big_vals["down"]]

    grads = ordered([s[1] for s in small], {k: results[k][0] for k in _SHARDED})
    deltas = ordered([u[0] for u in upd], {k: results[k][1] for k in _SHARDED})
    new_m = ordered([u[1] for u in upd], {k: results[k][2] for k in _SHARDED})
    new_v = ordered([u[2] for u in upd], {k: results[k][3] for k in _SHARDED})
    return (loss, grad_x[None], *grads, *deltas, *new_m, *new_v)
```

```python
import functools
import math

import jax
import jax.numpy as jnp
from jax import lax
from jax.experimental import pallas as pl
from jax.experimental.pallas import tpu as pltpu

F32 = jnp.float32
BF16 = jnp.bfloat16
MESH = pl.DeviceIdType.MESH

N_DEV = 8
HEAD = 128
CROSS_HEADS = 4
Q_BLOCK = 128
BRANCHES = ((128, 1), (512, 4), (2048, 16))
REL_BUCKETS = 32
REL_MAX_DIST = 2048
HG_CHUNK = 16
HG_HEADS_PER_STEP = 4
RMS_EPS = 1e-6
NEG_INF = -1e30
ADAM_LR, ADAM_B1, ADAM_B2, ADAM_EPS, ADAM_WD, ADAM_STEP = 0.001, 0.9, 0.999, 1e-08, 0.01, 10
VMEM_LIMIT = 48 * 1024 * 1024
NT_K_PER_STEP = 3584


def _tile(n, target, mult):
    best = None
    t = mult
    while t <= min(n, target):
        if n % t == 0:
            best = t
        t += mult
    return best if best is not None else n


def _cp(sem, vmem=VMEM_LIMIT):
    return pltpu.CompilerParams(dimension_semantics=sem, vmem_limit_bytes=vmem)


def _mm_nn(a, b3, out_dtype, name):
    M, K = a.shape
    J, K2, Nb = b3.shape
    assert K == K2
    tm, tn, tk = _tile(M, 1024, 8), _tile(Nb, 1408, 128), _tile(K, 2816, 128)
    nn, nk = Nb // tn, K // tk

    def body(a_ref, b_ref, o_ref, *acc):
        prod = jnp.dot(a_ref[...].astype(BF16), b_ref[...].astype(BF16), preferred_element_type=F32)
        if nk == 1:
            o_ref[...] = prod.astype(o_ref.dtype)
        else:
            k = pl.program_id(2)

            @pl.when(k == 0)
            def _():
                acc[0][...] = prod

            @pl.when(k > 0)
            def _():
                acc[0][...] += prod

            @pl.when(k == nk - 1)
            def _():
                o_ref[...] = acc[0][...].astype(o_ref.dtype)

    return pl.pallas_call(
        body, name=name, grid=(M // tm, J * nn, nk),
        in_specs=[pl.BlockSpec((tm, tk), lambda i, n, k: (i, k)),
                  pl.BlockSpec((None, tk, tn), lambda i, n, k: (n // nn, k, n % nn))],
        out_specs=pl.BlockSpec((tm, tn), lambda i, n, k: (i, n)),
        out_shape=jax.ShapeDtypeStruct((M, J * Nb), out_dtype),
        scratch_shapes=[] if nk == 1 else [pltpu.VMEM((tm, tn), F32)],
        compiler_params=_cp(("parallel", "parallel", "arbitrary")),
    )(a, b3)


def _mm_nt(a, b3, out_dtype, name):
    M, Kt = a.shape
    J, N, Kb = b3.shape
    assert Kt == J * Kb
    tm, tn, tk = _tile(M, 1024, 8), _tile(N, 1408, 128), _tile(Kb, 2048, 128)
    nkb = Kb // tk
    jb = max(j for j in range(1, J + 1) if J % j == 0 and j * Kb <= NT_K_PER_STEP) if nkb == 1 else 1
    nk = (J // jb) * nkb

    def body(a_ref, b_ref, o_ref, *acc):
        prod = None
        for j in range(jb):
            part = lax.dot_general(a_ref[:, j * tk:(j + 1) * tk].astype(BF16), b_ref[j].astype(BF16),
                                   (((1,), (1,)), ((), ())), preferred_element_type=F32)
            prod = part if prod is None else prod + part
        if nk == 1:
            o_ref[...] = prod.astype(o_ref.dtype)
        else:
            k = pl.program_id(2)

            @pl.when(k == 0)
            def _():
                acc[0][...] = prod

            @pl.when(k > 0)
            def _():
                acc[0][...] += prod

            @pl.when(k == nk - 1)
            def _():
                o_ref[...] = acc[0][...].astype(o_ref.dtype)

    return pl.pallas_call(
        body, name=name, grid=(M // tm, N // tn, nk),
        in_specs=[pl.BlockSpec((tm, jb * tk), lambda i, n, k: (i, k)),
                  pl.BlockSpec((jb, tn, tk), lambda i, n, k: (k // nkb, n, k % nkb))],
        out_specs=pl.BlockSpec((tm, tn), lambda i, n, k: (i, n)),
        out_shape=jax.ShapeDtypeStruct((M, N), out_dtype),
        scratch_shapes=[] if nk == 1 else [pltpu.VMEM((tm, tn), F32)],
        compiler_params=_cp(("parallel", "parallel", "arbitrary")),
    )(a, b3)


def _mm_tn(a, b, J, out_dtype, name):
    S, M = a.shape
    S2, Nt = b.shape
    assert S == S2 and Nt % J == 0
    Nb = Nt // J
    tm, tn, tk = _tile(M, 1408, 128), _tile(Nb, 1408, 128), _tile(S, 2048, 8)
    nn, nk = Nb // tn, S // tk

    def body(a_ref, b_ref, o_ref, *acc):
        prod = lax.dot_general(a_ref[...].astype(BF16), b_ref[...].astype(BF16), (((0,), (0,)), ((), ())),
                               preferred_element_type=F32)
        if nk == 1:
            o_ref[...] = prod.astype(o_ref.dtype)
        else:
            k = pl.program_id(2)

            @pl.when(k == 0)
            def _():
                acc[0][...] = prod

            @pl.when(k > 0)
            def _():
                acc[0][...] += prod

            @pl.when(k == nk - 1)
            def _():
                o_ref[...] = acc[0][...].astype(o_ref.dtype)

    return pl.pallas_call(
        body, name=name, grid=(M // tm, J * nn, nk),
        in_specs=[pl.BlockSpec((tk, tm), lambda i, n, k: (k, i)),
                  pl.BlockSpec((tk, tn), lambda i, n, k: (k, n))],
        out_specs=pl.BlockSpec((None, tm, tn), lambda i, n, k: (n // nn, i, n % nn)),
        out_shape=jax.ShapeDtypeStruct((J, M, Nb), out_dtype),
        scratch_shapes=[] if nk == 1 else [pltpu.VMEM((tm, tn), F32)],
        compiler_params=_cp(("parallel", "parallel", "arbitrary")),
    )(a, b)


def _rms_scale(v):
    return lax.rsqrt(jnp.mean(v * v, axis=-1, keepdims=True) + RMS_EPS)


def _rms_bwd(dy, v, g):
    r = _rms_scale(v)
    vh = v * r
    u = dy * g
    dv = r * (u - vh * jnp.mean(u * vh, axis=-1, keepdims=True))
    return dv, dy * vh


def _fold8(t):
    R, D = t.shape
    return jnp.sum(t.reshape(R // 8, 8, D), axis=0)


def _norm_fwd(x, z, g_post, g_pre, name):
    S, D = x.shape
    tr = _tile(S, 256, 8)
    has_z, has_pre = z is not None, g_pre is not None

    def body(*refs):
        it = iter(refs)
        x_ref = next(it)
        z_ref = next(it) if has_z else None
        gpost_ref = next(it) if has_z else None
        gpre_ref = next(it) if has_pre else None
        xn_ref = next(it) if has_z else None
        h_ref = next(it) if has_pre else None
        xn = x_ref[...]
        if has_z:
            zz = z_ref[...]
            xn = xn + zz * _rms_scale(zz) * gpost_ref[...]
            xn_ref[...] = xn
        if has_pre:
            h_ref[...] = (xn * _rms_scale(xn) * gpre_ref[...]).astype(BF16)

    row = pl.BlockSpec((tr, D), lambda i: (i, 0))
    gain = pl.BlockSpec((1, D), lambda i: (0, 0))
    ins, specs, outs, ospecs = [x], [row], [], []
    if has_z:
        ins += [z, g_post.reshape(1, D)]
        specs += [row, gain]
        outs.append(jax.ShapeDtypeStruct((S, D), F32))
        ospecs.append(row)
    if has_pre:
        ins.append(g_pre.reshape(1, D))
        specs.append(gain)
        outs.append(jax.ShapeDtypeStruct((S, D), BF16))
        ospecs.append(row)
    res = pl.pallas_call(body, name=name, grid=(S // tr,), in_specs=specs, out_specs=ospecs, out_shape=outs,
                         compiler_params=_cp(("parallel",)))(*ins)
    res = list(res)
    xn = res.pop(0) if has_z else x
    h = res.pop(0) if has_pre else None
    return xn, h


def _norm_bwd(dres, dh, xn, z, g_pre, g_post, name):
    S, D = xn.shape
    tr = _tile(S, 256, 8)
    has_res, has_pre, has_z = dres is not None, dh is not None, z is not None

    def body(*refs):
        it = iter(refs)
        dres_ref = next(it) if has_res else None
        dh_ref = next(it) if has_pre else None
        xn_ref = next(it) if has_pre else None
        gpre_ref = next(it) if has_pre else None
        z_ref = next(it) if has_z else None
        gpost_ref = next(it) if has_z else None
        t_ref = next(it)
        dz_ref = next(it) if has_z else None
        dgpre_ref = next(it) if has_pre else None
        dgpost_ref = next(it) if has_z else None
        first = pl.program_id(0) == 0
        t = dres_ref[...] if has_res else None
        if has_pre:
            dv, dg = _rms_bwd(dh_ref[...].astype(F32), xn_ref[...], gpre_ref[...])
            t = dv if t is None else t + dv
            part = _fold8(dg)

            @pl.when(first)
            def _():
                dgpre_ref[...] = part

            @pl.when(jnp.logical_not(first))
            def _():
                dgpre_ref[...] += part
        t_ref[...] = t
        if has_z:
            dv, dg = _rms_bwd(t, z_ref[...], gpost_ref[...])
            dz_ref[...] = dv.astype(BF16)
            part2 = _fold8(dg)

            @pl.when(first)
            def _():
                dgpost_ref[...] = part2

            @pl.when(jnp.logical_not(first))
            def _():
                dgpost_ref[...] += part2

    row = pl.BlockSpec((tr, D), lambda i: (i, 0))
    gain = pl.BlockSpec((1, D), lambda i: (0, 0))
    accs = pl.BlockSpec((8, D), lambda i: (0, 0))
    ins, specs = [], []
    if has_res:
        ins.append(dres)
        specs.append(row)
    if has_pre:
        ins += [dh, xn, g_pre.reshape(1, D)]
        specs += [row, row, gain]
    if has_z:
        ins += [z, g_post.reshape(1, D)]
        specs += [row, gain]
    outs, ospecs = [jax.ShapeDtypeStruct((S, D), F32)], [row]
    if has_z:
        outs.append(jax.ShapeDtypeStruct((S, D), BF16))
        ospecs.append(row)
    if has_pre:
        outs.append(jax.ShapeDtypeStruct((8, D), F32))
        ospecs.append(accs)
    if has_z:
        outs.append(jax.ShapeDtypeStruct((8, D), F32))
        ospecs.append(accs)
    res = list(pl.pallas_call(body, name=name, grid=(S // tr,), in_specs=specs, out_specs=ospecs, out_shape=outs,
                              compiler_params=_cp(("arbitrary",)))(*ins))
    t = res.pop(0)
    dz = res.pop(0) if has_z else None
    dgpre = jnp.sum(res.pop(0), axis=0) if has_pre else None
    dgpost = jnp.sum(res.pop(0), axis=0) if has_z else None
    return t, dz, dgpre, dgpost


def _loss_and_grad(y, target):
    S, D = y.shape
    tr = _tile(S, 256, 8)

    def body(y_ref, t_ref, dy_ref, l_ref):
        err = y_ref[...] - t_ref[...]
        dy_ref[...] = err * (1.0 / D)
        part = 0.5 * jnp.sum(jnp.mean(err * err, axis=-1, keepdims=True), axis=0, keepdims=True)
        first = pl.program_id(0) == 0

        @pl.when(first)
        def _():
            l_ref[...] = jnp.broadcast_to(part, l_ref.shape)

        @pl.when(jnp.logical_not(first))
        def _():
            l_ref[...] += jnp.broadcast_to(part, l_ref.shape)

    row = pl.BlockSpec((tr, D), lambda i: (i, 0))
    dy, l = pl.pallas_call(
        body, name="loss_grad", grid=(S // tr,), in_specs=[row, row],
        out_specs=[row, pl.BlockSpec((8, 128), lambda i: (0, 0))],
        out_shape=[jax.ShapeDtypeStruct((S, D), F32), jax.ShapeDtypeStruct((8, 128), F32)],
        compiler_params=_cp(("arbitrary",)))(y, target)
    return l[0, 0], dy


def _swiglu_fwd(gu, name):
    S, F2 = gu.shape
    F = F2 // 2
    tr, tc = _tile(S, 512, 8), _tile(F, 1536, 128)
    nf = F // tc

    def body(g_ref, u_ref, o_ref):
        g = g_ref[...]
        o_ref[...] = (g * jax.nn.sigmoid(g) * u_ref[...]).astype(BF16)

    return pl.pallas_call(
        body, name=name, grid=(S // tr, nf),
        in_specs=[pl.BlockSpec((tr, tc), lambda i, j: (i, j)), pl.BlockSpec((tr, tc), lambda i, j: (i, j + nf))],
        out_specs=pl.BlockSpec((tr, tc), lambda i, j: (i, j)),
        out_shape=jax.ShapeDtypeStruct((S, F), BF16), compiler_params=_cp(("parallel", "parallel")))(gu, gu)


def _swiglu_bwd(gu, dact, name):
    S, F2 = gu.shape
    F = F2 // 2
    tr, tc = _tile(S, 512, 8), _tile(F, 1536, 128)
    nf = F // tc

    def body(g_ref, u_ref, d_ref, o_ref):
        g = g_ref[...]
        d = d_ref[...].astype(F32)
        sg = jax.nn.sigmoid(g)

        @pl.when(pl.program_id(1) < nf)
        def _():
            o_ref[...] = (d * u_ref[...] * (sg * (1.0 + g * (1.0 - sg)))).astype(BF16)

        @pl.when(pl.program_id(1) >= nf)
        def _():
            o_ref[...] = (d * g * sg).astype(BF16)

    return pl.pallas_call(
        body, name=name, grid=(S // tr, 2 * nf),
        in_specs=[pl.BlockSpec((tr, tc), lambda i, j: (i, j % nf)), pl.BlockSpec((tr, tc), lambda i, j: (i, j % nf + nf)),
                  pl.BlockSpec((tr, tc), lambda i, j: (i, j % nf))],
        out_specs=pl.BlockSpec((tr, tc), lambda i, j: (i, j)),
        out_shape=jax.ShapeDtypeStruct((S, F2), BF16), compiler_params=_cp(("parallel", "parallel")))(gu, gu, dact)


def _cross_scores(q, k, scale):
    s = lax.dot_general(q, k, (((1,), (1,)), ((), ())), preferred_element_type=F32) * scale
    m = jnp.max(s, axis=-1, keepdims=True)
    e = jnp.exp(s - m)
    return e / jnp.sum(e, axis=-1, keepdims=True)


def _cross_fwd(cq, ckv, name):
    S, D = cq.shape
    Nm = ckv.shape[0]
    dh = D // CROSS_HEADS
    tq = _tile(S, 512, 8)
    scale = dh ** -0.5

    def body(q_ref, kv_ref, o_ref):
        for h in range(CROSS_HEADS):
            cols = slice(h * dh, (h + 1) * dh)
            p = _cross_scores(q_ref[:, cols], kv_ref[:, cols], scale)
            o_ref[:, cols] = jnp.dot(p.astype(BF16), kv_ref[:, D + h * dh:D + (h + 1) * dh],
                                     preferred_element_type=F32).astype(BF16)

    return pl.pallas_call(
        body, name=name, grid=(S // tq,),
        in_specs=[pl.BlockSpec((tq, D), lambda i: (i, 0)), pl.BlockSpec((Nm, 2 * D), lambda i: (0, 0))],
        out_specs=pl.BlockSpec((tq, D), lambda i: (i, 0)),
        out_shape=jax.ShapeDtypeStruct((S, D), BF16), compiler_params=_cp(("parallel",)))(cq, ckv)


def _cross_bwd(cq, ckv, do, name):
    S, D = cq.shape
    Nm = ckv.shape[0]
    dh = D // CROSS_HEADS
    tq = _tile(S, 512, 8)
    scale = dh ** -0.5

    def body(q_ref, kv_ref, do_ref, dq_ref, dkv_ref):
        first = pl.program_id(0) == 0

        @pl.when(first)
        def _():
            dkv_ref[...] = jnp.zeros_like(dkv_ref)

        for h in range(CROSS_HEADS):
            cols = slice(h * dh, (h + 1) * dh)
            vcols = slice(D + h * dh, D + (h + 1) * dh)
            q, k, v = q_ref[:, cols], kv_ref[:, cols], kv_ref[:, vcols]
            doh = do_ref[:, cols].astype(BF16)
            p = _cross_scores(q, k, scale)
            dp = lax.dot_general(doh, v, (((1,), (1,)), ((), ())), preferred_element_type=F32)
            ds = (p * (dp - jnp.sum(p * dp, axis=-1, keepdims=True)) * scale).astype(BF16)
            dq_ref[:, cols] = jnp.dot(ds, k, preferred_element_type=F32).astype(BF16)
            dkv_ref[:, cols] += lax.dot_general(ds, q, (((0,), (0,)), ((), ())), preferred_element_type=F32)
            dkv_ref[:, vcols] += lax.dot_general(p.astype(BF16), doh, (((0,), (0,)), ((), ())),
                                                 preferred_element_type=F32)

    return pl.pallas_call(
        body, name=name, grid=(S // tq,),
        in_specs=[pl.BlockSpec((tq, D), lambda i: (i, 0)), pl.BlockSpec((Nm, 2 * D), lambda i: (0, 0)),
                  pl.BlockSpec((tq, D), lambda i: (i, 0))],
        out_specs=[pl.BlockSpec((tq, D), lambda i: (i, 0)), pl.BlockSpec((Nm, 2 * D), lambda i: (0, 0))],
        out_shape=[jax.ShapeDtypeStruct((S, D), BF16), jax.ShapeDtypeStruct((Nm, 2 * D), F32)],
        compiler_params=_cp(("arbitrary",)))(cq, ckv, do)


def _rel_bucket(dist):
    max_exact = REL_BUCKETS // 2
    d_f = jnp.maximum(dist, 1).astype(F32)
    large = max_exact + (jnp.log(d_f / max_exact) / math.log(REL_MAX_DIST / max_exact)
                         * (REL_BUCKETS - max_exact)).astype(jnp.int32)
    large = jnp.minimum(large, REL_BUCKETS - 1)
    return jnp.where(dist < max_exact, dist, large)


def _branch_tables(dilation):
    qi = jnp.arange(Q_BLOCK)[:, None]
    kj = jnp.arange(2 * Q_BLOCK)[None, :]
    m = qi - kj + Q_BLOCK
    return _rel_bucket(jnp.maximum(m, 0) * dilation), (m >= 0) & (m <= Q_BLOCK)


def _attn_logits(q, kcat, bias, first_block, scale):
    s = lax.dot_general(q, kcat, (((1,), (1,)), ((), ())), preferred_element_type=F32) * scale + bias
    col = lax.broadcasted_iota(jnp.int32, s.shape, 1)
    return jnp.where(jnp.logical_and(first_block, col < Q_BLOCK), NEG_INF, s)


def _attn_branch_fwd(proj, bias, dilation, H, name):
    S, NC = proj.shape
    AW = H * HEAD
    r = dilation
    hb = H if r == 1 else 1
    G = Q_BLOCK * r
    ng, nh = S // G, H // hb
    scale = HEAD ** -0.5

    def body(q_ref, kp_ref, kc_ref, vp_ref, vc_ref, b_ref, o_ref, l_ref):
        first_block = pl.program_id(0) == 0
        h0 = pl.program_id(1) * hb
        for rho in range(r):
            rows = pl.ds(rho, Q_BLOCK, stride=r) if r > 1 else slice(None)
            for hh in range(hb):
                cols = slice(hh * HEAD, (hh + 1) * HEAD)
                q = q_ref[rows, cols].astype(BF16)
                kcat = jnp.concatenate([kp_ref[rows, cols], kc_ref[rows, cols]], axis=0).astype(BF16)
                vcat = jnp.concatenate([vp_ref[rows, cols], vc_ref[rows, cols]], axis=0).astype(BF16)
                s = _attn_logits(q, kcat, b_ref[h0 + hh], first_block, scale)
                m = jnp.max(s, axis=-1, keepdims=True)
                e = jnp.exp(s - m)
                den = jnp.sum(e, axis=-1, keepdims=True)
                o_ref[rows, cols] = jnp.dot(e.astype(BF16), vcat, preferred_element_type=F32) / den
                l_ref[rows, cols] = jnp.broadcast_to(m + jnp.log(den), (Q_BLOCK, HEAD))

    def blk(part, prev):
        if prev:
            return pl.BlockSpec((G, hb * HEAD), lambda n, h: (jnp.maximum(n - 1, 0), part * nh + h))
        return pl.BlockSpec((G, hb * HEAD), lambda n, h: (n, part * nh + h))

    out = pl.BlockSpec((G, hb * HEAD), lambda n, h: (n, h))
    return pl.pallas_call(
        body, name=name, grid=(ng, nh),
        in_specs=[blk(0, False), blk(1, True), blk(1, False), blk(2, True), blk(2, False),
                  pl.BlockSpec((H, Q_BLOCK, 2 * Q_BLOCK), lambda n, h: (0, 0, 0))],
        out_specs=[out, out],
        out_shape=[jax.ShapeDtypeStruct((S, AW), F32)] * 2,
        compiler_params=_cp(("parallel", "parallel")))(proj, proj, proj, proj, proj, bias)


def _attn_merge(outs, lses, width_out, name):
    S, AW = outs[0].shape
    tr = _tile(S, 256, 8)

    def body(o0, o1, o2, l0, l1, l2, cat_ref, om_ref, lt_ref):
        a, b, c = l0[...], l1[...], l2[...]
        m = jnp.maximum(jnp.maximum(a, b), c)
        ea, eb, ec = jnp.exp(a - m), jnp.exp(b - m), jnp.exp(c - m)
        tot = ea + eb + ec
        o = (ea * o0[...] + eb * o1[...] + ec * o2[...]) / tot
        om_ref[...] = o
        cat_ref[...] = o.astype(BF16)
        lt_ref[...] = m + jnp.log(tot)

    row = pl.BlockSpec((tr, AW), lambda i: (i, 0))
    return pl.pallas_call(
        body, name=name, grid=(S // tr,), in_specs=[row] * 6, out_specs=[row, row, row],
        out_shape=[jax.ShapeDtypeStruct((S, width_out), BF16), jax.ShapeDtypeStruct((S, AW), F32),
                   jax.ShapeDtypeStruct((S, AW), F32)],
        compiler_params=_cp(("parallel",)))(*outs, *lses)


def _attn_delta(dcat, o_attn, H, name):
    S, AW = o_attn.shape
    tr = _tile(S, 256, 8)

    def body(d_ref, o_ref, out_ref):
        for h in range(H):
            cols = slice(h * HEAD, (h + 1) * HEAD)
            out_ref[:, cols] = jnp.broadcast_to(
                jnp.sum(d_ref[:, cols] * o_ref[:, cols], axis=-1, keepdims=True), (tr, HEAD))

    row = pl.BlockSpec((tr, AW), lambda i: (i, 0))
    return pl.pallas_call(body, name=name, grid=(S // tr,), in_specs=[row, row], out_specs=row,
                          out_shape=jax.ShapeDtypeStruct((S, AW), F32),
                          compiler_params=_cp(("parallel",)))(dcat, o_attn)


def _attn_branch_bwd(proj, dcat, lse_tot, delta, bias, dilation, H, name):
    S, NC = proj.shape
    AW = H * HEAD
    r = dilation
    hb = H if r == 1 else 1
    G = Q_BLOCK * r
    ng, nh = S // G, H // hb
    scale = HEAD ** -0.5

    def body(q_ref, kp_ref, kc_ref, vp_ref, vc_ref, do_ref, l_ref, t_ref, b_ref,
             dq_ref, dk_ref, dv_ref, db_ref, ck_ref, cv_ref):
        hblk, n = pl.program_id(0), pl.program_id(1)
        h0 = hblk * hb

        @pl.when(jnp.logical_and(hblk == 0, n == 0))
        def _():
            db_ref[...] = jnp.zeros_like(db_ref)

        @pl.when(n == 0)
        def _():
            ck_ref[...] = jnp.zeros_like(ck_ref)
            cv_ref[...] = jnp.zeros_like(cv_ref)

        @pl.when(n < ng)
        def _():
            for rho in range(r):
                rows = pl.ds(rho, Q_BLOCK, stride=r) if r > 1 else slice(None)
                for hh in range(hb):
                    cols = slice(hh * HEAD, (hh + 1) * HEAD)
                    q = q_ref[rows, cols].astype(BF16)
                    kcat = jnp.concatenate([kp_ref[rows, cols], kc_ref[rows, cols]], axis=0).astype(BF16)
                    vcat = jnp.concatenate([vp_ref[rows, cols], vc_ref[rows, cols]], axis=0).astype(BF16)
                    doh = do_ref[rows, cols].astype(BF16)
                    s = _attn_logits(q, kcat, b_ref[h0 + hh], n == 0, scale)
                    lt = l_ref[rows, cols]
                    dl = t_ref[rows, cols]
                    p = jnp.exp(s - jnp.concatenate([lt, lt], axis=1))
                    dp = lax.dot_general(doh, vcat, (((1,), (1,)), ((), ())), preferred_element_type=F32)
                    ds = p * (dp - jnp.concatenate([dl, dl], axis=1))
                    db_ref[h0 + hh] += ds
                    dsb = (ds * scale).astype(BF16)
                    dq_ref[rows, cols] = jnp.dot(dsb, kcat, preferred_element_type=F32)
                    dkc = lax.dot_general(dsb, q, (((0,), (0,)), ((), ())), preferred_element_type=F32)
                    dvc = lax.dot_general(p.astype(BF16), doh, (((0,), (0,)), ((), ())), preferred_element_type=F32)
                    dk_ref[rows, cols] = ck_ref[rows, cols] + dkc[:Q_BLOCK]
                    dv_ref[rows, cols] = cv_ref[rows, cols] + dvc[:Q_BLOCK]
                    ck_ref[rows, cols] = dkc[Q_BLOCK:]
                    cv_ref[rows, cols] = dvc[Q_BLOCK:]

        @pl.when(n == ng)
        def _():
            dk_ref[...] = ck_ref[...]
            dv_ref[...] = cv_ref[...]

    last = ng - 1

    def cur(part):
        return pl.BlockSpec((G, hb * HEAD), lambda h, n: (jnp.minimum(n, last), part * nh + h))

    def prev(part):
        return pl.BlockSpec((G, hb * HEAD), lambda h, n: (jnp.clip(n - 1, 0, last), part * nh + h))

    table = pl.BlockSpec((H, Q_BLOCK, 2 * Q_BLOCK), lambda h, n: (0, 0, 0))
    return pl.pallas_call(
        body, name=name, grid=(nh, ng + 1),
        in_specs=[cur(0), prev(1), cur(1), prev(2), cur(2), cur(0), cur(0), cur(0), table],
        out_specs=[cur(0), prev(0), prev(0), table],
        out_shape=[jax.ShapeDtypeStruct((S, AW), F32)] * 3 + [jax.ShapeDtypeStruct((H, Q_BLOCK, 2 * Q_BLOCK), F32)],
        scratch_shapes=[pltpu.VMEM((G, hb * HEAD), F32), pltpu.VMEM((G, hb * HEAD), F32)],
        compiler_params=_cp(("arbitrary", "arbitrary")))(proj, proj, proj, proj, proj, dcat, lse_tot, delta, bias)


def _bias_grad(dbs, onehots, H):
    def body(d0, d1, d2, e0, e1, e2, o_ref):
        acc = jnp.zeros((H, REL_BUCKETS), F32)
        for d, e in ((d0, e0), (d1, e1), (d2, e2)):
            acc = acc + lax.dot_general(d[...], e[...], (((1,), (1,)), ((), ())), preferred_element_type=F32,
                                        precision=lax.Precision.HIGHEST)
        o_ref[...] = acc

    flat = [d.reshape(H, 2 * Q_BLOCK * Q_BLOCK) for d in dbs]
    return pl.pallas_call(body, name="bias_grad", out_shape=jax.ShapeDtypeStruct((H, REL_BUCKETS), F32),
                          compiler_params=pltpu.CompilerParams(vmem_limit_bytes=VMEM_LIMIT))(*flat, *onehots)


def _assemble_dproj(dqkv, dhg, name):
    S, AW = dqkv[0][0].shape
    tr = _tile(S, 256, 8)
    ins = [dqkv[g][c] for g in range(3) for c in range(3)] + list(dhg)

    def body(*refs):
        o_ref = refs[-1]
        j = pl.program_id(1)
        for c in range(3):
            @pl.when(j == c)
            def _(c=c):
                o_ref[...] = (refs[c][...] + refs[3 + c][...] + refs[6 + c][...]).astype(BF16)

        for k in range(4):
            @pl.when(j == 3 + k)
            def _(k=k):
                o_ref[...] = refs[9 + k][...]

    row = pl.BlockSpec((tr, AW), lambda i, j: (i, 0))
    return pl.pallas_call(
        body, name=name, grid=(S // tr, 7), in_specs=[row] * 13,
        out_specs=pl.BlockSpec((tr, AW), lambda i, j: (i, j)),
        out_shape=jax.ShapeDtypeStruct((S, 7 * AW), BF16),
        compiler_params=_cp(("parallel", "arbitrary")))(*ins)


def _tri(n, upper):
    r = lax.broadcasted_iota(jnp.int32, (n, n), 0)
    c = lax.broadcasted_iota(jnp.int32, (n, n), 1)
    return jnp.where(c >= r if upper else c <= r, 1.0, 0.0).astype(F32)


def _hg_gates(fz, qz, lb):
    sig = jax.nn.sigmoid(fz)
    f = lb + (1.0 - lb) * sig
    sq = jax.nn.sigmoid(qz)
    return sig, f, jnp.log(f), 1.0 - f, qz * sq, sq


def _hg_fwd(proj, cat, lb, gain, HH, name):
    S, NC = proj.shape
    W = HH * HEAD
    C = HG_CHUNK
    hb = min(HG_HEADS_PER_STEP, HH)
    nh = HH // hb
    tb = _tile(S, 512, C)
    ncb = tb // C
    base = 3 * nh
    cat_cols = cat.shape[1] // (hb * HEAD)

    def body(fz_ref, iv_ref, qz_ref, gz_ref, lb_ref, g_ref, cat_in, cat_ref, o_ref, st_ref, state):
        del cat_in

        @pl.when(pl.program_id(1) == 0)
        def _():
            state[...] = jnp.zeros_like(state)

        tri = _tri(C, False)
        row = lax.broadcasted_iota(jnp.int32, (C, 1), 0)

        def chunk(c, carry):
            rows = pl.ds(pl.multiple_of(c * C, C), C)
            for hh in range(hb):
                cols = slice(hh * HEAD, (hh + 1) * HEAD)
                lbv, gv = lb_ref[:, cols], g_ref[:, cols]
                fz, vv, qz, gz = fz_ref[rows, cols], iv_ref[rows, cols], qz_ref[rows, cols], gz_ref[rows, cols]
                _, f, lf, kk, qq, _ = _hg_gates(fz, qz, lbv)
                b = jnp.dot(tri, lf, preferred_element_type=F32, precision=lax.Precision.HIGHEST)
                bl = b[C - 1:C, :]
                st = state[hh]
                stb = st.astype(BF16)
                st_ref[hh, c] = stb
                o = lax.dot_general((qq * jnp.exp(b)).astype(BF16), stb, (((1,), (1,)), ((), ())),
                                    preferred_element_type=F32)
                for s in range(C):
                    dec = jnp.exp(jnp.minimum(b - b[s:s + 1, :], 0.0))
                    a = jnp.sum(qq * dec * kk[s:s + 1, :], axis=-1, keepdims=True)
                    o = o + jnp.where(row >= s, a, 0.0) * vv[s:s + 1, :]
                kh = (kk * jnp.exp(bl - b)).astype(BF16)
                upd = lax.dot_general(vv.astype(BF16), kh, (((0,), (0,)), ((), ())), preferred_element_type=F32)
                state[hh] = st * jnp.exp(bl) + upd
                o_ref[rows, cols] = o
                n = o * lax.rsqrt(jnp.mean(o * o, axis=-1, keepdims=True) + RMS_EPS)
                cat_ref[rows, cols] = (n * gv * (gz * jax.nn.sigmoid(gz))).astype(BF16)
            return carry

        lax.fori_loop(0, ncb, chunk, 0)

    def col(k):
        return pl.BlockSpec((tb, hb * HEAD), lambda h, i: (i, base + k * nh + h))

    vec = pl.BlockSpec((1, hb * HEAD), lambda h, i: (0, h))
    cat_out, o_raw, states = pl.pallas_call(
        body, name=name, grid=(nh, S // tb),
        in_specs=[col(0), col(1), col(2), col(3), vec, vec, pl.BlockSpec(memory_space=pl.ANY)],
        out_specs=[pl.BlockSpec((tb, hb * HEAD), lambda h, i: (i, cat_cols - nh + h)),
                   pl.BlockSpec((tb, hb * HEAD), lambda h, i: (i, h)),
                   pl.BlockSpec((hb, ncb, HEAD, HEAD), lambda h, i: (h, i, 0, 0))],
        out_shape=[jax.ShapeDtypeStruct(cat.shape, BF16), jax.ShapeDtypeStruct((S, W), F32),
                   jax.ShapeDtypeStruct((HH, S // C, HEAD, HEAD), BF16)],
        scratch_shapes=[pltpu.VMEM((hb, HEAD, HEAD), F32)],
        input_output_aliases={6: 0},
        compiler_params=_cp(("parallel", "arbitrary")))(proj, proj, proj, proj, lb.reshape(1, W), gain.reshape(1, W), cat)
    return cat_out, o_raw, states


def _hg_bwd(proj, dcat, o_raw, states, lb, gain, HH, name):
    S, NC = proj.shape
    W = HH * HEAD
    C = HG_CHUNK
    hb = min(HG_HEADS_PER_STEP, HH)
    nh = HH // hb
    tb = _tile(S, 512, C)
    ncb = tb // C
    nt = S // tb
    base = 3 * nh
    dcat_cols = dcat.shape[1] // (hb * HEAD)

    def body(fz_ref, iv_ref, qz_ref, gz_ref, dc_ref, o_ref, st_ref, lb_ref, g_ref,
             dfz_ref, div_ref, dqz_ref, dgz_ref, dlb_ref, dg_ref, dstate):
        @pl.when(pl.program_id(1) == 0)
        def _():
            dstate[...] = jnp.zeros_like(dstate)
            dlb_ref[...] = jnp.zeros_like(dlb_ref)
            dg_ref[...] = jnp.zeros_like(dg_ref)

        tri_lo, tri_up = _tri(C, False), _tri(C, True)
        row = lax.broadcasted_iota(jnp.int32, (C, 1), 0)

        def chunk(cc, carry):
            c = ncb - 1 - cc
            rows = pl.ds(pl.multiple_of(c * C, C), C)
            for hh in range(hb):
                cols = slice(hh * HEAD, (hh + 1) * HEAD)
                lbv, gv = lb_ref[:, cols], g_ref[:, cols]
                fz, vv, qz, gz = fz_ref[rows, cols], iv_ref[rows, cols], qz_ref[rows, cols], gz_ref[rows, cols]
                sig, f, lf, kk, qq, sq = _hg_gates(fz, qz, lbv)
                b = jnp.dot(tri_lo, lf, preferred_element_type=F32, precision=lax.Precision.HIGHEST)
                bl = b[C - 1:C, :]
                eb, ebl = jnp.exp(b), jnp.exp(bl)
                ekb = jnp.exp(bl - b)
                qh, kh = qq * eb, kk * ekb
                o = o_ref[rows, cols]
                dhg = dc_ref[rows, cols]
                sgz = jax.nn.sigmoid(gz)
                silu_g = gz * sgz
                rr = lax.rsqrt(jnp.mean(o * o, axis=-1, keepdims=True) + RMS_EPS)
                nrm = o * rr
                dpre = dhg * silu_g
                dgz_ref[rows, cols] = (dhg * nrm * gv * (sgz * (1.0 + gz * (1.0 - sgz)))).astype(BF16)
                dg_ref[:, cols] += jnp.sum(dpre * nrm, axis=0, keepdims=True)
                dn = dpre * gv
                do = rr * (dn - nrm * jnp.mean(dn * nrm, axis=-1, keepdims=True))
                dob = do.astype(BF16)
                st0 = st_ref[hh, c]
                dst1 = dstate[hh]
                dst1b = dst1.astype(BF16)
                dqh = jnp.dot(dob, st0, preferred_element_type=F32)
                dv = lax.dot_general(kh.astype(BF16), dst1b, (((1,), (1,)), ((), ())), preferred_element_type=F32)
                dkh = jnp.dot(vv.astype(BF16), dst1b, preferred_element_type=F32)
                dstate[hh] = dst1 * ebl + lax.dot_general(dob, qh.astype(BF16), (((0,), (0,)), ((), ())),
                                                          preferred_element_type=F32)
                extra = (jnp.sum(dkh * kh, axis=0, keepdims=True)
                         + ebl * jnp.sum(st0.astype(F32) * dst1, axis=0, keepdims=True))
                dq = dqh * eb
                dk = dkh * ekb
                dk_rows, dv_rows = [], []
                for s in range(C):
                    keep = row >= s
                    dec = jnp.exp(jnp.minimum(b - b[s:s + 1, :], 0.0))
                    ks, vs = kk[s:s + 1, :], vv[s:s + 1, :]
                    da = jnp.where(keep, jnp.sum(do * vs, axis=-1, keepdims=True), 0.0)
                    a = jnp.where(keep, jnp.sum(qq * dec * ks, axis=-1, keepdims=True), 0.0)
                    gd = da * dec
                    dq = dq + gd * ks
                    dk_rows.append(jnp.sum(gd * qq, axis=0, keepdims=True))
                    dv_rows.append(jnp.sum(a * do, axis=0, keepdims=True))
                dk = dk + jnp.concatenate(dk_rows, axis=0)
                dv = dv + jnp.concatenate(dv_rows, axis=0)
                db = qq * dq - kk * dk + jnp.where(row == C - 1, extra, 0.0)
                dlf = jnp.dot(tri_up, db, preferred_element_type=F32, precision=lax.Precision.HIGHEST)
                df = dlf / f - dk
                dfz_ref[rows, cols] = (df * (1.0 - lbv) * sig * (1.0 - sig)).astype(BF16)
                dlb_ref[:, cols] += jnp.sum(df * (1.0 - sig), axis=0, keepdims=True)
                dqz_ref[rows, cols] = (dq * (sq * (1.0 + qz * (1.0 - sq)))).astype(BF16)
                div_ref[rows, cols] = dv.astype(BF16)
            return carry

        lax.fori_loop(0, ncb, chunk, 0)

    def col(k):
        return pl.BlockSpec((tb, hb * HEAD), lambda h, i: (nt - 1 - i, base + k * nh + h))

    ocol = pl.BlockSpec((tb, hb * HEAD), lambda h, i: (nt - 1 - i, h))
    vec = pl.BlockSpec((1, hb * HEAD), lambda h, i: (0, h))
    outs = pl.pallas_call(
        body, name=name, grid=(nh, nt),
        in_specs=[col(0), col(1), col(2), col(3),
                  pl.BlockSpec((tb, hb * HEAD), lambda h, i: (nt - 1 - i, dcat_cols - nh + h)),
                  pl.BlockSpec((tb, hb * HEAD), lambda h, i: (nt - 1 - i, h)),
                  pl.BlockSpec((hb, ncb, HEAD, HEAD), lambda h, i: (h, nt - 1 - i, 0, 0)), vec, vec],
        out_specs=[ocol, ocol, ocol, ocol, vec, vec],
        out_shape=[jax.ShapeDtypeStruct((S, W), BF16)] * 4 + [jax.ShapeDtypeStruct((1, W), F32)] * 2,
        scratch_shapes=[pltpu.VMEM((hb, HEAD, HEAD), F32)],
        compiler_params=_cp(("parallel", "arbitrary")))(proj, proj, proj, proj, dcat, o_raw, states,
                                                        lb.reshape(1, W), gain.reshape(1, W))
    return outs


def _lb_all(lb_logits):
    p = jax.nn.softmax(lb_logits.astype(F32), axis=0)
    return jnp.cumsum(p, axis=0) - p


def _local_step(x, mem, target, rel_bias, lb_logits, gains, hg_norm, weights):
    S, D = x.shape
    L = len(weights)
    H = (D // 2) // HEAD
    lb = _lb_all(lb_logits)
    tables = [_branch_tables(r) for _, r in BRANCHES]
    onehots = [jnp.where(ok[None], jax.nn.one_hot(bk, REL_BUCKETS, dtype=F32, axis=0), 0.0)
               .reshape(REL_BUCKETS, 2 * Q_BLOCK * Q_BLOCK) for bk, ok in tables]
    bias = [jnp.where(ok[None], jnp.dot(rel_bias.astype(F32).T, oh, precision=lax.Precision.HIGHEST)
                      .reshape(H, Q_BLOCK, 2 * Q_BLOCK), NEG_INF) for oh, (_, ok) in zip(onehots, tables)]

    saved = []
    _, h = _norm_fwd(x, None, None, gains[0, 0], "norm_first")
    xl = x
    for l in range(L):
        g, w = gains[l], weights[l]
        proj = _mm_nn(h, w["in"], F32, "mm_in")
        outs, lses = [], []
        for gi, (_, r) in enumerate(BRANCHES):
            o, ls = _attn_branch_fwd(proj, bias[gi], r, H, f"attn_fwd_d{r}")
            outs.append(o)
            lses.append(ls)
        cat, o_attn, lse_tot = _attn_merge(outs, lses, D, "attn_merge")
        cat, o_raw, states = _hg_fwd(proj, cat, lb[l], hg_norm[l], H, "hgrn_fwd")
        mix = _mm_nn(cat, w["out"], F32, "mm_out")
        x1, hc = _norm_fwd(xl, mix, g[1], g[2], "norm_post_pre")
        _, mn = _norm_fwd(mem, None, None, g[3], "norm_mem")
        cq = _mm_nn(hc, w["cq"], BF16, "mm_cq")
        ckv = _mm_nn(mn, w["ckv"], BF16, "mm_ckv")
        co = _cross_fwd(cq, ckv, "cross_fwd")
        cop = _mm_nn(co, w["co"], F32, "mm_co")
        x2, hf = _norm_fwd(x1, cop, g[4], g[5], "norm_post_pre")
        gu = _mm_nn(hf, w["gu"], F32, "mm_gu")
        act = _swiglu_fwd(gu, "swiglu_fwd")
        y = _mm_nn(act, w["down"], F32, "mm_down")
        g_next = gains[l + 1, 0] if l + 1 < L else None
        x3, h_next = _norm_fwd(x2, y, g[6], g_next, "norm_post_pre" if l + 1 < L else "norm_post")
        saved.append(dict(x0=xl, h0=h, proj=proj, cat=cat, o_attn=o_attn, lse_tot=lse_tot, o_raw=o_raw, states=states,
                          mix=mix, x1=x1, hc=hc, mn=mn, cq=cq, ckv=ckv, co=co, cop=cop, x2=x2, hf=hf, gu=gu, act=act,
                          y=y, x3=x3))
        xl, h = x3, h_next

    loss, dres = _loss_and_grad(xl, target)

    dh_next = None
    wgrads = [None] * L
    d_gains = [[None] * 7 for _ in range(L)]
    d_lb = [None] * L
    d_hg = [None] * L
    d_bias = [jnp.zeros((H, Q_BLOCK, 2 * Q_BLOCK), F32) for _ in BRANCHES]
    for l in reversed(range(L)):
        g, w, sv = gains[l], weights[l], saved[l]
        g_next = gains[l + 1, 0] if l + 1 < L else None
        t3, dy, dg_next, dg6 = _norm_bwd(dres, dh_next, sv["x3"], sv["y"], g_next, g[6], "norm_bwd")
        if l + 1 < L:
            d_gains[l + 1][0] = dg_next
        d_gains[l][6] = dg6
        gw = {}
        gw["down"] = _mm_tn(sv["act"], dy, 1, BF16, "mm_dw_down")
        dact = _mm_nt(dy, w["down"], F32, "mm_dx_down")
        dgu = _swiglu_bwd(sv["gu"], dact, "swiglu_bwd")
        gw["gu"] = _mm_tn(sv["hf"], dgu, w["gu"].shape[0], BF16, "mm_dw_gu")
        dhf = _mm_nt(dgu, w["gu"], F32, "mm_dx_gu")
        t2, dcop, d_gains[l][5], d_gains[l][4] = _norm_bwd(t3, dhf, sv["x2"], sv["cop"], g[5], g[4], "norm_bwd")
        gw["co"] = _mm_tn(sv["co"], dcop, 1, BF16, "mm_dw_sq")
        dco = _mm_nt(dcop, w["co"], F32, "mm_dx_sq")
        dcq, dckv = _cross_bwd(sv["cq"], sv["ckv"], dco, "cross_bwd")
        gw["cq"] = _mm_tn(sv["hc"], dcq, 1, BF16, "mm_dw_sq")
        dhc = _mm_nt(dcq, w["cq"], F32, "mm_dx_sq")
        gw["ckv"] = _mm_tn(sv["mn"], dckv, w["ckv"].shape[0], BF16, "mm_dw_ckv")
        dmn = _mm_nt(dckv, w["ckv"], F32, "mm_dx_ckv")
        _, _, d_gains[l][3], _ = _norm_bwd(None, dmn, mem, None, g[3], None, "norm_bwd_mem")
        t1, dmix, d_gains[l][2], d_gains[l][1] = _norm_bwd(t2, dhc, sv["x1"], sv["mix"], g[2], g[1], "norm_bwd")
        gw["out"] = _mm_tn(sv["cat"], dmix, 1, BF16, "mm_dw_sq")
        dcat = _mm_nt(dmix, w["out"], F32, "mm_dx_sq")
        delta = _attn_delta(dcat, sv["o_attn"], H, "attn_delta")
        dqkv = []
        for gi, (_, r) in enumerate(BRANCHES):
            dq, dk, dv, db = _attn_branch_bwd(sv["proj"], dcat, sv["lse_tot"], delta, bias[gi], r, H, f"attn_bwd_d{r}")
            dqkv.append((dq, dk, dv))
            d_bias[gi] = d_bias[gi] + db
        dfz, div, dqz, dgz, dlb_l, dhg_l = _hg_bwd(sv["proj"], dcat, sv["o_raw"], sv["states"], lb[l], hg_norm[l], H,
                                                   "hgrn_bwd")
        d_lb[l], d_hg[l] = dlb_l[0], dhg_l[0]
        dproj = _assemble_dproj(dqkv, [dfz, div, dqz, dgz], "assemble_dproj")
        gw["in"] = _mm_tn(sv["h0"], dproj, w["in"].shape[0], BF16, "mm_dw_in")
        dh_next = _mm_nt(dproj, w["in"], F32, "mm_dx_in")
        dres = t1
        wgrads[l] = gw
    grad_x, _, d_gains[0][0], _ = _norm_bwd(dres, dh_next, x, None, gains[0, 0], None, "norm_bwd_first")

    d_rel_bias = _bias_grad(d_bias, onehots, H).T
    d_gains = jnp.stack([jnp.stack(row) for row in d_gains])
    return loss, grad_x, wgrads, d_rel_bias, jnp.stack(d_lb), jnp.stack(d_hg), d_gains


def _place():
    return lax.axis_index("x"), lax.axis_index("y"), lax.axis_index("c")


def _all_gather(shards, name):
    n = len(shards)
    HBM = pl.BlockSpec(memory_space=pltpu.HBM)

    def body(*refs):
        ins, outs = refs[:n], refs[n:2 * n]
        send_sems, recv_sems, local_sems = refs[2 * n:]
        x, y, c = _place()
        me, sibling = (x, y, c), (x, y, 1 - c)
        chips = [(1 - x, y), (x, 1 - y), (1 - x, 1 - y)]

        def copy(a, k, block, to, own=False):
            dst = outs[a].at[4 * block[0] + 2 * block[1] + block[2]]
            return pltpu.make_async_remote_copy(
                src_ref=ins[a] if own else dst, dst_ref=dst, send_sem=send_sems.at[7 * a + k],
                recv_sem=recv_sems.at[7 * a + k], device_id=to, device_id_type=MESH)

        mine = [pltpu.make_async_copy(ins[a], outs[a].at[4 * x + 2 * y + c], local_sems.at[a]) for a in range(n)]
        for cp in mine:
            cp.start()
        first = []
        for a in range(n):
            first.append(copy(a, 0, me, sibling, own=True))
            first += [copy(a, 1 + j, me, (*chip, c), own=True) for j, chip in enumerate(chips)]
        for cp in first:
            cp.start()
        passed = []
        for j, chip in enumerate(chips):
            for a in range(n):
                copy(a, 1 + j, (*chip, c), me).wait_recv()
                fwd = copy(a, 4 + j, (*chip, c), sibling)
                fwd.start()
                passed.append(fwd)
        for a in range(n):
            copy(a, 0, sibling, me).wait_recv()
            for j, chip in enumerate(chips):
                copy(a, 4 + j, (*chip, 1 - c), me).wait_recv()
        for cp in first + passed:
            cp.wait_send()
        for cp in mine:
            cp.wait()

    return pl.pallas_call(
        body, name=name, in_specs=[HBM] * n, out_specs=[HBM] * n,
        out_shape=[jax.ShapeDtypeStruct((N_DEV,) + s.shape, s.dtype) for s in shards],
        scratch_shapes=[pltpu.SemaphoreType.DMA((7 * n,)), pltpu.SemaphoreType.DMA((7 * n,)),
                        pltpu.SemaphoreType.DMA((n,))],
    )(*shards)


def _exchange_sibling(grads, name):
    n = len(grads)
    HBM = pl.BlockSpec(memory_space=pltpu.HBM)

    def body(*refs):
        ins, outs = refs[:n], refs[n:2 * n]
        send_sems, recv_sems = refs[2 * n:]
        x, y, c = _place()
        sibling = (x, y, 1 - c)
        for a in range(n):
            for q in range(4):
                pltpu.make_async_remote_copy(
                    src_ref=ins[a].at[2 * q + 1 - c], dst_ref=outs[a].at[q], send_sem=send_sems.at[a],
                    recv_sem=recv_sems.at[a], device_id=sibling, device_id_type=MESH).start()
        for a in range(n):
            pltpu.make_async_remote_copy(
                src_ref=ins[a].at[pl.ds(0, 4)], dst_ref=outs[a], send_sem=send_sems.at[a], recv_sem=recv_sems.at[a],
                device_id=sibling, device_id_type=MESH).wait()

    return pl.pallas_call(
        body, name=name, in_specs=[HBM] * n, out_specs=[HBM] * n,
        out_shape=[jax.ShapeDtypeStruct((4,) + g.shape[1:], g.dtype) for g in grads],
        scratch_shapes=[pltpu.SemaphoreType.DMA((n,)), pltpu.SemaphoreType.DMA((n,))],
    )(*grads)


def _exchange_chips(parts, name):
    n = len(parts)
    HBM = pl.BlockSpec(memory_space=pltpu.HBM)

    def body(*refs):
        ins, outs = refs[:n], refs[n:2 * n]
        send_sems, recv_sems = refs[2 * n:]
        x, y, c = _place()
        chips = [(1 - x, y), (x, 1 - y), (1 - x, 1 - y)]
        copies = []
        for a in range(n):
            for j, chip in enumerate(chips):
                cp = pltpu.make_async_remote_copy(
                    src_ref=ins[a].at[2 * chip[0] + chip[1]], dst_ref=outs[a].at[j], send_sem=send_sems.at[3 * a + j],
                    recv_sem=recv_sems.at[3 * a + j], device_id=(*chip, c), device_id_type=MESH)
                cp.start()
                copies.append(cp)
        for cp in copies:
            cp.wait()

    return pl.pallas_call(
        body, name=name, in_specs=[HBM] * n, out_specs=[HBM] * n,
        out_shape=[jax.ShapeDtypeStruct((3,) + p.shape[1:], p.dtype) for p in parts],
        scratch_shapes=[pltpu.SemaphoreType.DMA((3 * n,)), pltpu.SemaphoreType.DMA((3 * n,))],
    )(*parts)


def _pair_sum(grad, recv, c_idx, name):
    _, R, C = grad.shape
    tr = _tile(R, 512, 16)

    def body(c_ref, g_ref, p_ref, o_ref):
        del c_ref
        o_ref[...] = (g_ref[...].astype(F32) + p_ref[...].astype(F32)).astype(BF16)

    spec = pltpu.PrefetchScalarGridSpec(
        num_scalar_prefetch=1, grid=(4, R // tr),
        in_specs=[pl.BlockSpec((None, tr, C), lambda q, i, c: (2 * q + c[0], i, 0)),
                  pl.BlockSpec((None, tr, C), lambda q, i, c: (q, i, 0))],
        out_specs=pl.BlockSpec((None, tr, C), lambda q, i, c: (q, i, 0)))
    return pl.pallas_call(body, name=name, grid_spec=spec, out_shape=jax.ShapeDtypeStruct((4, R, C), BF16),
                          compiler_params=_cp(("parallel", "parallel")))(c_idx, grad, recv)


def _adam(w, g, m, v):
    m2 = ADAM_B1 * m + (1.0 - ADAM_B1) * g
    v2 = ADAM_B2 * v + (1.0 - ADAM_B2) * (g * g)
    m_hat = m2 / (1.0 - ADAM_B1 ** ADAM_STEP)
    v_hat = v2 / (1.0 - ADAM_B2 ** ADAM_STEP)
    return -ADAM_LR * (m_hat / (jnp.sqrt(v_hat) + ADAM_EPS) + ADAM_WD * w), m2, v2


def _reduce_update(part, recv, chip_idx, w, m, v, layer, prev, name):
    L, R, C = w.shape
    tr = _tile(R, 128, 16)
    has_prev = prev is not None

    def body(idx_ref, q_ref, r0_ref, r1_ref, r2_ref, w_ref, m_ref, v_ref, *rest):
        del idx_ref
        g_ref, d_ref, m2_ref, v2_ref = rest[-4:]
        g = ((q_ref[...].astype(F32) + r0_ref[...].astype(F32)) + r1_ref[...].astype(F32)) + r2_ref[...].astype(F32)
        d, m2, v2 = _adam(w_ref[...], g, m_ref[...], v_ref[...])
        g_ref[...] = g
        d_ref[...] = d
        m2_ref[...] = m2
        v2_ref[...] = v2

    def rblk(j):
        return pl.BlockSpec((None, tr, C), lambda i, k: (j, i, 0))

    lay = pl.BlockSpec((None, tr, C), lambda i, k: (layer, i, 0))
    in_specs = [pl.BlockSpec((None, tr, C), lambda i, k: (k[0], i, 0)), rblk(0), rblk(1), rblk(2), lay, lay, lay]
    ins = [part, recv, recv, recv, w, m, v]
    aliases = {}
    if has_prev:
        in_specs += [pl.BlockSpec(memory_space=pl.ANY)] * 4
        ins += list(prev)
        aliases = {8 + t: t for t in range(4)}
    spec = pltpu.PrefetchScalarGridSpec(num_scalar_prefetch=1, grid=(R // tr,), in_specs=in_specs,
                                        out_specs=[lay, lay, lay, lay])
    return pl.pallas_call(body, name=name, grid_spec=spec, out_shape=[jax.ShapeDtypeStruct((L, R, C), F32)] * 4,
                          input_output_aliases=aliases, compiler_params=_cp(("parallel",)))(chip_idx, *ins)


def _sum_devices(gathered):
    n, R, C = gathered.shape

    def body(g_ref, o_ref):
        acc = g_ref[0]
        for d in range(1, n):
            acc = acc + g_ref[d]
        o_ref[...] = acc

    return pl.pallas_call(body, name="sum_devices", out_shape=jax.ShapeDtypeStruct((R, C), F32))(gathered)


def _adam_small(w, g, m, v, name):
    shape = w.shape
    two = (-1, shape[-1])

    def body(w_ref, g_ref, m_ref, v_ref, d_ref, m2_ref, v2_ref):
        d, m2, v2 = _adam(w_ref[...], g_ref[...], m_ref[...], v_ref[...])
        d_ref[...] = d
        m2_ref[...] = m2
        v2_ref[...] = v2

    outs = pl.pallas_call(body, name=name, out_shape=[jax.ShapeDtypeStruct(w.reshape(two).shape, F32)] * 3)(
        w.reshape(two), g.reshape(two), m.reshape(two), v.reshape(two))
    return [o.reshape(shape) for o in outs]


_SHARDED = ("in", "out", "cq", "ckv", "co", "gu", "down")
_COLUMN_SHARDED = ("in", "ckv", "gu")


def kernel(x, mem, rel_bias, lb_logits, norm_gains, w_in, hg_norm, w_out, w_cq, w_ckv, w_co, w_gate_up, w_down, loss_target, m_rel_bias, m_lb_logits, m_norm_gains, m_w_in, m_hg_norm, m_w_out, m_w_cq, m_w_ckv, m_w_co, m_w_gate_up, m_w_down, v_rel_bias, v_lb_logits, v_norm_gains, v_w_in, v_hg_norm, v_w_out, v_w_cq, v_w_ckv, v_w_co, v_w_gate_up, v_w_down):
    L = w_in.shape[0]
    D = x.shape[-1]
    ws = dict(zip(_SHARDED, (w_in, w_out, w_cq, w_ckv, w_co, w_gate_up, w_down)))
    ms = dict(zip(_SHARDED, (m_w_in, m_w_out, m_w_cq, m_w_ckv, m_w_co, m_w_gate_up, m_w_down)))
    vs = dict(zip(_SHARDED, (v_w_in, v_w_out, v_w_cq, v_w_ckv, v_w_co, v_w_gate_up, v_w_down)))
    px, py, pc = _place()
    dev = 4 * px + 2 * py + pc
    c_idx = jnp.reshape(pc, (1,)).astype(jnp.int32)
    chip_idx = jnp.reshape(2 * px + py, (1,)).astype(jnp.int32)

    weights = []
    for l in range(L):
        got = _all_gather([ws[k][l].astype(BF16) for k in _SHARDED], "gather_weights")
        wl = {}
        for k, g in zip(_SHARDED, got):
            wl[k] = g if k in _COLUMN_SHARDED else g.reshape(1, N_DEV * g.shape[1], g.shape[2])
        weights.append(wl)
    gains = _all_gather([norm_gains], "gather_gains")[0]
    gains = jnp.transpose(gains, (1, 2, 0, 3)).reshape(L, 7, D)

    loss, grad_x, wgrads, d_rel_bias, d_lb, d_hg, d_gains = _local_step(
        x[0], mem[0], loss_target[0], rel_bias, lb_logits, gains, hg_norm, weights)

    results = {k: None for k in _SHARDED}
    for l in range(L):
        full = [wgrads[l][k].reshape((N_DEV,) + ws[k].shape[1:]) for k in _SHARDED]
        from_sibling = _exchange_sibling(full, "scatter_sibling")
        parts = [_pair_sum(g, r, c_idx, "pair_sum") for g, r in zip(full, from_sibling)]
        from_chips = _exchange_chips(parts, "scatter_chips")
        for k, part, recv in zip(_SHARDED, parts, from_chips):
            results[k] = _reduce_update(part, recv, chip_idx, ws[k], ms[k], vs[k], l, results[k], "reduce_update")

    pieces = [jnp.reshape(loss, (1,)), d_rel_bias.reshape(-1), d_lb.reshape(-1), d_hg.reshape(-1), d_gains.reshape(-1)]
    sizes = [p.shape[0] for p in pieces]
    total = sum(sizes)
    rows = -(-total // 1024) * 8
    pack = jnp.pad(jnp.concatenate(pieces), (0, rows * 128 - total)).reshape(rows, 128)
    summed = _sum_devices(_all_gather([pack], "gather_small")[0]).reshape(-1)
    offs = [sum(sizes[:i]) for i in range(len(sizes))]
    loss = summed[0]
    g_rel_bias = summed[offs[1]:offs[1] + sizes[1]].reshape(rel_bias.shape)
    g_lb_all = summed[offs[2]:offs[2] + sizes[2]].reshape(lb_logits.shape)
    g_hg = summed[offs[3]:offs[3] + sizes[3]].reshape(hg_norm.shape)
    g_gains_full = summed[offs[4]:offs[4] + sizes[4]].reshape(L, 7, D)
    g_gains = lax.dynamic_slice_in_dim(g_gains_full, dev * (D // N_DEV), D // N_DEV, axis=2)
    g_lb = jax.vjp(_lb_all, lb_logits)[1](g_lb_all)[0]

    small = [(rel_bias, g_rel_bias, m_rel_bias, v_rel_bias), (lb_logits, g_lb, m_lb_logits, v_lb_logits),
             (norm_gains, g_gains, m_norm_gains, v_norm_gains), (hg_norm, g_hg, m_hg_norm, v_hg_norm)]
    upd = [_adam_small(w, g, m, v, "adam_small") for w, g, m, v in small]

    def ordered(small_vals, big_vals):
        return [small_vals[0], small_vals[1], small_vals[2], big_vals["in"], small_vals[3], big_vals["out"],
                big_vals["cq"], big_vals["ckv"], big_vals["co"], big_vals["gu"], big_vals["down"]]

    grads = ordered([s[1] for s in small], {k: results[k][0] for k in _SHARDED})
    deltas = ordered([u[0] for u in upd], {k: results[k][1] for k in _SHARDED})
    new_m = ordered([u[1] for u in upd], {k: results[k][2] for k in _SHARDED})
    new_v = ordered([u[2] for u in upd], {k: results[k][3] for k in _SHARDED})
    return (loss, grad_x[None], *grads, *deltas, *new_m, *new_v)
```

```python
import functools
import math

import jax
import jax.numpy as jnp
from jax import lax
from jax.experimental import pallas as pl
from jax.experimental.pallas import tpu as pltpu

F32 = jnp.float32
BF16 = jnp.bfloat16
MESH = pl.DeviceIdType.MESH

N_DEV = 8
HEAD = 128
CROSS_HEADS = 4
Q_BLOCK = 128
BRANCHES = ((128, 1), (512, 4), (2048, 16))
REL_BUCKETS = 32
REL_MAX_DIST = 2048
HG_CHUNK = 16
HG_HEADS_PER_STEP = 8
RMS_EPS = 1e-6
NEG_INF = -1e30
ADAM_LR, ADAM_B1, ADAM_B2, ADAM_EPS, ADAM_WD, ADAM_STEP = 0.001, 0.9, 0.999, 1e-08, 0.01, 10
VMEM_LIMIT = 48 * 1024 * 1024
NT_K_PER_STEP = 3584


def _tile(n, target, mult):
    best = None
    t = mult
    while t <= min(n, target):
        if n % t == 0:
            best = t
        t += mult
    return best if best is not None else n


def _cp(sem, vmem=VMEM_LIMIT):
    return pltpu.CompilerParams(dimension_semantics=sem, vmem_limit_bytes=vmem)


def _mm_nn(a, b3, out_dtype, name, after=None):
    M, K = a.shape
    J, K2, Nb = b3.shape
    assert K == K2
    tm, tn, tk = _tile(M, 1024, 8), _tile(Nb, 1408, 128), _tile(K, 2816, 128)
    nn, nk = Nb // tn, K // tk
    extra = [] if after is None else [after]

    def body(a_ref, b_ref, *rest):
        o_ref, acc = rest[len(extra)], rest[len(extra) + 1:]
        prod = jnp.dot(a_ref[...].astype(BF16), b_ref[...].astype(BF16), preferred_element_type=F32)
        if nk == 1:
            o_ref[...] = prod.astype(o_ref.dtype)
        else:
            k = pl.program_id(2)

            @pl.when(k == 0)
            def _():
                acc[0][...] = prod

            @pl.when(k > 0)
            def _():
                acc[0][...] += prod

            @pl.when(k == nk - 1)
            def _():
                o_ref[...] = acc[0][...].astype(o_ref.dtype)

    return pl.pallas_call(
        body, name=name, grid=(M // tm, J * nn, nk),
        in_specs=[pl.BlockSpec((tm, tk), lambda i, n, k: (i, k)),
                  pl.BlockSpec((None, tk, tn), lambda i, n, k: (n // nn, k, n % nn))]
                 + [pl.BlockSpec(memory_space=pl.ANY)] * len(extra),
        out_specs=pl.BlockSpec((tm, tn), lambda i, n, k: (i, n)),
        out_shape=jax.ShapeDtypeStruct((M, J * Nb), out_dtype),
        scratch_shapes=[] if nk == 1 else [pltpu.VMEM((tm, tn), F32)],
        compiler_params=_cp(("parallel", "parallel", "arbitrary")),
    )(a, b3, *extra)


def _mm_nt(a, b3, out_dtype, name):
    M, Kt = a.shape
    J, N, Kb = b3.shape
    assert Kt == J * Kb
    tm, tn, tk = _tile(M, 1024, 8), _tile(N, 1408, 128), _tile(Kb, 2048, 128)
    nkb = Kb // tk
    jb = max(j for j in range(1, J + 1) if J % j == 0 and j * Kb <= NT_K_PER_STEP) if nkb == 1 else 1
    nk = (J // jb) * nkb

    def body(a_ref, b_ref, o_ref, *acc):
        prod = None
        for j in range(jb):
            part = lax.dot_general(a_ref[:, j * tk:(j + 1) * tk].astype(BF16), b_ref[j].astype(BF16),
                                   (((1,), (1,)), ((), ())), preferred_element_type=F32)
            prod = part if prod is None else prod + part
        if nk == 1:
            o_ref[...] = prod.astype(o_ref.dtype)
        else:
            k = pl.program_id(2)

            @pl.when(k == 0)
            def _():
                acc[0][...] = prod

            @pl.when(k > 0)
            def _():
                acc[0][...] += prod

            @pl.when(k == nk - 1)
            def _():
                o_ref[...] = acc[0][...].astype(o_ref.dtype)

    return pl.pallas_call(
        body, name=name, grid=(M // tm, N // tn, nk),
        in_specs=[pl.BlockSpec((tm, jb * tk), lambda i, n, k: (i, k)),
                  pl.BlockSpec((jb, tn, tk), lambda i, n, k: (k // nkb, n, k % nkb))],
        out_specs=pl.BlockSpec((tm, tn), lambda i, n, k: (i, n)),
        out_shape=jax.ShapeDtypeStruct((M, N), out_dtype),
        scratch_shapes=[] if nk == 1 else [pltpu.VMEM((tm, tn), F32)],
        compiler_params=_cp(("parallel", "parallel", "arbitrary")),
    )(a, b3)


def _mm_tn(a, b, J, out_dtype, name):
    S, M = a.shape
    S2, Nt = b.shape
    assert S == S2 and Nt % J == 0
    Nb = Nt // J
    tm, tn, tk = _tile(M, 1408, 128), _tile(Nb, 1408, 128), _tile(S, 2048, 8)
    nn, nk = Nb // tn, S // tk

    def body(a_ref, b_ref, o_ref, *acc):
        prod = lax.dot_general(a_ref[...].astype(BF16), b_ref[...].astype(BF16), (((0,), (0,)), ((), ())),
                               preferred_element_type=F32)
        if nk == 1:
            o_ref[...] = prod.astype(o_ref.dtype)
        else:
            k = pl.program_id(2)

            @pl.when(k == 0)
            def _():
                acc[0][...] = prod

            @pl.when(k > 0)
            def _():
                acc[0][...] += prod

            @pl.when(k == nk - 1)
            def _():
                o_ref[...] = acc[0][...].astype(o_ref.dtype)

    return pl.pallas_call(
        body, name=name, grid=(M // tm, J * nn, nk),
        in_specs=[pl.BlockSpec((tk, tm), lambda i, n, k: (k, i)),
                  pl.BlockSpec((tk, tn), lambda i, n, k: (k, n))],
        out_specs=pl.BlockSpec((None, tm, tn), lambda i, n, k: (n // nn, i, n % nn)),
        out_shape=jax.ShapeDtypeStruct((J, M, Nb), out_dtype),
        scratch_shapes=[] if nk == 1 else [pltpu.VMEM((tm, tn), F32)],
        compiler_params=_cp(("parallel", "parallel", "arbitrary")),
    )(a, b)


def _rms_scale(v):
    return lax.rsqrt(jnp.mean(v * v, axis=-1, keepdims=True) + RMS_EPS)


def _rms_bwd(dy, v, g):
    r = _rms_scale(v)
    vh = v * r
    u = dy * g
    dv = r * (u - vh * jnp.mean(u * vh, axis=-1, keepdims=True))
    return dv, dy * vh


def _fold8(t):
    R, D = t.shape
    return jnp.sum(t.reshape(R // 8, 8, D), axis=0)


def _norm_fwd(x, z, g_post, g_pre, name):
    S, D = x.shape
    tr = _tile(S, 256, 8)
    has_z, has_pre = z is not None, g_pre is not None

    def body(*refs):
        it = iter(refs)
        x_ref = next(it)
        z_ref = next(it) if has_z else None
        gpost_ref = next(it) if has_z else None
        gpre_ref = next(it) if has_pre else None
        xn_ref = next(it) if has_z else None
        h_ref = next(it) if has_pre else None
        xn = x_ref[...]
        if has_z:
            zz = z_ref[...]
            xn = xn + zz * _rms_scale(zz) * gpost_ref[...]
            xn_ref[...] = xn
        if has_pre:
            h_ref[...] = (xn * _rms_scale(xn) * gpre_ref[...]).astype(BF16)

    row = pl.BlockSpec((tr, D), lambda i: (i, 0))
    gain = pl.BlockSpec((1, D), lambda i: (0, 0))
    ins, specs, outs, ospecs = [x], [row], [], []
    if has_z:
        ins += [z, g_post.reshape(1, D)]
        specs += [row, gain]
        outs.append(jax.ShapeDtypeStruct((S, D), F32))
        ospecs.append(row)
    if has_pre:
        ins.append(g_pre.reshape(1, D))
        specs.append(gain)
        outs.append(jax.ShapeDtypeStruct((S, D), BF16))
        ospecs.append(row)
    res = pl.pallas_call(body, name=name, grid=(S // tr,), in_specs=specs, out_specs=ospecs, out_shape=outs,
                         compiler_params=_cp(("parallel",)))(*ins)
    res = list(res)
    xn = res.pop(0) if has_z else x
    h = res.pop(0) if has_pre else None
    return xn, h


def _norm_bwd(dres, dh, xn, z, g_pre, g_post, name):
    S, D = xn.shape
    tr = _tile(S, 256, 8)
    has_res, has_pre, has_z = dres is not None, dh is not None, z is not None

    def body(*refs):
        it = iter(refs)
        dres_ref = next(it) if has_res else None
        dh_ref = next(it) if has_pre else None
        xn_ref = next(it) if has_pre else None
        gpre_ref = next(it) if has_pre else None
        z_ref = next(it) if has_z else None
        gpost_ref = next(it) if has_z else None
        t_ref = next(it)
        dz_ref = next(it) if has_z else None
        dgpre_ref = next(it) if has_pre else None
        dgpost_ref = next(it) if has_z else None
        first = pl.program_id(0) == 0
        t = dres_ref[...] if has_res else None
        if has_pre:
            dv, dg = _rms_bwd(dh_ref[...].astype(F32), xn_ref[...], gpre_ref[...])
            t = dv if t is None else t + dv
            part = _fold8(dg)

            @pl.when(first)
            def _():
                dgpre_ref[...] = part

            @pl.when(jnp.logical_not(first))
            def _():
                dgpre_ref[...] += part
        t_ref[...] = t
        if has_z:
            dv, dg = _rms_bwd(t, z_ref[...], gpost_ref[...])
            dz_ref[...] = dv.astype(BF16)
            part2 = _fold8(dg)

            @pl.when(first)
            def _():
                dgpost_ref[...] = part2

            @pl.when(jnp.logical_not(first))
            def _():
                dgpost_ref[...] += part2

    row = pl.BlockSpec((tr, D), lambda i: (i, 0))
    gain = pl.BlockSpec((1, D), lambda i: (0, 0))
    accs = pl.BlockSpec((8, D), lambda i: (0, 0))
    ins, specs = [], []
    if has_res:
        ins.append(dres)
        specs.append(row)
    if has_pre:
        ins += [dh, xn, g_pre.reshape(1, D)]
        specs += [row, row, gain]
    if has_z:
        ins += [z, g_post.reshape(1, D)]
        specs += [row, gain]
    outs, ospecs = [jax.ShapeDtypeStruct((S, D), F32)], [row]
    if has_z:
        outs.append(jax.ShapeDtypeStruct((S, D), BF16))
        ospecs.append(row)
    if has_pre:
        outs.append(jax.ShapeDtypeStruct((8, D), F32))
        ospecs.append(accs)
    if has_z:
        outs.append(jax.ShapeDtypeStruct((8, D), F32))
        ospecs.append(accs)
    res = list(pl.pallas_call(body, name=name, grid=(S // tr,), in_specs=specs, out_specs=ospecs, out_shape=outs,
                              compiler_params=_cp(("arbitrary",)))(*ins))
    t = res.pop(0)
    dz = res.pop(0) if has_z else None
    dgpre = jnp.sum(res.pop(0), axis=0) if has_pre else None
    dgpost = jnp.sum(res.pop(0), axis=0) if has_z else None
    return t, dz, dgpre, dgpost


def _loss_and_grad(y, target):
    S, D = y.shape
    tr = _tile(S, 256, 8)

    def body(y_ref, t_ref, dy_ref, l_ref):
        err = y_ref[...] - t_ref[...]
        dy_ref[...] = err * (1.0 / D)
        part = 0.5 * jnp.sum(jnp.mean(err * err, axis=-1, keepdims=True), axis=0, keepdims=True)
        first = pl.program_id(0) == 0

        @pl.when(first)
        def _():
            l_ref[...] = jnp.broadcast_to(part, l_ref.shape)

        @pl.when(jnp.logical_not(first))
        def _():
            l_ref[...] += jnp.broadcast_to(part, l_ref.shape)

    row = pl.BlockSpec((tr, D), lambda i: (i, 0))
    dy, l = pl.pallas_call(
        body, name="loss_grad", grid=(S // tr,), in_specs=[row, row],
        out_specs=[row, pl.BlockSpec((8, 128), lambda i: (0, 0))],
        out_shape=[jax.ShapeDtypeStruct((S, D), F32), jax.ShapeDtypeStruct((8, 128), F32)],
        compiler_params=_cp(("arbitrary",)))(y, target)
    return l[0, 0], dy


def _swiglu_fwd(gu, name):
    S, F2 = gu.shape
    F = F2 // 2
    tr, tc = _tile(S, 512, 8), _tile(F, 1536, 128)
    nf = F // tc

    def body(g_ref, u_ref, o_ref):
        g = g_ref[...]
        o_ref[...] = (g * jax.nn.sigmoid(g) * u_ref[...]).astype(BF16)

    return pl.pallas_call(
        body, name=name, grid=(S // tr, nf),
        in_specs=[pl.BlockSpec((tr, tc), lambda i, j: (i, j)), pl.BlockSpec((tr, tc), lambda i, j: (i, j + nf))],
        out_specs=pl.BlockSpec((tr, tc), lambda i, j: (i, j)),
        out_shape=jax.ShapeDtypeStruct((S, F), BF16), compiler_params=_cp(("parallel", "parallel")))(gu, gu)


def _swiglu_bwd(gu, dact, name):
    S, F2 = gu.shape
    F = F2 // 2
    tr, tc = _tile(S, 512, 8), _tile(F, 1536, 128)
    nf = F // tc

    def body(g_ref, u_ref, d_ref, o_ref):
        g = g_ref[...]
        d = d_ref[...].astype(F32)
        sg = jax.nn.sigmoid(g)

        @pl.when(pl.program_id(1) < nf)
        def _():
            o_ref[...] = (d * u_ref[...] * (sg * (1.0 + g * (1.0 - sg)))).astype(BF16)

        @pl.when(pl.program_id(1) >= nf)
        def _():
            o_ref[...] = (d * g * sg).astype(BF16)

    return pl.pallas_call(
        body, name=name, grid=(S // tr, 2 * nf),
        in_specs=[pl.BlockSpec((tr, tc), lambda i, j: (i, j % nf)), pl.BlockSpec((tr, tc), lambda i, j: (i, j % nf + nf)),
                  pl.BlockSpec((tr, tc), lambda i, j: (i, j % nf))],
        out_specs=pl.BlockSpec((tr, tc), lambda i, j: (i, j)),
        out_shape=jax.ShapeDtypeStruct((S, F2), BF16), compiler_params=_cp(("parallel", "parallel")))(gu, gu, dact)


def _cross_scores(q, k, scale):
    s = lax.dot_general(q, k, (((1,), (1,)), ((), ())), preferred_element_type=F32) * scale
    m = jnp.max(s, axis=-1, keepdims=True)
    e = jnp.exp(s - m)
    return e / jnp.sum(e, axis=-1, keepdims=True)


def _cross_fwd(cq, ckv, name):
    S, D = cq.shape
    Nm = ckv.shape[0]
    dh = D // CROSS_HEADS
    tq = _tile(S, 512, 8)
    scale = dh ** -0.5

    def body(q_ref, kv_ref, o_ref):
        for h in range(CROSS_HEADS):
            cols = slice(h * dh, (h + 1) * dh)
            p = _cross_scores(q_ref[:, cols], kv_ref[:, cols], scale)
            o_ref[:, cols] = jnp.dot(p.astype(BF16), kv_ref[:, D + h * dh:D + (h + 1) * dh],
                                     preferred_element_type=F32).astype(BF16)

    return pl.pallas_call(
        body, name=name, grid=(S // tq,),
        in_specs=[pl.BlockSpec((tq, D), lambda i: (i, 0)), pl.BlockSpec((Nm, 2 * D), lambda i: (0, 0))],
        out_specs=pl.BlockSpec((tq, D), lambda i: (i, 0)),
        out_shape=jax.ShapeDtypeStruct((S, D), BF16), compiler_params=_cp(("parallel",)))(cq, ckv)


def _cross_bwd(cq, ckv, do, name):
    S, D = cq.shape
    Nm = ckv.shape[0]
    dh = D // CROSS_HEADS
    tq = _tile(S, 512, 8)
    scale = dh ** -0.5

    def body(q_ref, kv_ref, do_ref, dq_ref, dkv_ref):
        first = pl.program_id(0) == 0

        @pl.when(first)
        def _():
            dkv_ref[...] = jnp.zeros_like(dkv_ref)

        for h in range(CROSS_HEADS):
            cols = slice(h * dh, (h + 1) * dh)
            vcols = slice(D + h * dh, D + (h + 1) * dh)
            q, k, v = q_ref[:, cols], kv_ref[:, cols], kv_ref[:, vcols]
            doh = do_ref[:, cols].astype(BF16)
            p = _cross_scores(q, k, scale)
            dp = lax.dot_general(doh, v, (((1,), (1,)), ((), ())), preferred_element_type=F32)
            ds = (p * (dp - jnp.sum(p * dp, axis=-1, keepdims=True)) * scale).astype(BF16)
            dq_ref[:, cols] = jnp.dot(ds, k, preferred_element_type=F32).astype(BF16)
            dkv_ref[:, cols] += lax.dot_general(ds, q, (((0,), (0,)), ((), ())), preferred_element_type=F32)
            dkv_ref[:, vcols] += lax.dot_general(p.astype(BF16), doh, (((0,), (0,)), ((), ())),
                                                 preferred_element_type=F32)

    return pl.pallas_call(
        body, name=name, grid=(S // tq,),
        in_specs=[pl.BlockSpec((tq, D), lambda i: (i, 0)), pl.BlockSpec((Nm, 2 * D), lambda i: (0, 0)),
                  pl.BlockSpec((tq, D), lambda i: (i, 0))],
        out_specs=[pl.BlockSpec((tq, D), lambda i: (i, 0)), pl.BlockSpec((Nm, 2 * D), lambda i: (0, 0))],
        out_shape=[jax.ShapeDtypeStruct((S, D), BF16), jax.ShapeDtypeStruct((Nm, 2 * D), F32)],
        compiler_params=_cp(("arbitrary",)))(cq, ckv, do)


def _rel_bucket(dist):
    max_exact = REL_BUCKETS // 2
    d_f = jnp.maximum(dist, 1).astype(F32)
    large = max_exact + (jnp.log(d_f / max_exact) / math.log(REL_MAX_DIST / max_exact)
                         * (REL_BUCKETS - max_exact)).astype(jnp.int32)
    large = jnp.minimum(large, REL_BUCKETS - 1)
    return jnp.where(dist < max_exact, dist, large)


def _branch_tables(dilation):
    qi = jnp.arange(Q_BLOCK)[:, None]
    kj = jnp.arange(2 * Q_BLOCK)[None, :]
    m = qi - kj + Q_BLOCK
    return _rel_bucket(jnp.maximum(m, 0) * dilation), (m >= 0) & (m <= Q_BLOCK)


def _attn_logits(q, kcat, bias, first_block, scale):
    s = lax.dot_general(q, kcat, (((1,), (1,)), ((), ())), preferred_element_type=F32) * scale + bias
    col = lax.broadcasted_iota(jnp.int32, s.shape, 1)
    return jnp.where(jnp.logical_and(first_block, col < Q_BLOCK), NEG_INF, s)


def _attn_branch_fwd(proj, bias, dilation, H, name):
    S, NC = proj.shape
    AW = H * HEAD
    r = dilation
    hb = H if r == 1 else 1
    G = Q_BLOCK * r
    ng, nh = S // G, H // hb
    scale = HEAD ** -0.5

    def body(q_ref, kp_ref, kc_ref, vp_ref, vc_ref, b_ref, o_ref, l_ref):
        first_block = pl.program_id(0) == 0
        h0 = pl.program_id(1) * hb
        for rho in range(r):
            rows = pl.ds(rho, Q_BLOCK, stride=r) if r > 1 else slice(None)
            for hh in range(hb):
                cols = slice(hh * HEAD, (hh + 1) * HEAD)
                q = q_ref[rows, cols].astype(BF16)
                kcat = jnp.concatenate([kp_ref[rows, cols], kc_ref[rows, cols]], axis=0).astype(BF16)
                vcat = jnp.concatenate([vp_ref[rows, cols], vc_ref[rows, cols]], axis=0).astype(BF16)
                s = _attn_logits(q, kcat, b_ref[h0 + hh], first_block, scale)
                m = jnp.max(s, axis=-1, keepdims=True)
                e = jnp.exp(s - m)
                den = jnp.sum(e, axis=-1, keepdims=True)
                o_ref[rows, cols] = jnp.dot(e.astype(BF16), vcat, preferred_element_type=F32) / den
                l_ref[rows, cols] = jnp.broadcast_to(m + jnp.log(den), (Q_BLOCK, HEAD))

    def blk(part, prev):
        if prev:
            return pl.BlockSpec((G, hb * HEAD), lambda n, h: (jnp.maximum(n - 1, 0), part * nh + h))
        return pl.BlockSpec((G, hb * HEAD), lambda n, h: (n, part * nh + h))

    out = pl.BlockSpec((G, hb * HEAD), lambda n, h: (n, h))
    return pl.pallas_call(
        body, name=name, grid=(ng, nh),
        in_specs=[blk(0, False), blk(1, True), blk(1, False), blk(2, True), blk(2, False),
                  pl.BlockSpec((H, Q_BLOCK, 2 * Q_BLOCK), lambda n, h: (0, 0, 0))],
        out_specs=[out, out],
        out_shape=[jax.ShapeDtypeStruct((S, AW), F32)] * 2,
        compiler_params=_cp(("parallel", "parallel")))(proj, proj, proj, proj, proj, bias)


def _attn_merge(outs, lses, width_out, name):
    S, AW = outs[0].shape
    tr = _tile(S, 256, 8)

    def body(o0, o1, o2, l0, l1, l2, cat_ref, om_ref, lt_ref):
        a, b, c = l0[...], l1[...], l2[...]
        m = jnp.maximum(jnp.maximum(a, b), c)
        ea, eb, ec = jnp.exp(a - m), jnp.exp(b - m), jnp.exp(c - m)
        tot = ea + eb + ec
        o = (ea * o0[...] + eb * o1[...] + ec * o2[...]) / tot
        om_ref[...] = o
        cat_ref[...] = o.astype(BF16)
        lt_ref[...] = m + jnp.log(tot)

    row = pl.BlockSpec((tr, AW), lambda i: (i, 0))
    return pl.pallas_call(
        body, name=name, grid=(S // tr,), in_specs=[row] * 6, out_specs=[row, row, row],
        out_shape=[jax.ShapeDtypeStruct((S, width_out), BF16), jax.ShapeDtypeStruct((S, AW), F32),
                   jax.ShapeDtypeStruct((S, AW), F32)],
        compiler_params=_cp(("parallel",)))(*outs, *lses)


def _attn_delta(dcat, o_attn, H, name):
    S, AW = o_attn.shape
    tr = _tile(S, 256, 8)

    def body(d_ref, o_ref, out_ref):
        for h in range(H):
            cols = slice(h * HEAD, (h + 1) * HEAD)
            out_ref[:, cols] = jnp.broadcast_to(
                jnp.sum(d_ref[:, cols] * o_ref[:, cols], axis=-1, keepdims=True), (tr, HEAD))

    row = pl.BlockSpec((tr, AW), lambda i: (i, 0))
    return pl.pallas_call(body, name=name, grid=(S // tr,), in_specs=[row, row], out_specs=row,
                          out_shape=jax.ShapeDtypeStruct((S, AW), F32),
                          compiler_params=_cp(("parallel",)))(dcat, o_attn)


def _attn_branch_bwd(proj, dcat, lse_tot, delta, bias, dilation, H, name):
    S, NC = proj.shape
    AW = H * HEAD
    r = dilation
    hb = H if r == 1 else 1
    G = Q_BLOCK * r
    ng, nh = S // G, H // hb
    scale = HEAD ** -0.5

    def body(q_ref, kp_ref, kc_ref, vp_ref, vc_ref, do_ref, l_ref, t_ref, b_ref,
             dq_ref, dk_ref, dv_ref, db_ref, ck_ref, cv_ref):
        hblk, n = pl.program_id(0), pl.program_id(1)
        h0 = hblk * hb

        @pl.when(jnp.logical_and(hblk == 0, n == 0))
        def _():
            db_ref[...] = jnp.zeros_like(db_ref)

        @pl.when(n == 0)
        def _():
            ck_ref[...] = jnp.zeros_like(ck_ref)
            cv_ref[...] = jnp.zeros_like(cv_ref)

        @pl.when(n < ng)
        def _():
            for rho in range(r):
                rows = pl.ds(rho, Q_BLOCK, stride=r) if r > 1 else slice(None)
                for hh in range(hb):
                    cols = slice(hh * HEAD, (hh + 1) * HEAD)
                    q = q_ref[rows, cols].astype(BF16)
                    kcat = jnp.concatenate([kp_ref[rows, cols], kc_ref[rows, cols]], axis=0).astype(BF16)
                    vcat = jnp.concatenate([vp_ref[rows, cols], vc_ref[rows, cols]], axis=0).astype(BF16)
                    doh = do_ref[rows, cols].astype(BF16)
                    s = _attn_logits(q, kcat, b_ref[h0 + hh], n == 0, scale)
                    lt = l_ref[rows, cols]
                    dl = t_ref[rows, cols]
                    p = jnp.exp(s - jnp.concatenate([lt, lt], axis=1))
                    dp = lax.dot_general(doh, vcat, (((1,), (1,)), ((), ())), preferred_element_type=F32)
                    ds = p * (dp - jnp.concatenate([dl, dl], axis=1))
                    db_ref[h0 + hh] += ds
                    dsb = (ds * scale).astype(BF16)
                    dq_ref[rows, cols] = jnp.dot(dsb, kcat, preferred_element_type=F32)
                    dkc = lax.dot_general(dsb, q, (((0,), (0,)), ((), ())), preferred_element_type=F32)
                    dvc = lax.dot_general(p.astype(BF16), doh, (((0,), (0,)), ((), ())), preferred_element_type=F32)
                    dk_ref[rows, cols] = ck_ref[rows, cols] + dkc[:Q_BLOCK]
                    dv_ref[rows, cols] = cv_ref[rows, cols] + dvc[:Q_BLOCK]
                    ck_ref[rows, cols] = dkc[Q_BLOCK:]
                    cv_ref[rows, cols] = dvc[Q_BLOCK:]

        @pl.when(n == ng)
        def _():
            dk_ref[...] = ck_ref[...]
            dv_ref[...] = cv_ref[...]

    last = ng - 1

    def cur(part):
        return pl.BlockSpec((G, hb * HEAD), lambda h, n: (jnp.minimum(n, last), part * nh + h))

    def prev(part):
        return pl.BlockSpec((G, hb * HEAD), lambda h, n: (jnp.clip(n - 1, 0, last), part * nh + h))

    table = pl.BlockSpec((H, Q_BLOCK, 2 * Q_BLOCK), lambda h, n: (0, 0, 0))
    return pl.pallas_call(
        body, name=name, grid=(nh, ng + 1),
        in_specs=[cur(0), prev(1), cur(1), prev(2), cur(2), cur(0), cur(0), cur(0), table],
        out_specs=[cur(0), prev(0), prev(0), table],
        out_shape=[jax.ShapeDtypeStruct((S, AW), F32)] * 3 + [jax.ShapeDtypeStruct((H, Q_BLOCK, 2 * Q_BLOCK), F32)],
        scratch_shapes=[pltpu.VMEM((G, hb * HEAD), F32), pltpu.VMEM((G, hb * HEAD), F32)],
        compiler_params=_cp(("arbitrary", "arbitrary")))(proj, proj, proj, proj, proj, dcat, lse_tot, delta, bias)


def _bias_grad(dbs, onehots, H):
    def body(d0, d1, d2, e0, e1, e2, o_ref):
        acc = jnp.zeros((H, REL_BUCKETS), F32)
        for d, e in ((d0, e0), (d1, e1), (d2, e2)):
            acc = acc + lax.dot_general(d[...], e[...], (((1,), (1,)), ((), ())), preferred_element_type=F32,
                                        precision=lax.Precision.HIGHEST)
        o_ref[...] = acc

    flat = [d.reshape(H, 2 * Q_BLOCK * Q_BLOCK) for d in dbs]
    return pl.pallas_call(body, name="bias_grad", out_shape=jax.ShapeDtypeStruct((H, REL_BUCKETS), F32),
                          compiler_params=pltpu.CompilerParams(vmem_limit_bytes=VMEM_LIMIT))(*flat, *onehots)


def _assemble_dproj(dqkv, dhg, name):
    S, AW = dqkv[0][0].shape
    tr = _tile(S, 256, 8)
    ins = [dqkv[g][c] for g in range(3) for c in range(3)] + list(dhg)

    def body(*refs):
        o_ref = refs[-1]
        j = pl.program_id(1)
        for c in range(3):
            @pl.when(j == c)
            def _(c=c):
                o_ref[...] = (refs[c][...] + refs[3 + c][...] + refs[6 + c][...]).astype(BF16)

        for k in range(4):
            @pl.when(j == 3 + k)
            def _(k=k):
                o_ref[...] = refs[9 + k][...]

    row = pl.BlockSpec((tr, AW), lambda i, j: (i, 0))
    return pl.pallas_call(
        body, name=name, grid=(S // tr, 7), in_specs=[row] * 13,
        out_specs=pl.BlockSpec((tr, AW), lambda i, j: (i, j)),
        out_shape=jax.ShapeDtypeStruct((S, 7 * AW), BF16),
        compiler_params=_cp(("parallel", "arbitrary")))(*ins)


def _tri(n, upper):
    r = lax.broadcasted_iota(jnp.int32, (n, n), 0)
    c = lax.broadcasted_iota(jnp.int32, (n, n), 1)
    return jnp.where(c >= r if upper else c <= r, 1.0, 0.0).astype(F32)


def _hg_gates(fz, qz, lb):
    sig = jax.nn.sigmoid(fz)
    f = lb + (1.0 - lb) * sig
    sq = jax.nn.sigmoid(qz)
    return sig, f, jnp.log(f), 1.0 - f, qz * sq, sq


def _hg_fwd(proj, cat, lb, gain, HH, name):
    S, NC = proj.shape
    W = HH * HEAD
    C = HG_CHUNK
    hb = min(HG_HEADS_PER_STEP, HH)
    nh = HH // hb
    tb = _tile(S, 2048 // hb, C)
    ncb = tb // C
    base = 3 * nh
    cat_cols = cat.shape[1] // (hb * HEAD)

    def body(fz_ref, iv_ref, qz_ref, gz_ref, lb_ref, g_ref, cat_in, cat_ref, o_ref, st_ref, state):
        del cat_in

        @pl.when(pl.program_id(1) == 0)
        def _():
            state[...] = jnp.zeros_like(state)

        tri = _tri(C, False)
        row8 = lax.broadcasted_iota(jnp.int32, (8, 1), 0)

        def chunk(c, carry):
            rows = pl.ds(pl.multiple_of(c * C, C), C)
            for hh in range(hb):
                cols = slice(hh * HEAD, (hh + 1) * HEAD)
                lbv, gv = lb_ref[:, cols], g_ref[:, cols]
                fz, vv, qz, gz = fz_ref[rows, cols], iv_ref[rows, cols], qz_ref[rows, cols], gz_ref[rows, cols]
                _, f, lf, kk, qq, _ = _hg_gates(fz, qz, lbv)
                b = jnp.dot(tri, lf, preferred_element_type=F32, precision=lax.Precision.HIGHEST)
                bl = b[C - 1:C, :]
                st = state[hh]
                stb = st.astype(BF16)
                st_ref[hh, c] = stb
                o = lax.dot_general((qq * jnp.exp(b)).astype(BF16), stb, (((1,), (1,)), ((), ())),
                                    preferred_element_type=F32)
                tiles = []
                for t0 in range(0, C, 8):
                    bt, qt, acc = b[t0:t0 + 8, :], qq[t0:t0 + 8, :], o[t0:t0 + 8, :]
                    for s in range(min(C, t0 + 8)):
                        diff = bt - b[s:s + 1, :]
                        if s >= t0:
                            diff = jnp.minimum(diff, 0.0)
                        a = jnp.sum(qt * jnp.exp(diff) * kk[s:s + 1, :], axis=-1, keepdims=True)
                        if s >= t0:
                            a = jnp.where(row8 >= s - t0, a, 0.0)
                        acc = acc + a * vv[s:s + 1, :]
                    tiles.append(acc)
                o = jnp.concatenate(tiles, axis=0)
                kh = (kk * jnp.exp(bl - b)).astype(BF16)
                upd = lax.dot_general(vv.astype(BF16), kh, (((0,), (0,)), ((), ())), preferred_element_type=F32)
                state[hh] = st * jnp.exp(bl) + upd
                o_ref[rows, cols] = o
                n = o * lax.rsqrt(jnp.mean(o * o, axis=-1, keepdims=True) + RMS_EPS)
                cat_ref[rows, cols] = (n * gv * (gz * jax.nn.sigmoid(gz))).astype(BF16)
            return carry

        lax.fori_loop(0, ncb, chunk, 0)

    def col(k):
        return pl.BlockSpec((tb, hb * HEAD), lambda h, i: (i, base + k * nh + h))

    vec = pl.BlockSpec((1, hb * HEAD), lambda h, i: (0, h))
    cat_out, o_raw, states = pl.pallas_call(
        body, name=name, grid=(nh, S // tb),
        in_specs=[col(0), col(1), col(2), col(3), vec, vec, pl.BlockSpec(memory_space=pl.ANY)],
        out_specs=[pl.BlockSpec((tb, hb * HEAD), lambda h, i: (i, cat_cols - nh + h)),
                   pl.BlockSpec((tb, hb * HEAD), lambda h, i: (i, h)),
                   pl.BlockSpec((hb, ncb, HEAD, HEAD), lambda h, i: (h, i, 0, 0))],
        out_shape=[jax.ShapeDtypeStruct(cat.shape, BF16), jax.ShapeDtypeStruct((S, W), F32),
                   jax.ShapeDtypeStruct((HH, S // C, HEAD, HEAD), BF16)],
        scratch_shapes=[pltpu.VMEM((hb, HEAD, HEAD), F32)],
        input_output_aliases={6: 0},
        compiler_params=_cp(("parallel", "arbitrary")))(proj, proj, proj, proj, lb.reshape(1, W), gain.reshape(1, W), cat)
    return cat_out, o_raw, states


def _hg_bwd(proj, dcat, o_raw, states, lb, gain, HH, name):
    S, NC = proj.shape
    W = HH * HEAD
    C = HG_CHUNK
    hb = min(HG_HEADS_PER_STEP, HH)
    nh = HH // hb
    tb = _tile(S, 2048 // hb, C)
    ncb = tb // C
    nt = S // tb
    base = 3 * nh
    dcat_cols = dcat.shape[1] // (hb * HEAD)

    def body(fz_ref, iv_ref, qz_ref, gz_ref, dc_ref, o_ref, st_ref, lb_ref, g_ref,
             dfz_ref, div_ref, dqz_ref, dgz_ref, dlb_ref, dg_ref, dstate):
        @pl.when(pl.program_id(1) == 0)
        def _():
            dstate[...] = jnp.zeros_like(dstate)
            dlb_ref[...] = jnp.zeros_like(dlb_ref)
            dg_ref[...] = jnp.zeros_like(dg_ref)

        tri_lo, tri_up = _tri(C, False), _tri(C, True)
        row = lax.broadcasted_iota(jnp.int32, (C, 1), 0)
        row8 = lax.broadcasted_iota(jnp.int32, (8, 1), 0)

        def chunk(cc, carry):
            c = ncb - 1 - cc
            rows = pl.ds(pl.multiple_of(c * C, C), C)
            for hh in range(hb):
                cols = slice(hh * HEAD, (hh + 1) * HEAD)
                lbv, gv = lb_ref[:, cols], g_ref[:, cols]
                fz, vv, qz, gz = fz_ref[rows, cols], iv_ref[rows, cols], qz_ref[rows, cols], gz_ref[rows, cols]
                sig, f, lf, kk, qq, sq = _hg_gates(fz, qz, lbv)
                b = jnp.dot(tri_lo, lf, preferred_element_type=F32, precision=lax.Precision.HIGHEST)
                bl = b[C - 1:C, :]
                eb, ebl = jnp.exp(b), jnp.exp(bl)
                ekb = jnp.exp(bl - b)
                qh, kh = qq * eb, kk * ekb
                o = o_ref[rows, cols]
                dhg = dc_ref[rows, cols]
                sgz = jax.nn.sigmoid(gz)
                silu_g = gz * sgz
                rr = lax.rsqrt(jnp.mean(o * o, axis=-1, keepdims=True) + RMS_EPS)
                nrm = o * rr
                dpre = dhg * silu_g
                dgz_ref[rows, cols] = (dhg * nrm * gv * (sgz * (1.0 + gz * (1.0 - sgz)))).astype(BF16)
                dg_ref[:, cols] += jnp.sum(dpre * nrm, axis=0, keepdims=True)
                dn = dpre * gv
                do = rr * (dn - nrm * jnp.mean(dn * nrm, axis=-1, keepdims=True))
                dob = do.astype(BF16)
                st0 = st_ref[hh, c]
                dst1 = dstate[hh]
                dst1b = dst1.astype(BF16)
                dqh = jnp.dot(dob, st0, preferred_element_type=F32)
                dv = lax.dot_general(kh.astype(BF16), dst1b, (((1,), (1,)), ((), ())), preferred_element_type=F32)
                dkh = jnp.dot(vv.astype(BF16), dst1b, preferred_element_type=F32)
                dstate[hh] = dst1 * ebl + lax.dot_general(dob, qh.astype(BF16), (((0,), (0,)), ((), ())),
                                                          preferred_element_type=F32)
                extra = (jnp.sum(dkh * kh, axis=0, keepdims=True)
                         + ebl * jnp.sum(st0.astype(F32) * dst1, axis=0, keepdims=True))
                dq = dqh * eb
                dk = dkh * ekb
                starts = list(range(0, C, 8))
                b_t, q_t, do_t = ([v[t0:t0 + 8, :] for t0 in starts] for v in (b, qq, do))
                dq_t = [dq[t0:t0 + 8, :] for t0 in starts]
                dk_rows, dv_rows = [], []
                for s in range(C):
                    ks, vs, bs = kk[s:s + 1, :], vv[s:s + 1, :], b[s:s + 1, :]
                    dk_s = dv_s = None
                    for ti, t0 in enumerate(starts):
                        if t0 + 8 <= s:
                            continue
                        diff = b_t[ti] - bs
                        if s >= t0:
                            diff = jnp.minimum(diff, 0.0)
                        dec = jnp.exp(diff)
                        da = jnp.sum(do_t[ti] * vs, axis=-1, keepdims=True)
                        a = jnp.sum(q_t[ti] * dec * ks, axis=-1, keepdims=True)
                        if s >= t0:
                            keep = row8 >= s - t0
                            da, a = jnp.where(keep, da, 0.0), jnp.where(keep, a, 0.0)
                        gd = da * dec
                        dq_t[ti] = dq_t[ti] + gd * ks
                        pk = jnp.sum(gd * q_t[ti], axis=0, keepdims=True)
                        pv = jnp.sum(a * do_t[ti], axis=0, keepdims=True)
                        dk_s = pk if dk_s is None else dk_s + pk
                        dv_s = pv if dv_s is None else dv_s + pv
                    dk_rows.append(dk_s)
                    dv_rows.append(dv_s)
                dq = jnp.concatenate(dq_t, axis=0)
                dk = dk + jnp.concatenate(dk_rows, axis=0)
                dv = dv + jnp.concatenate(dv_rows, axis=0)
                db = qq * dq - kk * dk + jnp.where(row == C - 1, extra, 0.0)
                dlf = jnp.dot(tri_up, db, preferred_element_type=F32, precision=lax.Precision.HIGHEST)
                df = dlf / f - dk
                dfz_ref[rows, cols] = (df * (1.0 - lbv) * sig * (1.0 - sig)).astype(BF16)
                dlb_ref[:, cols] += jnp.sum(df * (1.0 - sig), axis=0, keepdims=True)
                dqz_ref[rows, cols] = (dq * (sq * (1.0 + qz * (1.0 - sq)))).astype(BF16)
                div_ref[rows, cols] = dv.astype(BF16)
            return carry

        lax.fori_loop(0, ncb, chunk, 0)

    def col(k):
        return pl.BlockSpec((tb, hb * HEAD), lambda h, i: (nt - 1 - i, base + k * nh + h))

    ocol = pl.BlockSpec((tb, hb * HEAD), lambda h, i: (nt - 1 - i, h))
    vec = pl.BlockSpec((1, hb * HEAD), lambda h, i: (0, h))
    outs = pl.pallas_call(
        body, name=name, grid=(nh, nt),
        in_specs=[col(0), col(1), col(2), col(3),
                  pl.BlockSpec((tb, hb * HEAD), lambda h, i: (nt - 1 - i, dcat_cols - nh + h)),
                  pl.BlockSpec((tb, hb * HEAD), lambda h, i: (nt - 1 - i, h)),
                  pl.BlockSpec((hb, ncb, HEAD, HEAD), lambda h, i: (h, nt - 1 - i, 0, 0)), vec, vec],
        out_specs=[ocol, ocol, ocol, ocol, vec, vec],
        out_shape=[jax.ShapeDtypeStruct((S, W), BF16)] * 4 + [jax.ShapeDtypeStruct((1, W), F32)] * 2,
        scratch_shapes=[pltpu.VMEM((hb, HEAD, HEAD), F32)],
        compiler_params=_cp(("parallel", "arbitrary")))(proj, proj, proj, proj, dcat, o_raw, states,
                                                        lb.reshape(1, W), gain.reshape(1, W))
    return outs


def _lb_all(lb_logits):
    p = jax.nn.softmax(lb_logits.astype(F32), axis=0)
    return jnp.cumsum(p, axis=0) - p


def _local_step(x, mem, target, rel_bias, lb_logits, gains, hg_norm, comm):
    S, D = x.shape
    L = gains.shape[0]
    H = (D // 2) // HEAD
    lb = _lb_all(lb_logits)
    tables = [_branch_tables(r) for _, r in BRANCHES]
    onehots = [jnp.where(ok[None], jax.nn.one_hot(bk, REL_BUCKETS, dtype=F32, axis=0), 0.0)
               .reshape(REL_BUCKETS, 2 * Q_BLOCK * Q_BLOCK) for bk, ok in tables]
    bias = [jnp.where(ok[None], jnp.dot(rel_bias.astype(F32).T, oh, precision=lax.Precision.HIGHEST)
                      .reshape(H, Q_BLOCK, 2 * Q_BLOCK), NEG_INF) for oh, (_, ok) in zip(onehots, tables)]

    saved = []
    _, h = _norm_fwd(x, None, None, gains[0, 0], "norm_first")
    xl = x
    for l in range(L):
        g = gains[l]
        w, token = comm.weights(l, xl)
        proj = _mm_nn(h, w["in"], F32, "mm_in", after=token)
        outs, lses = [], []
        for gi, (_, r) in enumerate(BRANCHES):
            o, ls = _attn_branch_fwd(proj, bias[gi], r, H, f"attn_fwd_d{r}")
            outs.append(o)
            lses.append(ls)
        cat, o_attn, lse_tot = _attn_merge(outs, lses, D, "attn_merge")
        cat, o_raw, states = _hg_fwd(proj, cat, lb[l], hg_norm[l], H, "hgrn_fwd")
        mix = _mm_nn(cat, w["out"], F32, "mm_out")
        x1, hc = _norm_fwd(xl, mix, g[1], g[2], "norm_post_pre")
        _, mn = _norm_fwd(mem, None, None, g[3], "norm_mem")
        cq = _mm_nn(hc, w["cq"], BF16, "mm_cq")
        ckv = _mm_nn(mn, w["ckv"], BF16, "mm_ckv")
        co = _cross_fwd(cq, ckv, "cross_fwd")
        cop = _mm_nn(co, w["co"], F32, "mm_co")
        x2, hf = _norm_fwd(x1, cop, g[4], g[5], "norm_post_pre")
        gu = _mm_nn(hf, w["gu"], F32, "mm_gu")
        act = _swiglu_fwd(gu, "swiglu_fwd")
        y = _mm_nn(act, w["down"], F32, "mm_down")
        g_next = gains[l + 1, 0] if l + 1 < L else None
        x3, h_next = _norm_fwd(x2, y, g[6], g_next, "norm_post_pre" if l + 1 < L else "norm_post")
        saved.append(dict(x0=xl, h0=h, proj=proj, cat=cat, o_attn=o_attn, lse_tot=lse_tot, o_raw=o_raw, states=states,
                          mix=mix, x1=x1, hc=hc, mn=mn, cq=cq, ckv=ckv, co=co, cop=cop, x2=x2, hf=hf, gu=gu, act=act,
                          y=y, x3=x3, w=w))
        comm.layer_done(l, x3)
        xl, h = x3, h_next

    loss, dres = _loss_and_grad(xl, target)

    dh_next = None
    d_gains = [[None] * 7 for _ in range(L)]
    d_lb = [None] * L
    d_hg = [None] * L
    d_bias = [jnp.zeros((H, Q_BLOCK, 2 * Q_BLOCK), F32) for _ in BRANCHES]
    for l in reversed(range(L)):
        g, sv = gains[l], saved[l]
        w = sv["w"]
        g_next = gains[l + 1, 0] if l + 1 < L else None
        t3, dy, dg_next, dg6 = _norm_bwd(dres, dh_next, sv["x3"], sv["y"], g_next, g[6], "norm_bwd")
        if l + 1 < L:
            d_gains[l + 1][0] = dg_next
        d_gains[l][6] = dg6
        gw = {}
        gw["down"] = _mm_tn(sv["act"], dy, 1, BF16, "mm_dw_down")
        dact = _mm_nt(dy, w["down"], F32, "mm_dx_down")
        dgu = _swiglu_bwd(sv["gu"], dact, "swiglu_bwd")
        gw["gu"] = _mm_tn(sv["hf"], dgu, w["gu"].shape[0], BF16, "mm_dw_gu")
        dhf = _mm_nt(dgu, w["gu"], F32, "mm_dx_gu")
        t2, dcop, d_gains[l][5], d_gains[l][4] = _norm_bwd(t3, dhf, sv["x2"], sv["cop"], g[5], g[4], "norm_bwd")
        gw["co"] = _mm_tn(sv["co"], dcop, 1, BF16, "mm_dw_sq")
        dco = _mm_nt(dcop, w["co"], F32, "mm_dx_sq")
        dcq, dckv = _cross_bwd(sv["cq"], sv["ckv"], dco, "cross_bwd")
        gw["cq"] = _mm_tn(sv["hc"], dcq, 1, BF16, "mm_dw_sq")
        dhc = _mm_nt(dcq, w["cq"], F32, "mm_dx_sq")
        gw["ckv"] = _mm_tn(sv["mn"], dckv, w["ckv"].shape[0], BF16, "mm_dw_ckv")
        dmn = _mm_nt(dckv, w["ckv"], F32, "mm_dx_ckv")
        _, _, d_gains[l][3], _ = _norm_bwd(None, dmn, mem, None, g[3], None, "norm_bwd_mem")
        t1, dmix, d_gains[l][2], d_gains[l][1] = _norm_bwd(t2, dhc, sv["x1"], sv["mix"], g[2], g[1], "norm_bwd")
        gw["out"] = _mm_tn(sv["cat"], dmix, 1, BF16, "mm_dw_sq")
        dcat = _mm_nt(dmix, w["out"], F32, "mm_dx_sq")
        delta = _attn_delta(dcat, sv["o_attn"], H, "attn_delta")
        dqkv = []
        for gi, (_, r) in enumerate(BRANCHES):
            dq, dk, dv, db = _attn_branch_bwd(sv["proj"], dcat, sv["lse_tot"], delta, bias[gi], r, H, f"attn_bwd_d{r}")
            dqkv.append((dq, dk, dv))
            d_bias[gi] = d_bias[gi] + db
        dfz, div, dqz, dgz, dlb_l, dhg_l = _hg_bwd(sv["proj"], dcat, sv["o_raw"], sv["states"], lb[l], hg_norm[l], H,
                                                   "hgrn_bwd")
        d_lb[l], d_hg[l] = dlb_l[0], dhg_l[0]
        dproj = _assemble_dproj(dqkv, [dfz, div, dqz, dgz], "assemble_dproj")
        gw["in"] = _mm_tn(sv["h0"], dproj, w["in"].shape[0], BF16, "mm_dw_in")
        dh_next = _mm_nt(dproj, w["in"], F32, "mm_dx_in")
        dres = t1
        comm.grads(l, gw, gw["in"])
    grad_x, _, d_gains[0][0], _ = _norm_bwd(dres, dh_next, x, None, gains[0, 0], None, "norm_bwd_first")

    d_rel_bias = _bias_grad(d_bias, onehots, H).T
    d_gains = jnp.stack([jnp.stack(row) for row in d_gains])
    return loss, grad_x, d_rel_bias, jnp.stack(d_lb), jnp.stack(d_hg), d_gains


def _place():
    return lax.axis_index("x"), lax.axis_index("y"), lax.axis_index("c")


def _all_gather(shards, name):
    n = len(shards)
    HBM = pl.BlockSpec(memory_space=pltpu.HBM)

    def body(*refs):
        ins, outs = refs[:n], refs[n:2 * n]
        send_sems, recv_sems, local_sems = refs[2 * n:]
        x, y, c = _place()
        me, sibling = (x, y, c), (x, y, 1 - c)
        chips = [(1 - x, y), (x, 1 - y), (1 - x, 1 - y)]

        def copy(a, k, block, to, own=False):
            dst = outs[a].at[4 * block[0] + 2 * block[1] + block[2]]
            return pltpu.make_async_remote_copy(
                src_ref=ins[a] if own else dst, dst_ref=dst, send_sem=send_sems.at[7 * a + k],
                recv_sem=recv_sems.at[7 * a + k], device_id=to, device_id_type=MESH)

        mine = [pltpu.make_async_copy(ins[a], outs[a].at[4 * x + 2 * y + c], local_sems.at[a]) for a in range(n)]
        for cp in mine:
            cp.start()
        first = []
        for a in range(n):
            first.append(copy(a, 0, me, sibling, own=True))
            first += [copy(a, 1 + j, me, (*chip, c), own=True) for j, chip in enumerate(chips)]
        for cp in first:
            cp.start()
        passed = []
        for j, chip in enumerate(chips):
            for a in range(n):
                copy(a, 1 + j, (*chip, c), me).wait_recv()
                fwd = copy(a, 4 + j, (*chip, c), sibling)
                fwd.start()
                passed.append(fwd)
        for a in range(n):
            copy(a, 0, sibling, me).wait_recv()
            for j, chip in enumerate(chips):
                copy(a, 4 + j, (*chip, 1 - c), me).wait_recv()
        for cp in first + passed:
            cp.wait_send()
        for cp in mine:
            cp.wait()

    return pl.pallas_call(
        body, name=name, in_specs=[HBM] * n, out_specs=[HBM] * n,
        out_shape=[jax.ShapeDtypeStruct((N_DEV,) + s.shape, s.dtype) for s in shards],
        scratch_shapes=[pltpu.SemaphoreType.DMA((7 * n,)), pltpu.SemaphoreType.DMA((7 * n,)),
                        pltpu.SemaphoreType.DMA((n,))],
    )(*shards)


def _exchange_sibling(grads, name):
    n = len(grads)
    HBM = pl.BlockSpec(memory_space=pltpu.HBM)

    def body(*refs):
        ins, outs = refs[:n], refs[n:2 * n]
        send_sems, recv_sems = refs[2 * n:]
        x, y, c = _place()
        sibling = (x, y, 1 - c)
        for a in range(n):
            for q in range(4):
                pltpu.make_async_remote_copy(
                    src_ref=ins[a].at[2 * q + 1 - c], dst_ref=outs[a].at[q], send_sem=send_sems.at[a],
                    recv_sem=recv_sems.at[a], device_id=sibling, device_id_type=MESH).start()
        for a in range(n):
            pltpu.make_async_remote_copy(
                src_ref=ins[a].at[pl.ds(0, 4)], dst_ref=outs[a], send_sem=send_sems.at[a], recv_sem=recv_sems.at[a],
                device_id=sibling, device_id_type=MESH).wait()

    return pl.pallas_call(
        body, name=name, in_specs=[HBM] * n, out_specs=[HBM] * n,
        out_shape=[jax.ShapeDtypeStruct((4,) + g.shape[1:], g.dtype) for g in grads],
        scratch_shapes=[pltpu.SemaphoreType.DMA((n,)), pltpu.SemaphoreType.DMA((n,))],
    )(*grads)


def _exchange_chips(parts, name):
    n = len(parts)
    HBM = pl.BlockSpec(memory_space=pltpu.HBM)

    def body(*refs):
        ins, outs = refs[:n], refs[n:2 * n]
        send_sems, recv_sems = refs[2 * n:]
        x, y, c = _place()
        chips = [(1 - x, y), (x, 1 - y), (1 - x, 1 - y)]
        copies = []
        for a in range(n):
            for j, chip in enumerate(chips):
                cp = pltpu.make_async_remote_copy(
                    src_ref=ins[a].at[2 * chip[0] + chip[1]], dst_ref=outs[a].at[j], send_sem=send_sems.at[3 * a + j],
                    recv_sem=recv_sems.at[3 * a + j], device_id=(*chip, c), device_id_type=MESH)
                cp.start()
                copies.append(cp)
        for cp in copies:
            cp.wait()

    return pl.pallas_call(
        body, name=name, in_specs=[HBM] * n, out_specs=[HBM] * n,
        out_shape=[jax.ShapeDtypeStruct((3,) + p.shape[1:], p.dtype) for p in parts],
        scratch_shapes=[pltpu.SemaphoreType.DMA((3 * n,)), pltpu.SemaphoreType.DMA((3 * n,))],
    )(*parts)


def _others(x, y, c):
    out = []
    for k in range(1, N_DEV):
        px = 1 - x if (k >> 2) & 1 else x
        py = 1 - y if (k >> 1) & 1 else y
        pc = 1 - c if k & 1 else c
        out.append(((px, py, pc), 4 * px + 2 * py + pc))
    return out


_HBM_SPEC = pl.BlockSpec(memory_space=pltpu.HBM)
_SEM_SPEC = pl.BlockSpec(memory_space=pltpu.SEMAPHORE)
_EFFECT = pltpu.SideEffectType.DATAFLOW_SIDE_EFFECTING


def _copies_start(srcs, lands, after, src_block, name):
    n = len(srcs)

    def body(*refs):
        src_refs, land_refs = refs[:n], refs[n:2 * n]
        send_sems, recv_sems = refs[2 * n + 1], refs[2 * n + 2]
        token = refs[-1]
        x, y, c = _place()
        mine = 4 * x + 2 * y + c
        for a in range(n):
            for k, (peer, block) in enumerate(_others(x, y, c)):
                pltpu.make_async_remote_copy(
                    src_ref=src_refs[a].at[block] if src_block else src_refs[a],
                    dst_ref=land_refs[a].at[k] if src_block else land_refs[a].at[mine],
                    send_sem=send_sems.at[7 * a + k], recv_sem=recv_sems.at[7 * a + k],
                    device_id=peer, device_id_type=MESH).start()
        token[...] = jnp.zeros_like(token)

    outs = pl.pallas_call(
        body, name=name,
        out_shape=(pltpu.SemaphoreType.DMA((7 * n,)), pltpu.SemaphoreType.DMA((7 * n,)),
                   *[pltpu.HBM(s.shape, s.dtype) for s in srcs], *[pltpu.HBM(t.shape, t.dtype) for t in lands],
                   jax.ShapeDtypeStruct((8, 128), F32)),
        in_specs=[_HBM_SPEC] * (2 * n) + [pl.BlockSpec(memory_space=pl.ANY)],
        out_specs=(_SEM_SPEC, _SEM_SPEC, *[_HBM_SPEC] * (2 * n), pl.BlockSpec(memory_space=pltpu.VMEM)),
        input_output_aliases={i: 2 + i for i in range(2 * n)},
        compiler_params=pltpu.CompilerParams(has_side_effects=_EFFECT),
    )(*[pltpu.with_memory_space_constraint(s, pltpu.HBM) for s in srcs],
      *[pltpu.with_memory_space_constraint(t, pltpu.HBM) for t in lands], after)
    return outs[0], outs[1], list(outs[2:2 + n]), list(outs[2 + n:2 + 2 * n]), outs[-1]


def _copies_wait(send_sems, recv_sems, srcs, lands, after, src_block, name):
    n = len(srcs)

    def body(*refs):
        src_refs, land_refs = refs[:n], refs[n:2 * n]
        send_sems, recv_sems = refs[2 * n], refs[2 * n + 1]
        x, y, c = _place()
        for a in range(n):
            for k, (peer, block) in enumerate(_others(x, y, c)):
                cp = pltpu.make_async_remote_copy(
                    src_ref=src_refs[a].at[block] if src_block else src_refs[a],
                    dst_ref=land_refs[a].at[k] if src_block else land_refs[a].at[block],
                    send_sem=send_sems.at[7 * a + k], recv_sem=recv_sems.at[7 * a + k],
                    device_id=peer, device_id_type=MESH)
                cp.wait_send()
                cp.wait_recv()

    outs = pl.pallas_call(
        body, name=name,
        out_shape=(*[pltpu.HBM(s.shape, s.dtype) for s in srcs], *[pltpu.HBM(t.shape, t.dtype) for t in lands]),
        in_specs=[_HBM_SPEC] * (2 * n) + [_SEM_SPEC, _SEM_SPEC, pl.BlockSpec(memory_space=pl.ANY)],
        out_specs=tuple([_HBM_SPEC] * (2 * n)),
        input_output_aliases={i: i for i in range(2 * n)},
        compiler_params=pltpu.CompilerParams(has_side_effects=_EFFECT),
    )(*srcs, *lands, send_sems, recv_sems, after)
    return list(outs[:n]), list(outs[n:])


def _pair_sum(grad, recv, c_idx, name):
    _, R, C = grad.shape
    tr = _tile(R, 512, 16)

    def body(c_ref, g_ref, p_ref, o_ref):
        del c_ref
        o_ref[...] = (g_ref[...].astype(F32) + p_ref[...].astype(F32)).astype(BF16)

    spec = pltpu.PrefetchScalarGridSpec(
        num_scalar_prefetch=1, grid=(4, R // tr),
        in_specs=[pl.BlockSpec((None, tr, C), lambda q, i, c: (2 * q + c[0], i, 0)),
                  pl.BlockSpec((None, tr, C), lambda q, i, c: (q, i, 0))],
        out_specs=pl.BlockSpec((None, tr, C), lambda q, i, c: (q, i, 0)))
    return pl.pallas_call(body, name=name, grid_spec=spec, out_shape=jax.ShapeDtypeStruct((4, R, C), BF16),
                          compiler_params=_cp(("parallel", "parallel")))(c_idx, grad, recv)


def _adam(w, g, m, v):
    m2 = ADAM_B1 * m + (1.0 - ADAM_B1) * g
    v2 = ADAM_B2 * v + (1.0 - ADAM_B2) * (g * g)
    m_hat = m2 / (1.0 - ADAM_B1 ** ADAM_STEP)
    v_hat = v2 / (1.0 - ADAM_B2 ** ADAM_STEP)
    return -ADAM_LR * (m_hat / (jnp.sqrt(v_hat) + ADAM_EPS) + ADAM_WD * w), m2, v2


def _reduce_update(part, recv, own_idx, w, m, v, layer, prev, name):
    L, R, C = w.shape
    nr = recv.shape[0]
    tr = _tile(R, 128, 16)
    has_prev = prev is not None

    def body(idx_ref, q_ref, *rest):
        del idx_ref
        r_refs, (w_ref, m_ref, v_ref) = rest[:nr], rest[nr:nr + 3]
        g_ref, d_ref, m2_ref, v2_ref = rest[-4:]
        g = q_ref[...].astype(F32)
        for r_ref in r_refs:
            g = g + r_ref[...].astype(F32)
        d, m2, v2 = _adam(w_ref[...], g, m_ref[...], v_ref[...])
        g_ref[...] = g
        d_ref[...] = d
        m2_ref[...] = m2
        v2_ref[...] = v2

    def rblk(j):
        return pl.BlockSpec((None, tr, C), lambda i, k: (j, i, 0))

    lay = pl.BlockSpec((None, tr, C), lambda i, k: (layer, i, 0))
    in_specs = [pl.BlockSpec((None, tr, C), lambda i, k: (k[0], i, 0))] + [rblk(j) for j in range(nr)] + [lay, lay, lay]
    ins = [part] + [recv] * nr + [w, m, v]
    aliases = {}
    if has_prev:
        in_specs += [pl.BlockSpec(memory_space=pl.ANY)] * 4
        ins += list(prev)
        aliases = {1 + len(ins) - 4 + t: t for t in range(4)}
    spec = pltpu.PrefetchScalarGridSpec(num_scalar_prefetch=1, grid=(R // tr,), in_specs=in_specs,
                                        out_specs=[lay, lay, lay, lay])
    return pl.pallas_call(body, name=name, grid_spec=spec, out_shape=[jax.ShapeDtypeStruct((L, R, C), F32)] * 4,
                          input_output_aliases=aliases, compiler_params=_cp(("parallel",)))(own_idx, *ins)


def _sum_devices(gathered):
    n, R, C = gathered.shape

    def body(g_ref, o_ref):
        acc = g_ref[0]
        for d in range(1, n):
            acc = acc + g_ref[d]
        o_ref[...] = acc

    return pl.pallas_call(body, name="sum_devices", out_shape=jax.ShapeDtypeStruct((R, C), F32))(gathered)


def _adam_small(w, g, m, v, name):
    shape = w.shape
    two = (-1, shape[-1])

    def body(w_ref, g_ref, m_ref, v_ref, d_ref, m2_ref, v2_ref):
        d, m2, v2 = _adam(w_ref[...], g_ref[...], m_ref[...], v_ref[...])
        d_ref[...] = d
        m2_ref[...] = m2
        v2_ref[...] = v2

    outs = pl.pallas_call(body, name=name, out_shape=[jax.ShapeDtypeStruct(w.reshape(two).shape, F32)] * 3)(
        w.reshape(two), g.reshape(two), m.reshape(two), v.reshape(two))
    return [o.reshape(shape) for o in outs]


_SHARDED = ("in", "out", "cq", "ckv", "co", "gu", "down")
_COLUMN_SHARDED = ("in", "ckv", "gu")


def kernel(x, mem, rel_bias, lb_logits, norm_gains, w_in, hg_norm, w_out, w_cq, w_ckv, w_co, w_gate_up, w_down, loss_target, m_rel_bias, m_lb_logits, m_norm_gains, m_w_in, m_hg_norm, m_w_out, m_w_cq, m_w_ckv, m_w_co, m_w_gate_up, m_w_down, v_rel_bias, v_lb_logits, v_norm_gains, v_w_in, v_hg_norm, v_w_out, v_w_cq, v_w_ckv, v_w_co, v_w_gate_up, v_w_down):
    L = w_in.shape[0]
    D = x.shape[-1]
    ws = dict(zip(_SHARDED, (w_in, w_out, w_cq, w_ckv, w_co, w_gate_up, w_down)))
    ms = dict(zip(_SHARDED, (m_w_in, m_w_out, m_w_cq, m_w_ckv, m_w_co, m_w_gate_up, m_w_down)))
    vs = dict(zip(_SHARDED, (v_w_in, v_w_out, v_w_cq, v_w_ckv, v_w_co, v_w_gate_up, v_w_down)))
    px, py, pc = _place()
    dev = 4 * px + 2 * py + pc
    c_idx = jnp.reshape(pc, (1,)).astype(jnp.int32)
    chip_idx = jnp.reshape(2 * px + py, (1,)).astype(jnp.int32)
    dev_idx = jnp.reshape(dev, (1,)).astype(jnp.int32)

    def shards_of(l):
        return [ws[k][l].astype(BF16) for k in _SHARDED]

    def as_weights(gathered):
        return {k: g if k in _COLUMN_SHARDED else g.reshape(1, N_DEV * g.shape[1], g.shape[2])
                for k, g in zip(_SHARDED, gathered)}

    class Comm:
        def __init__(self):
            self.ready = {0: as_weights(_all_gather(shards_of(0), "gather_weights"))}
            self.gather = None
            self.scatter = None
            self.results = {k: None for k in _SHARDED}

        def weights(self, l, x_in):
            del x_in
            mine, token = self.ready.pop(l), None
            if l + 1 < L:
                shards = shards_of(l + 1)
                lands = [lax.empty((N_DEV,) + s.shape, s.dtype) for s in shards]
                *self.gather, token = _copies_start(shards, lands, mine["in"], False, "gather_start")
            return mine, token

        def layer_done(self, l, x_out):
            if l + 1 < L:
                send_sems, recv_sems, shards, lands = self.gather
                shards, lands = _copies_wait(send_sems, recv_sems, shards, lands, x_out, False, "gather_wait")
                full = [lax.dynamic_update_slice_in_dim(t, s[None], dev, axis=0) for s, t in zip(shards, lands)]
                self.ready[l + 1] = as_weights(full)

        def update(self, l, parts, recvs, own_idx):
            for k, part, recv in zip(_SHARDED, parts, recvs):
                self.results[k] = _reduce_update(part, recv, own_idx, ws[k], ms[k], vs[k], l, self.results[k],
                                                 "reduce_update")

        def grads(self, l, gw, after):
            if self.scatter is not None:
                lp, send_sems, recv_sems, srcs, lands = self.scatter
                srcs, lands = _copies_wait(send_sems, recv_sems, srcs, lands, after, True, "scatter_wait")
                self.update(lp, srcs, lands, dev_idx)
            full = [gw[k].reshape((N_DEV,) + ws[k].shape[1:]) for k in _SHARDED]
            if l > 0:
                lands = [lax.empty((N_DEV - 1,) + g.shape[1:], g.dtype) for g in full]
                send_sems, recv_sems, srcs, lands, _ = _copies_start(full, lands, dev_idx, True, "scatter_start")
                self.scatter = (l, send_sems, recv_sems, srcs, lands)
            else:
                from_sibling = _exchange_sibling(full, "scatter_sibling")
                parts = [_pair_sum(g, r, c_idx, "pair_sum") for g, r in zip(full, from_sibling)]
                self.update(l, parts, _exchange_chips(parts, "scatter_chips"), chip_idx)

    comm = Comm()
    gains = _all_gather([norm_gains], "gather_gains")[0]
    gains = jnp.transpose(gains, (1, 2, 0, 3)).reshape(L, 7, D)

    loss, grad_x, d_rel_bias, d_lb, d_hg, d_gains = _local_step(
        x[0], mem[0], loss_target[0], rel_bias, lb_logits, gains, hg_norm, comm)
    results = comm.results

    pieces = [jnp.reshape(loss, (1,)), d_rel_bias.reshape(-1), d_lb.reshape(-1), d_hg.reshape(-1), d_gains.reshape(-1)]
    sizes = [p.shape[0] for p in pieces]
    total = sum(sizes)
    rows = -(-total // 1024) * 8
    pack = jnp.pad(jnp.concatenate(pieces), (0, rows * 128 - total)).reshape(rows, 128)
    summed = _sum_devices(_all_gather([pack], "gather_small")[0]).reshape(-1)
    offs = [sum(sizes[:i]) for i in range(len(sizes))]
    loss = summed[0]
    g_rel_bias = summed[offs[1]:offs[1] + sizes[1]].reshape(rel_bias.shape)
    g_lb_all = summed[offs[2]:offs[2] + sizes[2]].reshape(lb_logits.shape)
    g_hg = summed[offs[3]:offs[3] + sizes[3]].reshape(hg_norm.shape)
    g_gains_full = summed[offs[4]:offs[4] + sizes[4]].reshape(L, 7, D)
    g_gains = lax.dynamic_slice_in_dim(g_gains_full, dev * (D // N_DEV), D // N_DEV, axis=2)
    g_lb = jax.vjp(_lb_all, lb_logits)[1](g_lb_all)[0]

    small = [(rel_bias, g_rel_bias, m_rel_bias, v_rel_bias), (lb_logits, g_lb, m_lb_logits, v_lb_logits),
             (norm_gains, g_gains, m_norm_gains, v_norm_gains), (hg_norm, g_hg, m_hg_norm, v_hg_norm)]
    upd = [_adam_small(w, g, m, v, "adam_small") for w, g, m, v in small]

    def ordered(small_vals, big_vals):
        return [small_vals[0], small_vals[1], small_vals[2], big_vals["in"], small_vals[3], big_vals["out"],
                big_vals["cq"], big_vals["ckv"], big_vals["co"], big_vals["gu"], big_vals["down"]]

    grads = ordered([s[1] for s in small], {k: results[k][0] for k in _SHARDED})
    deltas = ordered([u[0] for u in upd], {k: results[k][1] for k in _SHARDED})
    new_m = ordered([u[1] for u in upd], {k: results[k][2] for k in _SHARDED})
    new_v = ordered([u[2] for u in upd], {k: results[k][3] for k in _SHARDED})
    return (loss, grad_x[None], *grads, *deltas, *new_m, *new_v)
```

```python
import functools
import math

import jax
import jax.numpy as jnp
from jax import lax
from jax.experimental import pallas as pl
from jax.experimental.pallas import tpu as pltpu

F32 = jnp.float32
BF16 = jnp.bfloat16
MESH = pl.DeviceIdType.MESH

N_DEV = 8
HEAD = 128
CROSS_HEADS = 4
Q_BLOCK = 128
BRANCHES = ((128, 1), (512, 4), (2048, 16))
REL_BUCKETS = 32
REL_MAX_DIST = 2048
HG_CHUNK = 16
HG_HEADS_PER_STEP = 8
RMS_EPS = 1e-6
NEG_INF = -1e30
ADAM_LR, ADAM_B1, ADAM_B2, ADAM_EPS, ADAM_WD, ADAM_STEP = 0.001, 0.9, 0.999, 1e-08, 0.01, 10
VMEM_LIMIT = 48 * 1024 * 1024
NT_K_PER_STEP = 3584


def _tile(n, target, mult):
    best = None
    t = mult
    while t <= min(n, target):
        if n % t == 0:
            best = t
        t += mult
    return best if best is not None else n


def _cp(sem, vmem=VMEM_LIMIT):
    return pltpu.CompilerParams(dimension_semantics=sem, vmem_limit_bytes=vmem)


def _mm_nn(a, b3, out_dtype, name, after=None):
    M, K = a.shape
    J, K2, Nb = b3.shape
    assert K == K2
    tm, tn, tk = _tile(M, 1024, 8), _tile(Nb, 1408, 128), _tile(K, 2816, 128)
    nn, nk = Nb // tn, K // tk
    extra = [] if after is None else [after]

    def body(a_ref, b_ref, *rest):
        o_ref, acc = rest[len(extra)], rest[len(extra) + 1:]
        prod = jnp.dot(a_ref[...].astype(BF16), b_ref[...].astype(BF16), preferred_element_type=F32)
        if nk == 1:
            o_ref[...] = prod.astype(o_ref.dtype)
        else:
            k = pl.program_id(2)

            @pl.when(k == 0)
            def _():
                acc[0][...] = prod

            @pl.when(k > 0)
            def _():
                acc[0][...] += prod

            @pl.when(k == nk - 1)
            def _():
                o_ref[...] = acc[0][...].astype(o_ref.dtype)

    return pl.pallas_call(
        body, name=name, grid=(M // tm, J * nn, nk),
        in_specs=[pl.BlockSpec((tm, tk), lambda i, n, k: (i, k)),
                  pl.BlockSpec((None, tk, tn), lambda i, n, k: (n // nn, k, n % nn))]
                 + [pl.BlockSpec(memory_space=pl.ANY)] * len(extra),
        out_specs=pl.BlockSpec((tm, tn), lambda i, n, k: (i, n)),
        out_shape=jax.ShapeDtypeStruct((M, J * Nb), out_dtype),
        scratch_shapes=[] if nk == 1 else [pltpu.VMEM((tm, tn), F32)],
        compiler_params=_cp(("parallel", "parallel", "arbitrary")),
    )(a, b3, *extra)


def _mm_nt(a, b3, out_dtype, name, after=None):
    extra = [] if after is None else [after]
    M, Kt = a.shape
    J, N, Kb = b3.shape
    assert Kt == J * Kb
    tm, tn, tk = _tile(M, 1024, 8), _tile(N, 1408, 128), _tile(Kb, 2048, 128)
    nkb = Kb // tk
    jb = max(j for j in range(1, J + 1) if J % j == 0 and j * Kb <= NT_K_PER_STEP) if nkb == 1 else 1
    nk = (J // jb) * nkb

    def body(a_ref, b_ref, *rest):
        o_ref, acc = rest[len(extra)], rest[len(extra) + 1:]
        prod = None
        for j in range(jb):
            part = lax.dot_general(a_ref[:, j * tk:(j + 1) * tk].astype(BF16), b_ref[j].astype(BF16),
                                   (((1,), (1,)), ((), ())), preferred_element_type=F32)
            prod = part if prod is None else prod + part
        if nk == 1:
            o_ref[...] = prod.astype(o_ref.dtype)
        else:
            k = pl.program_id(2)

            @pl.when(k == 0)
            def _():
                acc[0][...] = prod

            @pl.when(k > 0)
            def _():
                acc[0][...] += prod

            @pl.when(k == nk - 1)
            def _():
                o_ref[...] = acc[0][...].astype(o_ref.dtype)

    return pl.pallas_call(
        body, name=name, grid=(M // tm, N // tn, nk),
        in_specs=[pl.BlockSpec((tm, jb * tk), lambda i, n, k: (i, k)),
                  pl.BlockSpec((jb, tn, tk), lambda i, n, k: (k // nkb, n, k % nkb))]
                 + [pl.BlockSpec(memory_space=pl.ANY)] * len(extra),
        out_specs=pl.BlockSpec((tm, tn), lambda i, n, k: (i, n)),
        out_shape=jax.ShapeDtypeStruct((M, N), out_dtype),
        scratch_shapes=[] if nk == 1 else [pltpu.VMEM((tm, tn), F32)],
        compiler_params=_cp(("parallel", "parallel", "arbitrary")),
    )(a, b3, *extra)


def _mm_tn(a, b, J, out_dtype, name):
    S, M = a.shape
    S2, Nt = b.shape
    assert S == S2 and Nt % J == 0
    Nb = Nt // J
    tm, tn, tk = _tile(M, 1408, 128), _tile(Nb, 1408, 128), _tile(S, 2048, 8)
    nn, nk = Nb // tn, S // tk

    def body(a_ref, b_ref, o_ref, *acc):
        prod = lax.dot_general(a_ref[...].astype(BF16), b_ref[...].astype(BF16), (((0,), (0,)), ((), ())),
                               preferred_element_type=F32)
        if nk == 1:
            o_ref[...] = prod.astype(o_ref.dtype)
        else:
            k = pl.program_id(2)

            @pl.when(k == 0)
            def _():
                acc[0][...] = prod

            @pl.when(k > 0)
            def _():
                acc[0][...] += prod

            @pl.when(k == nk - 1)
            def _():
                o_ref[...] = acc[0][...].astype(o_ref.dtype)

    return pl.pallas_call(
        body, name=name, grid=(M // tm, J * nn, nk),
        in_specs=[pl.BlockSpec((tk, tm), lambda i, n, k: (k, i)),
                  pl.BlockSpec((tk, tn), lambda i, n, k: (k, n))],
        out_specs=pl.BlockSpec((None, tm, tn), lambda i, n, k: (n // nn, i, n % nn)),
        out_shape=jax.ShapeDtypeStruct((J, M, Nb), out_dtype),
        scratch_shapes=[] if nk == 1 else [pltpu.VMEM((tm, tn), F32)],
        compiler_params=_cp(("parallel", "parallel", "arbitrary")),
    )(a, b)


def _rms_scale(v):
    return lax.rsqrt(jnp.mean(v * v, axis=-1, keepdims=True) + RMS_EPS)


def _rms_bwd(dy, v, g):
    r = _rms_scale(v)
    vh = v * r
    u = dy * g
    dv = r * (u - vh * jnp.mean(u * vh, axis=-1, keepdims=True))
    return dv, dy * vh


def _fold8(t):
    R, D = t.shape
    return jnp.sum(t.reshape(R // 8, 8, D), axis=0)


def _norm_fwd(x, z, g_post, g_pre, name):
    S, D = x.shape
    tr = _tile(S, 256, 8)
    has_z, has_pre = z is not None, g_pre is not None

    def body(*refs):
        it = iter(refs)
        x_ref = next(it)
        z_ref = next(it) if has_z else None
        gpost_ref = next(it) if has_z else None
        gpre_ref = next(it) if has_pre else None
        xn_ref = next(it) if has_z else None
        h_ref = next(it) if has_pre else None
        xn = x_ref[...]
        if has_z:
            zz = z_ref[...]
            xn = xn + zz * _rms_scale(zz) * gpost_ref[...]
            xn_ref[...] = xn
        if has_pre:
            h_ref[...] = (xn * _rms_scale(xn) * gpre_ref[...]).astype(BF16)

    row = pl.BlockSpec((tr, D), lambda i: (i, 0))
    gain = pl.BlockSpec((1, D), lambda i: (0, 0))
    ins, specs, outs, ospecs = [x], [row], [], []
    if has_z:
        ins += [z, g_post.reshape(1, D)]
        specs += [row, gain]
        outs.append(jax.ShapeDtypeStruct((S, D), F32))
        ospecs.append(row)
    if has_pre:
        ins.append(g_pre.reshape(1, D))
        specs.append(gain)
        outs.append(jax.ShapeDtypeStruct((S, D), BF16))
        ospecs.append(row)
    res = pl.pallas_call(body, name=name, grid=(S // tr,), in_specs=specs, out_specs=ospecs, out_shape=outs,
                         compiler_params=_cp(("parallel",)))(*ins)
    res = list(res)
    xn = res.pop(0) if has_z else x
    h = res.pop(0) if has_pre else None
    return xn, h


def _norm_bwd(dres, dh, xn, z, g_pre, g_post, name):
    S, D = xn.shape
    tr = _tile(S, 256, 8)
    has_res, has_pre, has_z = dres is not None, dh is not None, z is not None

    def body(*refs):
        it = iter(refs)
        dres_ref = next(it) if has_res else None
        dh_ref = next(it) if has_pre else None
        xn_ref = next(it) if has_pre else None
        gpre_ref = next(it) if has_pre else None
        z_ref = next(it) if has_z else None
        gpost_ref = next(it) if has_z else None
        t_ref = next(it)
        dz_ref = next(it) if has_z else None
        dgpre_ref = next(it) if has_pre else None
        dgpost_ref = next(it) if has_z else None
        first = pl.program_id(0) == 0
        t = dres_ref[...] if has_res else None
        if has_pre:
            dv, dg = _rms_bwd(dh_ref[...].astype(F32), xn_ref[...], gpre_ref[...])
            t = dv if t is None else t + dv
            part = _fold8(dg)

            @pl.when(first)
            def _():
                dgpre_ref[...] = part

            @pl.when(jnp.logical_not(first))
            def _():
                dgpre_ref[...] += part
        t_ref[...] = t
        if has_z:
            dv, dg = _rms_bwd(t, z_ref[...], gpost_ref[...])
            dz_ref[...] = dv.astype(BF16)
            part2 = _fold8(dg)

            @pl.when(first)
            def _():
                dgpost_ref[...] = part2

            @pl.when(jnp.logical_not(first))
            def _():
                dgpost_ref[...] += part2

    row = pl.BlockSpec((tr, D), lambda i: (i, 0))
    gain = pl.BlockSpec((1, D), lambda i: (0, 0))
    accs = pl.BlockSpec((8, D), lambda i: (0, 0))
    ins, specs = [], []
    if has_res:
        ins.append(dres)
        specs.append(row)
    if has_pre:
        ins += [dh, xn, g_pre.reshape(1, D)]
        specs += [row, row, gain]
    if has_z:
        ins += [z, g_post.reshape(1, D)]
        specs += [row, gain]
    outs, ospecs = [jax.ShapeDtypeStruct((S, D), F32)], [row]
    if has_z:
        outs.append(jax.ShapeDtypeStruct((S, D), BF16))
        ospecs.append(row)
    if has_pre:
        outs.append(jax.ShapeDtypeStruct((8, D), F32))
        ospecs.append(accs)
    if has_z:
        outs.append(jax.ShapeDtypeStruct((8, D), F32))
        ospecs.append(accs)
    res = list(pl.pallas_call(body, name=name, grid=(S // tr,), in_specs=specs, out_specs=ospecs, out_shape=outs,
                              compiler_params=_cp(("arbitrary",)))(*ins))
    t = res.pop(0)
    dz = res.pop(0) if has_z else None
    dgpre = jnp.sum(res.pop(0), axis=0) if has_pre else None
    dgpost = jnp.sum(res.pop(0), axis=0) if has_z else None
    return t, dz, dgpre, dgpost


def _loss_and_grad(y, target):
    S, D = y.shape
    tr = _tile(S, 256, 8)

    def body(y_ref, t_ref, dy_ref, l_ref):
        err = y_ref[...] - t_ref[...]
        dy_ref[...] = err * (1.0 / D)
        part = 0.5 * jnp.sum(jnp.mean(err * err, axis=-1, keepdims=True), axis=0, keepdims=True)
        first = pl.program_id(0) == 0

        @pl.when(first)
        def _():
            l_ref[...] = jnp.broadcast_to(part, l_ref.shape)

        @pl.when(jnp.logical_not(first))
        def _():
            l_ref[...] += jnp.broadcast_to(part, l_ref.shape)

    row = pl.BlockSpec((tr, D), lambda i: (i, 0))
    dy, l = pl.pallas_call(
        body, name="loss_grad", grid=(S // tr,), in_specs=[row, row],
        out_specs=[row, pl.BlockSpec((8, 128), lambda i: (0, 0))],
        out_shape=[jax.ShapeDtypeStruct((S, D), F32), jax.ShapeDtypeStruct((8, 128), F32)],
        compiler_params=_cp(("arbitrary",)))(y, target)
    return l[0, 0], dy


def _swiglu_fwd(gu, name):
    S, F2 = gu.shape
    F = F2 // 2
    tr, tc = _tile(S, 512, 8), _tile(F, 1536, 128)
    nf = F // tc

    def body(g_ref, u_ref, o_ref):
        g = g_ref[...]
        o_ref[...] = (g * jax.nn.sigmoid(g) * u_ref[...]).astype(BF16)

    return pl.pallas_call(
        body, name=name, grid=(S // tr, nf),
        in_specs=[pl.BlockSpec((tr, tc), lambda i, j: (i, j)), pl.BlockSpec((tr, tc), lambda i, j: (i, j + nf))],
        out_specs=pl.BlockSpec((tr, tc), lambda i, j: (i, j)),
        out_shape=jax.ShapeDtypeStruct((S, F), BF16), compiler_params=_cp(("parallel", "parallel")))(gu, gu)


def _swiglu_bwd(gu, dact, name):
    S, F2 = gu.shape
    F = F2 // 2
    tr, tc = _tile(S, 512, 8), _tile(F, 1536, 128)
    nf = F // tc

    def body(g_ref, u_ref, d_ref, o_ref):
        g = g_ref[...]
        d = d_ref[...].astype(F32)
        sg = jax.nn.sigmoid(g)

        @pl.when(pl.program_id(1) < nf)
        def _():
            o_ref[...] = (d * u_ref[...] * (sg * (1.0 + g * (1.0 - sg)))).astype(BF16)

        @pl.when(pl.program_id(1) >= nf)
        def _():
            o_ref[...] = (d * g * sg).astype(BF16)

    return pl.pallas_call(
        body, name=name, grid=(S // tr, 2 * nf),
        in_specs=[pl.BlockSpec((tr, tc), lambda i, j: (i, j % nf)), pl.BlockSpec((tr, tc), lambda i, j: (i, j % nf + nf)),
                  pl.BlockSpec((tr, tc), lambda i, j: (i, j % nf))],
        out_specs=pl.BlockSpec((tr, tc), lambda i, j: (i, j)),
        out_shape=jax.ShapeDtypeStruct((S, F2), BF16), compiler_params=_cp(("parallel", "parallel")))(gu, gu, dact)


def _cross_scores(q, k, scale):
    s = lax.dot_general(q, k, (((1,), (1,)), ((), ())), preferred_element_type=F32) * scale
    m = jnp.max(s, axis=-1, keepdims=True)
    e = jnp.exp(s - m)
    return e / jnp.sum(e, axis=-1, keepdims=True)


def _cross_fwd(cq, ckv, name):
    S, D = cq.shape
    Nm = ckv.shape[0]
    dh = D // CROSS_HEADS
    tq = _tile(S, 512, 8)
    scale = dh ** -0.5

    def body(q_ref, kv_ref, o_ref):
        for h in range(CROSS_HEADS):
            cols = slice(h * dh, (h + 1) * dh)
            p = _cross_scores(q_ref[:, cols], kv_ref[:, cols], scale)
            o_ref[:, cols] = jnp.dot(p.astype(BF16), kv_ref[:, D + h * dh:D + (h + 1) * dh],
                                     preferred_element_type=F32).astype(BF16)

    return pl.pallas_call(
        body, name=name, grid=(S // tq,),
        in_specs=[pl.BlockSpec((tq, D), lambda i: (i, 0)), pl.BlockSpec((Nm, 2 * D), lambda i: (0, 0))],
        out_specs=pl.BlockSpec((tq, D), lambda i: (i, 0)),
        out_shape=jax.ShapeDtypeStruct((S, D), BF16), compiler_params=_cp(("parallel",)))(cq, ckv)


def _cross_bwd(cq, ckv, do, name):
    S, D = cq.shape
    Nm = ckv.shape[0]
    dh = D // CROSS_HEADS
    tq = _tile(S, 512, 8)
    scale = dh ** -0.5

    def body(q_ref, kv_ref, do_ref, dq_ref, dkv_ref):
        first = pl.program_id(0) == 0

        @pl.when(first)
        def _():
            dkv_ref[...] = jnp.zeros_like(dkv_ref)

        for h in range(CROSS_HEADS):
            cols = slice(h * dh, (h + 1) * dh)
            vcols = slice(D + h * dh, D + (h + 1) * dh)
            q, k, v = q_ref[:, cols], kv_ref[:, cols], kv_ref[:, vcols]
            doh = do_ref[:, cols].astype(BF16)
            p = _cross_scores(q, k, scale)
            dp = lax.dot_general(doh, v, (((1,), (1,)), ((), ())), preferred_element_type=F32)
            ds = (p * (dp - jnp.sum(p * dp, axis=-1, keepdims=True)) * scale).astype(BF16)
            dq_ref[:, cols] = jnp.dot(ds, k, preferred_element_type=F32).astype(BF16)
            dkv_ref[:, cols] += lax.dot_general(ds, q, (((0,), (0,)), ((), ())), preferred_element_type=F32)
            dkv_ref[:, vcols] += lax.dot_general(p.astype(BF16), doh, (((0,), (0,)), ((), ())),
                                                 preferred_element_type=F32)

    return pl.pallas_call(
        body, name=name, grid=(S // tq,),
        in_specs=[pl.BlockSpec((tq, D), lambda i: (i, 0)), pl.BlockSpec((Nm, 2 * D), lambda i: (0, 0)),
                  pl.BlockSpec((tq, D), lambda i: (i, 0))],
        out_specs=[pl.BlockSpec((tq, D), lambda i: (i, 0)), pl.BlockSpec((Nm, 2 * D), lambda i: (0, 0))],
        out_shape=[jax.ShapeDtypeStruct((S, D), BF16), jax.ShapeDtypeStruct((Nm, 2 * D), F32)],
        compiler_params=_cp(("arbitrary",)))(cq, ckv, do)


def _rel_bucket(dist):
    max_exact = REL_BUCKETS // 2
    d_f = jnp.maximum(dist, 1).astype(F32)
    large = max_exact + (jnp.log(d_f / max_exact) / math.log(REL_MAX_DIST / max_exact)
                         * (REL_BUCKETS - max_exact)).astype(jnp.int32)
    large = jnp.minimum(large, REL_BUCKETS - 1)
    return jnp.where(dist < max_exact, dist, large)


def _branch_tables(dilation):
    qi = jnp.arange(Q_BLOCK)[:, None]
    kj = jnp.arange(2 * Q_BLOCK)[None, :]
    m = qi - kj + Q_BLOCK
    return _rel_bucket(jnp.maximum(m, 0) * dilation), (m >= 0) & (m <= Q_BLOCK)


def _attn_logits(q, kcat, bias, first_block, scale):
    s = lax.dot_general(q, kcat, (((1,), (1,)), ((), ())), preferred_element_type=F32) * scale + bias
    col = lax.broadcasted_iota(jnp.int32, s.shape, 1)
    return jnp.where(jnp.logical_and(first_block, col < Q_BLOCK), NEG_INF, s)


def _attn_branch_fwd(proj, bias, dilation, H, name):
    S, NC = proj.shape
    AW = H * HEAD
    r = dilation
    hb = H if r == 1 else 1
    G = Q_BLOCK * r
    ng, nh = S // G, H // hb
    scale = HEAD ** -0.5

    def body(q_ref, kp_ref, kc_ref, vp_ref, vc_ref, b_ref, o_ref, l_ref):
        first_block = pl.program_id(0) == 0
        h0 = pl.program_id(1) * hb
        for rho in range(r):
            rows = pl.ds(rho, Q_BLOCK, stride=r) if r > 1 else slice(None)
            for hh in range(hb):
                cols = slice(hh * HEAD, (hh + 1) * HEAD)
                q = q_ref[rows, cols].astype(BF16)
                kcat = jnp.concatenate([kp_ref[rows, cols], kc_ref[rows, cols]], axis=0).astype(BF16)
                vcat = jnp.concatenate([vp_ref[rows, cols], vc_ref[rows, cols]], axis=0).astype(BF16)
                s = _attn_logits(q, kcat, b_ref[h0 + hh], first_block, scale)
                m = jnp.max(s, axis=-1, keepdims=True)
                e = jnp.exp(s - m)
                den = jnp.sum(e, axis=-1, keepdims=True)
                o_ref[rows, cols] = jnp.dot(e.astype(BF16), vcat, preferred_element_type=F32) / den
                l_ref[rows, cols] = jnp.broadcast_to(m + jnp.log(den), (Q_BLOCK, HEAD))

    def blk(part, prev):
        if prev:
            return pl.BlockSpec((G, hb * HEAD), lambda n, h: (jnp.maximum(n - 1, 0), part * nh + h))
        return pl.BlockSpec((G, hb * HEAD), lambda n, h: (n, part * nh + h))

    out = pl.BlockSpec((G, hb * HEAD), lambda n, h: (n, h))
    return pl.pallas_call(
        body, name=name, grid=(ng, nh),
        in_specs=[blk(0, False), blk(1, True), blk(1, False), blk(2, True), blk(2, False),
                  pl.BlockSpec((H, Q_BLOCK, 2 * Q_BLOCK), lambda n, h: (0, 0, 0))],
        out_specs=[out, out],
        out_shape=[jax.ShapeDtypeStruct((S, AW), F32)] * 2,
        compiler_params=_cp(("parallel", "parallel")))(proj, proj, proj, proj, proj, bias)


def _attn_merge(outs, lses, width_out, name):
    S, AW = outs[0].shape
    tr = _tile(S, 256, 8)

    def body(o0, o1, o2, l0, l1, l2, cat_ref, om_ref, lt_ref):
        a, b, c = l0[...], l1[...], l2[...]
        m = jnp.maximum(jnp.maximum(a, b), c)
        ea, eb, ec = jnp.exp(a - m), jnp.exp(b - m), jnp.exp(c - m)
        tot = ea + eb + ec
        o = (ea * o0[...] + eb * o1[...] + ec * o2[...]) / tot
        om_ref[...] = o
        cat_ref[...] = o.astype(BF16)
        lt_ref[...] = m + jnp.log(tot)

    row = pl.BlockSpec((tr, AW), lambda i: (i, 0))
    return pl.pallas_call(
        body, name=name, grid=(S // tr,), in_specs=[row] * 6, out_specs=[row, row, row],
        out_shape=[jax.ShapeDtypeStruct((S, width_out), BF16), jax.ShapeDtypeStruct((S, AW), F32),
                   jax.ShapeDtypeStruct((S, AW), F32)],
        compiler_params=_cp(("parallel",)))(*outs, *lses)


def _attn_delta(dcat, o_attn, H, name):
    S, AW = o_attn.shape
    tr = _tile(S, 256, 8)

    def body(d_ref, o_ref, out_ref):
        for h in range(H):
            cols = slice(h * HEAD, (h + 1) * HEAD)
            out_ref[:, cols] = jnp.broadcast_to(
                jnp.sum(d_ref[:, cols] * o_ref[:, cols], axis=-1, keepdims=True), (tr, HEAD))

    row = pl.BlockSpec((tr, AW), lambda i: (i, 0))
    return pl.pallas_call(body, name=name, grid=(S // tr,), in_specs=[row, row], out_specs=row,
                          out_shape=jax.ShapeDtypeStruct((S, AW), F32),
                          compiler_params=_cp(("parallel",)))(dcat, o_attn)


def _attn_branch_bwd(proj, dcat, lse_tot, delta, bias, dilation, H, name):
    S, NC = proj.shape
    AW = H * HEAD
    r = dilation
    hb = H if r == 1 else 1
    G = Q_BLOCK * r
    ng, nh = S // G, H // hb
    scale = HEAD ** -0.5

    def body(q_ref, kp_ref, kc_ref, vp_ref, vc_ref, do_ref, l_ref, t_ref, b_ref,
             dq_ref, dk_ref, dv_ref, db_ref, ck_ref, cv_ref):
        hblk, n = pl.program_id(0), pl.program_id(1)
        h0 = hblk * hb

        @pl.when(jnp.logical_and(hblk == 0, n == 0))
        def _():
            db_ref[...] = jnp.zeros_like(db_ref)

        @pl.when(n == 0)
        def _():
            ck_ref[...] = jnp.zeros_like(ck_ref)
            cv_ref[...] = jnp.zeros_like(cv_ref)

        @pl.when(n < ng)
        def _():
            for rho in range(r):
                rows = pl.ds(rho, Q_BLOCK, stride=r) if r > 1 else slice(None)
                for hh in range(hb):
                    cols = slice(hh * HEAD, (hh + 1) * HEAD)
                    q = q_ref[rows, cols].astype(BF16)
                    kcat = jnp.concatenate([kp_ref[rows, cols], kc_ref[rows, cols]], axis=0).astype(BF16)
                    vcat = jnp.concatenate([vp_ref[rows, cols], vc_ref[rows, cols]], axis=0).astype(BF16)
                    doh = do_ref[rows, cols].astype(BF16)
                    s = _attn_logits(q, kcat, b_ref[h0 + hh], n == 0, scale)
                    lt = l_ref[rows, cols]
                    dl = t_ref[rows, cols]
                    p = jnp.exp(s - jnp.concatenate([lt, lt], axis=1))
                    dp = lax.dot_general(doh, vcat, (((1,), (1,)), ((), ())), preferred_element_type=F32)
                    ds = p * (dp - jnp.concatenate([dl, dl], axis=1))
                    db_ref[h0 + hh] += ds
                    dsb = (ds * scale).astype(BF16)
                    dq_ref[rows, cols] = jnp.dot(dsb, kcat, preferred_element_type=F32)
                    dkc = lax.dot_general(dsb, q, (((0,), (0,)), ((), ())), preferred_element_type=F32)
                    dvc = lax.dot_general(p.astype(BF16), doh, (((0,), (0,)), ((), ())), preferred_element_type=F32)
                    dk_ref[rows, cols] = ck_ref[rows, cols] + dkc[:Q_BLOCK]
                    dv_ref[rows, cols] = cv_ref[rows, cols] + dvc[:Q_BLOCK]
                    ck_ref[rows, cols] = dkc[Q_BLOCK:]
                    cv_ref[rows, cols] = dvc[Q_BLOCK:]

        @pl.when(n == ng)
        def _():
            dk_ref[...] = ck_ref[...]
            dv_ref[...] = cv_ref[...]

    last = ng - 1

    def cur(part):
        return pl.BlockSpec((G, hb * HEAD), lambda h, n: (jnp.minimum(n, last), part * nh + h))

    def prev(part):
        return pl.BlockSpec((G, hb * HEAD), lambda h, n: (jnp.clip(n - 1, 0, last), part * nh + h))

    table = pl.BlockSpec((H, Q_BLOCK, 2 * Q_BLOCK), lambda h, n: (0, 0, 0))
    return pl.pallas_call(
        body, name=name, grid=(nh, ng + 1),
        in_specs=[cur(0), prev(1), cur(1), prev(2), cur(2), cur(0), cur(0), cur(0), table],
        out_specs=[cur(0), prev(0), prev(0), table],
        out_shape=[jax.ShapeDtypeStruct((S, AW), F32)] * 3 + [jax.ShapeDtypeStruct((H, Q_BLOCK, 2 * Q_BLOCK), F32)],
        scratch_shapes=[pltpu.VMEM((G, hb * HEAD), F32), pltpu.VMEM((G, hb * HEAD), F32)],
        compiler_params=_cp(("arbitrary", "arbitrary")))(proj, proj, proj, proj, proj, dcat, lse_tot, delta, bias)


def _bias_grad(dbs, onehots, H):
    def body(d0, d1, d2, e0, e1, e2, o_ref):
        acc = jnp.zeros((H, REL_BUCKETS), F32)
        for d, e in ((d0, e0), (d1, e1), (d2, e2)):
            acc = acc + lax.dot_general(d[...], e[...], (((1,), (1,)), ((), ())), preferred_element_type=F32,
                                        precision=lax.Precision.HIGHEST)
        o_ref[...] = acc

    flat = [d.reshape(H, 2 * Q_BLOCK * Q_BLOCK) for d in dbs]
    return pl.pallas_call(body, name="bias_grad", out_shape=jax.ShapeDtypeStruct((H, REL_BUCKETS), F32),
                          compiler_params=pltpu.CompilerParams(vmem_limit_bytes=VMEM_LIMIT))(*flat, *onehots)


def _assemble_dproj(dqkv, dhg, name):
    S, AW = dqkv[0][0].shape
    tr = _tile(S, 256, 8)
    ins = [dqkv[g][c] for g in range(3) for c in range(3)] + list(dhg)

    def body(*refs):
        o_ref = refs[-1]
        j = pl.program_id(1)
        for c in range(3):
            @pl.when(j == c)
            def _(c=c):
                o_ref[...] = (refs[c][...] + refs[3 + c][...] + refs[6 + c][...]).astype(BF16)

        for k in range(4):
            @pl.when(j == 3 + k)
            def _(k=k):
                o_ref[...] = refs[9 + k][...]

    row = pl.BlockSpec((tr, AW), lambda i, j: (i, 0))
    return pl.pallas_call(
        body, name=name, grid=(S // tr, 7), in_specs=[row] * 13,
        out_specs=pl.BlockSpec((tr, AW), lambda i, j: (i, j)),
        out_shape=jax.ShapeDtypeStruct((S, 7 * AW), BF16),
        compiler_params=_cp(("parallel", "arbitrary")))(*ins)


def _tri(n, upper):
    r = lax.broadcasted_iota(jnp.int32, (n, n), 0)
    c = lax.broadcasted_iota(jnp.int32, (n, n), 1)
    return jnp.where(c >= r if upper else c <= r, 1.0, 0.0).astype(F32)


def _hg_gates(fz, qz, lb):
    sig = jax.nn.sigmoid(fz)
    f = lb + (1.0 - lb) * sig
    sq = jax.nn.sigmoid(qz)
    return sig, f, jnp.log(f), 1.0 - f, qz * sq, sq


def _hg_fwd(proj, cat, lb, gain, HH, name):
    S, NC = proj.shape
    W = HH * HEAD
    C = HG_CHUNK
    hb = min(HG_HEADS_PER_STEP, HH)
    nh = HH // hb
    tb = _tile(S, 2048 // hb, C)
    ncb = tb // C
    base = 3 * nh
    cat_cols = cat.shape[1] // (hb * HEAD)

    def body(fz_ref, iv_ref, qz_ref, gz_ref, lb_ref, g_ref, cat_in, cat_ref, o_ref, st_ref, state):
        del cat_in

        @pl.when(pl.program_id(1) == 0)
        def _():
            state[...] = jnp.zeros_like(state)

        tri = _tri(C, False)
        row8 = lax.broadcasted_iota(jnp.int32, (8, 1), 0)

        def chunk(c, carry):
            rows = pl.ds(pl.multiple_of(c * C, C), C)
            for hh in range(hb):
                cols = slice(hh * HEAD, (hh + 1) * HEAD)
                lbv, gv = lb_ref[:, cols], g_ref[:, cols]
                fz, vv, qz, gz = fz_ref[rows, cols], iv_ref[rows, cols], qz_ref[rows, cols], gz_ref[rows, cols]
                _, f, lf, kk, qq, _ = _hg_gates(fz, qz, lbv)
                b = jnp.dot(tri, lf, preferred_element_type=F32, precision=lax.Precision.HIGHEST)
                bl = b[C - 1:C, :]
                st = state[hh]
                stb = st.astype(BF16)
                st_ref[hh, c] = stb
                o = lax.dot_general((qq * jnp.exp(b)).astype(BF16), stb, (((1,), (1,)), ((), ())),
                                    preferred_element_type=F32)
                tiles = []
                for t0 in range(0, C, 8):
                    bt, qt, acc = b[t0:t0 + 8, :], qq[t0:t0 + 8, :], o[t0:t0 + 8, :]
                    for s in range(min(C, t0 + 8)):
                        diff = bt - b[s:s + 1, :]
                        if s >= t0:
                            diff = jnp.minimum(diff, 0.0)
                        a = jnp.sum(qt * jnp.exp(diff) * kk[s:s + 1, :], axis=-1, keepdims=True)
                        if s >= t0:
                            a = jnp.where(row8 >= s - t0, a, 0.0)
                        acc = acc + a * vv[s:s + 1, :]
                    tiles.append(acc)
                o = jnp.concatenate(tiles, axis=0)
                kh = (kk * jnp.exp(bl - b)).astype(BF16)
                upd = lax.dot_general(vv.astype(BF16), kh, (((0,), (0,)), ((), ())), preferred_element_type=F32)
                state[hh] = st * jnp.exp(bl) + upd
                o_ref[rows, cols] = o
                n = o * lax.rsqrt(jnp.mean(o * o, axis=-1, keepdims=True) + RMS_EPS)
                cat_ref[rows, cols] = (n * gv * (gz * jax.nn.sigmoid(gz))).astype(BF16)
            return carry

        lax.fori_loop(0, ncb, chunk, 0)

    def col(k):
        return pl.BlockSpec((tb, hb * HEAD), lambda h, i: (i, base + k * nh + h))

    vec = pl.BlockSpec((1, hb * HEAD), lambda h, i: (0, h))
    cat_out, o_raw, states = pl.pallas_call(
        body, name=name, grid=(nh, S // tb),
        in_specs=[col(0), col(1), col(2), col(3), vec, vec, pl.BlockSpec(memory_space=pl.ANY)],
        out_specs=[pl.BlockSpec((tb, hb * HEAD), lambda h, i: (i, cat_cols - nh + h)),
                   pl.BlockSpec((tb, hb * HEAD), lambda h, i: (i, h)),
                   pl.BlockSpec((hb, ncb, HEAD, HEAD), lambda h, i: (h, i, 0, 0))],
        out_shape=[jax.ShapeDtypeStruct(cat.shape, BF16), jax.ShapeDtypeStruct((S, W), F32),
                   jax.ShapeDtypeStruct((HH, S // C, HEAD, HEAD), BF16)],
        scratch_shapes=[pltpu.VMEM((hb, HEAD, HEAD), F32)],
        input_output_aliases={6: 0},
        compiler_params=_cp(("parallel", "arbitrary")))(proj, proj, proj, proj, lb.reshape(1, W), gain.reshape(1, W), cat)
    return cat_out, o_raw, states


def _hg_bwd(proj, dcat, o_raw, states, lb, gain, HH, name):
    S, NC = proj.shape
    W = HH * HEAD
    C = HG_CHUNK
    hb = min(HG_HEADS_PER_STEP, HH)
    nh = HH // hb
    tb = _tile(S, 2048 // hb, C)
    ncb = tb // C
    nt = S // tb
    base = 3 * nh
    dcat_cols = dcat.shape[1] // (hb * HEAD)

    def body(fz_ref, iv_ref, qz_ref, gz_ref, dc_ref, o_ref, st_ref, lb_ref, g_ref,
             dfz_ref, div_ref, dqz_ref, dgz_ref, dlb_ref, dg_ref, dstate):
        @pl.when(pl.program_id(1) == 0)
        def _():
            dstate[...] = jnp.zeros_like(dstate)
            dlb_ref[...] = jnp.zeros_like(dlb_ref)
            dg_ref[...] = jnp.zeros_like(dg_ref)

        tri_lo, tri_up = _tri(C, False), _tri(C, True)
        row = lax.broadcasted_iota(jnp.int32, (C, 1), 0)
        row8 = lax.broadcasted_iota(jnp.int32, (8, 1), 0)

        def chunk(cc, carry):
            c = ncb - 1 - cc
            rows = pl.ds(pl.multiple_of(c * C, C), C)
            for hh in range(hb):
                cols = slice(hh * HEAD, (hh + 1) * HEAD)
                lbv, gv = lb_ref[:, cols], g_ref[:, cols]
                fz, vv, qz, gz = fz_ref[rows, cols], iv_ref[rows, cols], qz_ref[rows, cols], gz_ref[rows, cols]
                sig, f, lf, kk, qq, sq = _hg_gates(fz, qz, lbv)
                b = jnp.dot(tri_lo, lf, preferred_element_type=F32, precision=lax.Precision.HIGHEST)
                bl = b[C - 1:C, :]
                eb, ebl = jnp.exp(b), jnp.exp(bl)
                ekb = jnp.exp(bl - b)
                qh, kh = qq * eb, kk * ekb
                o = o_ref[rows, cols]
                dhg = dc_ref[rows, cols]
                sgz = jax.nn.sigmoid(gz)
                silu_g = gz * sgz
                rr = lax.rsqrt(jnp.mean(o * o, axis=-1, keepdims=True) + RMS_EPS)
                nrm = o * rr
                dpre = dhg * silu_g
                dgz_ref[rows, cols] = (dhg * nrm * gv * (sgz * (1.0 + gz * (1.0 - sgz)))).astype(BF16)
                dg_ref[:, cols] += jnp.sum(dpre * nrm, axis=0, keepdims=True)
                dn = dpre * gv
                do = rr * (dn - nrm * jnp.mean(dn * nrm, axis=-1, keepdims=True))
                dob = do.astype(BF16)
                st0 = st_ref[hh, c]
                dst1 = dstate[hh]
                dst1b = dst1.astype(BF16)
                dqh = jnp.dot(dob, st0, preferred_element_type=F32)
                dv = lax.dot_general(kh.astype(BF16), dst1b, (((1,), (1,)), ((), ())), preferred_element_type=F32)
                dkh = jnp.dot(vv.astype(BF16), dst1b, preferred_element_type=F32)
                dstate[hh] = dst1 * ebl + lax.dot_general(dob, qh.astype(BF16), (((0,), (0,)), ((), ())),
                                                          preferred_element_type=F32)
                extra = (jnp.sum(dkh * kh, axis=0, keepdims=True)
                         + ebl * jnp.sum(st0.astype(F32) * dst1, axis=0, keepdims=True))
                dq = dqh * eb
                dk = dkh * ekb
                starts = list(range(0, C, 8))
                b_t, q_t, do_t = ([v[t0:t0 + 8, :] for t0 in starts] for v in (b, qq, do))
                dq_t = [dq[t0:t0 + 8, :] for t0 in starts]
                dk_rows, dv_rows = [], []
                for s in range(C):
                    ks, vs, bs = kk[s:s + 1, :], vv[s:s + 1, :], b[s:s + 1, :]
                    dk_s = dv_s = None
                    for ti, t0 in enumerate(starts):
                        if t0 + 8 <= s:
                            continue
                        diff = b_t[ti] - bs
                        if s >= t0:
                            diff = jnp.minimum(diff, 0.0)
                        dec = jnp.exp(diff)
                        da = jnp.sum(do_t[ti] * vs, axis=-1, keepdims=True)
                        a = jnp.sum(q_t[ti] * dec * ks, axis=-1, keepdims=True)
                        if s >= t0:
                            keep = row8 >= s - t0
                            da, a = jnp.where(keep, da, 0.0), jnp.where(keep, a, 0.0)
                        gd = da * dec
                        dq_t[ti] = dq_t[ti] + gd * ks
                        pk = jnp.sum(gd * q_t[ti], axis=0, keepdims=True)
                        pv = jnp.sum(a * do_t[ti], axis=0, keepdims=True)
                        dk_s = pk if dk_s is None else dk_s + pk
                        dv_s = pv if dv_s is None else dv_s + pv
                    dk_rows.append(dk_s)
                    dv_rows.append(dv_s)
                dq = jnp.concatenate(dq_t, axis=0)
                dk = dk + jnp.concatenate(dk_rows, axis=0)
                dv = dv + jnp.concatenate(dv_rows, axis=0)
                db = qq * dq - kk * dk + jnp.where(row == C - 1, extra, 0.0)
                dlf = jnp.dot(tri_up, db, preferred_element_type=F32, precision=lax.Precision.HIGHEST)
                df = dlf / f - dk
                dfz_ref[rows, cols] = (df * (1.0 - lbv) * sig * (1.0 - sig)).astype(BF16)
                dlb_ref[:, cols] += jnp.sum(df * (1.0 - sig), axis=0, keepdims=True)
                dqz_ref[rows, cols] = (dq * (sq * (1.0 + qz * (1.0 - sq)))).astype(BF16)
                div_ref[rows, cols] = dv.astype(BF16)
            return carry

        lax.fori_loop(0, ncb, chunk, 0)

    def col(k):
        return pl.BlockSpec((tb, hb * HEAD), lambda h, i: (nt - 1 - i, base + k * nh + h))

    ocol = pl.BlockSpec((tb, hb * HEAD), lambda h, i: (nt - 1 - i, h))
    vec = pl.BlockSpec((1, hb * HEAD), lambda h, i: (0, h))
    outs = pl.pallas_call(
        body, name=name, grid=(nh, nt),
        in_specs=[col(0), col(1), col(2), col(3),
                  pl.BlockSpec((tb, hb * HEAD), lambda h, i: (nt - 1 - i, dcat_cols - nh + h)),
                  pl.BlockSpec((tb, hb * HEAD), lambda h, i: (nt - 1 - i, h)),
                  pl.BlockSpec((hb, ncb, HEAD, HEAD), lambda h, i: (h, nt - 1 - i, 0, 0)), vec, vec],
        out_specs=[ocol, ocol, ocol, ocol, vec, vec],
        out_shape=[jax.ShapeDtypeStruct((S, W), BF16)] * 4 + [jax.ShapeDtypeStruct((1, W), F32)] * 2,
        scratch_shapes=[pltpu.VMEM((hb, HEAD, HEAD), F32)],
        compiler_params=_cp(("parallel", "arbitrary")))(proj, proj, proj, proj, dcat, o_raw, states,
                                                        lb.reshape(1, W), gain.reshape(1, W))
    return outs


def _lb_all(lb_logits):
    p = jax.nn.softmax(lb_logits.astype(F32), axis=0)
    return jnp.cumsum(p, axis=0) - p


def _local_step(x, mem, target, rel_bias, lb_logits, gains, hg_norm, comm):
    S, D = x.shape
    L = gains.shape[0]
    H = (D // 2) // HEAD
    lb = _lb_all(lb_logits)
    tables = [_branch_tables(r) for _, r in BRANCHES]
    onehots = [jnp.where(ok[None], jax.nn.one_hot(bk, REL_BUCKETS, dtype=F32, axis=0), 0.0)
               .reshape(REL_BUCKETS, 2 * Q_BLOCK * Q_BLOCK) for bk, ok in tables]
    bias = [jnp.where(ok[None], jnp.dot(rel_bias.astype(F32).T, oh, precision=lax.Precision.HIGHEST)
                      .reshape(H, Q_BLOCK, 2 * Q_BLOCK), NEG_INF) for oh, (_, ok) in zip(onehots, tables)]

    saved = []
    _, h = _norm_fwd(x, None, None, gains[0, 0], "norm_first")
    xl = x
    for l in range(L):
        g = gains[l]
        w, token = comm.weights(l, xl)
        proj = _mm_nn(h, w["in"], F32, "mm_in", after=token)
        outs, lses = [], []
        for gi, (_, r) in enumerate(BRANCHES):
            o, ls = _attn_branch_fwd(proj, bias[gi], r, H, f"attn_fwd_d{r}")
            outs.append(o)
            lses.append(ls)
        cat, o_attn, lse_tot = _attn_merge(outs, lses, D, "attn_merge")
        cat, o_raw, states = _hg_fwd(proj, cat, lb[l], hg_norm[l], H, "hgrn_fwd")
        mix = _mm_nn(cat, w["out"], F32, "mm_out")
        x1, hc = _norm_fwd(xl, mix, g[1], g[2], "norm_post_pre")
        _, mn = _norm_fwd(mem, None, None, g[3], "norm_mem")
        cq = _mm_nn(hc, w["cq"], BF16, "mm_cq")
        ckv = _mm_nn(mn, w["ckv"], BF16, "mm_ckv")
        co = _cross_fwd(cq, ckv, "cross_fwd")
        cop = _mm_nn(co, w["co"], F32, "mm_co")
        x2, hf = _norm_fwd(x1, cop, g[4], g[5], "norm_post_pre")
        gu = _mm_nn(hf, w["gu"], F32, "mm_gu")
        act = _swiglu_fwd(gu, "swiglu_fwd")
        y = _mm_nn(act, w["down"], F32, "mm_down")
        g_next = gains[l + 1, 0] if l + 1 < L else None
        x3, h_next = _norm_fwd(x2, y, g[6], g_next, "norm_post_pre" if l + 1 < L else "norm_post")
        saved.append(dict(x0=xl, h0=h, proj=proj, cat=cat, o_attn=o_attn, lse_tot=lse_tot, o_raw=o_raw, states=states,
                          mix=mix, x1=x1, hc=hc, mn=mn, cq=cq, ckv=ckv, co=co, cop=cop, x2=x2, hf=hf, gu=gu, act=act,
                          y=y, x3=x3, w=w))
        comm.layer_done(l, x3)
        xl, h = x3, h_next

    loss, dres = _loss_and_grad(xl, target)

    dh_next = bwd_token = None
    d_gains = [[None] * 7 for _ in range(L)]
    d_lb = [None] * L
    d_hg = [None] * L
    d_bias = [jnp.zeros((H, Q_BLOCK, 2 * Q_BLOCK), F32) for _ in BRANCHES]
    for l in reversed(range(L)):
        g, sv = gains[l], saved[l]
        w = sv["w"]
        g_next = gains[l + 1, 0] if l + 1 < L else None
        t3, dy, dg_next, dg6 = _norm_bwd(dres, dh_next, sv["x3"], sv["y"], g_next, g[6], "norm_bwd")
        if l + 1 < L:
            d_gains[l + 1][0] = dg_next
        d_gains[l][6] = dg6
        gw = {}
        gw["down"] = _mm_tn(sv["act"], dy, 1, BF16, "mm_dw_down")
        dact = _mm_nt(dy, w["down"], F32, "mm_dx_down", after=bwd_token)
        dgu = _swiglu_bwd(sv["gu"], dact, "swiglu_bwd")
        gw["gu"] = _mm_tn(sv["hf"], dgu, w["gu"].shape[0], BF16, "mm_dw_gu")
        dhf = _mm_nt(dgu, w["gu"], F32, "mm_dx_gu")
        t2, dcop, d_gains[l][5], d_gains[l][4] = _norm_bwd(t3, dhf, sv["x2"], sv["cop"], g[5], g[4], "norm_bwd")
        gw["co"] = _mm_tn(sv["co"], dcop, 1, BF16, "mm_dw_sq")
        dco = _mm_nt(dcop, w["co"], F32, "mm_dx_sq")
        dcq, dckv = _cross_bwd(sv["cq"], sv["ckv"], dco, "cross_bwd")
        gw["cq"] = _mm_tn(sv["hc"], dcq, 1, BF16, "mm_dw_sq")
        dhc = _mm_nt(dcq, w["cq"], F32, "mm_dx_sq")
        gw["ckv"] = _mm_tn(sv["mn"], dckv, w["ckv"].shape[0], BF16, "mm_dw_ckv")
        dmn = _mm_nt(dckv, w["ckv"], F32, "mm_dx_ckv")
        _, _, d_gains[l][3], _ = _norm_bwd(None, dmn, mem, None, g[3], None, "norm_bwd_mem")
        t1, dmix, d_gains[l][2], d_gains[l][1] = _norm_bwd(t2, dhc, sv["x1"], sv["mix"], g[2], g[1], "norm_bwd")
        gw["out"] = _mm_tn(sv["cat"], dmix, 1, BF16, "mm_dw_sq")
        dcat = _mm_nt(dmix, w["out"], F32, "mm_dx_sq")
        delta = _attn_delta(dcat, sv["o_attn"], H, "attn_delta")
        dqkv = []
        for gi, (_, r) in enumerate(BRANCHES):
            dq, dk, dv, db = _attn_branch_bwd(sv["proj"], dcat, sv["lse_tot"], delta, bias[gi], r, H, f"attn_bwd_d{r}")
            dqkv.append((dq, dk, dv))
            d_bias[gi] = d_bias[gi] + db
        dfz, div, dqz, dgz, dlb_l, dhg_l = _hg_bwd(sv["proj"], dcat, sv["o_raw"], sv["states"], lb[l], hg_norm[l], H,
                                                   "hgrn_bwd")
        d_lb[l], d_hg[l] = dlb_l[0], dhg_l[0]
        dproj = _assemble_dproj(dqkv, [dfz, div, dqz, dgz], "assemble_dproj")
        gw["in"] = _mm_tn(sv["h0"], dproj, w["in"].shape[0], BF16, "mm_dw_in")
        dh_next = _mm_nt(dproj, w["in"], F32, "mm_dx_in")
        dres = t1
        bwd_token = comm.grads(l, gw, gw["in"])
    grad_x, _, d_gains[0][0], _ = _norm_bwd(dres, dh_next, x, None, gains[0, 0], None, "norm_bwd_first")

    d_rel_bias = _bias_grad(d_bias, onehots, H).T
    d_gains = jnp.stack([jnp.stack(row) for row in d_gains])
    return loss, grad_x, d_rel_bias, jnp.stack(d_lb), jnp.stack(d_hg), d_gains


def _place():
    return lax.axis_index("x"), lax.axis_index("y"), lax.axis_index("c")


def _all_gather(shards, name):
    n = len(shards)
    HBM = pl.BlockSpec(memory_space=pltpu.HBM)

    def body(*refs):
        ins, outs = refs[:n], refs[n:2 * n]
        send_sems, recv_sems, local_sems = refs[2 * n:]
        x, y, c = _place()
        me, sibling = (x, y, c), (x, y, 1 - c)
        chips = [(1 - x, y), (x, 1 - y), (1 - x, 1 - y)]

        def copy(a, k, block, to, own=False):
            dst = outs[a].at[4 * block[0] + 2 * block[1] + block[2]]
            return pltpu.make_async_remote_copy(
                src_ref=ins[a] if own else dst, dst_ref=dst, send_sem=send_sems.at[7 * a + k],
                recv_sem=recv_sems.at[7 * a + k], device_id=to, device_id_type=MESH)

        mine = [pltpu.make_async_copy(ins[a], outs[a].at[4 * x + 2 * y + c], local_sems.at[a]) for a in range(n)]
        for cp in mine:
            cp.start()
        first = []
        for a in range(n):
            first.append(copy(a, 0, me, sibling, own=True))
            first += [copy(a, 1 + j, me, (*chip, c), own=True) for j, chip in enumerate(chips)]
        for cp in first:
            cp.start()
        passed = []
        for j, chip in enumerate(chips):
            for a in range(n):
                copy(a, 1 + j, (*chip, c), me).wait_recv()
                fwd = copy(a, 4 + j, (*chip, c), sibling)
                fwd.start()
                passed.append(fwd)
        for a in range(n):
            copy(a, 0, sibling, me).wait_recv()
            for j, chip in enumerate(chips):
                copy(a, 4 + j, (*chip, 1 - c), me).wait_recv()
        for cp in first + passed:
            cp.wait_send()
        for cp in mine:
            cp.wait()

    return pl.pallas_call(
        body, name=name, in_specs=[HBM] * n, out_specs=[HBM] * n,
        out_shape=[jax.ShapeDtypeStruct((N_DEV,) + s.shape, s.dtype) for s in shards],
        scratch_shapes=[pltpu.SemaphoreType.DMA((7 * n,)), pltpu.SemaphoreType.DMA((7 * n,)),
                        pltpu.SemaphoreType.DMA((n,))],
    )(*shards)


def _exchange_sibling(grads, name):
    n = len(grads)
    HBM = pl.BlockSpec(memory_space=pltpu.HBM)

    def body(*refs):
        ins, outs = refs[:n], refs[n:2 * n]
        send_sems, recv_sems = refs[2 * n:]
        x, y, c = _place()
        sibling = (x, y, 1 - c)
        for a in range(n):
            for q in range(4):
                pltpu.make_async_remote_copy(
                    src_ref=ins[a].at[2 * q + 1 - c], dst_ref=outs[a].at[q], send_sem=send_sems.at[a],
                    recv_sem=recv_sems.at[a], device_id=sibling, device_id_type=MESH).start()
        for a in range(n):
            pltpu.make_async_remote_copy(
                src_ref=ins[a].at[pl.ds(0, 4)], dst_ref=outs[a], send_sem=send_sems.at[a], recv_sem=recv_sems.at[a],
                device_id=sibling, device_id_type=MESH).wait()

    return pl.pallas_call(
        body, name=name, in_specs=[HBM] * n, out_specs=[HBM] * n,
        out_shape=[jax.ShapeDtypeStruct((4,) + g.shape[1:], g.dtype) for g in grads],
        scratch_shapes=[pltpu.SemaphoreType.DMA((n,)), pltpu.SemaphoreType.DMA((n,))],
    )(*grads)


def _exchange_chips(parts, name):
    n = len(parts)
    HBM = pl.BlockSpec(memory_space=pltpu.HBM)

    def body(*refs):
        ins, outs = refs[:n], refs[n:2 * n]
        send_sems, recv_sems = refs[2 * n:]
        x, y, c = _place()
        chips = [(1 - x, y), (x, 1 - y), (1 - x, 1 - y)]
        copies = []
        for a in range(n):
            for j, chip in enumerate(chips):
                cp = pltpu.make_async_remote_copy(
                    src_ref=ins[a].at[2 * chip[0] + chip[1]], dst_ref=outs[a].at[j], send_sem=send_sems.at[3 * a + j],
                    recv_sem=recv_sems.at[3 * a + j], device_id=(*chip, c), device_id_type=MESH)
                cp.start()
                copies.append(cp)
        for cp in copies:
            cp.wait()

    return pl.pallas_call(
        body, name=name, in_specs=[HBM] * n, out_specs=[HBM] * n,
        out_shape=[jax.ShapeDtypeStruct((3,) + p.shape[1:], p.dtype) for p in parts],
        scratch_shapes=[pltpu.SemaphoreType.DMA((3 * n,)), pltpu.SemaphoreType.DMA((3 * n,))],
    )(*parts)


def _others(x, y, c):
    out = []
    for k in range(1, N_DEV):
        px = 1 - x if (k >> 2) & 1 else x
        py = 1 - y if (k >> 1) & 1 else y
        pc = 1 - c if k & 1 else c
        out.append(((px, py, pc), 4 * px + 2 * py + pc))
    return out


_HBM_SPEC = pl.BlockSpec(memory_space=pltpu.HBM)
_SEM_SPEC = pl.BlockSpec(memory_space=pltpu.SEMAPHORE)
_EFFECT = pltpu.SideEffectType.DATAFLOW_SIDE_EFFECTING


def _copies_start(srcs, lands, after, src_block, name):
    n = len(srcs)

    def body(*refs):
        src_refs, land_refs = refs[:n], refs[n:2 * n]
        send_sems, recv_sems = refs[2 * n + 1], refs[2 * n + 2]
        token = refs[-1]
        x, y, c = _place()
        mine = 4 * x + 2 * y + c
        for a in range(n):
            for k, (peer, block) in enumerate(_others(x, y, c)):
                pltpu.make_async_remote_copy(
                    src_ref=src_refs[a].at[block] if src_block else src_refs[a],
                    dst_ref=land_refs[a].at[k] if src_block else land_refs[a].at[mine],
                    send_sem=send_sems.at[7 * a + k], recv_sem=recv_sems.at[7 * a + k],
                    device_id=peer, device_id_type=MESH).start()
        token[...] = jnp.zeros_like(token)

    outs = pl.pallas_call(
        body, name=name,
        out_shape=(pltpu.SemaphoreType.DMA((7 * n,)), pltpu.SemaphoreType.DMA((7 * n,)),
                   *[pltpu.HBM(s.shape, s.dtype) for s in srcs], *[pltpu.HBM(t.shape, t.dtype) for t in lands],
                   jax.ShapeDtypeStruct((8, 128), F32)),
        in_specs=[_HBM_SPEC] * (2 * n) + [pl.BlockSpec(memory_space=pl.ANY)],
        out_specs=(_SEM_SPEC, _SEM_SPEC, *[_HBM_SPEC] * (2 * n), pl.BlockSpec(memory_space=pltpu.VMEM)),
        input_output_aliases={i: 2 + i for i in range(2 * n)},
        compiler_params=pltpu.CompilerParams(has_side_effects=_EFFECT),
    )(*[pltpu.with_memory_space_constraint(s, pltpu.HBM) for s in srcs],
      *[pltpu.with_memory_space_constraint(t, pltpu.HBM) for t in lands], after)
    return outs[0], outs[1], list(outs[2:2 + n]), list(outs[2 + n:2 + 2 * n]), outs[-1]


def _copies_wait(send_sems, recv_sems, srcs, lands, after, src_block, name):
    n = len(srcs)

    def body(*refs):
        src_refs, land_refs = refs[:n], refs[n:2 * n]
        send_sems, recv_sems = refs[2 * n], refs[2 * n + 1]
        x, y, c = _place()
        for a in range(n):
            for k, (peer, block) in enumerate(_others(x, y, c)):
                cp = pltpu.make_async_remote_copy(
                    src_ref=src_refs[a].at[block] if src_block else src_refs[a],
                    dst_ref=land_refs[a].at[k] if src_block else land_refs[a].at[block],
                    send_sem=send_sems.at[7 * a + k], recv_sem=recv_sems.at[7 * a + k],
                    device_id=peer, device_id_type=MESH)
                cp.wait_send()
                cp.wait_recv()

    outs = pl.pallas_call(
        body, name=name,
        out_shape=(*[pltpu.HBM(s.shape, s.dtype) for s in srcs], *[pltpu.HBM(t.shape, t.dtype) for t in lands]),
        in_specs=[_HBM_SPEC] * (2 * n) + [_SEM_SPEC, _SEM_SPEC, pl.BlockSpec(memory_space=pl.ANY)],
        out_specs=tuple([_HBM_SPEC] * (2 * n)),
        input_output_aliases={i: i for i in range(2 * n)},
        compiler_params=pltpu.CompilerParams(has_side_effects=_EFFECT),
    )(*srcs, *lands, send_sems, recv_sems, after)
    return list(outs[:n]), list(outs[n:])


def _pair_sum(grad, recv, c_idx, name):
    _, R, C = grad.shape
    tr = _tile(R, 512, 16)

    def body(c_ref, g_ref, p_ref, o_ref):
        del c_ref
        o_ref[...] = (g_ref[...].astype(F32) + p_ref[...].astype(F32)).astype(BF16)

    spec = pltpu.PrefetchScalarGridSpec(
        num_scalar_prefetch=1, grid=(4, R // tr),
        in_specs=[pl.BlockSpec((None, tr, C), lambda q, i, c: (2 * q + c[0], i, 0)),
                  pl.BlockSpec((None, tr, C), lambda q, i, c: (q, i, 0))],
        out_specs=pl.BlockSpec((None, tr, C), lambda q, i, c: (q, i, 0)))
    return pl.pallas_call(body, name=name, grid_spec=spec, out_shape=jax.ShapeDtypeStruct((4, R, C), BF16),
                          compiler_params=_cp(("parallel", "parallel")))(c_idx, grad, recv)


def _adam(w, g, m, v):
    m2 = ADAM_B1 * m + (1.0 - ADAM_B1) * g
    v2 = ADAM_B2 * v + (1.0 - ADAM_B2) * (g * g)
    m_hat = m2 / (1.0 - ADAM_B1 ** ADAM_STEP)
    v_hat = v2 / (1.0 - ADAM_B2 ** ADAM_STEP)
    return -ADAM_LR * (m_hat / (jnp.sqrt(v_hat) + ADAM_EPS) + ADAM_WD * w), m2, v2


def _reduce_update(part, recv, own_idx, w, m, v, layer, prev, name):
    L, R, C = w.shape
    nr = recv.shape[0]
    tr = _tile(R, 128, 16)
    has_prev = prev is not None

    def body(idx_ref, q_ref, *rest):
        del idx_ref
        r_refs, (w_ref, m_ref, v_ref) = rest[:nr], rest[nr:nr + 3]
        g_ref, d_ref, m2_ref, v2_ref = rest[-4:]
        g = q_ref[...].astype(F32)
        for r_ref in r_refs:
            g = g + r_ref[...].astype(F32)
        d, m2, v2 = _adam(w_ref[...], g, m_ref[...], v_ref[...])
        g_ref[...] = g
        d_ref[...] = d
        m2_ref[...] = m2
        v2_ref[...] = v2

    def rblk(j):
        return pl.BlockSpec((None, tr, C), lambda i, k: (j, i, 0))

    lay = pl.BlockSpec((None, tr, C), lambda i, k: (layer, i, 0))
    in_specs = [pl.BlockSpec((None, tr, C), lambda i, k: (k[0], i, 0))] + [rblk(j) for j in range(nr)] + [lay, lay, lay]
    ins = [part] + [recv] * nr + [w, m, v]
    aliases = {}
    if has_prev:
        in_specs += [pl.BlockSpec(memory_space=pl.ANY)] * 4
        ins += list(prev)
        aliases = {1 + len(ins) - 4 + t: t for t in range(4)}
    spec = pltpu.PrefetchScalarGridSpec(num_scalar_prefetch=1, grid=(R // tr,), in_specs=in_specs,
                                        out_specs=[lay, lay, lay, lay])
    return pl.pallas_call(body, name=name, grid_spec=spec, out_shape=[jax.ShapeDtypeStruct((L, R, C), F32)] * 4,
                          input_output_aliases=aliases, compiler_params=_cp(("parallel",)))(own_idx, *ins)


def _sum_devices(gathered):
    n, R, C = gathered.shape

    def body(g_ref, o_ref):
        acc = g_ref[0]
        for d in range(1, n):
            acc = acc + g_ref[d]
        o_ref[...] = acc

    return pl.pallas_call(body, name="sum_devices", out_shape=jax.ShapeDtypeStruct((R, C), F32))(gathered)


def _adam_small(w, g, m, v, name):
    shape = w.shape
    two = (-1, shape[-1])

    def body(w_ref, g_ref, m_ref, v_ref, d_ref, m2_ref, v2_ref):
        d, m2, v2 = _adam(w_ref[...], g_ref[...], m_ref[...], v_ref[...])
        d_ref[...] = d
        m2_ref[...] = m2
        v2_ref[...] = v2

    outs = pl.pallas_call(body, name=name, out_shape=[jax.ShapeDtypeStruct(w.reshape(two).shape, F32)] * 3)(
        w.reshape(two), g.reshape(two), m.reshape(two), v.reshape(two))
    return [o.reshape(shape) for o in outs]


_SHARDED = ("in", "out", "cq", "ckv", "co", "gu", "down")
_COLUMN_SHARDED = ("in", "ckv", "gu")


def kernel(x, mem, rel_bias, lb_logits, norm_gains, w_in, hg_norm, w_out, w_cq, w_ckv, w_co, w_gate_up, w_down, loss_target, m_rel_bias, m_lb_logits, m_norm_gains, m_w_in, m_hg_norm, m_w_out, m_w_cq, m_w_ckv, m_w_co, m_w_gate_up, m_w_down, v_rel_bias, v_lb_logits, v_norm_gains, v_w_in, v_hg_norm, v_w_out, v_w_cq, v_w_ckv, v_w_co, v_w_gate_up, v_w_down):
    L = w_in.shape[0]
    D = x.shape[-1]
    ws = dict(zip(_SHARDED, (w_in, w_out, w_cq, w_ckv, w_co, w_gate_up, w_down)))
    ms = dict(zip(_SHARDED, (m_w_in, m_w_out, m_w_cq, m_w_ckv, m_w_co, m_w_gate_up, m_w_down)))
    vs = dict(zip(_SHARDED, (v_w_in, v_w_out, v_w_cq, v_w_ckv, v_w_co, v_w_gate_up, v_w_down)))
    px, py, pc = _place()
    dev = 4 * px + 2 * py + pc
    c_idx = jnp.reshape(pc, (1,)).astype(jnp.int32)
    chip_idx = jnp.reshape(2 * px + py, (1,)).astype(jnp.int32)
    dev_idx = jnp.reshape(dev, (1,)).astype(jnp.int32)

    def shards_of(l):
        return [ws[k][l].astype(BF16) for k in _SHARDED]

    def as_weights(gathered):
        return {k: g if k in _COLUMN_SHARDED else g.reshape(1, N_DEV * g.shape[1], g.shape[2])
                for k, g in zip(_SHARDED, gathered)}

    class Comm:
        def __init__(self):
            self.ready = {0: as_weights(_all_gather(shards_of(0), "gather_weights"))}
            self.gather = None
            self.scatter = None
            self.results = {k: None for k in _SHARDED}

        def weights(self, l, x_in):
            del x_in
            mine, token = self.ready.pop(l), None
            if l + 1 < L:
                shards = shards_of(l + 1)
                lands = [lax.empty((N_DEV,) + s.shape, s.dtype) for s in shards]
                *self.gather, token = _copies_start(shards, lands, mine["in"], False, "gather_start")
            return mine, token

        def layer_done(self, l, x_out):
            if l + 1 < L:
                send_sems, recv_sems, shards, lands = self.gather
                shards, lands = _copies_wait(send_sems, recv_sems, shards, lands, x_out, False, "gather_wait")
                full = [lax.dynamic_update_slice_in_dim(t, s[None], dev, axis=0) for s, t in zip(shards, lands)]
                self.ready[l + 1] = as_weights(full)

        def update(self, l, parts, recvs, own_idx):
            for k, part, recv in zip(_SHARDED, parts, recvs):
                self.results[k] = _reduce_update(part, recv, own_idx, ws[k], ms[k], vs[k], l, self.results[k],
                                                 "reduce_update")

        def grads(self, l, gw, after):
            order = dev_idx
            if self.scatter is not None:
                lp, send_sems, recv_sems, srcs, lands = self.scatter
                srcs, lands = _copies_wait(send_sems, recv_sems, srcs, lands, after, True, "scatter_wait")
                self.update(lp, srcs, lands, dev_idx)
                order = lands[0]
            full = [gw[k].reshape((N_DEV,) + ws[k].shape[1:]) for k in _SHARDED]
            if l > 0:
                lands = [lax.empty((N_DEV - 1,) + g.shape[1:], g.dtype) for g in full]
                send_sems, recv_sems, srcs, lands, token = _copies_start(full, lands, order, True, "scatter_start")
                self.scatter = (l, send_sems, recv_sems, srcs, lands)
                return token
            else:
                from_sibling = _exchange_sibling(full, "scatter_sibling")
                parts = [_pair_sum(g, r, c_idx, "pair_sum") for g, r in zip(full, from_sibling)]
                self.update(l, parts, _exchange_chips(parts, "scatter_chips"), chip_idx)

    comm = Comm()
    gains = _all_gather([norm_gains], "gather_gains")[0]
    gains = jnp.transpose(gains, (1, 2, 0, 3)).reshape(L, 7, D)

    loss, grad_x, d_rel_bias, d_lb, d_hg, d_gains = _local_step(
        x[0], mem[0], loss_target[0], rel_bias, lb_logits, gains, hg_norm, comm)
    results = comm.results

    pieces = [jnp.reshape(loss, (1,)), d_rel_bias.reshape(-1), d_lb.reshape(-1), d_hg.reshape(-1), d_gains.reshape(-1)]
    sizes = [p.shape[0] for p in pieces]
    total = sum(sizes)
    rows = -(-total // 1024) * 8
    pack = jnp.pad(jnp.concatenate(pieces), (0, rows * 128 - total)).reshape(rows, 128)
    summed = _sum_devices(_all_gather([pack], "gather_small")[0]).reshape(-1)
    offs = [sum(sizes[:i]) for i in range(len(sizes))]
    loss = summed[0]
    g_rel_bias = summed[offs[1]:offs[1] + sizes[1]].reshape(rel_bias.shape)
    g_lb_all = summed[offs[2]:offs[2] + sizes[2]].reshape(lb_logits.shape)
    g_hg = summed[offs[3]:offs[3] + sizes[3]].reshape(hg_norm.shape)
    g_gains_full = summed[offs[4]:offs[4] + sizes[4]].reshape(L, 7, D)
    g_gains = lax.dynamic_slice_in_dim(g_gains_full, dev * (D // N_DEV), D // N_DEV, axis=2)
    g_lb = jax.vjp(_lb_all, lb_logits)[1](g_lb_all)[0]

    small = [(rel_bias, g_rel_bias, m_rel_bias, v_rel_bias), (lb_logits, g_lb, m_lb_logits, v_lb_logits),
             (norm_gains, g_gains, m_norm_gains, v_norm_gains), (hg_norm, g_hg, m_hg_norm, v_hg_norm)]
    upd = [_adam_small(w, g, m, v, "adam_small") for w, g, m, v in small]

    def ordered(small_vals, big_vals):
        return [small_vals[0], small_vals[1], small_vals[2], big_vals["in"], small_vals[3], big_vals["out"],
                big_vals["cq"], big_vals["ckv"], big_vals["co"], big_vals["gu"], big_vals["down"]]

    grads = ordered([s[1] for s in small], {k: results[k][0] for k in _SHARDED})
    deltas = ordered([u[0] for u in upd], {k: results[k][1] for k in _SHARDED})
    new_m = ordered([u[1] for u in upd], {k: results[k][2] for k in _SHARDED})
    new_v = ordered([u[2] for u in upd], {k: results[k][3] for k in _SHARDED})
    return (loss, grad_x[None], *grads, *deltas, *new_m, *new_v)
```

```python
import functools
import math

import jax
import jax.numpy as jnp
from jax import lax
from jax.experimental import pallas as pl
from jax.experimental.pallas import tpu as pltpu

F32 = jnp.float32
BF16 = jnp.bfloat16
MESH = pl.DeviceIdType.MESH

N_DEV = 8
HEAD = 128
CROSS_HEADS = 4
Q_BLOCK = 128
BRANCHES = ((128, 1), (512, 4), (2048, 16))
REL_BUCKETS = 32
REL_MAX_DIST = 2048
HG_CHUNK = 16
HG_HEADS_PER_STEP = 8
RMS_EPS = 1e-6
NEG_INF = -1e30
ADAM_LR, ADAM_B1, ADAM_B2, ADAM_EPS, ADAM_WD, ADAM_STEP = 0.001, 0.9, 0.999, 1e-08, 0.01, 10
VMEM_LIMIT = 48 * 1024 * 1024
NT_K_PER_STEP = 3584


def _tile(n, target, mult):
    best = None
    t = mult
    while t <= min(n, target):
        if n % t == 0:
            best = t
        t += mult
    return best if best is not None else n


def _cp(sem, vmem=VMEM_LIMIT):
    return pltpu.CompilerParams(dimension_semantics=sem, vmem_limit_bytes=vmem)


def _mm_nn(a, b3, out_dtype, name, after=None):
    M, K = a.shape
    J, K2, Nb = b3.shape
    assert K == K2
    tm, tn, tk = _tile(M, 1024, 8), _tile(Nb, 1408, 128), _tile(K, 2816, 128)
    nn, nk = Nb // tn, K // tk
    extra = [] if after is None else [after]

    def body(a_ref, b_ref, *rest):
        o_ref, acc = rest[len(extra)], rest[len(extra) + 1:]
        prod = jnp.dot(a_ref[...].astype(BF16), b_ref[...].astype(BF16), preferred_element_type=F32)
        if nk == 1:
            o_ref[...] = prod.astype(o_ref.dtype)
        else:
            k = pl.program_id(2)

            @pl.when(k == 0)
            def _():
                acc[0][...] = prod

            @pl.when(k > 0)
            def _():
                acc[0][...] += prod

            @pl.when(k == nk - 1)
            def _():
                o_ref[...] = acc[0][...].astype(o_ref.dtype)

    return pl.pallas_call(
        body, name=name, grid=(M // tm, J * nn, nk),
        in_specs=[pl.BlockSpec((tm, tk), lambda i, n, k: (i, k)),
                  pl.BlockSpec((None, tk, tn), lambda i, n, k: (n // nn, k, n % nn))]
                 + [pl.BlockSpec(memory_space=pl.ANY)] * len(extra),
        out_specs=pl.BlockSpec((tm, tn), lambda i, n, k: (i, n)),
        out_shape=jax.ShapeDtypeStruct((M, J * Nb), out_dtype),
        scratch_shapes=[] if nk == 1 else [pltpu.VMEM((tm, tn), F32)],
        compiler_params=_cp(("parallel", "parallel", "arbitrary")),
    )(a, b3, *extra)


def _mm_nt(a, b3, out_dtype, name, after=None):
    extra = [] if after is None else [after]
    halves = a.ndim == 3
    M, Kt = (a.shape[1], 2 * a.shape[2]) if halves else a.shape
    J, N, Kb = b3.shape
    assert Kt == J * Kb
    tm, tn = _tile(M, 1024, 8), _tile(N, 1408, 128)
    tk = _tile(math.gcd(Kb, Kt // 2) if halves else Kb, 2048, 128)
    nkb = Kb // tk
    groups = J // 2 if halves and J > 1 else J
    jb = max(j for j in range(1, groups + 1) if groups % j == 0 and j * Kb <= NT_K_PER_STEP) if nkb == 1 else 1
    nk = (J // jb) * nkb

    def body(a_ref, b_ref, *rest):
        o_ref, acc = rest[len(extra)], rest[len(extra) + 1:]
        prod = None
        for j in range(jb):
            part = lax.dot_general(a_ref[:, j * tk:(j + 1) * tk].astype(BF16), b_ref[j].astype(BF16),
                                   (((1,), (1,)), ((), ())), preferred_element_type=F32)
            prod = part if prod is None else prod + part
        if nk == 1:
            o_ref[...] = prod.astype(o_ref.dtype)
        else:
            k = pl.program_id(2)

            @pl.when(k == 0)
            def _():
                acc[0][...] = prod

            @pl.when(k > 0)
            def _():
                acc[0][...] += prod

            @pl.when(k == nk - 1)
            def _():
                o_ref[...] = acc[0][...].astype(o_ref.dtype)

    return pl.pallas_call(
        body, name=name, grid=(M // tm, N // tn, nk),
        in_specs=[pl.BlockSpec((None, tm, jb * tk), lambda i, n, k: (k // (nk // 2), i, k % (nk // 2))) if halves
                  else pl.BlockSpec((tm, jb * tk), lambda i, n, k: (i, k)),
                  pl.BlockSpec((jb, tn, tk), lambda i, n, k: (k // nkb, n, k % nkb))]
                 + [pl.BlockSpec(memory_space=pl.ANY)] * len(extra),
        out_specs=pl.BlockSpec((tm, tn), lambda i, n, k: (i, n)),
        out_shape=jax.ShapeDtypeStruct((M, N), out_dtype),
        scratch_shapes=[] if nk == 1 else [pltpu.VMEM((tm, tn), F32)],
        compiler_params=_cp(("parallel", "parallel", "arbitrary")),
    )(a, b3, *extra)


def _mm_tn(a, b, J, out_dtype, name):
    S, M = a.shape
    halves = b.ndim == 3
    S2, Nt = (b.shape[1], 2 * b.shape[2]) if halves else b.shape
    assert S == S2 and Nt % J == 0
    Nb = Nt // J
    tm, tk = _tile(M, 1408, 128), _tile(S, 2048, 8)
    tn = _tile(math.gcd(Nb, Nt // 2) if halves else Nb, 1408, 128)
    nn, nk = Nb // tn, S // tk
    per_half = J * nn // 2
    b_spec = (pl.BlockSpec((None, tk, tn), lambda i, n, k: (n // per_half, k, n % per_half)) if halves
              else pl.BlockSpec((tk, tn), lambda i, n, k: (k, n)))

    def body(a_ref, b_ref, o_ref, *acc):
        prod = lax.dot_general(a_ref[...].astype(BF16), b_ref[...].astype(BF16), (((0,), (0,)), ((), ())),
                               preferred_element_type=F32)
        if nk == 1:
            o_ref[...] = prod.astype(o_ref.dtype)
        else:
            k = pl.program_id(2)

            @pl.when(k == 0)
            def _():
                acc[0][...] = prod

            @pl.when(k > 0)
            def _():
                acc[0][...] += prod

            @pl.when(k == nk - 1)
            def _():
                o_ref[...] = acc[0][...].astype(o_ref.dtype)

    return pl.pallas_call(
        body, name=name, grid=(M // tm, J * nn, nk),
        in_specs=[pl.BlockSpec((tk, tm), lambda i, n, k: (k, i)), b_spec],
        out_specs=pl.BlockSpec((None, tm, tn), lambda i, n, k: (n // nn, i, n % nn)),
        out_shape=jax.ShapeDtypeStruct((J, M, Nb), out_dtype),
        scratch_shapes=[] if nk == 1 else [pltpu.VMEM((tm, tn), F32)],
        compiler_params=_cp(("parallel", "parallel", "arbitrary")),
    )(a, b)


def _rms_scale(v):
    return lax.rsqrt(jnp.mean(v * v, axis=-1, keepdims=True) + RMS_EPS)


def _rms_bwd(dy, v, g):
    r = _rms_scale(v)
    vh = v * r
    u = dy * g
    dv = r * (u - vh * jnp.mean(u * vh, axis=-1, keepdims=True))
    return dv, dy * vh


def _fold8(t):
    R, D = t.shape
    return jnp.sum(t.reshape(R // 8, 8, D), axis=0)


def _norm_fwd(x, z, g_post, g_pre, name):
    S, D = x.shape
    tr = _tile(S, 256, 8)
    has_z, has_pre = z is not None, g_pre is not None

    def body(*refs):
        it = iter(refs)
        x_ref = next(it)
        z_ref = next(it) if has_z else None
        gpost_ref = next(it) if has_z else None
        gpre_ref = next(it) if has_pre else None
        xn_ref = next(it) if has_z else None
        h_ref = next(it) if has_pre else None
        xn = x_ref[...]
        if has_z:
            zz = z_ref[...]
            xn = xn + zz * _rms_scale(zz) * gpost_ref[...]
            xn_ref[...] = xn
        if has_pre:
            h_ref[...] = (xn * _rms_scale(xn) * gpre_ref[...]).astype(BF16)

    row = pl.BlockSpec((tr, D), lambda i: (i, 0))
    gain = pl.BlockSpec((1, D), lambda i: (0, 0))
    ins, specs, outs, ospecs = [x], [row], [], []
    if has_z:
        ins += [z, g_post.reshape(1, D)]
        specs += [row, gain]
        outs.append(jax.ShapeDtypeStruct((S, D), F32))
        ospecs.append(row)
    if has_pre:
        ins.append(g_pre.reshape(1, D))
        specs.append(gain)
        outs.append(jax.ShapeDtypeStruct((S, D), BF16))
        ospecs.append(row)
    res = pl.pallas_call(body, name=name, grid=(S // tr,), in_specs=specs, out_specs=ospecs, out_shape=outs,
                         compiler_params=_cp(("parallel",)))(*ins)
    res = list(res)
    xn = res.pop(0) if has_z else x
    h = res.pop(0) if has_pre else None
    return xn, h


def _norm_bwd(dres, dh, xn, z, g_pre, g_post, name):
    S, D = xn.shape
    tr = _tile(S, 256, 8)
    has_res, has_pre, has_z = dres is not None, dh is not None, z is not None

    def body(*refs):
        it = iter(refs)
        dres_ref = next(it) if has_res else None
        dh_ref = next(it) if has_pre else None
        xn_ref = next(it) if has_pre else None
        gpre_ref = next(it) if has_pre else None
        z_ref = next(it) if has_z else None
        gpost_ref = next(it) if has_z else None
        t_ref = next(it)
        dz_ref = next(it) if has_z else None
        dgpre_ref = next(it) if has_pre else None
        dgpost_ref = next(it) if has_z else None
        first = pl.program_id(0) == 0
        t = dres_ref[...] if has_res else None
        if has_pre:
            dv, dg = _rms_bwd(dh_ref[...].astype(F32), xn_ref[...], gpre_ref[...])
            t = dv if t is None else t + dv
            part = _fold8(dg)

            @pl.when(first)
            def _():
                dgpre_ref[...] = part

            @pl.when(jnp.logical_not(first))
            def _():
                dgpre_ref[...] += part
        t_ref[...] = t
        if has_z:
            dv, dg = _rms_bwd(t, z_ref[...], gpost_ref[...])
            dz_ref[...] = dv.astype(BF16)
            part2 = _fold8(dg)

            @pl.when(first)
            def _():
                dgpost_ref[...] = part2

            @pl.when(jnp.logical_not(first))
            def _():
                dgpost_ref[...] += part2

    row = pl.BlockSpec((tr, D), lambda i: (i, 0))
    gain = pl.BlockSpec((1, D), lambda i: (0, 0))
    accs = pl.BlockSpec((8, D), lambda i: (0, 0))
    ins, specs = [], []
    if has_res:
        ins.append(dres)
        specs.append(row)
    if has_pre:
        ins += [dh, xn, g_pre.reshape(1, D)]
        specs += [row, row, gain]
    if has_z:
        ins += [z, g_post.reshape(1, D)]
        specs += [row, gain]
    outs, ospecs = [jax.ShapeDtypeStruct((S, D), F32)], [row]
    if has_z:
        outs.append(jax.ShapeDtypeStruct((S, D), BF16))
        ospecs.append(row)
    if has_pre:
        outs.append(jax.ShapeDtypeStruct((8, D), F32))
        ospecs.append(accs)
    if has_z:
        outs.append(jax.ShapeDtypeStruct((8, D), F32))
        ospecs.append(accs)
    res = list(pl.pallas_call(body, name=name, grid=(S // tr,), in_specs=specs, out_specs=ospecs, out_shape=outs,
                              compiler_params=_cp(("arbitrary",)))(*ins))
    t = res.pop(0)
    dz = res.pop(0) if has_z else None
    dgpre = jnp.sum(res.pop(0), axis=0) if has_pre else None
    dgpost = jnp.sum(res.pop(0), axis=0) if has_z else None
    return t, dz, dgpre, dgpost


def _loss_and_grad(y, target):
    S, D = y.shape
    tr = _tile(S, 256, 8)

    def body(y_ref, t_ref, dy_ref, l_ref):
        err = y_ref[...] - t_ref[...]
        dy_ref[...] = err * (1.0 / D)
        part = 0.5 * jnp.sum(jnp.mean(err * err, axis=-1, keepdims=True), axis=0, keepdims=True)
        first = pl.program_id(0) == 0

        @pl.when(first)
        def _():
            l_ref[...] = jnp.broadcast_to(part, l_ref.shape)

        @pl.when(jnp.logical_not(first))
        def _():
            l_ref[...] += jnp.broadcast_to(part, l_ref.shape)

    row = pl.BlockSpec((tr, D), lambda i: (i, 0))
    dy, l = pl.pallas_call(
        body, name="loss_grad", grid=(S // tr,), in_specs=[row, row],
        out_specs=[row, pl.BlockSpec((8, 128), lambda i: (0, 0))],
        out_shape=[jax.ShapeDtypeStruct((S, D), F32), jax.ShapeDtypeStruct((8, 128), F32)],
        compiler_params=_cp(("arbitrary",)))(y, target)
    return l[0, 0], dy


def _mm_gate_up(a, w3, name):
    S, D = a.shape
    J, D2, Nb = w3.shape
    F = J * Nb // 2
    assert D == D2
    tm, tn = _tile(S, 512, 8), _tile(math.gcd(Nb, F), 1408, 128)
    nn, nf = Nb // tn, F // tn

    def body(a_ref, wg_ref, wu_ref, gu_ref, act_ref):
        av = a_ref[...]
        g = jnp.dot(av, wg_ref[...], preferred_element_type=F32)
        u = jnp.dot(av, wu_ref[...], preferred_element_type=F32)
        gu_ref[0] = g
        gu_ref[1] = u
        act_ref[...] = (g * jax.nn.sigmoid(g) * u).astype(BF16)

    return pl.pallas_call(
        body, name=name, grid=(nf, S // tm),
        in_specs=[pl.BlockSpec((tm, D), lambda n, i: (i, 0)),
                  pl.BlockSpec((None, D, tn), lambda n, i: (n // nn, 0, n % nn)),
                  pl.BlockSpec((None, D, tn), lambda n, i: ((n + nf) // nn, 0, (n + nf) % nn))],
        out_specs=[pl.BlockSpec((2, tm, tn), lambda n, i: (0, i, n)), pl.BlockSpec((tm, tn), lambda n, i: (i, n))],
        out_shape=[jax.ShapeDtypeStruct((2, S, F), F32), jax.ShapeDtypeStruct((S, F), BF16)],
        compiler_params=_cp(("parallel", "parallel")))(a, w3, w3)


def _mm_dx_down_swiglu(dy, wdown3, gu, name, after=None):
    S, D = dy.shape
    _, F, D2 = wdown3.shape
    assert D == D2
    tm, tn = _tile(S, 1024, 8), _tile(F, 512, 128)
    extra = [] if after is None else [after]

    def body(a_ref, b_ref, gu_ref, *rest):
        o_ref = rest[len(extra)]
        d = lax.dot_general(a_ref[...], b_ref[...], (((1,), (1,)), ((), ())), preferred_element_type=F32)
        g, u = gu_ref[0], gu_ref[1]
        sg = jax.nn.sigmoid(g)
        o_ref[0] = (d * u * (sg * (1.0 + g * (1.0 - sg)))).astype(BF16)
        o_ref[1] = (d * g * sg).astype(BF16)

    return pl.pallas_call(
        body, name=name, grid=(S // tm, F // tn),
        in_specs=[pl.BlockSpec((tm, D), lambda i, n: (i, 0)), pl.BlockSpec((None, tn, D), lambda i, n: (0, n, 0)),
                  pl.BlockSpec((2, tm, tn), lambda i, n: (0, i, n))] + [pl.BlockSpec(memory_space=pl.ANY)] * len(extra),
        out_specs=pl.BlockSpec((2, tm, tn), lambda i, n: (0, i, n)),
        out_shape=jax.ShapeDtypeStruct((2, S, F), BF16),
        compiler_params=_cp(("parallel", "parallel")))(dy, wdown3, gu, *extra)


def _cross_scores(q, k, scale):
    s = lax.dot_general(q, k, (((1,), (1,)), ((), ())), preferred_element_type=F32) * scale
    m = jnp.max(s, axis=-1, keepdims=True)
    e = jnp.exp(s - m)
    return e / jnp.sum(e, axis=-1, keepdims=True)


def _cross_fwd(cq, ckv, name):
    S, D = cq.shape
    Nm = ckv.shape[0]
    dh = D // CROSS_HEADS
    tq = _tile(S, 512, 8)
    scale = dh ** -0.5

    def body(q_ref, kv_ref, o_ref):
        for h in range(CROSS_HEADS):
            cols = slice(h * dh, (h + 1) * dh)
            p = _cross_scores(q_ref[:, cols], kv_ref[:, cols], scale)
            o_ref[:, cols] = jnp.dot(p.astype(BF16), kv_ref[:, D + h * dh:D + (h + 1) * dh],
                                     preferred_element_type=F32).astype(BF16)

    return pl.pallas_call(
        body, name=name, grid=(S // tq,),
        in_specs=[pl.BlockSpec((tq, D), lambda i: (i, 0)), pl.BlockSpec((Nm, 2 * D), lambda i: (0, 0))],
        out_specs=pl.BlockSpec((tq, D), lambda i: (i, 0)),
        out_shape=jax.ShapeDtypeStruct((S, D), BF16), compiler_params=_cp(("parallel",)))(cq, ckv)


def _cross_bwd(cq, ckv, do, name):
    S, D = cq.shape
    Nm = ckv.shape[0]
    dh = D // CROSS_HEADS
    tq = _tile(S, 512, 8)
    scale = dh ** -0.5

    def body(q_ref, kv_ref, do_ref, dq_ref, dkv_ref):
        first = pl.program_id(0) == 0

        @pl.when(first)
        def _():
            dkv_ref[...] = jnp.zeros_like(dkv_ref)

        for h in range(CROSS_HEADS):
            cols = slice(h * dh, (h + 1) * dh)
            vcols = slice(D + h * dh, D + (h + 1) * dh)
            q, k, v = q_ref[:, cols], kv_ref[:, cols], kv_ref[:, vcols]
            doh = do_ref[:, cols].astype(BF16)
            p = _cross_scores(q, k, scale)
            dp = lax.dot_general(doh, v, (((1,), (1,)), ((), ())), preferred_element_type=F32)
            ds = (p * (dp - jnp.sum(p * dp, axis=-1, keepdims=True)) * scale).astype(BF16)
            dq_ref[:, cols] = jnp.dot(ds, k, preferred_element_type=F32).astype(BF16)
            dkv_ref[:, cols] += lax.dot_general(ds, q, (((0,), (0,)), ((), ())), preferred_element_type=F32)
            dkv_ref[:, vcols] += lax.dot_general(p.astype(BF16), doh, (((0,), (0,)), ((), ())),
                                                 preferred_element_type=F32)

    return pl.pallas_call(
        body, name=name, grid=(S // tq,),
        in_specs=[pl.BlockSpec((tq, D), lambda i: (i, 0)), pl.BlockSpec((Nm, 2 * D), lambda i: (0, 0)),
                  pl.BlockSpec((tq, D), lambda i: (i, 0))],
        out_specs=[pl.BlockSpec((tq, D), lambda i: (i, 0)), pl.BlockSpec((Nm, 2 * D), lambda i: (0, 0))],
        out_shape=[jax.ShapeDtypeStruct((S, D), BF16), jax.ShapeDtypeStruct((Nm, 2 * D), F32)],
        compiler_params=_cp(("arbitrary",)))(cq, ckv, do)


def _rel_bucket(dist):
    max_exact = REL_BUCKETS // 2
    d_f = jnp.maximum(dist, 1).astype(F32)
    large = max_exact + (jnp.log(d_f / max_exact) / math.log(REL_MAX_DIST / max_exact)
                         * (REL_BUCKETS - max_exact)).astype(jnp.int32)
    large = jnp.minimum(large, REL_BUCKETS - 1)
    return jnp.where(dist < max_exact, dist, large)


def _branch_tables(dilation):
    qi = jnp.arange(Q_BLOCK)[:, None]
    kj = jnp.arange(2 * Q_BLOCK)[None, :]
    m = qi - kj + Q_BLOCK
    return _rel_bucket(jnp.maximum(m, 0) * dilation), (m >= 0) & (m <= Q_BLOCK)


def _attn_logits(q, kcat, bias, first_block, scale):
    s = lax.dot_general(q, kcat, (((1,), (1,)), ((), ())), preferred_element_type=F32) * scale + bias
    col = lax.broadcasted_iota(jnp.int32, s.shape, 1)
    return jnp.where(jnp.logical_and(first_block, col < Q_BLOCK), NEG_INF, s)


def _attn_branch_fwd(proj, bias, dilation, H, name):
    S, NC = proj.shape
    AW = H * HEAD
    r = dilation
    hb = H if r == 1 else 1
    G = Q_BLOCK * r
    ng, nh = S // G, H // hb
    scale = HEAD ** -0.5

    def body(q_ref, kp_ref, kc_ref, vp_ref, vc_ref, b_ref, o_ref, l_ref):
        first_block = pl.program_id(0) == 0
        h0 = pl.program_id(1) * hb
        for rho in range(r):
            rows = pl.ds(rho, Q_BLOCK, stride=r) if r > 1 else slice(None)
            for hh in range(hb):
                cols = slice(hh * HEAD, (hh + 1) * HEAD)
                q = q_ref[rows, cols].astype(BF16)
                kcat = jnp.concatenate([kp_ref[rows, cols], kc_ref[rows, cols]], axis=0).astype(BF16)
                vcat = jnp.concatenate([vp_ref[rows, cols], vc_ref[rows, cols]], axis=0).astype(BF16)
                s = _attn_logits(q, kcat, b_ref[h0 + hh], first_block, scale)
                m = jnp.max(s, axis=-1, keepdims=True)
                e = jnp.exp(s - m)
                den = jnp.sum(e, axis=-1, keepdims=True)
                o_ref[rows, cols] = jnp.dot(e.astype(BF16), vcat, preferred_element_type=F32) / den
                l_ref[rows, cols] = jnp.broadcast_to(m + jnp.log(den), (Q_BLOCK, HEAD))

    def blk(part, prev):
        if prev:
            return pl.BlockSpec((G, hb * HEAD), lambda n, h: (jnp.maximum(n - 1, 0), part * nh + h))
        return pl.BlockSpec((G, hb * HEAD), lambda n, h: (n, part * nh + h))

    out = pl.BlockSpec((G, hb * HEAD), lambda n, h: (n, h))
    return pl.pallas_call(
        body, name=name, grid=(ng, nh),
        in_specs=[blk(0, False), blk(1, True), blk(1, False), blk(2, True), blk(2, False),
                  pl.BlockSpec((H, Q_BLOCK, 2 * Q_BLOCK), lambda n, h: (0, 0, 0))],
        out_specs=[out, out],
        out_shape=[jax.ShapeDtypeStruct((S, AW), F32)] * 2,
        compiler_params=_cp(("parallel", "parallel")))(proj, proj, proj, proj, proj, bias)


def _attn_merge(outs, lses, width_out, name):
    S, AW = outs[0].shape
    tr = _tile(S, 256, 8)

    def body(o0, o1, o2, l0, l1, l2, cat_ref, om_ref, lt_ref):
        a, b, c = l0[...], l1[...], l2[...]
        m = jnp.maximum(jnp.maximum(a, b), c)
        ea, eb, ec = jnp.exp(a - m), jnp.exp(b - m), jnp.exp(c - m)
        tot = ea + eb + ec
        o = (ea * o0[...] + eb * o1[...] + ec * o2[...]) / tot
        om_ref[...] = o
        cat_ref[...] = o.astype(BF16)
        lt_ref[...] = m + jnp.log(tot)

    row = pl.BlockSpec((tr, AW), lambda i: (i, 0))
    return pl.pallas_call(
        body, name=name, grid=(S // tr,), in_specs=[row] * 6, out_specs=[row, row, row],
        out_shape=[jax.ShapeDtypeStruct((S, width_out), BF16), jax.ShapeDtypeStruct((S, AW), F32),
                   jax.ShapeDtypeStruct((S, AW), F32)],
        compiler_params=_cp(("parallel",)))(*outs, *lses)


def _attn_delta(dcat, o_attn, H, name):
    S, AW = o_attn.shape
    tr = _tile(S, 256, 8)

    def body(d_ref, o_ref, out_ref):
        for h in range(H):
            cols = slice(h * HEAD, (h + 1) * HEAD)
            out_ref[:, cols] = jnp.broadcast_to(
                jnp.sum(d_ref[:, cols] * o_ref[:, cols], axis=-1, keepdims=True), (tr, HEAD))

    row = pl.BlockSpec((tr, AW), lambda i: (i, 0))
    return pl.pallas_call(body, name=name, grid=(S // tr,), in_specs=[row, row], out_specs=row,
                          out_shape=jax.ShapeDtypeStruct((S, AW), F32),
                          compiler_params=_cp(("parallel",)))(dcat, o_attn)


def _attn_branch_bwd(proj, dcat, lse_tot, delta, bias, dilation, H, name):
    S, NC = proj.shape
    AW = H * HEAD
    r = dilation
    hb = H if r == 1 else 1
    G = Q_BLOCK * r
    ng, nh = S // G, H // hb
    scale = HEAD ** -0.5

    def body(q_ref, kp_ref, kc_ref, vp_ref, vc_ref, do_ref, l_ref, t_ref, b_ref,
             dq_ref, dk_ref, dv_ref, db_ref, ck_ref, cv_ref):
        hblk, n = pl.program_id(0), pl.program_id(1)
        h0 = hblk * hb

        @pl.when(jnp.logical_and(hblk == 0, n == 0))
        def _():
            db_ref[...] = jnp.zeros_like(db_ref)

        @pl.when(n == 0)
        def _():
            ck_ref[...] = jnp.zeros_like(ck_ref)
            cv_ref[...] = jnp.zeros_like(cv_ref)

        @pl.when(n < ng)
        def _():
            for rho in range(r):
                rows = pl.ds(rho, Q_BLOCK, stride=r) if r > 1 else slice(None)
                for hh in range(hb):
                    cols = slice(hh * HEAD, (hh + 1) * HEAD)
                    q = q_ref[rows, cols].astype(BF16)
                    kcat = jnp.concatenate([kp_ref[rows, cols], kc_ref[rows, cols]], axis=0).astype(BF16)
                    vcat = jnp.concatenate([vp_ref[rows, cols], vc_ref[rows, cols]], axis=0).astype(BF16)
                    doh = do_ref[rows, cols].astype(BF16)
                    s = _attn_logits(q, kcat, b_ref[h0 + hh], n == 0, scale)
                    lt = l_ref[rows, cols]
                    dl = t_ref[rows, cols]
                    p = jnp.exp(s - jnp.concatenate([lt, lt], axis=1))
                    dp = lax.dot_general(doh, vcat, (((1,), (1,)), ((), ())), preferred_element_type=F32)
                    ds = p * (dp - jnp.concatenate([dl, dl], axis=1))
                    db_ref[h0 + hh] += ds
                    dsb = (ds * scale).astype(BF16)
                    dq_ref[rows, cols] = jnp.dot(dsb, kcat, preferred_element_type=F32)
                    dkc = lax.dot_general(dsb, q, (((0,), (0,)), ((), ())), preferred_element_type=F32)
                    dvc = lax.dot_general(p.astype(BF16), doh, (((0,), (0,)), ((), ())), preferred_element_type=F32)
                    dk_ref[rows, cols] = ck_ref[rows, cols] + dkc[:Q_BLOCK]
                    dv_ref[rows, cols] = cv_ref[rows, cols] + dvc[:Q_BLOCK]
                    ck_ref[rows, cols] = dkc[Q_BLOCK:]
                    cv_ref[rows, cols] = dvc[Q_BLOCK:]

        @pl.when(n == ng)
        def _():
            dk_ref[...] = ck_ref[...]
            dv_ref[...] = cv_ref[...]

    last = ng - 1

    def cur(part):
        return pl.BlockSpec((G, hb * HEAD), lambda h, n: (jnp.minimum(n, last), part * nh + h))

    def prev(part):
        return pl.BlockSpec((G, hb * HEAD), lambda h, n: (jnp.clip(n - 1, 0, last), part * nh + h))

    table = pl.BlockSpec((H, Q_BLOCK, 2 * Q_BLOCK), lambda h, n: (0, 0, 0))
    return pl.pallas_call(
        body, name=name, grid=(nh, ng + 1),
        in_specs=[cur(0), prev(1), cur(1), prev(2), cur(2), cur(0), cur(0), cur(0), table],
        out_specs=[cur(0), prev(0), prev(0), table],
        out_shape=[jax.ShapeDtypeStruct((S, AW), F32)] * 3 + [jax.ShapeDtypeStruct((H, Q_BLOCK, 2 * Q_BLOCK), F32)],
        scratch_shapes=[pltpu.VMEM((G, hb * HEAD), F32), pltpu.VMEM((G, hb * HEAD), F32)],
        compiler_params=_cp(("arbitrary", "arbitrary")))(proj, proj, proj, proj, proj, dcat, lse_tot, delta, bias)


def _bias_grad(dbs, onehots, H):
    def body(d0, d1, d2, e0, e1, e2, o_ref):
        acc = jnp.zeros((H, REL_BUCKETS), F32)
        for d, e in ((d0, e0), (d1, e1), (d2, e2)):
            acc = acc + lax.dot_general(d[...], e[...], (((1,), (1,)), ((), ())), preferred_element_type=F32,
                                        precision=lax.Precision.HIGHEST)
        o_ref[...] = acc

    flat = [d.reshape(H, 2 * Q_BLOCK * Q_BLOCK) for d in dbs]
    return pl.pallas_call(body, name="bias_grad", out_shape=jax.ShapeDtypeStruct((H, REL_BUCKETS), F32),
                          compiler_params=pltpu.CompilerParams(vmem_limit_bytes=VMEM_LIMIT))(*flat, *onehots)


def _assemble_dproj(dqkv, dhg, name):
    S, AW = dqkv[0][0].shape
    tr = _tile(S, 256, 8)
    ins = [dqkv[g][c] for g in range(3) for c in range(3)] + list(dhg)

    def body(*refs):
        o_ref = refs[-1]
        j = pl.program_id(1)
        for c in range(3):
            @pl.when(j == c)
            def _(c=c):
                o_ref[...] = (refs[c][...] + refs[3 + c][...] + refs[6 + c][...]).astype(BF16)

        for k in range(4):
            @pl.when(j == 3 + k)
            def _(k=k):
                o_ref[...] = refs[9 + k][...]

    row = pl.BlockSpec((tr, AW), lambda i, j: (i, 0))
    return pl.pallas_call(
        body, name=name, grid=(S // tr, 7), in_specs=[row] * 13,
        out_specs=pl.BlockSpec((tr, AW), lambda i, j: (i, j)),
        out_shape=jax.ShapeDtypeStruct((S, 7 * AW), BF16),
        compiler_params=_cp(("parallel", "arbitrary")))(*ins)


def _tri(n, upper):
    r = lax.broadcasted_iota(jnp.int32, (n, n), 0)
    c = lax.broadcasted_iota(jnp.int32, (n, n), 1)
    return jnp.where(c >= r if upper else c <= r, 1.0, 0.0).astype(F32)


def _hg_gates(fz, qz, lb):
    sig = jax.nn.sigmoid(fz)
    f = lb + (1.0 - lb) * sig
    sq = jax.nn.sigmoid(qz)
    return sig, f, jnp.log(f), 1.0 - f, qz * sq, sq


def _hg_fwd(proj, cat, lb, gain, HH, name):
    S, NC = proj.shape
    W = HH * HEAD
    C = HG_CHUNK
    hb = min(HG_HEADS_PER_STEP, HH)
    nh = HH // hb
    tb = _tile(S, 2048 // hb, C)
    ncb = tb // C
    base = 3 * nh
    cat_cols = cat.shape[1] // (hb * HEAD)

    def body(fz_ref, iv_ref, qz_ref, gz_ref, lb_ref, g_ref, cat_in, cat_ref, o_ref, st_ref, state):
        del cat_in

        @pl.when(pl.program_id(1) == 0)
        def _():
            state[...] = jnp.zeros_like(state)

        tri = _tri(C, False)
        row8 = lax.broadcasted_iota(jnp.int32, (8, 1), 0)

        def chunk(c, carry):
            rows = pl.ds(pl.multiple_of(c * C, C), C)
            for hh in range(hb):
                cols = slice(hh * HEAD, (hh + 1) * HEAD)
                lbv, gv = lb_ref[:, cols], g_ref[:, cols]
                fz, vv, qz, gz = fz_ref[rows, cols], iv_ref[rows, cols], qz_ref[rows, cols], gz_ref[rows, cols]
                _, f, lf, kk, qq, _ = _hg_gates(fz, qz, lbv)
                b = jnp.dot(tri, lf, preferred_element_type=F32, precision=lax.Precision.HIGHEST)
                bl = b[C - 1:C, :]
                st = state[hh]
                stb = st.astype(BF16)
                st_ref[hh, c] = stb
                o = lax.dot_general((qq * jnp.exp(b)).astype(BF16), stb, (((1,), (1,)), ((), ())),
                                    preferred_element_type=F32)
                tiles = []
                for t0 in range(0, C, 8):
                    bt, qt, acc = b[t0:t0 + 8, :], qq[t0:t0 + 8, :], o[t0:t0 + 8, :]
                    for s in range(min(C, t0 + 8)):
                        diff = bt - b[s:s + 1, :]
                        if s >= t0:
                            diff = jnp.minimum(diff, 0.0)
                        a = jnp.sum(qt * jnp.exp(diff) * kk[s:s + 1, :], axis=-1, keepdims=True)
                        if s >= t0:
                            a = jnp.where(row8 >= s - t0, a, 0.0)
                        acc = acc + a * vv[s:s + 1, :]
                    tiles.append(acc)
                o = jnp.concatenate(tiles, axis=0)
                kh = (kk * jnp.exp(bl - b)).astype(BF16)
                upd = lax.dot_general(vv.astype(BF16), kh, (((0,), (0,)), ((), ())), preferred_element_type=F32)
                state[hh] = st * jnp.exp(bl) + upd
                o_ref[rows, cols] = o
                n = o * lax.rsqrt(jnp.mean(o * o, axis=-1, keepdims=True) + RMS_EPS)
                cat_ref[rows, cols] = (n * gv * (gz * jax.nn.sigmoid(gz))).astype(BF16)
            return carry

        lax.fori_loop(0, ncb, chunk, 0)

    def col(k):
        return pl.BlockSpec((tb, hb * HEAD), lambda h, i: (i, base + k * nh + h))

    vec = pl.BlockSpec((1, hb * HEAD), lambda h, i: (0, h))
    cat_out, o_raw, states = pl.pallas_call(
        body, name=name, grid=(nh, S // tb),
        in_specs=[col(0), col(1), col(2), col(3), vec, vec, pl.BlockSpec(memory_space=pl.ANY)],
        out_specs=[pl.BlockSpec((tb, hb * HEAD), lambda h, i: (i, cat_cols - nh + h)),
                   pl.BlockSpec((tb, hb * HEAD), lambda h, i: (i, h)),
                   pl.BlockSpec((hb, ncb, HEAD, HEAD), lambda h, i: (h, i, 0, 0))],
        out_shape=[jax.ShapeDtypeStruct(cat.shape, BF16), jax.ShapeDtypeStruct((S, W), F32),
                   jax.ShapeDtypeStruct((HH, S // C, HEAD, HEAD), BF16)],
        scratch_shapes=[pltpu.VMEM((hb, HEAD, HEAD), F32)],
        input_output_aliases={6: 0},
        compiler_params=_cp(("parallel", "arbitrary")))(proj, proj, proj, proj, lb.reshape(1, W), gain.reshape(1, W), cat)
    return cat_out, o_raw, states


def _hg_bwd(proj, dcat, o_raw, states, lb, gain, HH, name):
    S, NC = proj.shape
    W = HH * HEAD
    C = HG_CHUNK
    hb = min(HG_HEADS_PER_STEP, HH)
    nh = HH // hb
    tb = _tile(S, 2048 // hb, C)
    ncb = tb // C
    nt = S // tb
    base = 3 * nh
    dcat_cols = dcat.shape[1] // (hb * HEAD)

    def body(fz_ref, iv_ref, qz_ref, gz_ref, dc_ref, o_ref, st_ref, lb_ref, g_ref,
             dfz_ref, div_ref, dqz_ref, dgz_ref, dlb_ref, dg_ref, dstate):
        @pl.when(pl.program_id(1) == 0)
        def _():
            dstate[...] = jnp.zeros_like(dstate)
            dlb_ref[...] = jnp.zeros_like(dlb_ref)
            dg_ref[...] = jnp.zeros_like(dg_ref)

        tri_lo, tri_up = _tri(C, False), _tri(C, True)
        row = lax.broadcasted_iota(jnp.int32, (C, 1), 0)
        row8 = lax.broadcasted_iota(jnp.int32, (8, 1), 0)

        def chunk(cc, carry):
            c = ncb - 1 - cc
            rows = pl.ds(pl.multiple_of(c * C, C), C)
            for hh in range(hb):
                cols = slice(hh * HEAD, (hh + 1) * HEAD)
                lbv, gv = lb_ref[:, cols], g_ref[:, cols]
                fz, vv, qz, gz = fz_ref[rows, cols], iv_ref[rows, cols], qz_ref[rows, cols], gz_ref[rows, cols]
                sig, f, lf, kk, qq, sq = _hg_gates(fz, qz, lbv)
                b = jnp.dot(tri_lo, lf, preferred_element_type=F32, precision=lax.Precision.HIGHEST)
                bl = b[C - 1:C, :]
                eb, ebl = jnp.exp(b), jnp.exp(bl)
                ekb = jnp.exp(bl - b)
                qh, kh = qq * eb, kk * ekb
                o = o_ref[rows, cols]
                dhg = dc_ref[rows, cols]
                sgz = jax.nn.sigmoid(gz)
                silu_g = gz * sgz
                rr = lax.rsqrt(jnp.mean(o * o, axis=-1, keepdims=True) + RMS_EPS)
                nrm = o * rr
                dpre = dhg * silu_g
                dgz_ref[rows, cols] = (dhg * nrm * gv * (sgz * (1.0 + gz * (1.0 - sgz)))).astype(BF16)
                dg_ref[:, cols] += jnp.sum(dpre * nrm, axis=0, keepdims=True)
                dn = dpre * gv
                do = rr * (dn - nrm * jnp.mean(dn * nrm, axis=-1, keepdims=True))
                dob = do.astype(BF16)
                st0 = st_ref[hh, c]
                dst1 = dstate[hh]
                dst1b = dst1.astype(BF16)
                dqh = jnp.dot(dob, st0, preferred_element_type=F32)
                dv = lax.dot_general(kh.astype(BF16), dst1b, (((1,), (1,)), ((), ())), preferred_element_type=F32)
                dkh = jnp.dot(vv.astype(BF16), dst1b, preferred_element_type=F32)
                dstate[hh] = dst1 * ebl + lax.dot_general(dob, qh.astype(BF16), (((0,), (0,)), ((), ())),
                                                          preferred_element_type=F32)
                extra = (jnp.sum(dkh * kh, axis=0, keepdims=True)
                         + ebl * jnp.sum(st0.astype(F32) * dst1, axis=0, keepdims=True))
                dq = dqh * eb
                dk = dkh * ekb
                starts = list(range(0, C, 8))
                b_t, q_t, do_t = ([v[t0:t0 + 8, :] for t0 in starts] for v in (b, qq, do))
                dq_t = [dq[t0:t0 + 8, :] for t0 in starts]
                dk_rows, dv_rows = [], []
                for s in range(C):
                    ks, vs, bs = kk[s:s + 1, :], vv[s:s + 1, :], b[s:s + 1, :]
                    dk_s = dv_s = None
                    for ti, t0 in enumerate(starts):
                        if t0 + 8 <= s:
                            continue
                        diff = b_t[ti] - bs
                        if s >= t0:
                            diff = jnp.minimum(diff, 0.0)
                        dec = jnp.exp(diff)
                        da = jnp.sum(do_t[ti] * vs, axis=-1, keepdims=True)
                        a = jnp.sum(q_t[ti] * dec * ks, axis=-1, keepdims=True)
                        if s >= t0:
                            keep = row8 >= s - t0
                            da, a = jnp.where(keep, da, 0.0), jnp.where(keep, a, 0.0)
                        gd = da * dec
                        dq_t[ti] = dq_t[ti] + gd * ks
                        pk = jnp.sum(gd * q_t[ti], axis=0, keepdims=True)
                        pv = jnp.sum(a * do_t[ti], axis=0, keepdims=True)
                        dk_s = pk if dk_s is None else dk_s + pk
                        dv_s = pv if dv_s is None else dv_s + pv
                    dk_rows.append(dk_s)
                    dv_rows.append(dv_s)
                dq = jnp.concatenate(dq_t, axis=0)
                dk = dk + jnp.concatenate(dk_rows, axis=0)
                dv = dv + jnp.concatenate(dv_rows, axis=0)
                db = qq * dq - kk * dk + jnp.where(row == C - 1, extra, 0.0)
                dlf = jnp.dot(tri_up, db, preferred_element_type=F32, precision=lax.Precision.HIGHEST)
                df = dlf / f - dk
                dfz_ref[rows, cols] = (df * (1.0 - lbv) * sig * (1.0 - sig)).astype(BF16)
                dlb_ref[:, cols] += jnp.sum(df * (1.0 - sig), axis=0, keepdims=True)
                dqz_ref[rows, cols] = (dq * (sq * (1.0 + qz * (1.0 - sq)))).astype(BF16)
                div_ref[rows, cols] = dv.astype(BF16)
            return carry

        lax.fori_loop(0, ncb, chunk, 0)

    def col(k):
        return pl.BlockSpec((tb, hb * HEAD), lambda h, i: (nt - 1 - i, base + k * nh + h))

    ocol = pl.BlockSpec((tb, hb * HEAD), lambda h, i: (nt - 1 - i, h))
    vec = pl.BlockSpec((1, hb * HEAD), lambda h, i: (0, h))
    outs = pl.pallas_call(
        body, name=name, grid=(nh, nt),
        in_specs=[col(0), col(1), col(2), col(3),
                  pl.BlockSpec((tb, hb * HEAD), lambda h, i: (nt - 1 - i, dcat_cols - nh + h)),
                  pl.BlockSpec((tb, hb * HEAD), lambda h, i: (nt - 1 - i, h)),
                  pl.BlockSpec((hb, ncb, HEAD, HEAD), lambda h, i: (h, nt - 1 - i, 0, 0)), vec, vec],
        out_specs=[ocol, ocol, ocol, ocol, vec, vec],
        out_shape=[jax.ShapeDtypeStruct((S, W), BF16)] * 4 + [jax.ShapeDtypeStruct((1, W), F32)] * 2,
        scratch_shapes=[pltpu.VMEM((hb, HEAD, HEAD), F32)],
        compiler_params=_cp(("parallel", "arbitrary")))(proj, proj, proj, proj, dcat, o_raw, states,
                                                        lb.reshape(1, W), gain.reshape(1, W))
    return outs


def _lb_all(lb_logits):
    p = jax.nn.softmax(lb_logits.astype(F32), axis=0)
    return jnp.cumsum(p, axis=0) - p


def _local_step(x, mem, target, rel_bias, lb_logits, gains, hg_norm, comm):
    S, D = x.shape
    L = gains.shape[0]
    H = (D // 2) // HEAD
    lb = _lb_all(lb_logits)
    tables = [_branch_tables(r) for _, r in BRANCHES]
    onehots = [jnp.where(ok[None], jax.nn.one_hot(bk, REL_BUCKETS, dtype=F32, axis=0), 0.0)
               .reshape(REL_BUCKETS, 2 * Q_BLOCK * Q_BLOCK) for bk, ok in tables]
    bias = [jnp.where(ok[None], jnp.dot(rel_bias.astype(F32).T, oh, precision=lax.Precision.HIGHEST)
                      .reshape(H, Q_BLOCK, 2 * Q_BLOCK), NEG_INF) for oh, (_, ok) in zip(onehots, tables)]

    saved = []
    _, h = _norm_fwd(x, None, None, gains[0, 0], "norm_first")
    xl = x
    for l in range(L):
        g = gains[l]
        w, token = comm.weights(l, xl)
        proj = _mm_nn(h, w["in"], F32, "mm_in", after=token)
        outs, lses = [], []
        for gi, (_, r) in enumerate(BRANCHES):
            o, ls = _attn_branch_fwd(proj, bias[gi], r, H, f"attn_fwd_d{r}")
            outs.append(o)
            lses.append(ls)
        cat, o_attn, lse_tot = _attn_merge(outs, lses, D, "attn_merge")
        cat, o_raw, states = _hg_fwd(proj, cat, lb[l], hg_norm[l], H, "hgrn_fwd")
        mix = _mm_nn(cat, w["out"], F32, "mm_out")
        x1, hc = _norm_fwd(xl, mix, g[1], g[2], "norm_post_pre")
        _, mn = _norm_fwd(mem, None, None, g[3], "norm_mem")
        cq = _mm_nn(hc, w["cq"], BF16, "mm_cq")
        ckv = _mm_nn(mn, w["ckv"], BF16, "mm_ckv")
        co = _cross_fwd(cq, ckv, "cross_fwd")
        cop = _mm_nn(co, w["co"], F32, "mm_co")
        x2, hf = _norm_fwd(x1, cop, g[4], g[5], "norm_post_pre")
        gu, act = _mm_gate_up(hf, w["gu"], "mm_gu")
        y = _mm_nn(act, w["down"], F32, "mm_down")
        g_next = gains[l + 1, 0] if l + 1 < L else None
        x3, h_next = _norm_fwd(x2, y, g[6], g_next, "norm_post_pre" if l + 1 < L else "norm_post")
        saved.append(dict(x0=xl, h0=h, proj=proj, cat=cat, o_attn=o_attn, lse_tot=lse_tot, o_raw=o_raw, states=states,
                          mix=mix, x1=x1, hc=hc, mn=mn, cq=cq, ckv=ckv, co=co, cop=cop, x2=x2, hf=hf, gu=gu, act=act,
                          y=y, x3=x3, w=w))
        comm.layer_done(l, x3)
        xl, h = x3, h_next

    loss, dres = _loss_and_grad(xl, target)

    dh_next = bwd_token = None
    d_gains = [[None] * 7 for _ in range(L)]
    d_lb = [None] * L
    d_hg = [None] * L
    d_bias = [jnp.zeros((H, Q_BLOCK, 2 * Q_BLOCK), F32) for _ in BRANCHES]
    for l in reversed(range(L)):
        g, sv = gains[l], saved[l]
        w = sv["w"]
        g_next = gains[l + 1, 0] if l + 1 < L else None
        t3, dy, dg_next, dg6 = _norm_bwd(dres, dh_next, sv["x3"], sv["y"], g_next, g[6], "norm_bwd")
        if l + 1 < L:
            d_gains[l + 1][0] = dg_next
        d_gains[l][6] = dg6
        gw = {}
        gw["down"] = _mm_tn(sv["act"], dy, 1, BF16, "mm_dw_down")
        dgu = _mm_dx_down_swiglu(dy, w["down"], sv["gu"], "mm_dx_down", after=bwd_token)
        gw["gu"] = _mm_tn(sv["hf"], dgu, w["gu"].shape[0], BF16, "mm_dw_gu")
        dhf = _mm_nt(dgu, w["gu"], F32, "mm_dx_gu")
        t2, dcop, d_gains[l][5], d_gains[l][4] = _norm_bwd(t3, dhf, sv["x2"], sv["cop"], g[5], g[4], "norm_bwd")
        gw["co"] = _mm_tn(sv["co"], dcop, 1, BF16, "mm_dw_sq")
        dco = _mm_nt(dcop, w["co"], F32, "mm_dx_sq")
        dcq, dckv = _cross_bwd(sv["cq"], sv["ckv"], dco, "cross_bwd")
        gw["cq"] = _mm_tn(sv["hc"], dcq, 1, BF16, "mm_dw_sq")
        dhc = _mm_nt(dcq, w["cq"], F32, "mm_dx_sq")
        gw["ckv"] = _mm_tn(sv["mn"], dckv, w["ckv"].shape[0], BF16, "mm_dw_ckv")
        dmn = _mm_nt(dckv, w["ckv"], F32, "mm_dx_ckv")
        _, _, d_gains[l][3], _ = _norm_bwd(None, dmn, mem, None, g[3], None, "norm_bwd_mem")
        t1, dmix, d_gains[l][2], d_gains[l][1] = _norm_bwd(t2, dhc, sv["x1"], sv["mix"], g[2], g[1], "norm_bwd")
        gw["out"] = _mm_tn(sv["cat"], dmix, 1, BF16, "mm_dw_sq")
        dcat = _mm_nt(dmix, w["out"], F32, "mm_dx_sq")
        delta = _attn_delta(dcat, sv["o_attn"], H, "attn_delta")
        dqkv = []
        for gi, (_, r) in enumerate(BRANCHES):
            dq, dk, dv, db = _attn_branch_bwd(sv["proj"], dcat, sv["lse_tot"], delta, bias[gi], r, H, f"attn_bwd_d{r}")
            dqkv.append((dq, dk, dv))
            d_bias[gi] = d_bias[gi] + db
        dfz, div, dqz, dgz, dlb_l, dhg_l = _hg_bwd(sv["proj"], dcat, sv["o_raw"], sv["states"], lb[l], hg_norm[l], H,
                                                   "hgrn_bwd")
        d_lb[l], d_hg[l] = dlb_l[0], dhg_l[0]
        dproj = _assemble_dproj(dqkv, [dfz, div, dqz, dgz], "assemble_dproj")
        gw["in"] = _mm_tn(sv["h0"], dproj, w["in"].shape[0], BF16, "mm_dw_in")
        dh_next = _mm_nt(dproj, w["in"], F32, "mm_dx_in")
        dres = t1
        bwd_token = comm.grads(l, gw, gw["in"])
    grad_x, _, d_gains[0][0], _ = _norm_bwd(dres, dh_next, x, None, gains[0, 0], None, "norm_bwd_first")

    d_rel_bias = _bias_grad(d_bias, onehots, H).T
    d_gains = jnp.stack([jnp.stack(row) for row in d_gains])
    return loss, grad_x, d_rel_bias, jnp.stack(d_lb), jnp.stack(d_hg), d_gains


def _place():
    return lax.axis_index("x"), lax.axis_index("y"), lax.axis_index("c")


def _all_gather(shards, name):
    n = len(shards)
    HBM = pl.BlockSpec(memory_space=pltpu.HBM)

    def body(*refs):
        ins, outs = refs[:n], refs[n:2 * n]
        send_sems, recv_sems, local_sems = refs[2 * n:]
        x, y, c = _place()
        me, sibling = (x, y, c), (x, y, 1 - c)
        chips = [(1 - x, y), (x, 1 - y), (1 - x, 1 - y)]

        def copy(a, k, block, to, own=False):
            dst = outs[a].at[4 * block[0] + 2 * block[1] + block[2]]
            return pltpu.make_async_remote_copy(
                src_ref=ins[a] if own else dst, dst_ref=dst, send_sem=send_sems.at[7 * a + k],
                recv_sem=recv_sems.at[7 * a + k], device_id=to, device_id_type=MESH)

        mine = [pltpu.make_async_copy(ins[a], outs[a].at[4 * x + 2 * y + c], local_sems.at[a]) for a in range(n)]
        for cp in mine:
            cp.start()
        first = []
        for a in range(n):
            first.append(copy(a, 0, me, sibling, own=True))
            first += [copy(a, 1 + j, me, (*chip, c), own=True) for j, chip in enumerate(chips)]
        for cp in first:
            cp.start()
        passed = []
        for j, chip in enumerate(chips):
            for a in range(n):
                copy(a, 1 + j, (*chip, c), me).wait_recv()
                fwd = copy(a, 4 + j, (*chip, c), sibling)
                fwd.start()
                passed.append(fwd)
        for a in range(n):
            copy(a, 0, sibling, me).wait_recv()
            for j, chip in enumerate(chips):
                copy(a, 4 + j, (*chip, 1 - c), me).wait_recv()
        for cp in first + passed:
            cp.wait_send()
        for cp in mine:
            cp.wait()

    return pl.pallas_call(
        body, name=name, in_specs=[HBM] * n, out_specs=[HBM] * n,
        out_shape=[jax.ShapeDtypeStruct((N_DEV,) + s.shape, s.dtype) for s in shards],
        scratch_shapes=[pltpu.SemaphoreType.DMA((7 * n,)), pltpu.SemaphoreType.DMA((7 * n,)),
                        pltpu.SemaphoreType.DMA((n,))],
    )(*shards)


def _exchange_sibling(grads, name):
    n = len(grads)
    HBM = pl.BlockSpec(memory_space=pltpu.HBM)

    def body(*refs):
        ins, outs = refs[:n], refs[n:2 * n]
        send_sems, recv_sems = refs[2 * n:]
        x, y, c = _place()
        sibling = (x, y, 1 - c)
        for a in range(n):
            for q in range(4):
                pltpu.make_async_remote_copy(
                    src_ref=ins[a].at[2 * q + 1 - c], dst_ref=outs[a].at[q], send_sem=send_sems.at[a],
                    recv_sem=recv_sems.at[a], device_id=sibling, device_id_type=MESH).start()
        for a in range(n):
            pltpu.make_async_remote_copy(
                src_ref=ins[a].at[pl.ds(0, 4)], dst_ref=outs[a], send_sem=send_sems.at[a], recv_sem=recv_sems.at[a],
                device_id=sibling, device_id_type=MESH).wait()

    return pl.pallas_call(
        body, name=name, in_specs=[HBM] * n, out_specs=[HBM] * n,
        out_shape=[jax.ShapeDtypeStruct((4,) + g.shape[1:], g.dtype) for g in grads],
        scratch_shapes=[pltpu.SemaphoreType.DMA((n,)), pltpu.SemaphoreType.DMA((n,))],
    )(*grads)


def _exchange_chips(parts, name):
    n = len(parts)
    HBM = pl.BlockSpec(memory_space=pltpu.HBM)

    def body(*refs):
        ins, outs = refs[:n], refs[n:2 * n]
        send_sems, recv_sems = refs[2 * n:]
        x, y, c = _place()
        chips = [(1 - x, y), (x, 1 - y), (1 - x, 1 - y)]
        copies = []
        for a in range(n):
            for j, chip in enumerate(chips):
                cp = pltpu.make_async_remote_copy(
                    src_ref=ins[a].at[2 * chip[0] + chip[1]], dst_ref=outs[a].at[j], send_sem=send_sems.at[3 * a + j],
                    recv_sem=recv_sems.at[3 * a + j], device_id=(*chip, c), device_id_type=MESH)
                cp.start()
                copies.append(cp)
        for cp in copies:
            cp.wait()

    return pl.pallas_call(
        body, name=name, in_specs=[HBM] * n, out_specs=[HBM] * n,
        out_shape=[jax.ShapeDtypeStruct((3,) + p.shape[1:], p.dtype) for p in parts],
        scratch_shapes=[pltpu.SemaphoreType.DMA((3 * n,)), pltpu.SemaphoreType.DMA((3 * n,))],
    )(*parts)


def _others(x, y, c):
    out = []
    for k in range(1, N_DEV):
        px = 1 - x if (k >> 2) & 1 else x
        py = 1 - y if (k >> 1) & 1 else y
        pc = 1 - c if k & 1 else c
        out.append(((px, py, pc), 4 * px + 2 * py + pc))
    return out


_HBM_SPEC = pl.BlockSpec(memory_space=pltpu.HBM)
_SEM_SPEC = pl.BlockSpec(memory_space=pltpu.SEMAPHORE)
_EFFECT = pltpu.SideEffectType.DATAFLOW_SIDE_EFFECTING


def _copies_start(srcs, lands, after, src_block, name):
    n = len(srcs)

    def body(*refs):
        src_refs, land_refs = refs[:n], refs[n:2 * n]
        send_sems, recv_sems = refs[2 * n + 1], refs[2 * n + 2]
        token = refs[-1]
        x, y, c = _place()
        mine = 4 * x + 2 * y + c
        for a in range(n):
            for k, (peer, block) in enumerate(_others(x, y, c)):
                pltpu.make_async_remote_copy(
                    src_ref=src_refs[a].at[block] if src_block else src_refs[a],
                    dst_ref=land_refs[a].at[k] if src_block else land_refs[a].at[mine],
                    send_sem=send_sems.at[7 * a + k], recv_sem=recv_sems.at[7 * a + k],
                    device_id=peer, device_id_type=MESH).start()
        token[...] = jnp.zeros_like(token)

    outs = pl.pallas_call(
        body, name=name,
        out_shape=(pltpu.SemaphoreType.DMA((7 * n,)), pltpu.SemaphoreType.DMA((7 * n,)),
                   *[pltpu.HBM(s.shape, s.dtype) for s in srcs], *[pltpu.HBM(t.shape, t.dtype) for t in lands],
                   jax.ShapeDtypeStruct((8, 128), F32)),
        in_specs=[_HBM_SPEC] * (2 * n) + [pl.BlockSpec(memory_space=pl.ANY)],
        out_specs=(_SEM_SPEC, _SEM_SPEC, *[_HBM_SPEC] * (2 * n), pl.BlockSpec(memory_space=pltpu.VMEM)),
        input_output_aliases={i: 2 + i for i in range(2 * n)},
        compiler_params=pltpu.CompilerParams(has_side_effects=_EFFECT),
    )(*[pltpu.with_memory_space_constraint(s, pltpu.HBM) for s in srcs],
      *[pltpu.with_memory_space_constraint(t, pltpu.HBM) for t in lands], after)
    return outs[0], outs[1], list(outs[2:2 + n]), list(outs[2 + n:2 + 2 * n]), outs[-1]


def _copies_wait(send_sems, recv_sems, srcs, lands, after, src_block, name):
    n = len(srcs)

    def body(*refs):
        src_refs, land_refs = refs[:n], refs[n:2 * n]
        send_sems, recv_sems = refs[2 * n], refs[2 * n + 1]
        x, y, c = _place()
        for a in range(n):
            for k, (peer, block) in enumerate(_others(x, y, c)):
                cp = pltpu.make_async_remote_copy(
                    src_ref=src_refs[a].at[block] if src_block else src_refs[a],
                    dst_ref=land_refs[a].at[k] if src_block else land_refs[a].at[block],
                    send_sem=send_sems.at[7 * a + k], recv_sem=recv_sems.at[7 * a + k],
                    device_id=peer, device_id_type=MESH)
                cp.wait_send()
                cp.wait_recv()

    outs = pl.pallas_call(
        body, name=name,
        out_shape=(*[pltpu.HBM(s.shape, s.dtype) for s in srcs], *[pltpu.HBM(t.shape, t.dtype) for t in lands]),
        in_specs=[_HBM_SPEC] * (2 * n) + [_SEM_SPEC, _SEM_SPEC, pl.BlockSpec(memory_space=pl.ANY)],
        out_specs=tuple([_HBM_SPEC] * (2 * n)),
        input_output_aliases={i: i for i in range(2 * n)},
        compiler_params=pltpu.CompilerParams(has_side_effects=_EFFECT),
    )(*srcs, *lands, send_sems, recv_sems, after)
    return list(outs[:n]), list(outs[n:])


def _pair_sum(grad, recv, c_idx, name):
    _, R, C = grad.shape
    tr = _tile(R, 512, 16)

    def body(c_ref, g_ref, p_ref, o_ref):
        del c_ref
        o_ref[...] = (g_ref[...].astype(F32) + p_ref[...].astype(F32)).astype(BF16)

    spec = pltpu.PrefetchScalarGridSpec(
        num_scalar_prefetch=1, grid=(4, R // tr),
        in_specs=[pl.BlockSpec((None, tr, C), lambda q, i, c: (2 * q + c[0], i, 0)),
                  pl.BlockSpec((None, tr, C), lambda q, i, c: (q, i, 0))],
        out_specs=pl.BlockSpec((None, tr, C), lambda q, i, c: (q, i, 0)))
    return pl.pallas_call(body, name=name, grid_spec=spec, out_shape=jax.ShapeDtypeStruct((4, R, C), BF16),
                          compiler_params=_cp(("parallel", "parallel")))(c_idx, grad, recv)


def _adam(w, g, m, v):
    m2 = ADAM_B1 * m + (1.0 - ADAM_B1) * g
    v2 = ADAM_B2 * v + (1.0 - ADAM_B2) * (g * g)
    m_hat = m2 / (1.0 - ADAM_B1 ** ADAM_STEP)
    v_hat = v2 / (1.0 - ADAM_B2 ** ADAM_STEP)
    return -ADAM_LR * (m_hat / (jnp.sqrt(v_hat) + ADAM_EPS) + ADAM_WD * w), m2, v2


def _reduce_update(part, recv, own_idx, w, m, v, layer, prev, name):
    L, R, C = w.shape
    nr = recv.shape[0]
    tr = _tile(R, 128, 16)
    has_prev = prev is not None

    def body(idx_ref, q_ref, *rest):
        del idx_ref
        r_refs, (w_ref, m_ref, v_ref) = rest[:nr], rest[nr:nr + 3]
        g_ref, d_ref, m2_ref, v2_ref = rest[-4:]
        g = q_ref[...].astype(F32)
        for r_ref in r_refs:
            g = g + r_ref[...].astype(F32)
        d, m2, v2 = _adam(w_ref[...], g, m_ref[...], v_ref[...])
        g_ref[...] = g
        d_ref[...] = d
        m2_ref[...] = m2
        v2_ref[...] = v2

    def rblk(j):
        return pl.BlockSpec((None, tr, C), lambda i, k: (j, i, 0))

    lay = pl.BlockSpec((None, tr, C), lambda i, k: (layer, i, 0))
    in_specs = [pl.BlockSpec((None, tr, C), lambda i, k: (k[0], i, 0))] + [rblk(j) for j in range(nr)] + [lay, lay, lay]
    ins = [part] + [recv] * nr + [w, m, v]
    aliases = {}
    if has_prev:
        in_specs += [pl.BlockSpec(memory_space=pl.ANY)] * 4
        ins += list(prev)
        aliases = {1 + len(ins) - 4 + t: t for t in range(4)}
    spec = pltpu.PrefetchScalarGridSpec(num_scalar_prefetch=1, grid=(R // tr,), in_specs=in_specs,
                                        out_specs=[lay, lay, lay, lay])
    return pl.pallas_call(body, name=name, grid_spec=spec, out_shape=[jax.ShapeDtypeStruct((L, R, C), F32)] * 4,
                          input_output_aliases=aliases, compiler_params=_cp(("parallel",)))(own_idx, *ins)


def _sum_devices(gathered):
    n, R, C = gathered.shape

    def body(g_ref, o_ref):
        acc = g_ref[0]
        for d in range(1, n):
            acc = acc + g_ref[d]
        o_ref[...] = acc

    return pl.pallas_call(body, name="sum_devices", out_shape=jax.ShapeDtypeStruct((R, C), F32))(gathered)


def _adam_small(w, g, m, v, name):
    shape = w.shape
    two = (-1, shape[-1])

    def body(w_ref, g_ref, m_ref, v_ref, d_ref, m2_ref, v2_ref):
        d, m2, v2 = _adam(w_ref[...], g_ref[...], m_ref[...], v_ref[...])
        d_ref[...] = d
        m2_ref[...] = m2
        v2_ref[...] = v2

    outs = pl.pallas_call(body, name=name, out_shape=[jax.ShapeDtypeStruct(w.reshape(two).shape, F32)] * 3)(
        w.reshape(two), g.reshape(two), m.reshape(two), v.reshape(two))
    return [o.reshape(shape) for o in outs]


_SHARDED = ("in", "out", "cq", "ckv", "co", "gu", "down")
_COLUMN_SHARDED = ("in", "ckv", "gu")


def kernel(x, mem, rel_bias, lb_logits, norm_gains, w_in, hg_norm, w_out, w_cq, w_ckv, w_co, w_gate_up, w_down, loss_target, m_rel_bias, m_lb_logits, m_norm_gains, m_w_in, m_hg_norm, m_w_out, m_w_cq, m_w_ckv, m_w_co, m_w_gate_up, m_w_down, v_rel_bias, v_lb_logits, v_norm_gains, v_w_in, v_hg_norm, v_w_out, v_w_cq, v_w_ckv, v_w_co, v_w_gate_up, v_w_down):
    L = w_in.shape[0]
    D = x.shape[-1]
    ws = dict(zip(_SHARDED, (w_in, w_out, w_cq, w_ckv, w_co, w_gate_up, w_down)))
    ms = dict(zip(_SHARDED, (m_w_in, m_w_out, m_w_cq, m_w_ckv, m_w_co, m_w_gate_up, m_w_down)))
    vs = dict(zip(_SHARDED, (v_w_in, v_w_out, v_w_cq, v_w_ckv, v_w_co, v_w_gate_up, v_w_down)))
    px, py, pc = _place()
    dev = 4 * px + 2 * py + pc
    c_idx = jnp.reshape(pc, (1,)).astype(jnp.int32)
    chip_idx = jnp.reshape(2 * px + py, (1,)).astype(jnp.int32)
    dev_idx = jnp.reshape(dev, (1,)).astype(jnp.int32)

    def shards_of(l):
        return [ws[k][l].astype(BF16) for k in _SHARDED]

    def as_weights(gathered):
        return {k: g if k in _COLUMN_SHARDED else g.reshape(1, N_DEV * g.shape[1], g.shape[2])
                for k, g in zip(_SHARDED, gathered)}

    class Comm:
        def __init__(self):
            *first, gains = _all_gather(shards_of(0) + [norm_gains], "gather_weights")
            self.gains = jnp.transpose(gains, (1, 2, 0, 3)).reshape(L, 7, D)
            self.ready = {0: as_weights(first)}
            self.gather = None
            self.scatter = None
            self.results = {k: None for k in _SHARDED}

        def weights(self, l, x_in):
            del x_in
            mine, token = self.ready.pop(l), None
            if l + 1 < L:
                shards = shards_of(l + 1)
                lands = [lax.empty((N_DEV,) + s.shape, s.dtype) for s in shards]
                *self.gather, token = _copies_start(shards, lands, mine["in"], False, "gather_start")
            return mine, token

        def layer_done(self, l, x_out):
            if l + 1 < L:
                send_sems, recv_sems, shards, lands = self.gather
                shards, lands = _copies_wait(send_sems, recv_sems, shards, lands, x_out, False, "gather_wait")
                full = [lax.dynamic_update_slice_in_dim(t, s[None], dev, axis=0) for s, t in zip(shards, lands)]
                self.ready[l + 1] = as_weights(full)

        def update(self, l, parts, recvs, own_idx):
            for k, part, recv in zip(_SHARDED, parts, recvs):
                self.results[k] = _reduce_update(part, recv, own_idx, ws[k], ms[k], vs[k], l, self.results[k],
                                                 "reduce_update")

        def grads(self, l, gw, after):
            order = dev_idx
            if self.scatter is not None:
                lp, send_sems, recv_sems, srcs, lands = self.scatter
                srcs, lands = _copies_wait(send_sems, recv_sems, srcs, lands, after, True, "scatter_wait")
                self.update(lp, srcs, lands, dev_idx)
                order = lands[0]
            full = [gw[k].reshape((N_DEV,) + ws[k].shape[1:]) for k in _SHARDED]
            if l > 0:
                lands = [lax.empty((N_DEV - 1,) + g.shape[1:], g.dtype) for g in full]
                send_sems, recv_sems, srcs, lands, token = _copies_start(full, lands, order, True, "scatter_start")
                self.scatter = (l, send_sems, recv_sems, srcs, lands)
                return token
            else:
                from_sibling = _exchange_sibling(full, "scatter_sibling")
                parts = [_pair_sum(g, r, c_idx, "pair_sum") for g, r in zip(full, from_sibling)]
                self.update(l, parts, _exchange_chips(parts, "scatter_chips"), chip_idx)

    comm = Comm()
    loss, grad_x, d_rel_bias, d_lb, d_hg, d_gains = _local_step(
        x[0], mem[0], loss_target[0], rel_bias, lb_logits, comm.gains, hg_norm, comm)
    results = comm.results

    pieces = [jnp.reshape(loss, (1,)), d_rel_bias.reshape(-1), d_lb.reshape(-1), d_hg.reshape(-1), d_gains.reshape(-1)]
    sizes = [p.shape[0] for p in pieces]
    total = sum(sizes)
    rows = -(-total // 1024) * 8
    pack = jnp.pad(jnp.concatenate(pieces), (0, rows * 128 - total)).reshape(rows, 128)
    summed = _sum_devices(_all_gather([pack], "gather_small")[0]).reshape(-1)
    offs = [sum(sizes[:i]) for i in range(len(sizes))]
    loss = summed[0]
    g_rel_bias = summed[offs[1]:offs[1] + sizes[1]].reshape(rel_bias.shape)
    g_lb_all = summed[offs[2]:offs[2] + sizes[2]].reshape(lb_logits.shape)
    g_hg = summed[offs[3]:offs[3] + sizes[3]].reshape(hg_norm.shape)
    g_gains_full = summed[offs[4]:offs[4] + sizes[4]].reshape(L, 7, D)
    g_gains = lax.dynamic_slice_in_dim(g_gains_full, dev * (D // N_DEV), D // N_DEV, axis=2)
    g_lb = jax.vjp(_lb_all, lb_logits)[1](g_lb_all)[0]

    small = [(rel_bias, g_rel_bias, m_rel_bias, v_rel_bias), (lb_logits, g_lb, m_lb_logits, v_lb_logits),
             (norm_gains, g_gains, m_norm_gains, v_norm_gains), (hg_norm, g_hg, m_hg_norm, v_hg_norm)]
    upd = [_adam_small(w, g, m, v, "adam_small") for w, g, m, v in small]

    def ordered(small_vals, big_vals):
        return [small_vals[0], small_vals[1], small_vals[2], big_vals["in"], small_vals[3], big_vals["out"],
                big_vals["cq"], big_vals["ckv"], big_vals["co"], big_vals["gu"], big_vals["down"]]

    grads = ordered([s[1] for s in small], {k: results[k][0] for k in _SHARDED})
    deltas = ordered([u[0] for u in upd], {k: results[k][1] for k in _SHARDED})
    new_m = ordered([u[1] for u in upd], {k: results[k][2] for k in _SHARDED})
    new_v = ordered([u[2] for u in upd], {k: results[k][3] for k in _SHARDED})
    return (loss, grad_x[None], *grads, *deltas, *new_m, *new_v)
```

```python
import functools
import math

import jax
import jax.numpy as jnp
from jax import lax
from jax.experimental import pallas as pl
from jax.experimental.pallas import tpu as pltpu

F32 = jnp.float32
BF16 = jnp.bfloat16
MESH = pl.DeviceIdType.MESH

N_DEV = 8
HEAD = 128
CROSS_HEADS = 4
Q_BLOCK = 128
BRANCHES = ((128, 1), (512, 4), (2048, 16))
REL_BUCKETS = 32
REL_MAX_DIST = 2048
HG_CHUNK = 16
HG_HEADS_PER_STEP = 8
RMS_EPS = 1e-6
NEG_INF = -1e30
ADAM_LR, ADAM_B1, ADAM_B2, ADAM_EPS, ADAM_WD, ADAM_STEP = 0.001, 0.9, 0.999, 1e-08, 0.01, 10
VMEM_LIMIT = 48 * 1024 * 1024
NT_K_PER_STEP = 3584


def _tile(n, target, mult):
    best = None
    t = mult
    while t <= min(n, target):
        if n % t == 0:
            best = t
        t += mult
    return best if best is not None else n


def _cp(sem, vmem=VMEM_LIMIT):
    return pltpu.CompilerParams(dimension_semantics=sem, vmem_limit_bytes=vmem)


def _mm_nn(a, b3, out_dtype, name, after=None):
    M, K = a.shape
    J, K2, Nb = b3.shape
    assert K == K2
    tm, tn, tk = _tile(M, 1024, 8), _tile(Nb, 1408, 128), _tile(K, 2816, 128)
    nn, nk = Nb // tn, K // tk
    extra = [] if after is None else [after]

    def body(a_ref, b_ref, *rest):
        o_ref, acc = rest[len(extra)], rest[len(extra) + 1:]
        prod = jnp.dot(a_ref[...].astype(BF16), b_ref[...].astype(BF16), preferred_element_type=F32)
        if nk == 1:
            o_ref[...] = prod.astype(o_ref.dtype)
        else:
            k = pl.program_id(2)

            @pl.when(k == 0)
            def _():
                acc[0][...] = prod

            @pl.when(k > 0)
            def _():
                acc[0][...] += prod

            @pl.when(k == nk - 1)
            def _():
                o_ref[...] = acc[0][...].astype(o_ref.dtype)

    return pl.pallas_call(
        body, name=name, grid=(M // tm, J * nn, nk),
        in_specs=[pl.BlockSpec((tm, tk), lambda i, n, k: (i, k)),
                  pl.BlockSpec((None, tk, tn), lambda i, n, k: (n // nn, k, n % nn))]
                 + [pl.BlockSpec(memory_space=pl.ANY)] * len(extra),
        out_specs=pl.BlockSpec((tm, tn), lambda i, n, k: (i, n)),
        out_shape=jax.ShapeDtypeStruct((M, J * Nb), out_dtype),
        scratch_shapes=[] if nk == 1 else [pltpu.VMEM((tm, tn), F32)],
        compiler_params=_cp(("parallel", "parallel", "arbitrary")),
    )(a, b3, *extra)


def _mm_nt(a, b3, out_dtype, name, after=None):
    extra = [] if after is None else [after]
    halves = a.ndim == 3
    M, Kt = (a.shape[1], 2 * a.shape[2]) if halves else a.shape
    J, N, Kb = b3.shape
    assert Kt == J * Kb
    tm, tn = _tile(M, 1024, 8), _tile(N, 1408, 128)
    tk = _tile(math.gcd(Kb, Kt // 2) if halves else Kb, 2048, 128)
    nkb = Kb // tk
    groups = J // 2 if halves and J > 1 else J
    jb = max(j for j in range(1, groups + 1) if groups % j == 0 and j * Kb <= NT_K_PER_STEP) if nkb == 1 else 1
    nk = (J // jb) * nkb

    def body(a_ref, b_ref, *rest):
        o_ref, acc = rest[len(extra)], rest[len(extra) + 1:]
        prod = None
        for j in range(jb):
            part = lax.dot_general(a_ref[:, j * tk:(j + 1) * tk].astype(BF16), b_ref[j].astype(BF16),
                                   (((1,), (1,)), ((), ())), preferred_element_type=F32)
            prod = part if prod is None else prod + part
        if nk == 1:
            o_ref[...] = prod.astype(o_ref.dtype)
        else:
            k = pl.program_id(2)

            @pl.when(k == 0)
            def _():
                acc[0][...] = prod

            @pl.when(k > 0)
            def _():
                acc[0][...] += prod

            @pl.when(k == nk - 1)
            def _():
                o_ref[...] = acc[0][...].astype(o_ref.dtype)

    return pl.pallas_call(
        body, name=name, grid=(M // tm, N // tn, nk),
        in_specs=[pl.BlockSpec((None, tm, jb * tk), lambda i, n, k: (k // (nk // 2), i, k % (nk // 2))) if halves
                  else pl.BlockSpec((tm, jb * tk), lambda i, n, k: (i, k)),
                  pl.BlockSpec((jb, tn, tk), lambda i, n, k: (k // nkb, n, k % nkb))]
                 + [pl.BlockSpec(memory_space=pl.ANY)] * len(extra),
        out_specs=pl.BlockSpec((tm, tn), lambda i, n, k: (i, n)),
        out_shape=jax.ShapeDtypeStruct((M, N), out_dtype),
        scratch_shapes=[] if nk == 1 else [pltpu.VMEM((tm, tn), F32)],
        compiler_params=_cp(("parallel", "parallel", "arbitrary")),
    )(a, b3, *extra)


def _mm_tn(a, b, J, out_dtype, name):
    S, M = a.shape
    halves = b.ndim == 3
    S2, Nt = (b.shape[1], 2 * b.shape[2]) if halves else b.shape
    assert S == S2 and Nt % J == 0
    Nb = Nt // J
    tm, tk = _tile(M, 1408, 128), _tile(S, 2048, 8)
    tn = _tile(math.gcd(Nb, Nt // 2) if halves else Nb, 1408, 128)
    nn, nk = Nb // tn, S // tk
    per_half = J * nn // 2
    b_spec = (pl.BlockSpec((None, tk, tn), lambda i, n, k: (n // per_half, k, n % per_half)) if halves
              else pl.BlockSpec((tk, tn), lambda i, n, k: (k, n)))

    def body(a_ref, b_ref, o_ref, *acc):
        prod = lax.dot_general(a_ref[...].astype(BF16), b_ref[...].astype(BF16), (((0,), (0,)), ((), ())),
                               preferred_element_type=F32)
        if nk == 1:
            o_ref[...] = prod.astype(o_ref.dtype)
        else:
            k = pl.program_id(2)

            @pl.when(k == 0)
            def _():
                acc[0][...] = prod

            @pl.when(k > 0)
            def _():
                acc[0][...] += prod

            @pl.when(k == nk - 1)
            def _():
                o_ref[...] = acc[0][...].astype(o_ref.dtype)

    return pl.pallas_call(
        body, name=name, grid=(M // tm, J * nn, nk),
        in_specs=[pl.BlockSpec((tk, tm), lambda i, n, k: (k, i)), b_spec],
        out_specs=pl.BlockSpec((None, tm, tn), lambda i, n, k: (n // nn, i, n % nn)),
        out_shape=jax.ShapeDtypeStruct((J, M, Nb), out_dtype),
        scratch_shapes=[] if nk == 1 else [pltpu.VMEM((tm, tn), F32)],
        compiler_params=_cp(("parallel", "parallel", "arbitrary")),
    )(a, b)


def _rms_scale(v):
    return lax.rsqrt(jnp.mean(v * v, axis=-1, keepdims=True) + RMS_EPS)


def _rms_bwd(dy, v, g):
    r = _rms_scale(v)
    vh = v * r
    u = dy * g
    dv = r * (u - vh * jnp.mean(u * vh, axis=-1, keepdims=True))
    return dv, dy * vh


def _fold8(t):
    R, D = t.shape
    return jnp.sum(t.reshape(R // 8, 8, D), axis=0)


def _norm_fwd(x, z, g_post, g_pre, name):
    S, D = x.shape
    tr = _tile(S, 256, 8)
    has_z, has_pre = z is not None, g_pre is not None

    def body(*refs):
        it = iter(refs)
        x_ref = next(it)
        z_ref = next(it) if has_z else None
        gpost_ref = next(it) if has_z else None
        gpre_ref = next(it) if has_pre else None
        xn_ref = next(it) if has_z else None
        h_ref = next(it) if has_pre else None
        xn = x_ref[...]
        if has_z:
            zz = z_ref[...]
            xn = xn + zz * _rms_scale(zz) * gpost_ref[...]
            xn_ref[...] = xn
        if has_pre:
            h_ref[...] = (xn * _rms_scale(xn) * gpre_ref[...]).astype(BF16)

    row = pl.BlockSpec((tr, D), lambda i: (i, 0))
    gain = pl.BlockSpec((1, D), lambda i: (0, 0))
    ins, specs, outs, ospecs = [x], [row], [], []
    if has_z:
        ins += [z, g_post.reshape(1, D)]
        specs += [row, gain]
        outs.append(jax.ShapeDtypeStruct((S, D), F32))
        ospecs.append(row)
    if has_pre:
        ins.append(g_pre.reshape(1, D))
        specs.append(gain)
        outs.append(jax.ShapeDtypeStruct((S, D), BF16))
        ospecs.append(row)
    res = pl.pallas_call(body, name=name, grid=(S // tr,), in_specs=specs, out_specs=ospecs, out_shape=outs,
                         compiler_params=_cp(("parallel",)))(*ins)
    res = list(res)
    xn = res.pop(0) if has_z else x
    h = res.pop(0) if has_pre else None
    return xn, h


def _norm_bwd(dres, dh, xn, z, g_pre, g_post, name):
    S, D = xn.shape
    tr = _tile(S, 256, 8)
    has_res, has_pre, has_z = dres is not None, dh is not None, z is not None

    def body(*refs):
        it = iter(refs)
        dres_ref = next(it) if has_res else None
        dh_ref = next(it) if has_pre else None
        xn_ref = next(it) if has_pre else None
        gpre_ref = next(it) if has_pre else None
        z_ref = next(it) if has_z else None
        gpost_ref = next(it) if has_z else None
        t_ref = next(it)
        dz_ref = next(it) if has_z else None
        dgpre_ref = next(it) if has_pre else None
        dgpost_ref = next(it) if has_z else None
        first = pl.program_id(0) == 0
        t = dres_ref[...] if has_res else None
        if has_pre:
            dv, dg = _rms_bwd(dh_ref[...].astype(F32), xn_ref[...], gpre_ref[...])
            t = dv if t is None else t + dv
            part = _fold8(dg)

            @pl.when(first)
            def _():
                dgpre_ref[...] = part

            @pl.when(jnp.logical_not(first))
            def _():
                dgpre_ref[...] += part
        t_ref[...] = t
        if has_z:
            dv, dg = _rms_bwd(t, z_ref[...], gpost_ref[...])
            dz_ref[...] = dv.astype(BF16)
            part2 = _fold8(dg)

            @pl.when(first)
            def _():
                dgpost_ref[...] = part2

            @pl.when(jnp.logical_not(first))
            def _():
                dgpost_ref[...] += part2

    row = pl.BlockSpec((tr, D), lambda i: (i, 0))
    gain = pl.BlockSpec((1, D), lambda i: (0, 0))
    accs = pl.BlockSpec((8, D), lambda i: (0, 0))
    ins, specs = [], []
    if has_res:
        ins.append(dres)
        specs.append(row)
    if has_pre:
        ins += [dh, xn, g_pre.reshape(1, D)]
        specs += [row, row, gain]
    if has_z:
        ins += [z, g_post.reshape(1, D)]
        specs += [row, gain]
    outs, ospecs = [jax.ShapeDtypeStruct((S, D), F32)], [row]
    if has_z:
        outs.append(jax.ShapeDtypeStruct((S, D), BF16))
        ospecs.append(row)
    if has_pre:
        outs.append(jax.ShapeDtypeStruct((8, D), F32))
        ospecs.append(accs)
    if has_z:
        outs.append(jax.ShapeDtypeStruct((8, D), F32))
        ospecs.append(accs)
    res = list(pl.pallas_call(body, name=name, grid=(S // tr,), in_specs=specs, out_specs=ospecs, out_shape=outs,
                              compiler_params=_cp(("arbitrary",)))(*ins))
    t = res.pop(0)
    dz = res.pop(0) if has_z else None
    dgpre = jnp.sum(res.pop(0), axis=0) if has_pre else None
    dgpost = jnp.sum(res.pop(0), axis=0) if has_z else None
    return t, dz, dgpre, dgpost


def _loss_and_grad(y, target):
    S, D = y.shape
    tr = _tile(S, 256, 8)

    def body(y_ref, t_ref, dy_ref, l_ref):
        err = y_ref[...] - t_ref[...]
        dy_ref[...] = err * (1.0 / D)
        part = 0.5 * jnp.sum(jnp.mean(err * err, axis=-1, keepdims=True), axis=0, keepdims=True)
        first = pl.program_id(0) == 0

        @pl.when(first)
        def _():
            l_ref[...] = jnp.broadcast_to(part, l_ref.shape)

        @pl.when(jnp.logical_not(first))
        def _():
            l_ref[...] += jnp.broadcast_to(part, l_ref.shape)

    row = pl.BlockSpec((tr, D), lambda i: (i, 0))
    dy, l = pl.pallas_call(
        body, name="loss_grad", grid=(S // tr,), in_specs=[row, row],
        out_specs=[row, pl.BlockSpec((8, 128), lambda i: (0, 0))],
        out_shape=[jax.ShapeDtypeStruct((S, D), F32), jax.ShapeDtypeStruct((8, 128), F32)],
        compiler_params=_cp(("arbitrary",)))(y, target)
    return l[0, 0], dy


def _mm_gate_up(a, w3, name):
    S, D = a.shape
    J, D2, Nb = w3.shape
    F = J * Nb // 2
    assert D == D2
    tm, tn = _tile(S, 512, 8), _tile(math.gcd(Nb, F), 1408, 128)
    nn, nf = Nb // tn, F // tn

    def body(a_ref, wg_ref, wu_ref, gu_ref, act_ref):
        av = a_ref[...]
        g = jnp.dot(av, wg_ref[...], preferred_element_type=F32)
        u = jnp.dot(av, wu_ref[...], preferred_element_type=F32)
        gu_ref[0] = g
        gu_ref[1] = u
        act_ref[...] = (g * jax.nn.sigmoid(g) * u).astype(BF16)

    return pl.pallas_call(
        body, name=name, grid=(nf, S // tm),
        in_specs=[pl.BlockSpec((tm, D), lambda n, i: (i, 0)),
                  pl.BlockSpec((None, D, tn), lambda n, i: (n // nn, 0, n % nn)),
                  pl.BlockSpec((None, D, tn), lambda n, i: ((n + nf) // nn, 0, (n + nf) % nn))],
        out_specs=[pl.BlockSpec((2, tm, tn), lambda n, i: (0, i, n)), pl.BlockSpec((tm, tn), lambda n, i: (i, n))],
        out_shape=[jax.ShapeDtypeStruct((2, S, F), F32), jax.ShapeDtypeStruct((S, F), BF16)],
        compiler_params=_cp(("parallel", "parallel")))(a, w3, w3)


def _mm_dx_down_swiglu(dy, wdown3, gu, name, after=None):
    S, D = dy.shape
    _, F, D2 = wdown3.shape
    assert D == D2
    tm, tn = _tile(S, 1024, 8), _tile(F, 512, 128)
    extra = [] if after is None else [after]

    def body(a_ref, b_ref, gu_ref, *rest):
        o_ref = rest[len(extra)]
        d = lax.dot_general(a_ref[...], b_ref[...], (((1,), (1,)), ((), ())), preferred_element_type=F32)
        g, u = gu_ref[0], gu_ref[1]
        sg = jax.nn.sigmoid(g)
        o_ref[0] = (d * u * (sg * (1.0 + g * (1.0 - sg)))).astype(BF16)
        o_ref[1] = (d * g * sg).astype(BF16)

    return pl.pallas_call(
        body, name=name, grid=(S // tm, F // tn),
        in_specs=[pl.BlockSpec((tm, D), lambda i, n: (i, 0)), pl.BlockSpec((None, tn, D), lambda i, n: (0, n, 0)),
                  pl.BlockSpec((2, tm, tn), lambda i, n: (0, i, n))] + [pl.BlockSpec(memory_space=pl.ANY)] * len(extra),
        out_specs=pl.BlockSpec((2, tm, tn), lambda i, n: (0, i, n)),
        out_shape=jax.ShapeDtypeStruct((2, S, F), BF16),
        compiler_params=_cp(("parallel", "parallel")))(dy, wdown3, gu, *extra)


def _cross_scores(q, k, scale):
    s = lax.dot_general(q, k, (((1,), (1,)), ((), ())), preferred_element_type=F32) * scale
    m = jnp.max(s, axis=-1, keepdims=True)
    e = jnp.exp(s - m)
    return e / jnp.sum(e, axis=-1, keepdims=True)


def _cross_fwd(cq, ckv, name):
    S, D = cq.shape
    Nm = ckv.shape[0]
    dh = D // CROSS_HEADS
    tq = _tile(S, 512, 8)
    scale = dh ** -0.5

    def body(q_ref, kv_ref, o_ref):
        for h in range(CROSS_HEADS):
            cols = slice(h * dh, (h + 1) * dh)
            p = _cross_scores(q_ref[:, cols], kv_ref[:, cols], scale)
            o_ref[:, cols] = jnp.dot(p.astype(BF16), kv_ref[:, D + h * dh:D + (h + 1) * dh],
                                     preferred_element_type=F32).astype(BF16)

    return pl.pallas_call(
        body, name=name, grid=(S // tq,),
        in_specs=[pl.BlockSpec((tq, D), lambda i: (i, 0)), pl.BlockSpec((Nm, 2 * D), lambda i: (0, 0))],
        out_specs=pl.BlockSpec((tq, D), lambda i: (i, 0)),
        out_shape=jax.ShapeDtypeStruct((S, D), BF16), compiler_params=_cp(("parallel",)))(cq, ckv)


def _cross_bwd(cq, ckv, do, name):
    S, D = cq.shape
    Nm = ckv.shape[0]
    dh = D // CROSS_HEADS
    tq = _tile(S, 512, 8)
    scale = dh ** -0.5

    def body(q_ref, kv_ref, do_ref, dq_ref, dkv_ref):
        first = pl.program_id(0) == 0

        @pl.when(first)
        def _():
            dkv_ref[...] = jnp.zeros_like(dkv_ref)

        for h in range(CROSS_HEADS):
            cols = slice(h * dh, (h + 1) * dh)
            vcols = slice(D + h * dh, D + (h + 1) * dh)
            q, k, v = q_ref[:, cols], kv_ref[:, cols], kv_ref[:, vcols]
            doh = do_ref[:, cols].astype(BF16)
            p = _cross_scores(q, k, scale)
            dp = lax.dot_general(doh, v, (((1,), (1,)), ((), ())), preferred_element_type=F32)
            ds = (p * (dp - jnp.sum(p * dp, axis=-1, keepdims=True)) * scale).astype(BF16)
            dq_ref[:, cols] = jnp.dot(ds, k, preferred_element_type=F32).astype(BF16)
            dkv_ref[:, cols] += lax.dot_general(ds, q, (((0,), (0,)), ((), ())), preferred_element_type=F32)
            dkv_ref[:, vcols] += lax.dot_general(p.astype(BF16), doh, (((0,), (0,)), ((), ())),
                                                 preferred_element_type=F32)

    return pl.pallas_call(
        body, name=name, grid=(S // tq,),
        in_specs=[pl.BlockSpec((tq, D), lambda i: (i, 0)), pl.BlockSpec((Nm, 2 * D), lambda i: (0, 0)),
                  pl.BlockSpec((tq, D), lambda i: (i, 0))],
        out_specs=[pl.BlockSpec((tq, D), lambda i: (i, 0)), pl.BlockSpec((Nm, 2 * D), lambda i: (0, 0))],
        out_shape=[jax.ShapeDtypeStruct((S, D), BF16), jax.ShapeDtypeStruct((Nm, 2 * D), F32)],
        compiler_params=_cp(("arbitrary",)))(cq, ckv, do)


def _rel_bucket(dist):
    max_exact = REL_BUCKETS // 2
    d_f = jnp.maximum(dist, 1).astype(F32)
    large = max_exact + (jnp.log(d_f / max_exact) / math.log(REL_MAX_DIST / max_exact)
                         * (REL_BUCKETS - max_exact)).astype(jnp.int32)
    large = jnp.minimum(large, REL_BUCKETS - 1)
    return jnp.where(dist < max_exact, dist, large)


def _branch_tables(dilation):
    qi = jnp.arange(Q_BLOCK)[:, None]
    kj = jnp.arange(2 * Q_BLOCK)[None, :]
    m = qi - kj + Q_BLOCK
    return _rel_bucket(jnp.maximum(m, 0) * dilation), (m >= 0) & (m <= Q_BLOCK)


def _attn_logits(q, kcat, bias, first_block, scale):
    s = lax.dot_general(q, kcat, (((1,), (1,)), ((), ())), preferred_element_type=F32) * scale + bias
    col = lax.broadcasted_iota(jnp.int32, s.shape, 1)
    return jnp.where(jnp.logical_and(first_block, col < Q_BLOCK), NEG_INF, s)


def _attn_branch_fwd(proj, bias, dilation, H, name):
    S, NC = proj.shape
    AW = H * HEAD
    r = dilation
    hb = H if r == 1 else 1
    G = Q_BLOCK * r
    ng, nh = S // G, H // hb
    scale = HEAD ** -0.5

    def body(q_ref, kp_ref, kc_ref, vp_ref, vc_ref, b_ref, o_ref, l_ref):
        first_block = pl.program_id(0) == 0
        h0 = pl.program_id(1) * hb
        for rho in range(r):
            rows = pl.ds(rho, Q_BLOCK, stride=r) if r > 1 else slice(None)
            for hh in range(hb):
                cols = slice(hh * HEAD, (hh + 1) * HEAD)
                q = q_ref[rows, cols].astype(BF16)
                kcat = jnp.concatenate([kp_ref[rows, cols], kc_ref[rows, cols]], axis=0).astype(BF16)
                vcat = jnp.concatenate([vp_ref[rows, cols], vc_ref[rows, cols]], axis=0).astype(BF16)
                s = _attn_logits(q, kcat, b_ref[h0 + hh], first_block, scale)
                m = jnp.max(s, axis=-1, keepdims=True)
                e = jnp.exp(s - m)
                den = jnp.sum(e, axis=-1, keepdims=True)
                o_ref[rows, cols] = jnp.dot(e.astype(BF16), vcat, preferred_element_type=F32) / den
                l_ref[rows, cols] = jnp.broadcast_to(m + jnp.log(den), (Q_BLOCK, HEAD))

    def blk(part, prev):
        if prev:
            return pl.BlockSpec((G, hb * HEAD), lambda n, h: (jnp.maximum(n - 1, 0), part * nh + h))
        return pl.BlockSpec((G, hb * HEAD), lambda n, h: (n, part * nh + h))

    out = pl.BlockSpec((G, hb * HEAD), lambda n, h: (n, h))
    return pl.pallas_call(
        body, name=name, grid=(ng, nh),
        in_specs=[blk(0, False), blk(1, True), blk(1, False), blk(2, True), blk(2, False),
                  pl.BlockSpec((H, Q_BLOCK, 2 * Q_BLOCK), lambda n, h: (0, 0, 0))],
        out_specs=[out, out],
        out_shape=[jax.ShapeDtypeStruct((S, AW), F32)] * 2,
        compiler_params=_cp(("parallel", "parallel")))(proj, proj, proj, proj, proj, bias)


def _attn_merge(outs, lses, width_out, name):
    S, AW = outs[0].shape
    tr = _tile(S, 256, 8)

    def body(o0, o1, o2, l0, l1, l2, cat_ref, om_ref, lt_ref):
        a, b, c = l0[...], l1[...], l2[...]
        m = jnp.maximum(jnp.maximum(a, b), c)
        ea, eb, ec = jnp.exp(a - m), jnp.exp(b - m), jnp.exp(c - m)
        tot = ea + eb + ec
        o = (ea * o0[...] + eb * o1[...] + ec * o2[...]) / tot
        om_ref[...] = o
        cat_ref[...] = o.astype(BF16)
        lt_ref[...] = m + jnp.log(tot)

    row = pl.BlockSpec((tr, AW), lambda i: (i, 0))
    return pl.pallas_call(
        body, name=name, grid=(S // tr,), in_specs=[row] * 6, out_specs=[row, row, row],
        out_shape=[jax.ShapeDtypeStruct((S, width_out), BF16), jax.ShapeDtypeStruct((S, AW), F32),
                   jax.ShapeDtypeStruct((S, AW), F32)],
        compiler_params=_cp(("parallel",)))(*outs, *lses)


def _attn_delta(dcat, o_attn, H, name, after=None):
    S, AW = o_attn.shape
    tr = _tile(S, 256, 8)
    extra = [] if after is None else [after]

    def body(d_ref, o_ref, *rest):
        out_ref = rest[len(extra)]
        for h in range(H):
            cols = slice(h * HEAD, (h + 1) * HEAD)
            out_ref[:, cols] = jnp.broadcast_to(
                jnp.sum(d_ref[:, cols] * o_ref[:, cols], axis=-1, keepdims=True), (tr, HEAD))

    row = pl.BlockSpec((tr, AW), lambda i: (i, 0))
    return pl.pallas_call(body, name=name, grid=(S // tr,),
                          in_specs=[row, row] + [pl.BlockSpec(memory_space=pl.ANY)] * len(extra), out_specs=row,
                          out_shape=jax.ShapeDtypeStruct((S, AW), F32),
                          compiler_params=_cp(("parallel",)))(dcat, o_attn, *extra)


def _attn_branch_bwd(proj, dcat, lse_tot, delta, bias, dilation, H, name):
    S, NC = proj.shape
    AW = H * HEAD
    r = dilation
    hb = H if r == 1 else 1
    G = Q_BLOCK * r
    ng, nh = S // G, H // hb
    scale = HEAD ** -0.5

    def body(q_ref, kp_ref, kc_ref, vp_ref, vc_ref, do_ref, l_ref, t_ref, b_ref,
             dq_ref, dk_ref, dv_ref, db_ref, ck_ref, cv_ref):
        hblk, n = pl.program_id(0), pl.program_id(1)
        h0 = hblk * hb

        @pl.when(jnp.logical_and(hblk == 0, n == 0))
        def _():
            db_ref[...] = jnp.zeros_like(db_ref)

        @pl.when(n == 0)
        def _():
            ck_ref[...] = jnp.zeros_like(ck_ref)
            cv_ref[...] = jnp.zeros_like(cv_ref)

        @pl.when(n < ng)
        def _():
            for rho in range(r):
                rows = pl.ds(rho, Q_BLOCK, stride=r) if r > 1 else slice(None)
                for hh in range(hb):
                    cols = slice(hh * HEAD, (hh + 1) * HEAD)
                    q = q_ref[rows, cols].astype(BF16)
                    kcat = jnp.concatenate([kp_ref[rows, cols], kc_ref[rows, cols]], axis=0).astype(BF16)
                    vcat = jnp.concatenate([vp_ref[rows, cols], vc_ref[rows, cols]], axis=0).astype(BF16)
                    doh = do_ref[rows, cols].astype(BF16)
                    s = _attn_logits(q, kcat, b_ref[h0 + hh], n == 0, scale)
                    lt = l_ref[rows, cols]
                    dl = t_ref[rows, cols]
                    p = jnp.exp(s - jnp.concatenate([lt, lt], axis=1))
                    dp = lax.dot_general(doh, vcat, (((1,), (1,)), ((), ())), preferred_element_type=F32)
                    ds = p * (dp - jnp.concatenate([dl, dl], axis=1))
                    db_ref[h0 + hh] += ds
                    dsb = (ds * scale).astype(BF16)
                    dq_ref[rows, cols] = jnp.dot(dsb, kcat, preferred_element_type=F32)
                    dkc = lax.dot_general(dsb, q, (((0,), (0,)), ((), ())), preferred_element_type=F32)
                    dvc = lax.dot_general(p.astype(BF16), doh, (((0,), (0,)), ((), ())), preferred_element_type=F32)
                    dk_ref[rows, cols] = ck_ref[rows, cols] + dkc[:Q_BLOCK]
                    dv_ref[rows, cols] = cv_ref[rows, cols] + dvc[:Q_BLOCK]
                    ck_ref[rows, cols] = dkc[Q_BLOCK:]
                    cv_ref[rows, cols] = dvc[Q_BLOCK:]

        @pl.when(n == ng)
        def _():
            dk_ref[...] = ck_ref[...]
            dv_ref[...] = cv_ref[...]

    last = ng - 1

    def cur(part):
        return pl.BlockSpec((G, hb * HEAD), lambda h, n: (jnp.minimum(n, last), part * nh + h))

    def prev(part):
        return pl.BlockSpec((G, hb * HEAD), lambda h, n: (jnp.clip(n - 1, 0, last), part * nh + h))

    table = pl.BlockSpec((H, Q_BLOCK, 2 * Q_BLOCK), lambda h, n: (0, 0, 0))
    return pl.pallas_call(
        body, name=name, grid=(nh, ng + 1),
        in_specs=[cur(0), prev(1), cur(1), prev(2), cur(2), cur(0), cur(0), cur(0), table],
        out_specs=[cur(0), prev(0), prev(0), table],
        out_shape=[jax.ShapeDtypeStruct((S, AW), F32)] * 3 + [jax.ShapeDtypeStruct((H, Q_BLOCK, 2 * Q_BLOCK), F32)],
        scratch_shapes=[pltpu.VMEM((G, hb * HEAD), F32), pltpu.VMEM((G, hb * HEAD), F32)],
        compiler_params=_cp(("arbitrary", "arbitrary")))(proj, proj, proj, proj, proj, dcat, lse_tot, delta, bias)


def _bias_grad(dbs, onehots, H):
    def body(d0, d1, d2, e0, e1, e2, o_ref):
        acc = jnp.zeros((H, REL_BUCKETS), F32)
        for d, e in ((d0, e0), (d1, e1), (d2, e2)):
            acc = acc + lax.dot_general(d[...], e[...], (((1,), (1,)), ((), ())), preferred_element_type=F32,
                                        precision=lax.Precision.HIGHEST)
        o_ref[...] = acc

    flat = [d.reshape(H, 2 * Q_BLOCK * Q_BLOCK) for d in dbs]
    return pl.pallas_call(body, name="bias_grad", out_shape=jax.ShapeDtypeStruct((H, REL_BUCKETS), F32),
                          compiler_params=pltpu.CompilerParams(vmem_limit_bytes=VMEM_LIMIT))(*flat, *onehots)


def _assemble_dproj(dqkv, dhg, name):
    S, AW = dqkv[0][0].shape
    tr = _tile(S, 256, 8)
    ins = [dqkv[g][c] for g in range(3) for c in range(3)] + list(dhg)

    def body(*refs):
        o_ref = refs[-1]
        j = pl.program_id(1)
        for c in range(3):
            @pl.when(j == c)
            def _(c=c):
                o_ref[...] = (refs[c][...] + refs[3 + c][...] + refs[6 + c][...]).astype(BF16)

        for k in range(4):
            @pl.when(j == 3 + k)
            def _(k=k):
                o_ref[...] = refs[9 + k][...]

    row = pl.BlockSpec((tr, AW), lambda i, j: (i, 0))
    return pl.pallas_call(
        body, name=name, grid=(S // tr, 7), in_specs=[row] * 13,
        out_specs=pl.BlockSpec((tr, AW), lambda i, j: (i, j)),
        out_shape=jax.ShapeDtypeStruct((S, 7 * AW), BF16),
        compiler_params=_cp(("parallel", "arbitrary")))(*ins)


def _tri(n, upper):
    r = lax.broadcasted_iota(jnp.int32, (n, n), 0)
    c = lax.broadcasted_iota(jnp.int32, (n, n), 1)
    return jnp.where(c >= r if upper else c <= r, 1.0, 0.0).astype(F32)


def _hg_gates(fz, qz, lb):
    sig = jax.nn.sigmoid(fz)
    f = lb + (1.0 - lb) * sig
    sq = jax.nn.sigmoid(qz)
    return sig, f, jnp.log(f), 1.0 - f, qz * sq, sq


def _hg_fwd(proj, cat, lb, gain, HH, name):
    S, NC = proj.shape
    W = HH * HEAD
    C = HG_CHUNK
    hb = min(HG_HEADS_PER_STEP, HH)
    nh = HH // hb
    tb = _tile(S, 2048 // hb, C)
    ncb = tb // C
    base = 3 * nh
    cat_cols = cat.shape[1] // (hb * HEAD)

    def body(fz_ref, iv_ref, qz_ref, gz_ref, lb_ref, g_ref, cat_in, cat_ref, o_ref, st_ref, state):
        del cat_in

        @pl.when(pl.program_id(1) == 0)
        def _():
            state[...] = jnp.zeros_like(state)

        tri = _tri(C, False)
        row8 = lax.broadcasted_iota(jnp.int32, (8, 1), 0)

        def chunk(c, carry):
            rows = pl.ds(pl.multiple_of(c * C, C), C)
            for hh in range(hb):
                cols = slice(hh * HEAD, (hh + 1) * HEAD)
                lbv, gv = lb_ref[:, cols], g_ref[:, cols]
                fz, vv, qz, gz = fz_ref[rows, cols], iv_ref[rows, cols], qz_ref[rows, cols], gz_ref[rows, cols]
                _, f, lf, kk, qq, _ = _hg_gates(fz, qz, lbv)
                b = jnp.dot(tri, lf, preferred_element_type=F32, precision=lax.Precision.HIGHEST)
                bl = b[C - 1:C, :]
                st = state[hh]
                stb = st.astype(BF16)
                st_ref[hh, c] = stb
                o = lax.dot_general((qq * jnp.exp(b)).astype(BF16), stb, (((1,), (1,)), ((), ())),
                                    preferred_element_type=F32)
                tiles = []
                for t0 in range(0, C, 8):
                    bt, qt, acc = b[t0:t0 + 8, :], qq[t0:t0 + 8, :], o[t0:t0 + 8, :]
                    for s in range(min(C, t0 + 8)):
                        diff = bt - b[s:s + 1, :]
                        if s >= t0:
                            diff = jnp.minimum(diff, 0.0)
                        a = jnp.sum(qt * jnp.exp(diff) * kk[s:s + 1, :], axis=-1, keepdims=True)
                        if s >= t0:
                            a = jnp.where(row8 >= s - t0, a, 0.0)
                        acc = acc + a * vv[s:s + 1, :]
                    tiles.append(acc)
                o = jnp.concatenate(tiles, axis=0)
                kh = (kk * jnp.exp(bl - b)).astype(BF16)
                upd = lax.dot_general(vv.astype(BF16), kh, (((0,), (0,)), ((), ())), preferred_element_type=F32)
                state[hh] = st * jnp.exp(bl) + upd
                o_ref[rows, cols] = o
                n = o * lax.rsqrt(jnp.mean(o * o, axis=-1, keepdims=True) + RMS_EPS)
                cat_ref[rows, cols] = (n * gv * (gz * jax.nn.sigmoid(gz))).astype(BF16)
            return carry

        lax.fori_loop(0, ncb, chunk, 0)

    def col(k):
        return pl.BlockSpec((tb, hb * HEAD), lambda h, i: (i, base + k * nh + h))

    vec = pl.BlockSpec((1, hb * HEAD), lambda h, i: (0, h))
    cat_out, o_raw, states = pl.pallas_call(
        body, name=name, grid=(nh, S // tb),
        in_specs=[col(0), col(1), col(2), col(3), vec, vec, pl.BlockSpec(memory_space=pl.ANY)],
        out_specs=[pl.BlockSpec((tb, hb * HEAD), lambda h, i: (i, cat_cols - nh + h)),
                   pl.BlockSpec((tb, hb * HEAD), lambda h, i: (i, h)),
                   pl.BlockSpec((hb, ncb, HEAD, HEAD), lambda h, i: (h, i, 0, 0))],
        out_shape=[jax.ShapeDtypeStruct(cat.shape, BF16), jax.ShapeDtypeStruct((S, W), F32),
                   jax.ShapeDtypeStruct((HH, S // C, HEAD, HEAD), BF16)],
        scratch_shapes=[pltpu.VMEM((hb, HEAD, HEAD), F32)],
        input_output_aliases={6: 0},
        compiler_params=_cp(("parallel", "arbitrary")))(proj, proj, proj, proj, lb.reshape(1, W), gain.reshape(1, W), cat)
    return cat_out, o_raw, states


def _hg_bwd(proj, dcat, o_raw, states, lb, gain, after, HH, name):
    S, NC = proj.shape
    W = HH * HEAD
    C = HG_CHUNK
    hb = min(HG_HEADS_PER_STEP, HH)
    nh = HH // hb
    tb = _tile(S, 2048 // hb, C)
    ncb = tb // C
    nt = S // tb
    base = 3 * nh
    dcat_cols = dcat.shape[1] // (hb * HEAD)

    def body(fz_ref, iv_ref, qz_ref, gz_ref, dc_ref, o_ref, st_ref, lb_ref, g_ref, after_ref,
             dfz_ref, div_ref, dqz_ref, dgz_ref, dlb_ref, dg_ref, dstate):
        del after_ref

        @pl.when(pl.program_id(1) == 0)
        def _():
            dstate[...] = jnp.zeros_like(dstate)
            dlb_ref[...] = jnp.zeros_like(dlb_ref)
            dg_ref[...] = jnp.zeros_like(dg_ref)

        tri_lo, tri_up = _tri(C, False), _tri(C, True)
        row = lax.broadcasted_iota(jnp.int32, (C, 1), 0)
        row8 = lax.broadcasted_iota(jnp.int32, (8, 1), 0)

        def chunk(cc, carry):
            c = ncb - 1 - cc
            rows = pl.ds(pl.multiple_of(c * C, C), C)
            for hh in range(hb):
                cols = slice(hh * HEAD, (hh + 1) * HEAD)
                lbv, gv = lb_ref[:, cols], g_ref[:, cols]
                fz, vv, qz, gz = fz_ref[rows, cols], iv_ref[rows, cols], qz_ref[rows, cols], gz_ref[rows, cols]
                sig, f, lf, kk, qq, sq = _hg_gates(fz, qz, lbv)
                b = jnp.dot(tri_lo, lf, preferred_element_type=F32, precision=lax.Precision.HIGHEST)
                bl = b[C - 1:C, :]
                eb, ebl = jnp.exp(b), jnp.exp(bl)
                ekb = jnp.exp(bl - b)
                qh, kh = qq * eb, kk * ekb
                o = o_ref[rows, cols]
                dhg = dc_ref[rows, cols]
                sgz = jax.nn.sigmoid(gz)
                silu_g = gz * sgz
                rr = lax.rsqrt(jnp.mean(o * o, axis=-1, keepdims=True) + RMS_EPS)
                nrm = o * rr
                dpre = dhg * silu_g
                dgz_ref[rows, cols] = (dhg * nrm * gv * (sgz * (1.0 + gz * (1.0 - sgz)))).astype(BF16)
                dg_ref[:, cols] += jnp.sum(dpre * nrm, axis=0, keepdims=True)
                dn = dpre * gv
                do = rr * (dn - nrm * jnp.mean(dn * nrm, axis=-1, keepdims=True))
                dob = do.astype(BF16)
                st0 = st_ref[hh, c]
                dst1 = dstate[hh]
                dst1b = dst1.astype(BF16)
                dqh = jnp.dot(dob, st0, preferred_element_type=F32)
                dv = lax.dot_general(kh.astype(BF16), dst1b, (((1,), (1,)), ((), ())), preferred_element_type=F32)
                dkh = jnp.dot(vv.astype(BF16), dst1b, preferred_element_type=F32)
                dstate[hh] = dst1 * ebl + lax.dot_general(dob, qh.astype(BF16), (((0,), (0,)), ((), ())),
                                                          preferred_element_type=F32)
                extra = (jnp.sum(dkh * kh, axis=0, keepdims=True)
                         + ebl * jnp.sum(st0.astype(F32) * dst1, axis=0, keepdims=True))
                dq = dqh * eb
                dk = dkh * ekb
                starts = list(range(0, C, 8))
                b_t, q_t, do_t = ([v[t0:t0 + 8, :] for t0 in starts] for v in (b, qq, do))
                dq_t = [dq[t0:t0 + 8, :] for t0 in starts]
                dk_rows, dv_rows = [], []
                for s in range(C):
                    ks, vs, bs = kk[s:s + 1, :], vv[s:s + 1, :], b[s:s + 1, :]
                    dk_s = dv_s = None
                    for ti, t0 in enumerate(starts):
                        if t0 + 8 <= s:
                            continue
                        diff = b_t[ti] - bs
                        if s >= t0:
                            diff = jnp.minimum(diff, 0.0)
                        dec = jnp.exp(diff)
                        da = jnp.sum(do_t[ti] * vs, axis=-1, keepdims=True)
                        a = jnp.sum(q_t[ti] * dec * ks, axis=-1, keepdims=True)
                        if s >= t0:
                            keep = row8 >= s - t0
                            da, a = jnp.where(keep, da, 0.0), jnp.where(keep, a, 0.0)
                        gd = da * dec
                        dq_t[ti] = dq_t[ti] + gd * ks
                        pk = jnp.sum(gd * q_t[ti], axis=0, keepdims=True)
                        pv = jnp.sum(a * do_t[ti], axis=0, keepdims=True)
                        dk_s = pk if dk_s is None else dk_s + pk
                        dv_s = pv if dv_s is None else dv_s + pv
                    dk_rows.append(dk_s)
                    dv_rows.append(dv_s)
                dq = jnp.concatenate(dq_t, axis=0)
                dk = dk + jnp.concatenate(dk_rows, axis=0)
                dv = dv + jnp.concatenate(dv_rows, axis=0)
                db = qq * dq - kk * dk + jnp.where(row == C - 1, extra, 0.0)
                dlf = jnp.dot(tri_up, db, preferred_element_type=F32, precision=lax.Precision.HIGHEST)
                df = dlf / f - dk
                dfz_ref[rows, cols] = (df * (1.0 - lbv) * sig * (1.0 - sig)).astype(BF16)
                dlb_ref[:, cols] += jnp.sum(df * (1.0 - sig), axis=0, keepdims=True)
                dqz_ref[rows, cols] = (dq * (sq * (1.0 + qz * (1.0 - sq)))).astype(BF16)
                div_ref[rows, cols] = dv.astype(BF16)
            return carry

        lax.fori_loop(0, ncb, chunk, 0)

    def col(k):
        return pl.BlockSpec((tb, hb * HEAD), lambda h, i: (nt - 1 - i, base + k * nh + h))

    ocol = pl.BlockSpec((tb, hb * HEAD), lambda h, i: (nt - 1 - i, h))
    vec = pl.BlockSpec((1, hb * HEAD), lambda h, i: (0, h))
    outs = pl.pallas_call(
        body, name=name, grid=(nh, nt),
        in_specs=[col(0), col(1), col(2), col(3),
                  pl.BlockSpec((tb, hb * HEAD), lambda h, i: (nt - 1 - i, dcat_cols - nh + h)),
                  pl.BlockSpec((tb, hb * HEAD), lambda h, i: (nt - 1 - i, h)),
                  pl.BlockSpec((hb, ncb, HEAD, HEAD), lambda h, i: (h, nt - 1 - i, 0, 0)), vec, vec,
                  pl.BlockSpec(memory_space=pl.ANY)],
        out_specs=[ocol, ocol, ocol, ocol, vec, vec],
        out_shape=[jax.ShapeDtypeStruct((S, W), BF16)] * 4 + [jax.ShapeDtypeStruct((1, W), F32)] * 2,
        scratch_shapes=[pltpu.VMEM((hb, HEAD, HEAD), F32)],
        compiler_params=_cp(("parallel", "arbitrary")))(proj, proj, proj, proj, dcat, o_raw, states,
                                                        lb.reshape(1, W), gain.reshape(1, W), after)
    return outs


def _lb_all(lb_logits):
    p = jax.nn.softmax(lb_logits.astype(F32), axis=0)
    return jnp.cumsum(p, axis=0) - p


def _local_step(x, mem, target, rel_bias, lb_logits, gains, hg_norm, comm):
    S, D = x.shape
    L = gains.shape[0]
    H = (D // 2) // HEAD
    lb = _lb_all(lb_logits)
    tables = [_branch_tables(r) for _, r in BRANCHES]
    onehots = [jnp.where(ok[None], jax.nn.one_hot(bk, REL_BUCKETS, dtype=F32, axis=0), 0.0)
               .reshape(REL_BUCKETS, 2 * Q_BLOCK * Q_BLOCK) for bk, ok in tables]
    bias = [jnp.where(ok[None], jnp.dot(rel_bias.astype(F32).T, oh, precision=lax.Precision.HIGHEST)
                      .reshape(H, Q_BLOCK, 2 * Q_BLOCK), NEG_INF) for oh, (_, ok) in zip(onehots, tables)]

    saved = []
    _, h = _norm_fwd(x, None, None, gains[0, 0], "norm_first")
    xl = x
    for l in range(L):
        g = gains[l]
        w_in, token = comm.weights(l, xl)
        proj = _mm_nn(h, w_in, F32, "mm_in", after=token)
        outs, lses = [], []
        for gi, (_, r) in enumerate(BRANCHES):
            o, ls = _attn_branch_fwd(proj, bias[gi], r, H, f"attn_fwd_d{r}")
            outs.append(o)
            lses.append(ls)
        cat, o_attn, lse_tot = _attn_merge(outs, lses, D, "attn_merge")
        cat, o_raw, states = _hg_fwd(proj, cat, lb[l], hg_norm[l], H, "hgrn_fwd")
        w = dict(comm.rest(l, cat), **{"in": w_in})
        mix = _mm_nn(cat, w["out"], F32, "mm_out")
        x1, hc = _norm_fwd(xl, mix, g[1], g[2], "norm_post_pre")
        _, mn = _norm_fwd(mem, None, None, g[3], "norm_mem")
        cq = _mm_nn(hc, w["cq"], BF16, "mm_cq")
        ckv = _mm_nn(mn, w["ckv"], BF16, "mm_ckv")
        co = _cross_fwd(cq, ckv, "cross_fwd")
        cop = _mm_nn(co, w["co"], F32, "mm_co")
        x2, hf = _norm_fwd(x1, cop, g[4], g[5], "norm_post_pre")
        gu, act = _mm_gate_up(hf, w["gu"], "mm_gu")
        y = _mm_nn(act, w["down"], F32, "mm_down")
        g_next = gains[l + 1, 0] if l + 1 < L else None
        x3, h_next = _norm_fwd(x2, y, g[6], g_next, "norm_post_pre" if l + 1 < L else "norm_post")
        saved.append(dict(x0=xl, h0=h, proj=proj, cat=cat, o_attn=o_attn, lse_tot=lse_tot, o_raw=o_raw, states=states,
                          mix=mix, x1=x1, hc=hc, mn=mn, cq=cq, ckv=ckv, co=co, cop=cop, x2=x2, hf=hf, gu=gu, act=act,
                          y=y, x3=x3, w=w))
        comm.layer_done(l, x3)
        xl, h = x3, h_next

    loss, dres = _loss_and_grad(xl, target)

    dh_next = bwd_token = None
    d_gains = [[None] * 7 for _ in range(L)]
    d_lb = [None] * L
    d_hg = [None] * L
    d_bias = [jnp.zeros((H, Q_BLOCK, 2 * Q_BLOCK), F32) for _ in BRANCHES]
    for l in reversed(range(L)):
        g, sv = gains[l], saved[l]
        w = sv["w"]
        g_next = gains[l + 1, 0] if l + 1 < L else None
        t3, dy, dg_next, dg6 = _norm_bwd(dres, dh_next, sv["x3"], sv["y"], g_next, g[6], "norm_bwd")
        if l + 1 < L:
            d_gains[l + 1][0] = dg_next
        d_gains[l][6] = dg6
        gw = {}
        gw["down"] = _mm_tn(sv["act"], dy, 1, BF16, "mm_dw_down")
        dgu = _mm_dx_down_swiglu(dy, w["down"], sv["gu"], "mm_dx_down", after=bwd_token)
        gw["gu"] = _mm_tn(sv["hf"], dgu, w["gu"].shape[0], BF16, "mm_dw_gu")
        dhf = _mm_nt(dgu, w["gu"], F32, "mm_dx_gu")
        t2, dcop, d_gains[l][5], d_gains[l][4] = _norm_bwd(t3, dhf, sv["x2"], sv["cop"], g[5], g[4], "norm_bwd")
        gw["co"] = _mm_tn(sv["co"], dcop, 1, BF16, "mm_dw_sq")
        dco = _mm_nt(dcop, w["co"], F32, "mm_dx_sq")
        dcq, dckv = _cross_bwd(sv["cq"], sv["ckv"], dco, "cross_bwd")
        gw["cq"] = _mm_tn(sv["hc"], dcq, 1, BF16, "mm_dw_sq")
        dhc = _mm_nt(dcq, w["cq"], F32, "mm_dx_sq")
        gw["ckv"] = _mm_tn(sv["mn"], dckv, w["ckv"].shape[0], BF16, "mm_dw_ckv")
        dmn = _mm_nt(dckv, w["ckv"], F32, "mm_dx_ckv")
        _, _, d_gains[l][3], _ = _norm_bwd(None, dmn, mem, None, g[3], None, "norm_bwd_mem")
        t1, dmix, d_gains[l][2], d_gains[l][1] = _norm_bwd(t2, dhc, sv["x1"], sv["mix"], g[2], g[1], "norm_bwd")
        gw["out"] = _mm_tn(sv["cat"], dmix, 1, BF16, "mm_dw_sq")
        dcat = _mm_nt(dmix, w["out"], F32, "mm_dx_sq")
        delta = _attn_delta(dcat, sv["o_attn"], H, "attn_delta", after=comm.early_grads(l, gw))
        dqkv = []
        for gi, (_, r) in enumerate(BRANCHES):
            dq, dk, dv, db = _attn_branch_bwd(sv["proj"], dcat, sv["lse_tot"], delta, bias[gi], r, H, f"attn_bwd_d{r}")
            dqkv.append((dq, dk, dv))
            d_bias[gi] = d_bias[gi] + db
        dfz, div, dqz, dgz, dlb_l, dhg_l = _hg_bwd(sv["proj"], dcat, sv["o_raw"], sv["states"], lb[l], hg_norm[l],
                                                   delta, H, "hgrn_bwd")
        d_lb[l], d_hg[l] = dlb_l[0], dhg_l[0]
        dproj = _assemble_dproj(dqkv, [dfz, div, dqz, dgz], "assemble_dproj")
        gw["in"] = _mm_tn(sv["h0"], dproj, w["in"].shape[0], BF16, "mm_dw_in")
        dh_next = _mm_nt(dproj, w["in"], F32, "mm_dx_in")
        dres = t1
        bwd_token = comm.grads(l, gw, gw["in"])
    grad_x, _, d_gains[0][0], _ = _norm_bwd(dres, dh_next, x, None, gains[0, 0], None, "norm_bwd_first")

    d_rel_bias = _bias_grad(d_bias, onehots, H).T
    d_gains = jnp.stack([jnp.stack(row) for row in d_gains])
    return loss, grad_x, d_rel_bias, jnp.stack(d_lb), jnp.stack(d_hg), d_gains


def _place():
    return lax.axis_index("x"), lax.axis_index("y"), lax.axis_index("c")


def _all_gather(shards, name):
    n = len(shards)
    HBM = pl.BlockSpec(memory_space=pltpu.HBM)

    def body(*refs):
        ins, outs = refs[:n], refs[n:2 * n]
        send_sems, recv_sems, local_sems = refs[2 * n:]
        x, y, c = _place()
        me, sibling = (x, y, c), (x, y, 1 - c)
        chips = [(1 - x, y), (x, 1 - y), (1 - x, 1 - y)]

        def copy(a, k, block, to, own=False):
            dst = outs[a].at[4 * block[0] + 2 * block[1] + block[2]]
            return pltpu.make_async_remote_copy(
                src_ref=ins[a] if own else dst, dst_ref=dst, send_sem=send_sems.at[7 * a + k],
                recv_sem=recv_sems.at[7 * a + k], device_id=to, device_id_type=MESH)

        mine = [pltpu.make_async_copy(ins[a], outs[a].at[4 * x + 2 * y + c], local_sems.at[a]) for a in range(n)]
        for cp in mine:
            cp.start()
        first = []
        for a in range(n):
            first.append(copy(a, 0, me, sibling, own=True))
            first += [copy(a, 1 + j, me, (*chip, c), own=True) for j, chip in enumerate(chips)]
        for cp in first:
            cp.start()
        passed = []
        for j, chip in enumerate(chips):
            for a in range(n):
                copy(a, 1 + j, (*chip, c), me).wait_recv()
                fwd = copy(a, 4 + j, (*chip, c), sibling)
                fwd.start()
                passed.append(fwd)
        for a in range(n):
            copy(a, 0, sibling, me).wait_recv()
            for j, chip in enumerate(chips):
                copy(a, 4 + j, (*chip, 1 - c), me).wait_recv()
        for cp in first + passed:
            cp.wait_send()
        for cp in mine:
            cp.wait()

    return pl.pallas_call(
        body, name=name, in_specs=[HBM] * n, out_specs=[HBM] * n,
        out_shape=[jax.ShapeDtypeStruct((N_DEV,) + s.shape, s.dtype) for s in shards],
        scratch_shapes=[pltpu.SemaphoreType.DMA((7 * n,)), pltpu.SemaphoreType.DMA((7 * n,)),
                        pltpu.SemaphoreType.DMA((n,))],
    )(*shards)


def _exchange_sibling(grads, name):
    n = len(grads)
    HBM = pl.BlockSpec(memory_space=pltpu.HBM)

    def body(*refs):
        ins, outs = refs[:n], refs[n:2 * n]
        send_sems, recv_sems = refs[2 * n:]
        x, y, c = _place()
        sibling = (x, y, 1 - c)
        for a in range(n):
            for q in range(4):
                pltpu.make_async_remote_copy(
                    src_ref=ins[a].at[2 * q + 1 - c], dst_ref=outs[a].at[q], send_sem=send_sems.at[a],
                    recv_sem=recv_sems.at[a], device_id=sibling, device_id_type=MESH).start()
        for a in range(n):
            pltpu.make_async_remote_copy(
                src_ref=ins[a].at[pl.ds(0, 4)], dst_ref=outs[a], send_sem=send_sems.at[a], recv_sem=recv_sems.at[a],
                device_id=sibling, device_id_type=MESH).wait()

    return pl.pallas_call(
        body, name=name, in_specs=[HBM] * n, out_specs=[HBM] * n,
        out_shape=[jax.ShapeDtypeStruct((4,) + g.shape[1:], g.dtype) for g in grads],
        scratch_shapes=[pltpu.SemaphoreType.DMA((n,)), pltpu.SemaphoreType.DMA((n,))],
    )(*grads)


def _exchange_chips(parts, name):
    n = len(parts)
    HBM = pl.BlockSpec(memory_space=pltpu.HBM)

    def body(*refs):
        ins, outs = refs[:n], refs[n:2 * n]
        send_sems, recv_sems = refs[2 * n:]
        x, y, c = _place()
        chips = [(1 - x, y), (x, 1 - y), (1 - x, 1 - y)]
        copies = []
        for a in range(n):
            for j, chip in enumerate(chips):
                cp = pltpu.make_async_remote_copy(
                    src_ref=ins[a].at[2 * chip[0] + chip[1]], dst_ref=outs[a].at[j], send_sem=send_sems.at[3 * a + j],
                    recv_sem=recv_sems.at[3 * a + j], device_id=(*chip, c), device_id_type=MESH)
                cp.start()
                copies.append(cp)
        for cp in copies:
            cp.wait()

    return pl.pallas_call(
        body, name=name, in_specs=[HBM] * n, out_specs=[HBM] * n,
        out_shape=[jax.ShapeDtypeStruct((3,) + p.shape[1:], p.dtype) for p in parts],
        scratch_shapes=[pltpu.SemaphoreType.DMA((3 * n,)), pltpu.SemaphoreType.DMA((3 * n,))],
    )(*parts)


def _others(x, y, c):
    out = []
    for k in range(1, N_DEV):
        px = 1 - x if (k >> 2) & 1 else x
        py = 1 - y if (k >> 1) & 1 else y
        pc = 1 - c if k & 1 else c
        out.append(((px, py, pc), 4 * px + 2 * py + pc))
    return out


_HBM_SPEC = pl.BlockSpec(memory_space=pltpu.HBM)
_SEM_SPEC = pl.BlockSpec(memory_space=pltpu.SEMAPHORE)
_EFFECT = pltpu.SideEffectType.DATAFLOW_SIDE_EFFECTING


def _copies_start(srcs, lands, after, src_block, name):
    n = len(srcs)

    def body(*refs):
        src_refs, land_refs = refs[:n], refs[n:2 * n]
        send_sems, recv_sems = refs[2 * n + 1], refs[2 * n + 2]
        token = refs[-1]
        x, y, c = _place()
        mine = 4 * x + 2 * y + c
        for a in range(n):
            for k, (peer, block) in enumerate(_others(x, y, c)):
                pltpu.make_async_remote_copy(
                    src_ref=src_refs[a].at[block] if src_block else src_refs[a],
                    dst_ref=land_refs[a].at[k] if src_block else land_refs[a].at[mine],
                    send_sem=send_sems.at[7 * a + k], recv_sem=recv_sems.at[7 * a + k],
                    device_id=peer, device_id_type=MESH).start()
        token[...] = jnp.zeros_like(token)

    outs = pl.pallas_call(
        body, name=name,
        out_shape=(pltpu.SemaphoreType.DMA((7 * n,)), pltpu.SemaphoreType.DMA((7 * n,)),
                   *[pltpu.HBM(s.shape, s.dtype) for s in srcs], *[pltpu.HBM(t.shape, t.dtype) for t in lands],
                   jax.ShapeDtypeStruct((8, 128), F32)),
        in_specs=[_HBM_SPEC] * (2 * n) + [pl.BlockSpec(memory_space=pl.ANY)],
        out_specs=(_SEM_SPEC, _SEM_SPEC, *[_HBM_SPEC] * (2 * n), pl.BlockSpec(memory_space=pltpu.VMEM)),
        input_output_aliases={i: 2 + i for i in range(2 * n)},
        compiler_params=pltpu.CompilerParams(has_side_effects=_EFFECT),
    )(*[pltpu.with_memory_space_constraint(s, pltpu.HBM) for s in srcs],
      *[pltpu.with_memory_space_constraint(t, pltpu.HBM) for t in lands], after)
    return outs[0], outs[1], list(outs[2:2 + n]), list(outs[2 + n:2 + 2 * n]), outs[-1]


def _copies_wait(send_sems, recv_sems, srcs, lands, after, src_block, name):
    n = len(srcs)

    def body(*refs):
        src_refs, land_refs = refs[:n], refs[n:2 * n]
        send_sems, recv_sems = refs[2 * n], refs[2 * n + 1]
        x, y, c = _place()
        for a in range(n):
            for k, (peer, block) in enumerate(_others(x, y, c)):
                cp = pltpu.make_async_remote_copy(
                    src_ref=src_refs[a].at[block] if src_block else src_refs[a],
                    dst_ref=land_refs[a].at[k] if src_block else land_refs[a].at[block],
                    send_sem=send_sems.at[7 * a + k], recv_sem=recv_sems.at[7 * a + k],
                    device_id=peer, device_id_type=MESH)
                cp.wait_send()
                cp.wait_recv()

    outs = pl.pallas_call(
        body, name=name,
        out_shape=(*[pltpu.HBM(s.shape, s.dtype) for s in srcs], *[pltpu.HBM(t.shape, t.dtype) for t in lands]),
        in_specs=[_HBM_SPEC] * (2 * n) + [_SEM_SPEC, _SEM_SPEC, pl.BlockSpec(memory_space=pl.ANY)],
        out_specs=tuple([_HBM_SPEC] * (2 * n)),
        input_output_aliases={i: i for i in range(2 * n)},
        compiler_params=pltpu.CompilerParams(has_side_effects=_EFFECT),
    )(*srcs, *lands, send_sems, recv_sems, after)
    return list(outs[:n]), list(outs[n:])


def _pair_sum(grad, recv, c_idx, name):
    _, R, C = grad.shape
    tr = _tile(R, 512, 16)

    def body(c_ref, g_ref, p_ref, o_ref):
        del c_ref
        o_ref[...] = (g_ref[...].astype(F32) + p_ref[...].astype(F32)).astype(BF16)

    spec = pltpu.PrefetchScalarGridSpec(
        num_scalar_prefetch=1, grid=(4, R // tr),
        in_specs=[pl.BlockSpec((None, tr, C), lambda q, i, c: (2 * q + c[0], i, 0)),
                  pl.BlockSpec((None, tr, C), lambda q, i, c: (q, i, 0))],
        out_specs=pl.BlockSpec((None, tr, C), lambda q, i, c: (q, i, 0)))
    return pl.pallas_call(body, name=name, grid_spec=spec, out_shape=jax.ShapeDtypeStruct((4, R, C), BF16),
                          compiler_params=_cp(("parallel", "parallel")))(c_idx, grad, recv)


def _adam(w, g, m, v):
    m2 = ADAM_B1 * m + (1.0 - ADAM_B1) * g
    v2 = ADAM_B2 * v + (1.0 - ADAM_B2) * (g * g)
    m_hat = m2 / (1.0 - ADAM_B1 ** ADAM_STEP)
    v_hat = v2 / (1.0 - ADAM_B2 ** ADAM_STEP)
    return -ADAM_LR * (m_hat / (jnp.sqrt(v_hat) + ADAM_EPS) + ADAM_WD * w), m2, v2


def _reduce_update(part, recv, own_idx, w, m, v, layer, prev, name):
    L, R, C = w.shape
    nr = recv.shape[0]
    tr = _tile(R, 128, 16)
    has_prev = prev is not None

    def body(idx_ref, q_ref, *rest):
        del idx_ref
        r_refs, (w_ref, m_ref, v_ref) = rest[:nr], rest[nr:nr + 3]
        g_ref, d_ref, m2_ref, v2_ref = rest[-4:]
        g = q_ref[...].astype(F32)
        for r_ref in r_refs:
            g = g + r_ref[...].astype(F32)
        d, m2, v2 = _adam(w_ref[...], g, m_ref[...], v_ref[...])
        g_ref[...] = g
        d_ref[...] = d
        m2_ref[...] = m2
        v2_ref[...] = v2

    def rblk(j):
        return pl.BlockSpec((None, tr, C), lambda i, k: (j, i, 0))

    lay = pl.BlockSpec((None, tr, C), lambda i, k: (layer, i, 0))
    in_specs = [pl.BlockSpec((None, tr, C), lambda i, k: (k[0], i, 0))] + [rblk(j) for j in range(nr)] + [lay, lay, lay]
    ins = [part] + [recv] * nr + [w, m, v]
    aliases = {}
    if has_prev:
        in_specs += [pl.BlockSpec(memory_space=pl.ANY)] * 4
        ins += list(prev)
        aliases = {1 + len(ins) - 4 + t: t for t in range(4)}
    spec = pltpu.PrefetchScalarGridSpec(num_scalar_prefetch=1, grid=(R // tr,), in_specs=in_specs,
                                        out_specs=[lay, lay, lay, lay])
    return pl.pallas_call(body, name=name, grid_spec=spec, out_shape=[jax.ShapeDtypeStruct((L, R, C), F32)] * 4,
                          input_output_aliases=aliases, compiler_params=_cp(("parallel",)))(own_idx, *ins)


def _sum_devices(gathered):
    n, R, C = gathered.shape

    def body(g_ref, o_ref):
        acc = g_ref[0]
        for d in range(1, n):
            acc = acc + g_ref[d]
        o_ref[...] = acc

    return pl.pallas_call(body, name="sum_devices", out_shape=jax.ShapeDtypeStruct((R, C), F32))(gathered)


def _adam_small(w, g, m, v, name):
    shape = w.shape
    two = (-1, shape[-1])

    def body(w_ref, g_ref, m_ref, v_ref, d_ref, m2_ref, v2_ref):
        d, m2, v2 = _adam(w_ref[...], g_ref[...], m_ref[...], v_ref[...])
        d_ref[...] = d
        m2_ref[...] = m2
        v2_ref[...] = v2

    outs = pl.pallas_call(body, name=name, out_shape=[jax.ShapeDtypeStruct(w.reshape(two).shape, F32)] * 3)(
        w.reshape(two), g.reshape(two), m.reshape(two), v.reshape(two))
    return [o.reshape(shape) for o in outs]


_SHARDED = ("in", "out", "cq", "ckv", "co", "gu", "down")
_COLUMN_SHARDED = ("in", "ckv", "gu")


def kernel(x, mem, rel_bias, lb_logits, norm_gains, w_in, hg_norm, w_out, w_cq, w_ckv, w_co, w_gate_up, w_down, loss_target, m_rel_bias, m_lb_logits, m_norm_gains, m_w_in, m_hg_norm, m_w_out, m_w_cq, m_w_ckv, m_w_co, m_w_gate_up, m_w_down, v_rel_bias, v_lb_logits, v_norm_gains, v_w_in, v_hg_norm, v_w_out, v_w_cq, v_w_ckv, v_w_co, v_w_gate_up, v_w_down):
    L = w_in.shape[0]
    D = x.shape[-1]
    ws = dict(zip(_SHARDED, (w_in, w_out, w_cq, w_ckv, w_co, w_gate_up, w_down)))
    ms = dict(zip(_SHARDED, (m_w_in, m_w_out, m_w_cq, m_w_ckv, m_w_co, m_w_gate_up, m_w_down)))
    vs = dict(zip(_SHARDED, (v_w_in, v_w_out, v_w_cq, v_w_ckv, v_w_co, v_w_gate_up, v_w_down)))
    px, py, pc = _place()
    dev = 4 * px + 2 * py + pc
    c_idx = jnp.reshape(pc, (1,)).astype(jnp.int32)
    chip_idx = jnp.reshape(2 * px + py, (1,)).astype(jnp.int32)
    dev_idx = jnp.reshape(dev, (1,)).astype(jnp.int32)

    rest_keys = _SHARDED[1:]

    def shards_of(l, keys):
        return [ws[k][l].astype(BF16) for k in keys]

    def as_weights(keys, gathered):
        return {k: g if k in _COLUMN_SHARDED else g.reshape(1, N_DEV * g.shape[1], g.shape[2])
                for k, g in zip(keys, gathered)}

    def start_gather(shards, after, name):
        lands = [lax.empty((N_DEV,) + s.shape, s.dtype) for s in shards]
        return _copies_start(shards, lands, after, False, name)

    def wait_gather(pending, keys, after, name):
        send_sems, recv_sems, shards, lands = pending
        shards, lands = _copies_wait(send_sems, recv_sems, shards, lands, after, False, name)
        full = [lax.dynamic_update_slice_in_dim(t, s[None], dev, axis=0) for s, t in zip(shards, lands)]
        return as_weights(keys, full)

    def full_grads(gw, keys):
        return [gw[k].reshape((N_DEV,) + ws[k].shape[1:]) for k in keys]

    def start_scatter(full, after, name):
        lands = [lax.empty((N_DEV - 1,) + g.shape[1:], g.dtype) for g in full]
        return _copies_start(full, lands, after, True, name)

    class Comm:
        def __init__(self):
            first, gains = _all_gather(shards_of(0, _SHARDED[:1]) + [norm_gains], "gather_first")
            self.gains = jnp.transpose(gains, (1, 2, 0, 3)).reshape(L, 7, D)
            self.ready = {0: as_weights(_SHARDED[:1], [first])}
            *self.rest0, self.order = start_gather(shards_of(0, rest_keys), first, "gather_start_rest")
            self.gather = None
            self.scatter = None
            self.early = None
            self.results = {k: None for k in _SHARDED}

        def weights(self, l, x_in):
            del x_in
            token = None
            if l + 1 < L:
                after = self.order if l == 0 else self.ready[l]["in"]
                *self.gather, token = start_gather(shards_of(l + 1, _SHARDED), after, "gather_start")
            return self.ready[l]["in"], token

        def rest(self, l, after):
            if l == 0:
                self.ready[0].update(wait_gather(self.rest0, rest_keys, after, "gather_wait_rest"))
            mine = self.ready.pop(l)
            return {k: mine[k] for k in rest_keys}

        def layer_done(self, l, x_out):
            if l + 1 < L:
                self.ready[l + 1] = wait_gather(self.gather, _SHARDED, x_out, "gather_wait")

        def update(self, l, keys, parts, recvs, own_idx):
            for k, part, recv in zip(keys, parts, recvs):
                self.results[k] = _reduce_update(part, recv, own_idx, ws[k], ms[k], vs[k], l, self.results[k],
                                                 "reduce_update")

        def finish_scatter(self, after):
            lp, send_sems, recv_sems, srcs, lands = self.scatter
            srcs, lands = _copies_wait(send_sems, recv_sems, srcs, lands, after, True, "scatter_wait")
            self.update(lp, _SHARDED, srcs, lands, dev_idx)
            self.scatter = None
            return lands[0]

        def early_grads(self, l, gw):
            if l > 0 or L == 1:
                return None
            order = self.finish_scatter(gw["out"])
            *self.early, token = start_scatter(full_grads(gw, rest_keys), order, "scatter_start_rest")
            return token

        def grads(self, l, gw, after):
            if l > 0:
                order = self.finish_scatter(after) if self.scatter is not None else dev_idx
                send_sems, recv_sems, srcs, lands, token = start_scatter(full_grads(gw, _SHARDED), order,
                                                                         "scatter_start")
                self.scatter = (l, send_sems, recv_sems, srcs, lands)
                return token
            last_keys = _SHARDED
            if self.early is not None:
                send_sems, recv_sems, srcs, lands = self.early
                srcs, lands = _copies_wait(send_sems, recv_sems, srcs, lands, after, True, "scatter_wait_rest")
                self.update(0, rest_keys, srcs, lands, dev_idx)
                last_keys = _SHARDED[:1]
            full = full_grads(gw, last_keys)
            from_sibling = _exchange_sibling(full, "scatter_sibling")
            parts = [_pair_sum(g, r, c_idx, "pair_sum") for g, r in zip(full, from_sibling)]
            self.update(0, last_keys, parts, _exchange_chips(parts, "scatter_chips"), chip_idx)
            return None

    comm = Comm()
    loss, grad_x, d_rel_bias, d_lb, d_hg, d_gains = _local_step(
        x[0], mem[0], loss_target[0], rel_bias, lb_logits, comm.gains, hg_norm, comm)
    results = comm.results

    pieces = [jnp.reshape(loss, (1,)), d_rel_bias.reshape(-1), d_lb.reshape(-1), d_hg.reshape(-1), d_gains.reshape(-1)]
    sizes = [p.shape[0] for p in pieces]
    total = sum(sizes)
    rows = -(-total // 1024) * 8
    pack = jnp.pad(jnp.concatenate(pieces), (0, rows * 128 - total)).reshape(rows, 128)
    summed = _sum_devices(_all_gather([pack], "gather_small")[0]).reshape(-1)
    offs = [sum(sizes[:i]) for i in range(len(sizes))]
    loss = summed[0]
    g_rel_bias = summed[offs[1]:offs[1] + sizes[1]].reshape(rel_bias.shape)
    g_lb_all = summed[offs[2]:offs[2] + sizes[2]].reshape(lb_logits.shape)
    g_hg = summed[offs[3]:offs[3] + sizes[3]].reshape(hg_norm.shape)
    g_gains_full = summed[offs[4]:offs[4] + sizes[4]].reshape(L, 7, D)
    g_gains = lax.dynamic_slice_in_dim(g_gains_full, dev * (D // N_DEV), D // N_DEV, axis=2)
    g_lb = jax.vjp(_lb_all, lb_logits)[1](g_lb_all)[0]

    small = [(rel_bias, g_rel_bias, m_rel_bias, v_rel_bias), (lb_logits, g_lb, m_lb_logits, v_lb_logits),
             (norm_gains, g_gains, m_norm_gains, v_norm_gains), (hg_norm, g_hg, m_hg_norm, v_hg_norm)]
    upd = [_adam_small(w, g, m, v, "adam_small") for w, g, m, v in small]

    def ordered(small_vals, big_vals):
        return [small_vals[0], small_vals[1], small_vals[2], big_vals["in"], small_vals[3], big_vals["out"],
                big_vals["cq"], big_vals["ckv"], big_vals["co"], big_vals["gu"], big_vals["down"]]

    grads = ordered([s[1] for s in small], {k: results[k][0] for k in _SHARDED})
    deltas = ordered([u[0] for u in upd], {k: results[k][1] for k in _SHARDED})
    new_m = ordered([u[1] for u in upd], {k: results[k][2] for k in _SHARDED})
    new_v = ordered([u[2] for u in upd], {k: results[k][3] for k in _SHARDED})
    return (loss, grad_x[None], *grads, *deltas, *new_m, *new_v)
```

```python
import functools
import math

import jax
import jax.numpy as jnp
from jax import lax
from jax.experimental import pallas as pl
from jax.experimental.pallas import tpu as pltpu

F32 = jnp.float32
BF16 = jnp.bfloat16
MESH = pl.DeviceIdType.MESH

N_DEV = 8
HEAD = 128
CROSS_HEADS = 4
Q_BLOCK = 128
BRANCHES = ((128, 1), (512, 4), (2048, 16))
REL_BUCKETS = 32
REL_MAX_DIST = 2048
HG_CHUNK = 16
HG_HEADS_PER_STEP = 8
ATTN_PAIRS_PER_STAGE = 4
RMS_EPS = 1e-6
NEG_INF = -1e30
ADAM_LR, ADAM_B1, ADAM_B2, ADAM_EPS, ADAM_WD, ADAM_STEP = 0.001, 0.9, 0.999, 1e-08, 0.01, 10
VMEM_LIMIT = 48 * 1024 * 1024
NT_K_PER_STEP = 3584


def _tile(n, target, mult):
    best = None
    t = mult
    while t <= min(n, target):
        if n % t == 0:
            best = t
        t += mult
    return best if best is not None else n


def _cp(sem, vmem=VMEM_LIMIT):
    return pltpu.CompilerParams(dimension_semantics=sem, vmem_limit_bytes=vmem)


def _mm_nn(a, b3, out_dtype, name, after=None):
    M, K = a.shape
    J, K2, Nb = b3.shape
    assert K == K2
    tm, tn, tk = _tile(M, 1024, 8), _tile(Nb, 1408, 128), _tile(K, 2816, 128)
    nn, nk = Nb // tn, K // tk
    extra = [] if after is None else [after]

    def body(a_ref, b_ref, *rest):
        o_ref, acc = rest[len(extra)], rest[len(extra) + 1:]
        prod = jnp.dot(a_ref[...].astype(BF16), b_ref[...].astype(BF16), preferred_element_type=F32)
        if nk == 1:
            o_ref[...] = prod.astype(o_ref.dtype)
        else:
            k = pl.program_id(2)

            @pl.when(k == 0)
            def _():
                acc[0][...] = prod

            @pl.when(k > 0)
            def _():
                acc[0][...] += prod

            @pl.when(k == nk - 1)
            def _():
                o_ref[...] = acc[0][...].astype(o_ref.dtype)

    return pl.pallas_call(
        body, name=name, grid=(M // tm, J * nn, nk),
        in_specs=[pl.BlockSpec((tm, tk), lambda i, n, k: (i, k)),
                  pl.BlockSpec((None, tk, tn), lambda i, n, k: (n // nn, k, n % nn))]
                 + [pl.BlockSpec(memory_space=pl.ANY)] * len(extra),
        out_specs=pl.BlockSpec((tm, tn), lambda i, n, k: (i, n)),
        out_shape=jax.ShapeDtypeStruct((M, J * Nb), out_dtype),
        scratch_shapes=[] if nk == 1 else [pltpu.VMEM((tm, tn), F32)],
        compiler_params=_cp(("parallel", "parallel", "arbitrary")),
    )(a, b3, *extra)


def _mm_nt(a, b3, out_dtype, name, after=None):
    extra = [] if after is None else [after]
    halves = a.ndim == 3
    M, Kt = (a.shape[1], 2 * a.shape[2]) if halves else a.shape
    J, N, Kb = b3.shape
    assert Kt == J * Kb
    tm, tn = _tile(M, 1024, 8), _tile(N, 1408, 128)
    tk = _tile(math.gcd(Kb, Kt // 2) if halves else Kb, 2048, 128)
    nkb = Kb // tk
    groups = J // 2 if halves and J > 1 else J
    jb = max(j for j in range(1, groups + 1) if groups % j == 0 and j * Kb <= NT_K_PER_STEP) if nkb == 1 else 1
    nk = (J // jb) * nkb

    def body(a_ref, b_ref, *rest):
        o_ref, acc = rest[len(extra)], rest[len(extra) + 1:]
        prod = None
        for j in range(jb):
            part = lax.dot_general(a_ref[:, j * tk:(j + 1) * tk].astype(BF16), b_ref[j].astype(BF16),
                                   (((1,), (1,)), ((), ())), preferred_element_type=F32)
            prod = part if prod is None else prod + part
        if nk == 1:
            o_ref[...] = prod.astype(o_ref.dtype)
        else:
            k = pl.program_id(2)

            @pl.when(k == 0)
            def _():
                acc[0][...] = prod

            @pl.when(k > 0)
            def _():
                acc[0][...] += prod

            @pl.when(k == nk - 1)
            def _():
                o_ref[...] = acc[0][...].astype(o_ref.dtype)

    return pl.pallas_call(
        body, name=name, grid=(M // tm, N // tn, nk),
        in_specs=[pl.BlockSpec((None, tm, jb * tk), lambda i, n, k: (k // (nk // 2), i, k % (nk // 2))) if halves
                  else pl.BlockSpec((tm, jb * tk), lambda i, n, k: (i, k)),
                  pl.BlockSpec((jb, tn, tk), lambda i, n, k: (k // nkb, n, k % nkb))]
                 + [pl.BlockSpec(memory_space=pl.ANY)] * len(extra),
        out_specs=pl.BlockSpec((tm, tn), lambda i, n, k: (i, n)),
        out_shape=jax.ShapeDtypeStruct((M, N), out_dtype),
        scratch_shapes=[] if nk == 1 else [pltpu.VMEM((tm, tn), F32)],
        compiler_params=_cp(("parallel", "parallel", "arbitrary")),
    )(a, b3, *extra)


def _mm_tn(a, b, J, out_dtype, name):
    S, M = a.shape
    halves = b.ndim == 3
    S2, Nt = (b.shape[1], 2 * b.shape[2]) if halves else b.shape
    assert S == S2 and Nt % J == 0
    Nb = Nt // J
    tm, tk = _tile(M, 1408, 128), _tile(S, 2048, 8)
    tn = _tile(math.gcd(Nb, Nt // 2) if halves else Nb, 1408, 128)
    nn, nk = Nb // tn, S // tk
    per_half = J * nn // 2
    b_spec = (pl.BlockSpec((None, tk, tn), lambda i, n, k: (n // per_half, k, n % per_half)) if halves
              else pl.BlockSpec((tk, tn), lambda i, n, k: (k, n)))

    def body(a_ref, b_ref, o_ref, *acc):
        prod = lax.dot_general(a_ref[...].astype(BF16), b_ref[...].astype(BF16), (((0,), (0,)), ((), ())),
                               preferred_element_type=F32)
        if nk == 1:
            o_ref[...] = prod.astype(o_ref.dtype)
        else:
            k = pl.program_id(2)

            @pl.when(k == 0)
            def _():
                acc[0][...] = prod

            @pl.when(k > 0)
            def _():
                acc[0][...] += prod

            @pl.when(k == nk - 1)
            def _():
                o_ref[...] = acc[0][...].astype(o_ref.dtype)

    return pl.pallas_call(
        body, name=name, grid=(M // tm, J * nn, nk),
        in_specs=[pl.BlockSpec((tk, tm), lambda i, n, k: (k, i)), b_spec],
        out_specs=pl.BlockSpec((None, tm, tn), lambda i, n, k: (n // nn, i, n % nn)),
        out_shape=jax.ShapeDtypeStruct((J, M, Nb), out_dtype),
        scratch_shapes=[] if nk == 1 else [pltpu.VMEM((tm, tn), F32)],
        compiler_params=_cp(("parallel", "parallel", "arbitrary")),
    )(a, b)


def _rms_scale(v):
    return lax.rsqrt(jnp.mean(v * v, axis=-1, keepdims=True) + RMS_EPS)


def _rms_bwd(dy, v, g):
    r = _rms_scale(v)
    vh = v * r
    u = dy * g
    dv = r * (u - vh * jnp.mean(u * vh, axis=-1, keepdims=True))
    return dv, dy * vh


def _fold8(t):
    R, D = t.shape
    return jnp.sum(t.reshape(R // 8, 8, D), axis=0)


def _norm_fwd(x, z, g_post, g_pre, name):
    S, D = x.shape
    tr = _tile(S, 256, 8)
    has_z, has_pre = z is not None, g_pre is not None

    def body(*refs):
        it = iter(refs)
        x_ref = next(it)
        z_ref = next(it) if has_z else None
        gpost_ref = next(it) if has_z else None
        gpre_ref = next(it) if has_pre else None
        xn_ref = next(it) if has_z else None
        h_ref = next(it) if has_pre else None
        xn = x_ref[...]
        if has_z:
            zz = z_ref[...]
            xn = xn + zz * _rms_scale(zz) * gpost_ref[...]
            xn_ref[...] = xn
        if has_pre:
            h_ref[...] = (xn * _rms_scale(xn) * gpre_ref[...]).astype(BF16)

    row = pl.BlockSpec((tr, D), lambda i: (i, 0))
    gain = pl.BlockSpec((1, D), lambda i: (0, 0))
    ins, specs, outs, ospecs = [x], [row], [], []
    if has_z:
        ins += [z, g_post.reshape(1, D)]
        specs += [row, gain]
        outs.append(jax.ShapeDtypeStruct((S, D), F32))
        ospecs.append(row)
    if has_pre:
        ins.append(g_pre.reshape(1, D))
        specs.append(gain)
        outs.append(jax.ShapeDtypeStruct((S, D), BF16))
        ospecs.append(row)
    res = pl.pallas_call(body, name=name, grid=(S // tr,), in_specs=specs, out_specs=ospecs, out_shape=outs,
                         compiler_params=_cp(("parallel",)))(*ins)
    res = list(res)
    xn = res.pop(0) if has_z else x
    h = res.pop(0) if has_pre else None
    return xn, h


def _norm_bwd(dres, dh, xn, z, g_pre, g_post, name):
    S, D = xn.shape
    tr = _tile(S, 256, 8)
    has_res, has_pre, has_z = dres is not None, dh is not None, z is not None

    def body(*refs):
        it = iter(refs)
        dres_ref = next(it) if has_res else None
        dh_ref = next(it) if has_pre else None
        xn_ref = next(it) if has_pre else None
        gpre_ref = next(it) if has_pre else None
        z_ref = next(it) if has_z else None
        gpost_ref = next(it) if has_z else None
        t_ref = next(it)
        dz_ref = next(it) if has_z else None
        dgpre_ref = next(it) if has_pre else None
        dgpost_ref = next(it) if has_z else None
        first = pl.program_id(0) == 0
        t = dres_ref[...] if has_res else None
        if has_pre:
            dv, dg = _rms_bwd(dh_ref[...].astype(F32), xn_ref[...], gpre_ref[...])
            t = dv if t is None else t + dv
            part = _fold8(dg)

            @pl.when(first)
            def _():
                dgpre_ref[...] = part

            @pl.when(jnp.logical_not(first))
            def _():
                dgpre_ref[...] += part
        t_ref[...] = t
        if has_z:
            dv, dg = _rms_bwd(t, z_ref[...], gpost_ref[...])
            dz_ref[...] = dv.astype(BF16)
            part2 = _fold8(dg)

            @pl.when(first)
            def _():
                dgpost_ref[...] = part2

            @pl.when(jnp.logical_not(first))
            def _():
                dgpost_ref[...] += part2

    row = pl.BlockSpec((tr, D), lambda i: (i, 0))
    gain = pl.BlockSpec((1, D), lambda i: (0, 0))
    accs = pl.BlockSpec((8, D), lambda i: (0, 0))
    ins, specs = [], []
    if has_res:
        ins.append(dres)
        specs.append(row)
    if has_pre:
        ins += [dh, xn, g_pre.reshape(1, D)]
        specs += [row, row, gain]
    if has_z:
        ins += [z, g_post.reshape(1, D)]
        specs += [row, gain]
    outs, ospecs = [jax.ShapeDtypeStruct((S, D), F32)], [row]
    if has_z:
        outs.append(jax.ShapeDtypeStruct((S, D), BF16))
        ospecs.append(row)
    if has_pre:
        outs.append(jax.ShapeDtypeStruct((8, D), F32))
        ospecs.append(accs)
    if has_z:
        outs.append(jax.ShapeDtypeStruct((8, D), F32))
        ospecs.append(accs)
    res = list(pl.pallas_call(body, name=name, grid=(S // tr,), in_specs=specs, out_specs=ospecs, out_shape=outs,
                              compiler_params=_cp(("arbitrary",)))(*ins))
    t = res.pop(0)
    dz = res.pop(0) if has_z else None
    dgpre = jnp.sum(res.pop(0), axis=0) if has_pre else None
    dgpost = jnp.sum(res.pop(0), axis=0) if has_z else None
    return t, dz, dgpre, dgpost


def _loss_and_grad(y, target):
    S, D = y.shape
    tr = _tile(S, 256, 8)

    def body(y_ref, t_ref, dy_ref, l_ref):
        err = y_ref[...] - t_ref[...]
        dy_ref[...] = err * (1.0 / D)
        part = 0.5 * jnp.sum(jnp.mean(err * err, axis=-1, keepdims=True), axis=0, keepdims=True)
        first = pl.program_id(0) == 0

        @pl.when(first)
        def _():
            l_ref[...] = jnp.broadcast_to(part, l_ref.shape)

        @pl.when(jnp.logical_not(first))
        def _():
            l_ref[...] += jnp.broadcast_to(part, l_ref.shape)

    row = pl.BlockSpec((tr, D), lambda i: (i, 0))
    dy, l = pl.pallas_call(
        body, name="loss_grad", grid=(S // tr,), in_specs=[row, row],
        out_specs=[row, pl.BlockSpec((8, 128), lambda i: (0, 0))],
        out_shape=[jax.ShapeDtypeStruct((S, D), F32), jax.ShapeDtypeStruct((8, 128), F32)],
        compiler_params=_cp(("arbitrary",)))(y, target)
    return l[0, 0], dy


def _mm_gate_up(a, w3, name):
    S, D = a.shape
    J, D2, Nb = w3.shape
    F = J * Nb // 2
    assert D == D2
    tm, tn = _tile(S, 512, 8), _tile(math.gcd(Nb, F), 1408, 128)
    nn, nf = Nb // tn, F // tn

    def body(a_ref, wg_ref, wu_ref, gu_ref, act_ref):
        av = a_ref[...]
        g = jnp.dot(av, wg_ref[...], preferred_element_type=F32)
        u = jnp.dot(av, wu_ref[...], preferred_element_type=F32)
        gu_ref[0] = g
        gu_ref[1] = u
        act_ref[...] = (g * jax.nn.sigmoid(g) * u).astype(BF16)

    return pl.pallas_call(
        body, name=name, grid=(nf, S // tm),
        in_specs=[pl.BlockSpec((tm, D), lambda n, i: (i, 0)),
                  pl.BlockSpec((None, D, tn), lambda n, i: (n // nn, 0, n % nn)),
                  pl.BlockSpec((None, D, tn), lambda n, i: ((n + nf) // nn, 0, (n + nf) % nn))],
        out_specs=[pl.BlockSpec((2, tm, tn), lambda n, i: (0, i, n)), pl.BlockSpec((tm, tn), lambda n, i: (i, n))],
        out_shape=[jax.ShapeDtypeStruct((2, S, F), F32), jax.ShapeDtypeStruct((S, F), BF16)],
        compiler_params=_cp(("parallel", "parallel")))(a, w3, w3)


def _mm_dx_down_swiglu(dy, wdown3, gu, name, after=None):
    S, D = dy.shape
    _, F, D2 = wdown3.shape
    assert D == D2
    tm, tn = _tile(S, 1024, 8), _tile(F, 512, 128)
    extra = [] if after is None else [after]

    def body(a_ref, b_ref, gu_ref, *rest):
        o_ref = rest[len(extra)]
        d = lax.dot_general(a_ref[...], b_ref[...], (((1,), (1,)), ((), ())), preferred_element_type=F32)
        g, u = gu_ref[0], gu_ref[1]
        sg = jax.nn.sigmoid(g)
        o_ref[0] = (d * u * (sg * (1.0 + g * (1.0 - sg)))).astype(BF16)
        o_ref[1] = (d * g * sg).astype(BF16)

    return pl.pallas_call(
        body, name=name, grid=(S // tm, F // tn),
        in_specs=[pl.BlockSpec((tm, D), lambda i, n: (i, 0)), pl.BlockSpec((None, tn, D), lambda i, n: (0, n, 0)),
                  pl.BlockSpec((2, tm, tn), lambda i, n: (0, i, n))] + [pl.BlockSpec(memory_space=pl.ANY)] * len(extra),
        out_specs=pl.BlockSpec((2, tm, tn), lambda i, n: (0, i, n)),
        out_shape=jax.ShapeDtypeStruct((2, S, F), BF16),
        compiler_params=_cp(("parallel", "parallel")))(dy, wdown3, gu, *extra)


def _cross_scores(q, k, scale):
    s = lax.dot_general(q, k, (((1,), (1,)), ((), ())), preferred_element_type=F32) * scale
    m = jnp.max(s, axis=-1, keepdims=True)
    e = jnp.exp(s - m)
    return e / jnp.sum(e, axis=-1, keepdims=True)


def _cross_fwd(cq, ckv, name):
    S, D = cq.shape
    Nm = ckv.shape[0]
    dh = D // CROSS_HEADS
    tq = _tile(S, 512, 8)
    scale = dh ** -0.5

    def body(q_ref, kv_ref, o_ref):
        for h in range(CROSS_HEADS):
            cols = slice(h * dh, (h + 1) * dh)
            p = _cross_scores(q_ref[:, cols], kv_ref[:, cols], scale)
            o_ref[:, cols] = jnp.dot(p.astype(BF16), kv_ref[:, D + h * dh:D + (h + 1) * dh],
                                     preferred_element_type=F32).astype(BF16)

    return pl.pallas_call(
        body, name=name, grid=(S // tq,),
        in_specs=[pl.BlockSpec((tq, D), lambda i: (i, 0)), pl.BlockSpec((Nm, 2 * D), lambda i: (0, 0))],
        out_specs=pl.BlockSpec((tq, D), lambda i: (i, 0)),
        out_shape=jax.ShapeDtypeStruct((S, D), BF16), compiler_params=_cp(("parallel",)))(cq, ckv)


def _cross_bwd(cq, ckv, do, name):
    S, D = cq.shape
    Nm = ckv.shape[0]
    dh = D // CROSS_HEADS
    tq = _tile(S, 512, 8)
    scale = dh ** -0.5

    def body(q_ref, kv_ref, do_ref, dq_ref, dkv_ref):
        first = pl.program_id(0) == 0

        @pl.when(first)
        def _():
            dkv_ref[...] = jnp.zeros_like(dkv_ref)

        for h in range(CROSS_HEADS):
            cols = slice(h * dh, (h + 1) * dh)
            vcols = slice(D + h * dh, D + (h + 1) * dh)
            q, k, v = q_ref[:, cols], kv_ref[:, cols], kv_ref[:, vcols]
            doh = do_ref[:, cols].astype(BF16)
            p = _cross_scores(q, k, scale)
            dp = lax.dot_general(doh, v, (((1,), (1,)), ((), ())), preferred_element_type=F32)
            ds = (p * (dp - jnp.sum(p * dp, axis=-1, keepdims=True)) * scale).astype(BF16)
            dq_ref[:, cols] = jnp.dot(ds, k, preferred_element_type=F32).astype(BF16)
            dkv_ref[:, cols] += lax.dot_general(ds, q, (((0,), (0,)), ((), ())), preferred_element_type=F32)
            dkv_ref[:, vcols] += lax.dot_general(p.astype(BF16), doh, (((0,), (0,)), ((), ())),
                                                 preferred_element_type=F32)

    return pl.pallas_call(
        body, name=name, grid=(S // tq,),
        in_specs=[pl.BlockSpec((tq, D), lambda i: (i, 0)), pl.BlockSpec((Nm, 2 * D), lambda i: (0, 0)),
                  pl.BlockSpec((tq, D), lambda i: (i, 0))],
        out_specs=[pl.BlockSpec((tq, D), lambda i: (i, 0)), pl.BlockSpec((Nm, 2 * D), lambda i: (0, 0))],
        out_shape=[jax.ShapeDtypeStruct((S, D), BF16), jax.ShapeDtypeStruct((Nm, 2 * D), F32)],
        compiler_params=_cp(("arbitrary",)))(cq, ckv, do)


def _rel_bucket(dist):
    max_exact = REL_BUCKETS // 2
    d_f = jnp.maximum(dist, 1).astype(F32)
    large = max_exact + (jnp.log(d_f / max_exact) / math.log(REL_MAX_DIST / max_exact)
                         * (REL_BUCKETS - max_exact)).astype(jnp.int32)
    large = jnp.minimum(large, REL_BUCKETS - 1)
    return jnp.where(dist < max_exact, dist, large)


def _branch_tables(dilation):
    qi = jnp.arange(Q_BLOCK)[:, None]
    kj = jnp.arange(2 * Q_BLOCK)[None, :]
    m = qi - kj + Q_BLOCK
    return _rel_bucket(jnp.maximum(m, 0) * dilation), (m >= 0) & (m <= Q_BLOCK)


def _attn_logits(q, kcat, bias, first_block, scale):
    s = lax.dot_general(q, kcat, (((1,), (1,)), ((), ())), preferred_element_type=F32) * scale + bias
    col = lax.broadcasted_iota(jnp.int32, s.shape, 1)
    return jnp.where(jnp.logical_and(first_block, col < Q_BLOCK), NEG_INF, s)


def _attn_branch_fwd(proj, bias, dilation, H, name):
    S, NC = proj.shape
    AW = H * HEAD
    r = dilation
    hb = H if r == 1 else 1
    G = Q_BLOCK * r
    ng, nh = S // G, H // hb
    scale = HEAD ** -0.5

    def body(q_ref, kp_ref, kc_ref, vp_ref, vc_ref, b_ref, o_ref, l_ref):
        first_block = pl.program_id(0) == 0
        h0 = pl.program_id(1) * hb
        pairs = [(pl.ds(rho, Q_BLOCK, stride=r) if r > 1 else slice(None), hh) for rho in range(r) for hh in range(hb)]
        for g0 in range(0, len(pairs), ATTN_PAIRS_PER_STAGE):
            group = pairs[g0:g0 + ATTN_PAIRS_PER_STAGE]
            sl = [(rows, slice(hh * HEAD, (hh + 1) * HEAD)) for rows, hh in group]
            q = [q_ref[rw, cl].astype(BF16) for rw, cl in sl]
            kcat = [jnp.concatenate([kp_ref[rw, cl], kc_ref[rw, cl]], axis=0).astype(BF16) for rw, cl in sl]
            vcat = [jnp.concatenate([vp_ref[rw, cl], vc_ref[rw, cl]], axis=0).astype(BF16) for rw, cl in sl]
            s = [_attn_logits(q[i], kcat[i], b_ref[h0 + hh], first_block, scale) for i, (_, hh) in enumerate(group)]
            m = [jnp.max(si, axis=-1, keepdims=True) for si in s]
            e = [jnp.exp(si - mi) for si, mi in zip(s, m)]
            den = [jnp.sum(ei, axis=-1, keepdims=True) for ei in e]
            o = [jnp.dot(ei.astype(BF16), vi, preferred_element_type=F32) for ei, vi in zip(e, vcat)]
            for i, (rw, cl) in enumerate(sl):
                o_ref[rw, cl] = o[i] / den[i]
                l_ref[rw, cl] = jnp.broadcast_to(m[i] + jnp.log(den[i]), (Q_BLOCK, HEAD))

    def blk(part, prev):
        if prev:
            return pl.BlockSpec((G, hb * HEAD), lambda n, h: (jnp.maximum(n - 1, 0), part * nh + h))
        return pl.BlockSpec((G, hb * HEAD), lambda n, h: (n, part * nh + h))

    out = pl.BlockSpec((G, hb * HEAD), lambda n, h: (n, h))
    return pl.pallas_call(
        body, name=name, grid=(ng, nh),
        in_specs=[blk(0, False), blk(1, True), blk(1, False), blk(2, True), blk(2, False),
                  pl.BlockSpec((H, Q_BLOCK, 2 * Q_BLOCK), lambda n, h: (0, 0, 0))],
        out_specs=[out, out],
        out_shape=[jax.ShapeDtypeStruct((S, AW), F32)] * 2,
        compiler_params=_cp(("parallel", "parallel")))(proj, proj, proj, proj, proj, bias)


def _attn_merge(outs, lses, width_out, name):
    S, AW = outs[0].shape
    tr = _tile(S, 256, 8)

    def body(o0, o1, o2, l0, l1, l2, cat_ref, om_ref, lt_ref):
        a, b, c = l0[...], l1[...], l2[...]
        m = jnp.maximum(jnp.maximum(a, b), c)
        ea, eb, ec = jnp.exp(a - m), jnp.exp(b - m), jnp.exp(c - m)
        tot = ea + eb + ec
        o = (ea * o0[...] + eb * o1[...] + ec * o2[...]) / tot
        om_ref[...] = o
        cat_ref[...] = o.astype(BF16)
        lt_ref[...] = m + jnp.log(tot)

    row = pl.BlockSpec((tr, AW), lambda i: (i, 0))
    return pl.pallas_call(
        body, name=name, grid=(S // tr,), in_specs=[row] * 6, out_specs=[row, row, row],
        out_shape=[jax.ShapeDtypeStruct((S, width_out), BF16), jax.ShapeDtypeStruct((S, AW), F32),
                   jax.ShapeDtypeStruct((S, AW), F32)],
        compiler_params=_cp(("parallel",)))(*outs, *lses)


def _attn_delta(dcat, o_attn, H, name, after=None):
    S, AW = o_attn.shape
    tr = _tile(S, 256, 8)
    extra = [] if after is None else [after]

    def body(d_ref, o_ref, *rest):
        out_ref = rest[len(extra)]
        for h in range(H):
            cols = slice(h * HEAD, (h + 1) * HEAD)
            out_ref[:, cols] = jnp.broadcast_to(
                jnp.sum(d_ref[:, cols] * o_ref[:, cols], axis=-1, keepdims=True), (tr, HEAD))

    row = pl.BlockSpec((tr, AW), lambda i: (i, 0))
    return pl.pallas_call(body, name=name, grid=(S // tr,),
                          in_specs=[row, row] + [pl.BlockSpec(memory_space=pl.ANY)] * len(extra), out_specs=row,
                          out_shape=jax.ShapeDtypeStruct((S, AW), F32),
                          compiler_params=_cp(("parallel",)))(dcat, o_attn, *extra)


def _attn_branch_bwd(proj, dcat, lse_tot, delta, bias, dilation, H, name):
    S, NC = proj.shape
    AW = H * HEAD
    r = dilation
    hb = H if r == 1 else 1
    G = Q_BLOCK * r
    ng, nh = S // G, H // hb
    scale = HEAD ** -0.5

    def body(q_ref, kp_ref, kc_ref, vp_ref, vc_ref, do_ref, l_ref, t_ref, b_ref,
             dq_ref, dk_ref, dv_ref, db_ref, ck_ref, cv_ref):
        hblk, n = pl.program_id(0), pl.program_id(1)
        h0 = hblk * hb

        @pl.when(jnp.logical_and(hblk == 0, n == 0))
        def _():
            db_ref[...] = jnp.zeros_like(db_ref)

        @pl.when(n == 0)
        def _():
            ck_ref[...] = jnp.zeros_like(ck_ref)
            cv_ref[...] = jnp.zeros_like(cv_ref)

        @pl.when(n < ng)
        def _():
            pairs = [(pl.ds(rho, Q_BLOCK, stride=r) if r > 1 else slice(None), hh)
                     for rho in range(r) for hh in range(hb)]
            for g0 in range(0, len(pairs), ATTN_PAIRS_PER_STAGE):
                group = pairs[g0:g0 + ATTN_PAIRS_PER_STAGE]
                idx = range(len(group))
                sl = [(rows, slice(hh * HEAD, (hh + 1) * HEAD)) for rows, hh in group]
                q = [q_ref[rw, cl].astype(BF16) for rw, cl in sl]
                kcat = [jnp.concatenate([kp_ref[rw, cl], kc_ref[rw, cl]], axis=0).astype(BF16) for rw, cl in sl]
                vcat = [jnp.concatenate([vp_ref[rw, cl], vc_ref[rw, cl]], axis=0).astype(BF16) for rw, cl in sl]
                doh = [do_ref[rw, cl].astype(BF16) for rw, cl in sl]
                s = [_attn_logits(q[i], kcat[i], b_ref[h0 + group[i][1]], n == 0, scale) for i in idx]
                dp = [lax.dot_general(doh[i], vcat[i], (((1,), (1,)), ((), ())), preferred_element_type=F32) for i in idx]
                p, ds = [], []
                for i, (rw, cl) in enumerate(sl):
                    lt, dl = l_ref[rw, cl], t_ref[rw, cl]
                    p.append(jnp.exp(s[i] - jnp.concatenate([lt, lt], axis=1)))
                    ds.append(p[i] * (dp[i] - jnp.concatenate([dl, dl], axis=1)))
                dsb = [(d * scale).astype(BF16) for d in ds]
                dq = [jnp.dot(dsb[i], kcat[i], preferred_element_type=F32) for i in idx]
                dkc = [lax.dot_general(dsb[i], q[i], (((0,), (0,)), ((), ())), preferred_element_type=F32) for i in idx]
                dvc = [lax.dot_general(p[i].astype(BF16), doh[i], (((0,), (0,)), ((), ())), preferred_element_type=F32)
                       for i in idx]
                for i, (rw, cl) in enumerate(sl):
                    dq_ref[rw, cl] = dq[i]
                    dk_ref[rw, cl] = ck_ref[rw, cl] + dkc[i][:Q_BLOCK]
                    dv_ref[rw, cl] = cv_ref[rw, cl] + dvc[i][:Q_BLOCK]
                    ck_ref[rw, cl] = dkc[i][Q_BLOCK:]
                    cv_ref[rw, cl] = dvc[i][Q_BLOCK:]
                for i in idx:
                    db_ref[h0 + group[i][1]] += ds[i]

        @pl.when(n == ng)
        def _():
            dk_ref[...] = ck_ref[...]
            dv_ref[...] = cv_ref[...]

    last = ng - 1

    def cur(part):
        return pl.BlockSpec((G, hb * HEAD), lambda h, n: (jnp.minimum(n, last), part * nh + h))

    def prev(part):
        return pl.BlockSpec((G, hb * HEAD), lambda h, n: (jnp.clip(n - 1, 0, last), part * nh + h))

    table = pl.BlockSpec((H, Q_BLOCK, 2 * Q_BLOCK), lambda h, n: (0, 0, 0))
    return pl.pallas_call(
        body, name=name, grid=(nh, ng + 1),
        in_specs=[cur(0), prev(1), cur(1), prev(2), cur(2), cur(0), cur(0), cur(0), table],
        out_specs=[cur(0), prev(0), prev(0), table],
        out_shape=[jax.ShapeDtypeStruct((S, AW), F32)] * 3 + [jax.ShapeDtypeStruct((H, Q_BLOCK, 2 * Q_BLOCK), F32)],
        scratch_shapes=[pltpu.VMEM((G, hb * HEAD), F32), pltpu.VMEM((G, hb * HEAD), F32)],
        compiler_params=_cp(("arbitrary", "arbitrary")))(proj, proj, proj, proj, proj, dcat, lse_tot, delta, bias)


def _bias_grad(dbs, onehots, H):
    def body(d0, d1, d2, e0, e1, e2, o_ref):
        acc = jnp.zeros((H, REL_BUCKETS), F32)
        for d, e in ((d0, e0), (d1, e1), (d2, e2)):
            acc = acc + lax.dot_general(d[...], e[...], (((1,), (1,)), ((), ())), preferred_element_type=F32,
                                        precision=lax.Precision.HIGHEST)
        o_ref[...] = acc

    flat = [d.reshape(H, 2 * Q_BLOCK * Q_BLOCK) for d in dbs]
    return pl.pallas_call(body, name="bias_grad", out_shape=jax.ShapeDtypeStruct((H, REL_BUCKETS), F32),
                          compiler_params=pltpu.CompilerParams(vmem_limit_bytes=VMEM_LIMIT))(*flat, *onehots)


def _assemble_dproj(dqkv, dhg, name):
    S, AW = dqkv[0][0].shape
    tr = _tile(S, 256, 8)
    ins = [dqkv[g][c] for g in range(3) for c in range(3)] + list(dhg)

    def body(*refs):
        o_ref = refs[-1]
        j = pl.program_id(1)
        for c in range(3):
            @pl.when(j == c)
            def _(c=c):
                o_ref[...] = (refs[c][...] + refs[3 + c][...] + refs[6 + c][...]).astype(BF16)

        for k in range(4):
            @pl.when(j == 3 + k)
            def _(k=k):
                o_ref[...] = refs[9 + k][...]

    row = pl.BlockSpec((tr, AW), lambda i, j: (i, 0))
    return pl.pallas_call(
        body, name=name, grid=(S // tr, 7), in_specs=[row] * 13,
        out_specs=pl.BlockSpec((tr, AW), lambda i, j: (i, j)),
        out_shape=jax.ShapeDtypeStruct((S, 7 * AW), BF16),
        compiler_params=_cp(("parallel", "arbitrary")))(*ins)


def _tri(n, upper):
    r = lax.broadcasted_iota(jnp.int32, (n, n), 0)
    c = lax.broadcasted_iota(jnp.int32, (n, n), 1)
    return jnp.where(c >= r if upper else c <= r, 1.0, 0.0).astype(F32)


def _hg_gates(fz, qz, lb):
    sig = jax.nn.sigmoid(fz)
    f = lb + (1.0 - lb) * sig
    sq = jax.nn.sigmoid(qz)
    return sig, f, jnp.log(f), 1.0 - f, qz * sq, sq


def _hg_fwd(proj, cat, lb, gain, HH, name):
    S, NC = proj.shape
    W = HH * HEAD
    C = HG_CHUNK
    hb = min(HG_HEADS_PER_STEP, HH)
    nh = HH // hb
    tb = _tile(S, 2048 // hb, C)
    ncb = tb // C
    base = 3 * nh
    cat_cols = cat.shape[1] // (hb * HEAD)

    def body(fz_ref, iv_ref, qz_ref, gz_ref, lb_ref, g_ref, cat_in, cat_ref, o_ref, st_ref, state):
        del cat_in

        @pl.when(pl.program_id(1) == 0)
        def _():
            state[...] = jnp.zeros_like(state)

        tri = _tri(C, False)
        row8 = lax.broadcasted_iota(jnp.int32, (8, 1), 0)

        def chunk(c, carry):
            rows = pl.ds(pl.multiple_of(c * C, C), C)
            heads = range(hb)
            cols = [slice(hh * HEAD, (hh + 1) * HEAD) for hh in heads]
            vv = [iv_ref[rows, cl] for cl in cols]
            gates = [_hg_gates(fz_ref[rows, cl], qz_ref[rows, cl], lb_ref[:, cl]) for cl in cols]
            kk, qq = [gt[3] for gt in gates], [gt[4] for gt in gates]
            b = [jnp.dot(tri, gt[2], preferred_element_type=F32, precision=lax.Precision.HIGHEST) for gt in gates]
            bl = [bh[C - 1:C, :] for bh in b]
            st = [state[hh] for hh in heads]
            stb = [s_.astype(BF16) for s_ in st]
            for hh in heads:
                st_ref[hh, c] = stb[hh]
            o = [lax.dot_general((qq[hh] * jnp.exp(b[hh])).astype(BF16), stb[hh], (((1,), (1,)), ((), ())),
                                 preferred_element_type=F32) for hh in heads]
            upd = [lax.dot_general(vv[hh].astype(BF16), (kk[hh] * jnp.exp(bl[hh] - b[hh])).astype(BF16),
                                   (((0,), (0,)), ((), ())), preferred_element_type=F32) for hh in heads]
            for hh in heads:
                state[hh] = st[hh] * jnp.exp(bl[hh]) + upd[hh]
            wts = {}
            for t0 in range(0, C, 8):
                for s in range(min(C, t0 + 8)):
                    for hh in heads:
                        diff = b[hh][t0:t0 + 8, :] - b[hh][s:s + 1, :]
                        if s >= t0:
                            diff = jnp.minimum(diff, 0.0)
                        a = jnp.sum(qq[hh][t0:t0 + 8, :] * jnp.exp(diff) * kk[hh][s:s + 1, :], axis=-1, keepdims=True)
                        wts[t0, s, hh] = jnp.where(row8 >= s - t0, a, 0.0) if s >= t0 else a
            for hh in heads:
                tiles = []
                for t0 in range(0, C, 8):
                    acc = o[hh][t0:t0 + 8, :]
                    for s in range(min(C, t0 + 8)):
                        acc = acc + wts[t0, s, hh] * vv[hh][s:s + 1, :]
                    tiles.append(acc)
                oh = jnp.concatenate(tiles, axis=0)
                o_ref[rows, cols[hh]] = oh
                n = oh * lax.rsqrt(jnp.mean(oh * oh, axis=-1, keepdims=True) + RMS_EPS)
                gz = gz_ref[rows, cols[hh]]
                cat_ref[rows, cols[hh]] = (n * g_ref[:, cols[hh]] * (gz * jax.nn.sigmoid(gz))).astype(BF16)
            return carry

        lax.fori_loop(0, ncb, chunk, 0)

    def col(k):
        return pl.BlockSpec((tb, hb * HEAD), lambda h, i: (i, base + k * nh + h))

    vec = pl.BlockSpec((1, hb * HEAD), lambda h, i: (0, h))
    cat_out, o_raw, states = pl.pallas_call(
        body, name=name, grid=(nh, S // tb),
        in_specs=[col(0), col(1), col(2), col(3), vec, vec, pl.BlockSpec(memory_space=pl.ANY)],
        out_specs=[pl.BlockSpec((tb, hb * HEAD), lambda h, i: (i, cat_cols - nh + h)),
                   pl.BlockSpec((tb, hb * HEAD), lambda h, i: (i, h)),
                   pl.BlockSpec((hb, ncb, HEAD, HEAD), lambda h, i: (h, i, 0, 0))],
        out_shape=[jax.ShapeDtypeStruct(cat.shape, BF16), jax.ShapeDtypeStruct((S, W), F32),
                   jax.ShapeDtypeStruct((HH, S // C, HEAD, HEAD), BF16)],
        scratch_shapes=[pltpu.VMEM((hb, HEAD, HEAD), F32)],
        input_output_aliases={6: 0},
        compiler_params=_cp(("parallel", "arbitrary")))(proj, proj, proj, proj, lb.reshape(1, W), gain.reshape(1, W), cat)
    return cat_out, o_raw, states


def _hg_bwd(proj, dcat, o_raw, states, lb, gain, after, HH, name):
    S, NC = proj.shape
    W = HH * HEAD
    C = HG_CHUNK
    hb = min(HG_HEADS_PER_STEP, HH)
    nh = HH // hb
    tb = _tile(S, 2048 // hb, C)
    ncb = tb // C
    nt = S // tb
    base = 3 * nh
    dcat_cols = dcat.shape[1] // (hb * HEAD)

    def body(fz_ref, iv_ref, qz_ref, gz_ref, dc_ref, o_ref, st_ref, lb_ref, g_ref, after_ref,
             dfz_ref, div_ref, dqz_ref, dgz_ref, dlb_ref, dg_ref, dstate):
        del after_ref

        @pl.when(pl.program_id(1) == 0)
        def _():
            dstate[...] = jnp.zeros_like(dstate)
            dlb_ref[...] = jnp.zeros_like(dlb_ref)
            dg_ref[...] = jnp.zeros_like(dg_ref)

        tri_lo, tri_up = _tri(C, False), _tri(C, True)
        row = lax.broadcasted_iota(jnp.int32, (C, 1), 0)
        row8 = lax.broadcasted_iota(jnp.int32, (8, 1), 0)

        def chunk(cc, carry):
            c = ncb - 1 - cc
            rows = pl.ds(pl.multiple_of(c * C, C), C)
            heads = range(hb)
            starts = list(range(0, C, 8))
            hd = []
            for hh in heads:
                cols = slice(hh * HEAD, (hh + 1) * HEAD)
                lbv, gv = lb_ref[:, cols], g_ref[:, cols]
                fz, vv, qz, gz = fz_ref[rows, cols], iv_ref[rows, cols], qz_ref[rows, cols], gz_ref[rows, cols]
                sig, f, lf, kk, qq, sq = _hg_gates(fz, qz, lbv)
                b = jnp.dot(tri_lo, lf, preferred_element_type=F32, precision=lax.Precision.HIGHEST)
                bl = b[C - 1:C, :]
                eb, ebl = jnp.exp(b), jnp.exp(bl)
                ekb = jnp.exp(bl - b)
                qh, kh = qq * eb, kk * ekb
                o = o_ref[rows, cols]
                dhg = dc_ref[rows, cols]
                sgz = jax.nn.sigmoid(gz)
                rr = lax.rsqrt(jnp.mean(o * o, axis=-1, keepdims=True) + RMS_EPS)
                nrm = o * rr
                dpre = dhg * (gz * sgz)
                dgz_ref[rows, cols] = (dhg * nrm * gv * (sgz * (1.0 + gz * (1.0 - sgz)))).astype(BF16)
                dg_ref[:, cols] += jnp.sum(dpre * nrm, axis=0, keepdims=True)
                dn = dpre * gv
                do = rr * (dn - nrm * jnp.mean(dn * nrm, axis=-1, keepdims=True))
                dob = do.astype(BF16)
                st0 = st_ref[hh, c]
                dst1 = dstate[hh]
                dst1b = dst1.astype(BF16)
                dqh = jnp.dot(dob, st0, preferred_element_type=F32)
                dv = lax.dot_general(kh.astype(BF16), dst1b, (((1,), (1,)), ((), ())), preferred_element_type=F32)
                dkh = jnp.dot(vv.astype(BF16), dst1b, preferred_element_type=F32)
                dstate[hh] = dst1 * ebl + lax.dot_general(dob, qh.astype(BF16), (((0,), (0,)), ((), ())),
                                                          preferred_element_type=F32)
                extra = (jnp.sum(dkh * kh, axis=0, keepdims=True)
                         + ebl * jnp.sum(st0.astype(F32) * dst1, axis=0, keepdims=True))
                hd.append(dict(cols=cols, lbv=lbv, sig=sig, f=f, kk=kk, qq=qq, sq=sq, qz=qz, vv=vv, b=b, do=do, dv=dv,
                               extra=extra, dq=dqh * eb, dk=dkh * ekb))
            red = {}
            for s in range(C):
                for ti, t0 in enumerate(starts):
                    if t0 + 8 <= s:
                        continue
                    for hh in heads:
                        v = hd[hh]
                        diff = v["b"][t0:t0 + 8, :] - v["b"][s:s + 1, :]
                        if s >= t0:
                            diff = jnp.minimum(diff, 0.0)
                        dec = jnp.exp(diff)
                        da = jnp.sum(v["do"][t0:t0 + 8, :] * v["vv"][s:s + 1, :], axis=-1, keepdims=True)
                        a = jnp.sum(v["qq"][t0:t0 + 8, :] * dec * v["kk"][s:s + 1, :], axis=-1, keepdims=True)
                        if s >= t0:
                            keep = row8 >= s - t0
                            da, a = jnp.where(keep, da, 0.0), jnp.where(keep, a, 0.0)
                        red[s, ti, hh] = (da * dec, a)
            for hh in heads:
                v = hd[hh]
                cols, kk, qq, do = v["cols"], v["kk"], v["qq"], v["do"]
                dq_t = [v["dq"][t0:t0 + 8, :] for t0 in starts]
                dk_rows, dv_rows = [], []
                for s in range(C):
                    dk_s = dv_s = None
                    for ti, t0 in enumerate(starts):
                        if t0 + 8 <= s:
                            continue
                        gd, a = red[s, ti, hh]
                        dq_t[ti] = dq_t[ti] + gd * kk[s:s + 1, :]
                        pk = jnp.sum(gd * qq[t0:t0 + 8, :], axis=0, keepdims=True)
                        pv = jnp.sum(a * do[t0:t0 + 8, :], axis=0, keepdims=True)
                        dk_s = pk if dk_s is None else dk_s + pk
                        dv_s = pv if dv_s is None else dv_s + pv
                    dk_rows.append(dk_s)
                    dv_rows.append(dv_s)
                dq = jnp.concatenate(dq_t, axis=0)
                dk = v["dk"] + jnp.concatenate(dk_rows, axis=0)
                dv = v["dv"] + jnp.concatenate(dv_rows, axis=0)
                db = qq * dq - kk * dk + jnp.where(row == C - 1, v["extra"], 0.0)
                dlf = jnp.dot(tri_up, db, preferred_element_type=F32, precision=lax.Precision.HIGHEST)
                df = dlf / v["f"] - dk
                sig, sq, qz = v["sig"], v["sq"], v["qz"]
                dfz_ref[rows, cols] = (df * (1.0 - v["lbv"]) * sig * (1.0 - sig)).astype(BF16)
                dlb_ref[:, cols] += jnp.sum(df * (1.0 - sig), axis=0, keepdims=True)
                dqz_ref[rows, cols] = (dq * (sq * (1.0 + qz * (1.0 - sq)))).astype(BF16)
                div_ref[rows, cols] = dv.astype(BF16)
            return carry

        lax.fori_loop(0, ncb, chunk, 0)

    def col(k):
        return pl.BlockSpec((tb, hb * HEAD), lambda h, i: (nt - 1 - i, base + k * nh + h))

    ocol = pl.BlockSpec((tb, hb * HEAD), lambda h, i: (nt - 1 - i, h))
    vec = pl.BlockSpec((1, hb * HEAD), lambda h, i: (0, h))
    outs = pl.pallas_call(
        body, name=name, grid=(nh, nt),
        in_specs=[col(0), col(1), col(2), col(3),
                  pl.BlockSpec((tb, hb * HEAD), lambda h, i: (nt - 1 - i, dcat_cols - nh + h)),
                  pl.BlockSpec((tb, hb * HEAD), lambda h, i: (nt - 1 - i, h)),
                  pl.BlockSpec((hb, ncb, HEAD, HEAD), lambda h, i: (h, nt - 1 - i, 0, 0)), vec, vec,
                  pl.BlockSpec(memory_space=pl.ANY)],
        out_specs=[ocol, ocol, ocol, ocol, vec, vec],
        out_shape=[jax.ShapeDtypeStruct((S, W), BF16)] * 4 + [jax.ShapeDtypeStruct((1, W), F32)] * 2,
        scratch_shapes=[pltpu.VMEM((hb, HEAD, HEAD), F32)],
        compiler_params=_cp(("parallel", "arbitrary")))(proj, proj, proj, proj, dcat, o_raw, states,
                                                        lb.reshape(1, W), gain.reshape(1, W), after)
    return outs


def _lb_all(lb_logits):
    p = jax.nn.softmax(lb_logits.astype(F32), axis=0)
    return jnp.cumsum(p, axis=0) - p


def _local_step(x, mem, target, rel_bias, lb_logits, gains, hg_norm, comm):
    S, D = x.shape
    L = gains.shape[0]
    H = (D // 2) // HEAD
    lb = _lb_all(lb_logits)
    tables = [_branch_tables(r) for _, r in BRANCHES]
    onehots = [jnp.where(ok[None], jax.nn.one_hot(bk, REL_BUCKETS, dtype=F32, axis=0), 0.0)
               .reshape(REL_BUCKETS, 2 * Q_BLOCK * Q_BLOCK) for bk, ok in tables]
    bias = [jnp.where(ok[None], jnp.dot(rel_bias.astype(F32).T, oh, precision=lax.Precision.HIGHEST)
                      .reshape(H, Q_BLOCK, 2 * Q_BLOCK), NEG_INF) for oh, (_, ok) in zip(onehots, tables)]

    saved = []
    _, h = _norm_fwd(x, None, None, gains[0, 0], "norm_first")
    xl = x
    for l in range(L):
        g = gains[l]
        w_in, token = comm.weights(l, xl)
        proj = _mm_nn(h, w_in, F32, "mm_in", after=token)
        outs, lses = [], []
        for gi, (_, r) in enumerate(BRANCHES):
            o, ls = _attn_branch_fwd(proj, bias[gi], r, H, f"attn_fwd_d{r}")
            outs.append(o)
            lses.append(ls)
        cat, o_attn, lse_tot = _attn_merge(outs, lses, D, "attn_merge")
        cat, o_raw, states = _hg_fwd(proj, cat, lb[l], hg_norm[l], H, "hgrn_fwd")
        w = dict(comm.rest(l, cat), **{"in": w_in})
        mix = _mm_nn(cat, w["out"], F32, "mm_out")
        x1, hc = _norm_fwd(xl, mix, g[1], g[2], "norm_post_pre")
        _, mn = _norm_fwd(mem, None, None, g[3], "norm_mem")
        cq = _mm_nn(hc, w["cq"], BF16, "mm_cq")
        ckv = _mm_nn(mn, w["ckv"], BF16, "mm_ckv")
        co = _cross_fwd(cq, ckv, "cross_fwd")
        cop = _mm_nn(co, w["co"], F32, "mm_co")
        x2, hf = _norm_fwd(x1, cop, g[4], g[5], "norm_post_pre")
        gu, act = _mm_gate_up(hf, w["gu"], "mm_gu")
        y = _mm_nn(act, w["down"], F32, "mm_down")
        g_next = gains[l + 1, 0] if l + 1 < L else None
        x3, h_next = _norm_fwd(x2, y, g[6], g_next, "norm_post_pre" if l + 1 < L else "norm_post")
        saved.append(dict(x0=xl, h0=h, proj=proj, cat=cat, o_attn=o_attn, lse_tot=lse_tot, o_raw=o_raw, states=states,
                          mix=mix, x1=x1, hc=hc, mn=mn, cq=cq, ckv=ckv, co=co, cop=cop, x2=x2, hf=hf, gu=gu, act=act,
                          y=y, x3=x3, w=w))
        comm.layer_done(l, x3)
        xl, h = x3, h_next

    loss, dres = _loss_and_grad(xl, target)

    dh_next = bwd_token = None
    d_gains = [[None] * 7 for _ in range(L)]
    d_lb = [None] * L
    d_hg = [None] * L
    d_bias = [jnp.zeros((H, Q_BLOCK, 2 * Q_BLOCK), F32) for _ in BRANCHES]
    for l in reversed(range(L)):
        g, sv = gains[l], saved[l]
        w = sv["w"]
        g_next = gains[l + 1, 0] if l + 1 < L else None
        t3, dy, dg_next, dg6 = _norm_bwd(dres, dh_next, sv["x3"], sv["y"], g_next, g[6], "norm_bwd")
        if l + 1 < L:
            d_gains[l + 1][0] = dg_next
        d_gains[l][6] = dg6
        gw = {}
        gw["down"] = _mm_tn(sv["act"], dy, 1, BF16, "mm_dw_down")
        dgu = _mm_dx_down_swiglu(dy, w["down"], sv["gu"], "mm_dx_down", after=bwd_token)
        gw["gu"] = _mm_tn(sv["hf"], dgu, w["gu"].shape[0], BF16, "mm_dw_gu")
        dhf = _mm_nt(dgu, w["gu"], F32, "mm_dx_gu")
        t2, dcop, d_gains[l][5], d_gains[l][4] = _norm_bwd(t3, dhf, sv["x2"], sv["cop"], g[5], g[4], "norm_bwd")
        gw["co"] = _mm_tn(sv["co"], dcop, 1, BF16, "mm_dw_sq")
        dco = _mm_nt(dcop, w["co"], F32, "mm_dx_sq")
        dcq, dckv = _cross_bwd(sv["cq"], sv["ckv"], dco, "cross_bwd")
        gw["cq"] = _mm_tn(sv["hc"], dcq, 1, BF16, "mm_dw_sq")
        dhc = _mm_nt(dcq, w["cq"], F32, "mm_dx_sq")
        gw["ckv"] = _mm_tn(sv["mn"], dckv, w["ckv"].shape[0], BF16, "mm_dw_ckv")
        dmn = _mm_nt(dckv, w["ckv"], F32, "mm_dx_ckv")
        _, _, d_gains[l][3], _ = _norm_bwd(None, dmn, mem, None, g[3], None, "norm_bwd_mem")
        t1, dmix, d_gains[l][2], d_gains[l][1] = _norm_bwd(t2, dhc, sv["x1"], sv["mix"], g[2], g[1], "norm_bwd")
        gw["out"] = _mm_tn(sv["cat"], dmix, 1, BF16, "mm_dw_sq")
        dcat = _mm_nt(dmix, w["out"], F32, "mm_dx_sq")
        delta = _attn_delta(dcat, sv["o_attn"], H, "attn_delta", after=comm.early_grads(l, gw))
        dqkv = []
        for gi, (_, r) in enumerate(BRANCHES):
            dq, dk, dv, db = _attn_branch_bwd(sv["proj"], dcat, sv["lse_tot"], delta, bias[gi], r, H, f"attn_bwd_d{r}")
            dqkv.append((dq, dk, dv))
            d_bias[gi] = d_bias[gi] + db
        dfz, div, dqz, dgz, dlb_l, dhg_l = _hg_bwd(sv["proj"], dcat, sv["o_raw"], sv["states"], lb[l], hg_norm[l],
                                                   delta, H, "hgrn_bwd")
        d_lb[l], d_hg[l] = dlb_l[0], dhg_l[0]
        dproj = _assemble_dproj(dqkv, [dfz, div, dqz, dgz], "assemble_dproj")
        gw["in"] = _mm_tn(sv["h0"], dproj, w["in"].shape[0], BF16, "mm_dw_in")
        dh_next = _mm_nt(dproj, w["in"], F32, "mm_dx_in")
        dres = t1
        bwd_token = comm.grads(l, gw, gw["in"])
    grad_x, _, d_gains[0][0], _ = _norm_bwd(dres, dh_next, x, None, gains[0, 0], None, "norm_bwd_first")

    d_rel_bias = _bias_grad(d_bias, onehots, H).T
    d_gains = jnp.stack([jnp.stack(row) for row in d_gains])
    return loss, grad_x, d_rel_bias, jnp.stack(d_lb), jnp.stack(d_hg), d_gains


def _place():
    return lax.axis_index("x"), lax.axis_index("y"), lax.axis_index("c")


def _all_gather(shards, name):
    n = len(shards)
    HBM = pl.BlockSpec(memory_space=pltpu.HBM)

    def body(*refs):
        ins, outs = refs[:n], refs[n:2 * n]
        send_sems, recv_sems, local_sems = refs[2 * n:]
        x, y, c = _place()
        me, sibling = (x, y, c), (x, y, 1 - c)
        chips = [(1 - x, y), (x, 1 - y), (1 - x, 1 - y)]

        def copy(a, k, block, to, own=False):
            dst = outs[a].at[4 * block[0] + 2 * block[1] + block[2]]
            return pltpu.make_async_remote_copy(
                src_ref=ins[a] if own else dst, dst_ref=dst, send_sem=send_sems.at[7 * a + k],
                recv_sem=recv_sems.at[7 * a + k], device_id=to, device_id_type=MESH)

        mine = [pltpu.make_async_copy(ins[a], outs[a].at[4 * x + 2 * y + c], local_sems.at[a]) for a in range(n)]
        for cp in mine:
            cp.start()
        first = []
        for a in range(n):
            first.append(copy(a, 0, me, sibling, own=True))
            first += [copy(a, 1 + j, me, (*chip, c), own=True) for j, chip in enumerate(chips)]
        for cp in first:
            cp.start()
        passed = []
        for j, chip in enumerate(chips):
            for a in range(n):
                copy(a, 1 + j, (*chip, c), me).wait_recv()
                fwd = copy(a, 4 + j, (*chip, c), sibling)
                fwd.start()
                passed.append(fwd)
        for a in range(n):
            copy(a, 0, sibling, me).wait_recv()
            for j, chip in enumerate(chips):
                copy(a, 4 + j, (*chip, 1 - c), me).wait_recv()
        for cp in first + passed:
            cp.wait_send()
        for cp in mine:
            cp.wait()

    return pl.pallas_call(
        body, name=name, in_specs=[HBM] * n, out_specs=[HBM] * n,
        out_shape=[jax.ShapeDtypeStruct((N_DEV,) + s.shape, s.dtype) for s in shards],
        scratch_shapes=[pltpu.SemaphoreType.DMA((7 * n,)), pltpu.SemaphoreType.DMA((7 * n,)),
                        pltpu.SemaphoreType.DMA((n,))],
    )(*shards)


def _exchange_sibling(grads, name):
    n = len(grads)
    HBM = pl.BlockSpec(memory_space=pltpu.HBM)

    def body(*refs):
        ins, outs = refs[:n], refs[n:2 * n]
        send_sems, recv_sems = refs[2 * n:]
        x, y, c = _place()
        sibling = (x, y, 1 - c)
        for a in range(n):
            for q in range(4):
                pltpu.make_async_remote_copy(
                    src_ref=ins[a].at[2 * q + 1 - c], dst_ref=outs[a].at[q], send_sem=send_sems.at[a],
                    recv_sem=recv_sems.at[a], device_id=sibling, device_id_type=MESH).start()
        for a in range(n):
            pltpu.make_async_remote_copy(
                src_ref=ins[a].at[pl.ds(0, 4)], dst_ref=outs[a], send_sem=send_sems.at[a], recv_sem=recv_sems.at[a],
                device_id=sibling, device_id_type=MESH).wait()

    return pl.pallas_call(
        body, name=name, in_specs=[HBM] * n, out_specs=[HBM] * n,
        out_shape=[jax.ShapeDtypeStruct((4,) + g.shape[1:], g.dtype) for g in grads],
        scratch_shapes=[pltpu.SemaphoreType.DMA((n,)), pltpu.SemaphoreType.DMA((n,))],
    )(*grads)


def _exchange_chips(parts, name):
    n = len(parts)
    HBM = pl.BlockSpec(memory_space=pltpu.HBM)

    def body(*refs):
        ins, outs = refs[:n], refs[n:2 * n]
        send_sems, recv_sems = refs[2 * n:]
        x, y, c = _place()
        chips = [(1 - x, y), (x, 1 - y), (1 - x, 1 - y)]
        copies = []
        for a in range(n):
            for j, chip in enumerate(chips):
                cp = pltpu.make_async_remote_copy(
                    src_ref=ins[a].at[2 * chip[0] + chip[1]], dst_ref=outs[a].at[j], send_sem=send_sems.at[3 * a + j],
                    recv_sem=recv_sems.at[3 * a + j], device_id=(*chip, c), device_id_type=MESH)
                cp.start()
                copies.append(cp)
        for cp in copies:
            cp.wait()

    return pl.pallas_call(
        body, name=name, in_specs=[HBM] * n, out_specs=[HBM] * n,
        out_shape=[jax.ShapeDtypeStruct((3,) + p.shape[1:], p.dtype) for p in parts],
        scratch_shapes=[pltpu.SemaphoreType.DMA((3 * n,)), pltpu.SemaphoreType.DMA((3 * n,))],
    )(*parts)


def _others(x, y, c):
    out = []
    for k in range(1, N_DEV):
        px = 1 - x if (k >> 2) & 1 else x
        py = 1 - y if (k >> 1) & 1 else y
        pc = 1 - c if k & 1 else c
        out.append(((px, py, pc), 4 * px + 2 * py + pc))
    return out


_HBM_SPEC = pl.BlockSpec(memory_space=pltpu.HBM)
_SEM_SPEC = pl.BlockSpec(memory_space=pltpu.SEMAPHORE)
_EFFECT = pltpu.SideEffectType.DATAFLOW_SIDE_EFFECTING


def _copies_start(srcs, lands, after, src_block, name):
    n = len(srcs)

    def body(*refs):
        src_refs, land_refs = refs[:n], refs[n:2 * n]
        send_sems, recv_sems = refs[2 * n + 1], refs[2 * n + 2]
        token = refs[-1]
        x, y, c = _place()
        mine = 4 * x + 2 * y + c
        for a in range(n):
            for k, (peer, block) in enumerate(_others(x, y, c)):
                pltpu.make_async_remote_copy(
                    src_ref=src_refs[a].at[block] if src_block else src_refs[a],
                    dst_ref=land_refs[a].at[k] if src_block else land_refs[a].at[mine],
                    send_sem=send_sems.at[7 * a + k], recv_sem=recv_sems.at[7 * a + k],
                    device_id=peer, device_id_type=MESH).start()
        token[...] = jnp.zeros_like(token)

    outs = pl.pallas_call(
        body, name=name,
        out_shape=(pltpu.SemaphoreType.DMA((7 * n,)), pltpu.SemaphoreType.DMA((7 * n,)),
                   *[pltpu.HBM(s.shape, s.dtype) for s in srcs], *[pltpu.HBM(t.shape, t.dtype) for t in lands],
                   jax.ShapeDtypeStruct((8, 128), F32)),
        in_specs=[_HBM_SPEC] * (2 * n) + [pl.BlockSpec(memory_space=pl.ANY)],
        out_specs=(_SEM_SPEC, _SEM_SPEC, *[_HBM_SPEC] * (2 * n), pl.BlockSpec(memory_space=pltpu.VMEM)),
        input_output_aliases={i: 2 + i for i in range(2 * n)},
        compiler_params=pltpu.CompilerParams(has_side_effects=_EFFECT),
    )(*[pltpu.with_memory_space_constraint(s, pltpu.HBM) for s in srcs],
      *[pltpu.with_memory_space_constraint(t, pltpu.HBM) for t in lands], after)
    return outs[0], outs[1], list(outs[2:2 + n]), list(outs[2 + n:2 + 2 * n]), outs[-1]


def _copies_wait(send_sems, recv_sems, srcs, lands, after, src_block, name):
    n = len(srcs)

    def body(*refs):
        src_refs, land_refs = refs[:n], refs[n:2 * n]
        send_sems, recv_sems = refs[2 * n], refs[2 * n + 1]
        x, y, c = _place()
        for a in range(n):
            for k, (peer, block) in enumerate(_others(x, y, c)):
                cp = pltpu.make_async_remote_copy(
                    src_ref=src_refs[a].at[block] if src_block else src_refs[a],
                    dst_ref=land_refs[a].at[k] if src_block else land_refs[a].at[block],
                    send_sem=send_sems.at[7 * a + k], recv_sem=recv_sems.at[7 * a + k],
                    device_id=peer, device_id_type=MESH)
                cp.wait_send()
                cp.wait_recv()

    outs = pl.pallas_call(
        body, name=name,
        out_shape=(*[pltpu.HBM(s.shape, s.dtype) for s in srcs], *[pltpu.HBM(t.shape, t.dtype) for t in lands]),
        in_specs=[_HBM_SPEC] * (2 * n) + [_SEM_SPEC, _SEM_SPEC, pl.BlockSpec(memory_space=pl.ANY)],
        out_specs=tuple([_HBM_SPEC] * (2 * n)),
        input_output_aliases={i: i for i in range(2 * n)},
        compiler_params=pltpu.CompilerParams(has_side_effects=_EFFECT),
    )(*srcs, *lands, send_sems, recv_sems, after)
    return list(outs[:n]), list(outs[n:])


def _pair_sum(grad, recv, c_idx, name):
    _, R, C = grad.shape
    tr = _tile(R, 512, 16)

    def body(c_ref, g_ref, p_ref, o_ref):
        del c_ref
        o_ref[...] = (g_ref[...].astype(F32) + p_ref[...].astype(F32)).astype(BF16)

    spec = pltpu.PrefetchScalarGridSpec(
        num_scalar_prefetch=1, grid=(4, R // tr),
        in_specs=[pl.BlockSpec((None, tr, C), lambda q, i, c: (2 * q + c[0], i, 0)),
                  pl.BlockSpec((None, tr, C), lambda q, i, c: (q, i, 0))],
        out_specs=pl.BlockSpec((None, tr, C), lambda q, i, c: (q, i, 0)))
    return pl.pallas_call(body, name=name, grid_spec=spec, out_shape=jax.ShapeDtypeStruct((4, R, C), BF16),
                          compiler_params=_cp(("parallel", "parallel")))(c_idx, grad, recv)


def _adam(w, g, m, v):
    m2 = ADAM_B1 * m + (1.0 - ADAM_B1) * g
    v2 = ADAM_B2 * v + (1.0 - ADAM_B2) * (g * g)
    m_hat = m2 / (1.0 - ADAM_B1 ** ADAM_STEP)
    v_hat = v2 / (1.0 - ADAM_B2 ** ADAM_STEP)
    return -ADAM_LR * (m_hat / (jnp.sqrt(v_hat) + ADAM_EPS) + ADAM_WD * w), m2, v2


def _reduce_update(part, recv, own_idx, w, m, v, layer, prev, name):
    L, R, C = w.shape
    nr = recv.shape[0]
    tr = _tile(R, 128, 16)
    has_prev = prev is not None

    def body(idx_ref, q_ref, *rest):
        del idx_ref
        r_refs, (w_ref, m_ref, v_ref) = rest[:nr], rest[nr:nr + 3]
        g_ref, d_ref, m2_ref, v2_ref = rest[-4:]
        g = q_ref[...].astype(F32)
        for r_ref in r_refs:
            g = g + r_ref[...].astype(F32)
        d, m2, v2 = _adam(w_ref[...], g, m_ref[...], v_ref[...])
        g_ref[...] = g
        d_ref[...] = d
        m2_ref[...] = m2
        v2_ref[...] = v2

    def rblk(j):
        return pl.BlockSpec((None, tr, C), lambda i, k: (j, i, 0))

    lay = pl.BlockSpec((None, tr, C), lambda i, k: (layer, i, 0))
    in_specs = [pl.BlockSpec((None, tr, C), lambda i, k: (k[0], i, 0))] + [rblk(j) for j in range(nr)] + [lay, lay, lay]
    ins = [part] + [recv] * nr + [w, m, v]
    aliases = {}
    if has_prev:
        in_specs += [pl.BlockSpec(memory_space=pl.ANY)] * 4
        ins += list(prev)
        aliases = {1 + len(ins) - 4 + t: t for t in range(4)}
    spec = pltpu.PrefetchScalarGridSpec(num_scalar_prefetch=1, grid=(R // tr,), in_specs=in_specs,
                                        out_specs=[lay, lay, lay, lay])
    return pl.pallas_call(body, name=name, grid_spec=spec, out_shape=[jax.ShapeDtypeStruct((L, R, C), F32)] * 4,
                          input_output_aliases=aliases, compiler_params=_cp(("parallel",)))(own_idx, *ins)


def _sum_devices(gathered):
    n, R, C = gathered.shape

    def body(g_ref, o_ref):
        acc = g_ref[0]
        for d in range(1, n):
            acc = acc + g_ref[d]
        o_ref[...] = acc

    return pl.pallas_call(body, name="sum_devices", out_shape=jax.ShapeDtypeStruct((R, C), F32))(gathered)


def _adam_small(w, g, m, v, name):
    shape = w.shape
    two = (-1, shape[-1])

    def body(w_ref, g_ref, m_ref, v_ref, d_ref, m2_ref, v2_ref):
        d, m2, v2 = _adam(w_ref[...], g_ref[...], m_ref[...], v_ref[...])
        d_ref[...] = d
        m2_ref[...] = m2
        v2_ref[...] = v2

    outs = pl.pallas_call(body, name=name, out_shape=[jax.ShapeDtypeStruct(w.reshape(two).shape, F32)] * 3)(
        w.reshape(two), g.reshape(two), m.reshape(two), v.reshape(two))
    return [o.reshape(shape) for o in outs]


_SHARDED = ("in", "out", "cq", "ckv", "co", "gu", "down")
_COLUMN_SHARDED = ("in", "ckv", "gu")


def kernel(x, mem, rel_bias, lb_logits, norm_gains, w_in, hg_norm, w_out, w_cq, w_ckv, w_co, w_gate_up, w_down, loss_target, m_rel_bias, m_lb_logits, m_norm_gains, m_w_in, m_hg_norm, m_w_out, m_w_cq, m_w_ckv, m_w_co, m_w_gate_up, m_w_down, v_rel_bias, v_lb_logits, v_norm_gains, v_w_in, v_hg_norm, v_w_out, v_w_cq, v_w_ckv, v_w_co, v_w_gate_up, v_w_down):
    L = w_in.shape[0]
    D = x.shape[-1]
    ws = dict(zip(_SHARDED, (w_in, w_out, w_cq, w_ckv, w_co, w_gate_up, w_down)))
    ms = dict(zip(_SHARDED, (m_w_in, m_w_out, m_w_cq, m_w_ckv, m_w_co, m_w_gate_up, m_w_down)))
    vs = dict(zip(_SHARDED, (v_w_in, v_w_out, v_w_cq, v_w_ckv, v_w_co, v_w_gate_up, v_w_down)))
    px, py, pc = _place()
    dev = 4 * px + 2 * py + pc
    c_idx = jnp.reshape(pc, (1,)).astype(jnp.int32)
    chip_idx = jnp.reshape(2 * px + py, (1,)).astype(jnp.int32)
    dev_idx = jnp.reshape(dev, (1,)).astype(jnp.int32)

    rest_keys = _SHARDED[1:]

    def shards_of(l, keys):
        return [ws[k][l].astype(BF16) for k in keys]

    def as_weights(keys, gathered):
        return {k: g if k in _COLUMN_SHARDED else g.reshape(1, N_DEV * g.shape[1], g.shape[2])
                for k, g in zip(keys, gathered)}

    def start_gather(shards, after, name):
        lands = [lax.empty((N_DEV,) + s.shape, s.dtype) for s in shards]
        return _copies_start(shards, lands, after, False, name)

    def wait_gather(pending, keys, after, name):
        send_sems, recv_sems, shards, lands = pending
        shards, lands = _copies_wait(send_sems, recv_sems, shards, lands, after, False, name)
        full = [lax.dynamic_update_slice_in_dim(t, s[None], dev, axis=0) for s, t in zip(shards, lands)]
        return as_weights(keys, full)

    def full_grads(gw, keys):
        return [gw[k].reshape((N_DEV,) + ws[k].shape[1:]) for k in keys]

    def start_scatter(full, after, name):
        lands = [lax.empty((N_DEV - 1,) + g.shape[1:], g.dtype) for g in full]
        return _copies_start(full, lands, after, True, name)

    class Comm:
        def __init__(self):
            first, gains = _all_gather(shards_of(0, _SHARDED[:1]) + [norm_gains], "gather_first")
            self.gains = jnp.transpose(gains, (1, 2, 0, 3)).reshape(L, 7, D)
            self.ready = {0: as_weights(_SHARDED[:1], [first])}
            *self.rest0, self.order = start_gather(shards_of(0, rest_keys), first, "gather_start_rest")
            self.gather = None
            self.scatter = None
            self.early = None
            self.results = {k: None for k in _SHARDED}

        def weights(self, l, x_in):
            del x_in
            token = None
            if l + 1 < L:
                after = self.order if l == 0 else self.ready[l]["in"]
                *self.gather, token = start_gather(shards_of(l + 1, _SHARDED), after, "gather_start")
            return self.ready[l]["in"], token

        def rest(self, l, after):
            if l == 0:
                self.ready[0].update(wait_gather(self.rest0, rest_keys, after, "gather_wait_rest"))
            mine = self.ready.pop(l)
            return {k: mine[k] for k in rest_keys}

        def layer_done(self, l, x_out):
            if l + 1 < L:
                self.ready[l + 1] = wait_gather(self.gather, _SHARDED, x_out, "gather_wait")

        def update(self, l, keys, parts, recvs, own_idx):
            for k, part, recv in zip(keys, parts, recvs):
                self.results[k] = _reduce_update(part, recv, own_idx, ws[k], ms[k], vs[k], l, self.results[k],
                                                 "reduce_update")

        def finish_scatter(self, after):
            lp, send_sems, recv_sems, srcs, lands = self.scatter
            srcs, lands = _copies_wait(send_sems, recv_sems, srcs, lands, after, True, "scatter_wait")
            self.update(lp, _SHARDED, srcs, lands, dev_idx)
            self.scatter = None
            return lands[0]

        def early_grads(self, l, gw):
            if l > 0 or L == 1:
                return None
            order = self.finish_scatter(gw["out"])
            *self.early, token = start_scatter(full_grads(gw, rest_keys), order, "scatter_start_rest")
            return token

        def grads(self, l, gw, after):
            if l > 0:
                order = self.finish_scatter(after) if self.scatter is not None else dev_idx
                send_sems, recv_sems, srcs, lands, token = start_scatter(full_grads(gw, _SHARDED), order,
                                                                         "scatter_start")
                self.scatter = (l, send_sems, recv_sems, srcs, lands)
                return token
            last_keys = _SHARDED
            if self.early is not None:
                send_sems, recv_sems, srcs, lands = self.early
                srcs, lands = _copies_wait(send_sems, recv_sems, srcs, lands, after, True, "scatter_wait_rest")
                self.update(0, rest_keys, srcs, lands, dev_idx)
                last_keys = _SHARDED[:1]
            full = full_grads(gw, last_keys)
            from_sibling = _exchange_sibling(full, "scatter_sibling")
            parts = [_pair_sum(g, r, c_idx, "pair_sum") for g, r in zip(full, from_sibling)]
            self.update(0, last_keys, parts, _exchange_chips(parts, "scatter_chips"), chip_idx)
            return None

    comm = Comm()
    loss, grad_x, d_rel_bias, d_lb, d_hg, d_gains = _local_step(
        x[0], mem[0], loss_target[0], rel_bias, lb_logits, comm.gains, hg_norm, comm)
    results = comm.results

    pieces = [jnp.reshape(loss, (1,)), d_rel_bias.reshape(-1), d_lb.reshape(-1), d_hg.reshape(-1), d_gains.reshape(-1)]
    sizes = [p.shape[0] for p in pieces]
    total = sum(sizes)
    rows = -(-total // 1024) * 8
    pack = jnp.pad(jnp.concatenate(pieces), (0, rows * 128 - total)).reshape(rows, 128)
    summed = _sum_devices(_all_gather([pack], "gather_small")[0]).reshape(-1)
    offs = [sum(sizes[:i]) for i in range(len(sizes))]
    loss = summed[0]
    g_rel_bias = summed[offs[1]:offs[1] + sizes[1]].reshape(rel_bias.shape)
    g_lb_all = summed[offs[2]:offs[2] + sizes[2]].reshape(lb_logits.shape)
    g_hg = summed[offs[3]:offs[3] + sizes[3]].reshape(hg_norm.shape)
    g_gains_full = summed[offs[4]:offs[4] + sizes[4]].reshape(L, 7, D)
    g_gains = lax.dynamic_slice_in_dim(g_gains_full, dev * (D // N_DEV), D // N_DEV, axis=2)
    g_lb = jax.vjp(_lb_all, lb_logits)[1](g_lb_all)[0]

    small = [(rel_bias, g_rel_bias, m_rel_bias, v_rel_bias), (lb_logits, g_lb, m_lb_logits, v_lb_logits),
             (norm_gains, g_gains, m_norm_gains, v_norm_gains), (hg_norm, g_hg, m_hg_norm, v_hg_norm)]
    upd = [_adam_small(w, g, m, v, "adam_small") for w, g, m, v in small]

    def ordered(small_vals, big_vals):
        return [small_vals[0], small_vals[1], small_vals[2], big_vals["in"], small_vals[3], big_vals["out"],
                big_vals["cq"], big_vals["ckv"], big_vals["co"], big_vals["gu"], big_vals["down"]]

    grads = ordered([s[1] for s in small], {k: results[k][0] for k in _SHARDED})
    deltas = ordered([u[0] for u in upd], {k: results[k][1] for k in _SHARDED})
    new_m = ordered([u[1] for u in upd], {k: results[k][2] for k in _SHARDED})
    new_v = ordered([u[2] for u in upd], {k: results[k][3] for k in _SHARDED})
    return (loss, grad_x[None], *grads, *deltas, *new_m, *new_v)
```

```python
import functools
import math

import jax
import jax.numpy as jnp
from jax import lax
from jax.experimental import pallas as pl
from jax.experimental.pallas import tpu as pltpu

F32 = jnp.float32
BF16 = jnp.bfloat16
MESH = pl.DeviceIdType.MESH

N_DEV = 8
HEAD = 128
CROSS_HEADS = 4
Q_BLOCK = 128
BRANCHES = ((128, 1), (512, 4), (2048, 16))
REL_BUCKETS = 32
REL_MAX_DIST = 2048
HG_CHUNK = 16
HG_HEADS_PER_STEP = 8
ATTN_PAIRS_PER_STAGE = 4
RMS_EPS = 1e-6
NEG_INF = -1e30
ADAM_LR, ADAM_B1, ADAM_B2, ADAM_EPS, ADAM_WD, ADAM_STEP = 0.001, 0.9, 0.999, 1e-08, 0.01, 10
VMEM_LIMIT = 48 * 1024 * 1024
MXU_WIDTH = 256
NT_K_PER_STEP = 3584


def _tile(n, target, mult):
    best = None
    t = mult
    while t <= min(n, target):
        if n % t == 0:
            best = t
        t += mult
    return best if best is not None else n


def _cp(sem, vmem=VMEM_LIMIT):
    return pltpu.CompilerParams(dimension_semantics=sem, vmem_limit_bytes=vmem)


def _mm_nn(a, b3, out_dtype, name, after=None):
    M, K = a.shape
    J, K2, Nb = b3.shape
    assert K == K2
    tm, tn, tk = _tile(M, 1024, 8), _tile(Nb, 1408, 128), _tile(K, 2816, 128)
    nn, nk = Nb // tn, K // tk
    extra = [] if after is None else [after]
    jb = 2 if nn == 1 and tn % MXU_WIDTH and J % 2 == 0 else 1

    def body(a_ref, b_ref, *rest):
        o_ref, acc = rest[len(extra)], rest[len(extra) + 1:]
        bv = b_ref[0] if jb == 1 else jnp.concatenate([b_ref[j] for j in range(jb)], axis=1)
        prod = jnp.dot(a_ref[...].astype(BF16), bv.astype(BF16), preferred_element_type=F32)
        if nk == 1:
            o_ref[...] = prod.astype(o_ref.dtype)
        else:
            k = pl.program_id(2)

            @pl.when(k == 0)
            def _():
                acc[0][...] = prod

            @pl.when(k > 0)
            def _():
                acc[0][...] += prod

            @pl.when(k == nk - 1)
            def _():
                o_ref[...] = acc[0][...].astype(o_ref.dtype)

    return pl.pallas_call(
        body, name=name, grid=(M // tm, J * nn // jb, nk),
        in_specs=[pl.BlockSpec((tm, tk), lambda i, n, k: (i, k)),
                  pl.BlockSpec((jb, tk, tn), lambda i, n, k: (n // nn, k, n % nn))]
                 + [pl.BlockSpec(memory_space=pl.ANY)] * len(extra),
        out_specs=pl.BlockSpec((tm, jb * tn), lambda i, n, k: (i, n)),
        out_shape=jax.ShapeDtypeStruct((M, J * Nb), out_dtype),
        scratch_shapes=[] if nk == 1 else [pltpu.VMEM((tm, jb * tn), F32)],
        compiler_params=_cp(("parallel", "parallel", "arbitrary")),
    )(a, b3, *extra)


def _mm_nt(a, b3, out_dtype, name, after=None):
    extra = [] if after is None else [after]
    halves = a.ndim == 3
    M, Kt = (a.shape[1], 2 * a.shape[2]) if halves else a.shape
    J, N, Kb = b3.shape
    assert Kt == J * Kb
    tm, tn = _tile(M, 1024, 8), _tile(N, 1408, 128)
    tk = _tile(math.gcd(Kb, Kt // 2) if halves else Kb, 2048, 128)
    nkb = Kb // tk
    groups = J // 2 if halves and J > 1 else J
    jb = max(j for j in range(1, groups + 1) if groups % j == 0 and j * Kb <= NT_K_PER_STEP) if nkb == 1 else 1
    nk = (J // jb) * nkb

    def body(a_ref, b_ref, *rest):
        o_ref, acc = rest[len(extra)], rest[len(extra) + 1:]
        bv = b_ref[0] if jb == 1 else jnp.concatenate([b_ref[j] for j in range(jb)], axis=1)
        prod = lax.dot_general(a_ref[...].astype(BF16), bv.astype(BF16), (((1,), (1,)), ((), ())),
                               preferred_element_type=F32)
        if nk == 1:
            o_ref[...] = prod.astype(o_ref.dtype)
        else:
            k = pl.program_id(2)

            @pl.when(k == 0)
            def _():
                acc[0][...] = prod

            @pl.when(k > 0)
            def _():
                acc[0][...] += prod

            @pl.when(k == nk - 1)
            def _():
                o_ref[...] = acc[0][...].astype(o_ref.dtype)

    return pl.pallas_call(
        body, name=name, grid=(M // tm, N // tn, nk),
        in_specs=[pl.BlockSpec((None, tm, jb * tk), lambda i, n, k: (k // (nk // 2), i, k % (nk // 2))) if halves
                  else pl.BlockSpec((tm, jb * tk), lambda i, n, k: (i, k)),
                  pl.BlockSpec((jb, tn, tk), lambda i, n, k: (k // nkb, n, k % nkb))]
                 + [pl.BlockSpec(memory_space=pl.ANY)] * len(extra),
        out_specs=pl.BlockSpec((tm, tn), lambda i, n, k: (i, n)),
        out_shape=jax.ShapeDtypeStruct((M, N), out_dtype),
        scratch_shapes=[] if nk == 1 else [pltpu.VMEM((tm, tn), F32)],
        compiler_params=_cp(("parallel", "parallel", "arbitrary")),
    )(a, b3, *extra)


def _mm_tn(a, b, J, out_dtype, name):
    S, M = a.shape
    halves = b.ndim == 3
    S2, Nt = (b.shape[1], 2 * b.shape[2]) if halves else b.shape
    assert S == S2 and Nt % J == 0
    Nb = Nt // J
    tm, tk = _tile(M, 1408, 128), _tile(S, 2048, 8)
    tn = _tile(math.gcd(Nb, Nt // 2) if halves else Nb, 1408, 128)
    nn, nk = Nb // tn, S // tk
    per_half = J * nn // 2
    jb = 2 if nn == 1 and tn % MXU_WIDTH and J % 2 == 0 and not halves and 2 * tn <= 2048 else 1
    b_spec = (pl.BlockSpec((None, tk, tn), lambda i, n, k: (n // per_half, k, n % per_half)) if halves
              else pl.BlockSpec((tk, jb * tn), lambda i, n, k: (k, n)))

    def body(a_ref, b_ref, o_ref, *acc):
        prod = lax.dot_general(a_ref[...].astype(BF16), b_ref[...].astype(BF16), (((0,), (0,)), ((), ())),
                               preferred_element_type=F32)

        def write(val):
            for j in range(jb):
                o_ref[j] = val[:, j * tn:(j + 1) * tn].astype(o_ref.dtype)

        if nk == 1:
            write(prod)
        else:
            k = pl.program_id(2)

            @pl.when(k == 0)
            def _():
                acc[0][...] = prod

            @pl.when(k > 0)
            def _():
                acc[0][...] += prod

            @pl.when(k == nk - 1)
            def _():
                write(acc[0][...])

    return pl.pallas_call(
        body, name=name, grid=(M // tm, J * nn // jb, nk),
        in_specs=[pl.BlockSpec((tk, tm), lambda i, n, k: (k, i)), b_spec],
        out_specs=pl.BlockSpec((jb, tm, tn), lambda i, n, k: (n // nn, i, n % nn)),
        out_shape=jax.ShapeDtypeStruct((J, M, Nb), out_dtype),
        scratch_shapes=[] if nk == 1 else [pltpu.VMEM((tm, jb * tn), F32)],
        compiler_params=_cp(("parallel", "parallel", "arbitrary")),
    )(a, b)


def _rms_scale(v):
    return lax.rsqrt(jnp.mean(v * v, axis=-1, keepdims=True) + RMS_EPS)


def _rms_bwd(dy, v, g):
    r = _rms_scale(v)
    vh = v * r
    u = dy * g
    dv = r * (u - vh * jnp.mean(u * vh, axis=-1, keepdims=True))
    return dv, dy * vh


def _fold8(t):
    R, D = t.shape
    return jnp.sum(t.reshape(R // 8, 8, D), axis=0)


def _norm_fwd(x, z, g_post, g_pre, name):
    S, D = x.shape
    tr = _tile(S, 256, 8)
    has_z, has_pre = z is not None, g_pre is not None

    def body(*refs):
        it = iter(refs)
        x_ref = next(it)
        z_ref = next(it) if has_z else None
        gpost_ref = next(it) if has_z else None
        gpre_ref = next(it) if has_pre else None
        xn_ref = next(it) if has_z else None
        h_ref = next(it) if has_pre else None
        xn = x_ref[...]
        if has_z:
            zz = z_ref[...]
            xn = xn + zz * _rms_scale(zz) * gpost_ref[...]
            xn_ref[...] = xn
        if has_pre:
            h_ref[...] = (xn * _rms_scale(xn) * gpre_ref[...]).astype(BF16)

    row = pl.BlockSpec((tr, D), lambda i: (i, 0))
    gain = pl.BlockSpec((1, D), lambda i: (0, 0))
    ins, specs, outs, ospecs = [x], [row], [], []
    if has_z:
        ins += [z, g_post.reshape(1, D)]
        specs += [row, gain]
        outs.append(jax.ShapeDtypeStruct((S, D), F32))
        ospecs.append(row)
    if has_pre:
        ins.append(g_pre.reshape(1, D))
        specs.append(gain)
        outs.append(jax.ShapeDtypeStruct((S, D), BF16))
        ospecs.append(row)
    res = pl.pallas_call(body, name=name, grid=(S // tr,), in_specs=specs, out_specs=ospecs, out_shape=outs,
                         compiler_params=_cp(("parallel",)))(*ins)
    res = list(res)
    xn = res.pop(0) if has_z else x
    h = res.pop(0) if has_pre else None
    return xn, h


def _norm_bwd(dres, dh, xn, z, g_pre, g_post, name):
    S, D = xn.shape
    tr = _tile(S, 256, 8)
    has_res, has_pre, has_z = dres is not None, dh is not None, z is not None

    def body(*refs):
        it = iter(refs)
        dres_ref = next(it) if has_res else None
        dh_ref = next(it) if has_pre else None
        xn_ref = next(it) if has_pre else None
        gpre_ref = next(it) if has_pre else None
        z_ref = next(it) if has_z else None
        gpost_ref = next(it) if has_z else None
        t_ref = next(it)
        dz_ref = next(it) if has_z else None
        dgpre_ref = next(it) if has_pre else None
        dgpost_ref = next(it) if has_z else None
        first = pl.program_id(0) == 0
        t = dres_ref[...] if has_res else None
        if has_pre:
            dv, dg = _rms_bwd(dh_ref[...].astype(F32), xn_ref[...], gpre_ref[...])
            t = dv if t is None else t + dv
            part = _fold8(dg)

            @pl.when(first)
            def _():
                dgpre_ref[...] = part

            @pl.when(jnp.logical_not(first))
            def _():
                dgpre_ref[...] += part
        t_ref[...] = t
        if has_z:
            dv, dg = _rms_bwd(t, z_ref[...], gpost_ref[...])
            dz_ref[...] = dv.astype(BF16)
            part2 = _fold8(dg)

            @pl.when(first)
            def _():
                dgpost_ref[...] = part2

            @pl.when(jnp.logical_not(first))
            def _():
                dgpost_ref[...] += part2

    row = pl.BlockSpec((tr, D), lambda i: (i, 0))
    gain = pl.BlockSpec((1, D), lambda i: (0, 0))
    accs = pl.BlockSpec((8, D), lambda i: (0, 0))
    ins, specs = [], []
    if has_res:
        ins.append(dres)
        specs.append(row)
    if has_pre:
        ins += [dh, xn, g_pre.reshape(1, D)]
        specs += [row, row, gain]
    if has_z:
        ins += [z, g_post.reshape(1, D)]
        specs += [row, gain]
    outs, ospecs = [jax.ShapeDtypeStruct((S, D), F32)], [row]
    if has_z:
        outs.append(jax.ShapeDtypeStruct((S, D), BF16))
        ospecs.append(row)
    if has_pre:
        outs.append(jax.ShapeDtypeStruct((8, D), F32))
        ospecs.append(accs)
    if has_z:
        outs.append(jax.ShapeDtypeStruct((8, D), F32))
        ospecs.append(accs)
    res = list(pl.pallas_call(body, name=name, grid=(S // tr,), in_specs=specs, out_specs=ospecs, out_shape=outs,
                              compiler_params=_cp(("arbitrary",)))(*ins))
    t = res.pop(0)
    dz = res.pop(0) if has_z else None
    dgpre = jnp.sum(res.pop(0), axis=0) if has_pre else None
    dgpost = jnp.sum(res.pop(0), axis=0) if has_z else None
    return t, dz, dgpre, dgpost


def _loss_and_grad(y, target):
    S, D = y.shape
    tr = _tile(S, 256, 8)

    def body(y_ref, t_ref, dy_ref, l_ref):
        err = y_ref[...] - t_ref[...]
        dy_ref[...] = err * (1.0 / D)
        part = 0.5 * jnp.sum(jnp.mean(err * err, axis=-1, keepdims=True), axis=0, keepdims=True)
        first = pl.program_id(0) == 0

        @pl.when(first)
        def _():
            l_ref[...] = jnp.broadcast_to(part, l_ref.shape)

        @pl.when(jnp.logical_not(first))
        def _():
            l_ref[...] += jnp.broadcast_to(part, l_ref.shape)

    row = pl.BlockSpec((tr, D), lambda i: (i, 0))
    dy, l = pl.pallas_call(
        body, name="loss_grad", grid=(S // tr,), in_specs=[row, row],
        out_specs=[row, pl.BlockSpec((8, 128), lambda i: (0, 0))],
        out_shape=[jax.ShapeDtypeStruct((S, D), F32), jax.ShapeDtypeStruct((8, 128), F32)],
        compiler_params=_cp(("arbitrary",)))(y, target)
    return l[0, 0], dy


def _mm_gate_up(a, w3, name):
    S, D = a.shape
    J, D2, Nb = w3.shape
    F = J * Nb // 2
    assert D == D2
    tm, tn = _tile(S, 512, 8), _tile(math.gcd(Nb, F), 1408, 128)
    nn, nf = Nb // tn, F // tn

    def body(a_ref, wg_ref, wu_ref, gu_ref, act_ref):
        both = jnp.dot(a_ref[...], jnp.concatenate([wg_ref[...], wu_ref[...]], axis=1), preferred_element_type=F32)
        g, u = both[:, :tn], both[:, tn:]
        gu_ref[0] = g
        gu_ref[1] = u
        act_ref[...] = (g * jax.nn.sigmoid(g) * u).astype(BF16)

    return pl.pallas_call(
        body, name=name, grid=(nf, S // tm),
        in_specs=[pl.BlockSpec((tm, D), lambda n, i: (i, 0)),
                  pl.BlockSpec((None, D, tn), lambda n, i: (n // nn, 0, n % nn)),
                  pl.BlockSpec((None, D, tn), lambda n, i: ((n + nf) // nn, 0, (n + nf) % nn))],
        out_specs=[pl.BlockSpec((2, tm, tn), lambda n, i: (0, i, n)), pl.BlockSpec((tm, tn), lambda n, i: (i, n))],
        out_shape=[jax.ShapeDtypeStruct((2, S, F), F32), jax.ShapeDtypeStruct((S, F), BF16)],
        compiler_params=_cp(("parallel", "parallel")))(a, w3, w3)


def _mm_dx_down_swiglu(dy, wdown3, gu, name, after=None):
    S, D = dy.shape
    _, F, D2 = wdown3.shape
    assert D == D2
    tm, tn = _tile(S, 1024, 8), _tile(F, 512, 128)
    extra = [] if after is None else [after]

    def body(a_ref, b_ref, gu_ref, *rest):
        o_ref = rest[len(extra)]
        d = lax.dot_general(a_ref[...], b_ref[...], (((1,), (1,)), ((), ())), preferred_element_type=F32)
        g, u = gu_ref[0], gu_ref[1]
        sg = jax.nn.sigmoid(g)
        o_ref[0] = (d * u * (sg * (1.0 + g * (1.0 - sg)))).astype(BF16)
        o_ref[1] = (d * g * sg).astype(BF16)

    return pl.pallas_call(
        body, name=name, grid=(S // tm, F // tn),
        in_specs=[pl.BlockSpec((tm, D), lambda i, n: (i, 0)), pl.BlockSpec((None, tn, D), lambda i, n: (0, n, 0)),
                  pl.BlockSpec((2, tm, tn), lambda i, n: (0, i, n))] + [pl.BlockSpec(memory_space=pl.ANY)] * len(extra),
        out_specs=pl.BlockSpec((2, tm, tn), lambda i, n: (0, i, n)),
        out_shape=jax.ShapeDtypeStruct((2, S, F), BF16),
        compiler_params=_cp(("parallel", "parallel")))(dy, wdown3, gu, *extra)


def _cross_scores(q, k, scale):
    s = lax.dot_general(q, k, (((1,), (1,)), ((), ())), preferred_element_type=F32) * scale
    m = jnp.max(s, axis=-1, keepdims=True)
    e = jnp.exp(s - m)
    return e / jnp.sum(e, axis=-1, keepdims=True)


def _cross_fwd(cq, ckv, name):
    S, D = cq.shape
    Nm = ckv.shape[0]
    dh = D // CROSS_HEADS
    tq = _tile(S, 512, 8)
    scale = dh ** -0.5

    def body(q_ref, kv_ref, o_ref):
        for h in range(CROSS_HEADS):
            cols = slice(h * dh, (h + 1) * dh)
            p = _cross_scores(q_ref[:, cols], kv_ref[:, cols], scale)
            o_ref[:, cols] = jnp.dot(p.astype(BF16), kv_ref[:, D + h * dh:D + (h + 1) * dh],
                                     preferred_element_type=F32).astype(BF16)

    return pl.pallas_call(
        body, name=name, grid=(S // tq,),
        in_specs=[pl.BlockSpec((tq, D), lambda i: (i, 0)), pl.BlockSpec((Nm, 2 * D), lambda i: (0, 0))],
        out_specs=pl.BlockSpec((tq, D), lambda i: (i, 0)),
        out_shape=jax.ShapeDtypeStruct((S, D), BF16), compiler_params=_cp(("parallel",)))(cq, ckv)


def _cross_bwd(cq, ckv, do, name):
    S, D = cq.shape
    Nm = ckv.shape[0]
    dh = D // CROSS_HEADS
    tq = _tile(S, 512, 8)
    scale = dh ** -0.5

    def body(q_ref, kv_ref, do_ref, dq_ref, dkv_ref):
        first = pl.program_id(0) == 0

        @pl.when(first)
        def _():
            dkv_ref[...] = jnp.zeros_like(dkv_ref)

        for h in range(CROSS_HEADS):
            cols = slice(h * dh, (h + 1) * dh)
            vcols = slice(D + h * dh, D + (h + 1) * dh)
            q, k, v = q_ref[:, cols], kv_ref[:, cols], kv_ref[:, vcols]
            doh = do_ref[:, cols].astype(BF16)
            p = _cross_scores(q, k, scale)
            dp = lax.dot_general(doh, v, (((1,), (1,)), ((), ())), preferred_element_type=F32)
            ds = (p * (dp - jnp.sum(p * dp, axis=-1, keepdims=True)) * scale).astype(BF16)
            dq_ref[:, cols] = jnp.dot(ds, k, preferred_element_type=F32).astype(BF16)
            dkv_ref[:, cols] += lax.dot_general(ds, q, (((0,), (0,)), ((), ())), preferred_element_type=F32)
            dkv_ref[:, vcols] += lax.dot_general(p.astype(BF16), doh, (((0,), (0,)), ((), ())),
                                                 preferred_element_type=F32)

    return pl.pallas_call(
        body, name=name, grid=(S // tq,),
        in_specs=[pl.BlockSpec((tq, D), lambda i: (i, 0)), pl.BlockSpec((Nm, 2 * D), lambda i: (0, 0)),
                  pl.BlockSpec((tq, D), lambda i: (i, 0))],
        out_specs=[pl.BlockSpec((tq, D), lambda i: (i, 0)), pl.BlockSpec((Nm, 2 * D), lambda i: (0, 0))],
        out_shape=[jax.ShapeDtypeStruct((S, D), BF16), jax.ShapeDtypeStruct((Nm, 2 * D), F32)],
        compiler_params=_cp(("arbitrary",)))(cq, ckv, do)


def _rel_bucket(dist):
    max_exact = REL_BUCKETS // 2
    d_f = jnp.maximum(dist, 1).astype(F32)
    large = max_exact + (jnp.log(d_f / max_exact) / math.log(REL_MAX_DIST / max_exact)
                         * (REL_BUCKETS - max_exact)).astype(jnp.int32)
    large = jnp.minimum(large, REL_BUCKETS - 1)
    return jnp.where(dist < max_exact, dist, large)


def _branch_tables(dilation):
    qi = jnp.arange(Q_BLOCK)[:, None]
    kj = jnp.arange(2 * Q_BLOCK)[None, :]
    m = qi - kj + Q_BLOCK
    return _rel_bucket(jnp.maximum(m, 0) * dilation), (m >= 0) & (m <= Q_BLOCK)


def _attn_logits(q, kcat, bias, first_block, scale):
    s = lax.dot_general(q, kcat, (((1,), (1,)), ((), ())), preferred_element_type=F32) * scale + bias
    col = lax.broadcasted_iota(jnp.int32, s.shape, 1)
    return jnp.where(jnp.logical_and(first_block, col < Q_BLOCK), NEG_INF, s)


def _attn_branch_fwd(proj, bias, dilation, H, name):
    S, NC = proj.shape
    AW = H * HEAD
    r = dilation
    hb = H if r == 1 else 1
    G = Q_BLOCK * r
    ng, nh = S // G, H // hb
    scale = HEAD ** -0.5

    def body(q_ref, kp_ref, kc_ref, vp_ref, vc_ref, b_ref, o_ref, l_ref):
        first_block = pl.program_id(0) == 0
        h0 = pl.program_id(1) * hb
        pairs = [(pl.ds(rho, Q_BLOCK, stride=r) if r > 1 else slice(None), hh) for rho in range(r) for hh in range(hb)]
        for g0 in range(0, len(pairs), ATTN_PAIRS_PER_STAGE):
            group = pairs[g0:g0 + ATTN_PAIRS_PER_STAGE]
            sl = [(rows, slice(hh * HEAD, (hh + 1) * HEAD)) for rows, hh in group]
            q = [q_ref[rw, cl].astype(BF16) for rw, cl in sl]
            kcat = [jnp.concatenate([kp_ref[rw, cl], kc_ref[rw, cl]], axis=0).astype(BF16) for rw, cl in sl]
            vcat = [jnp.concatenate([vp_ref[rw, cl], vc_ref[rw, cl]], axis=0).astype(BF16) for rw, cl in sl]
            s = [_attn_logits(q[i], kcat[i], b_ref[h0 + hh], first_block, scale) for i, (_, hh) in enumerate(group)]
            m = [jnp.max(si, axis=-1, keepdims=True) for si in s]
            e = [jnp.exp(si - mi) for si, mi in zip(s, m)]
            den = [jnp.sum(ei, axis=-1, keepdims=True) for ei in e]
            o = [jnp.dot(ei.astype(BF16), vi, preferred_element_type=F32) for ei, vi in zip(e, vcat)]
            for i, (rw, cl) in enumerate(sl):
                o_ref[rw, cl] = o[i] / den[i]
                l_ref[rw, cl] = jnp.broadcast_to(m[i] + jnp.log(den[i]), (Q_BLOCK, HEAD))

    def blk(part, prev):
        if prev:
            return pl.BlockSpec((G, hb * HEAD), lambda n, h: (jnp.maximum(n - 1, 0), part * nh + h))
        return pl.BlockSpec((G, hb * HEAD), lambda n, h: (n, part * nh + h))

    out = pl.BlockSpec((G, hb * HEAD), lambda n, h: (n, h))
    return pl.pallas_call(
        body, name=name, grid=(ng, nh),
        in_specs=[blk(0, False), blk(1, True), blk(1, False), blk(2, True), blk(2, False),
                  pl.BlockSpec((H, Q_BLOCK, 2 * Q_BLOCK), lambda n, h: (0, 0, 0))],
        out_specs=[out, out],
        out_shape=[jax.ShapeDtypeStruct((S, AW), F32)] * 2,
        compiler_params=_cp(("parallel", "parallel")))(proj, proj, proj, proj, proj, bias)


def _attn_merge(outs, lses, width_out, name):
    S, AW = outs[0].shape
    tr = _tile(S, 256, 8)

    def body(o0, o1, o2, l0, l1, l2, cat_ref, om_ref, lt_ref):
        a, b, c = l0[...], l1[...], l2[...]
        m = jnp.maximum(jnp.maximum(a, b), c)
        ea, eb, ec = jnp.exp(a - m), jnp.exp(b - m), jnp.exp(c - m)
        tot = ea + eb + ec
        o = (ea * o0[...] + eb * o1[...] + ec * o2[...]) / tot
        om_ref[...] = o
        cat_ref[...] = o.astype(BF16)
        lt_ref[...] = m + jnp.log(tot)

    row = pl.BlockSpec((tr, AW), lambda i: (i, 0))
    return pl.pallas_call(
        body, name=name, grid=(S // tr,), in_specs=[row] * 6, out_specs=[row, row, row],
        out_shape=[jax.ShapeDtypeStruct((S, width_out), BF16), jax.ShapeDtypeStruct((S, AW), F32),
                   jax.ShapeDtypeStruct((S, AW), F32)],
        compiler_params=_cp(("parallel",)))(*outs, *lses)


def _attn_delta(dcat, o_attn, H, name, after=None):
    S, AW = o_attn.shape
    tr = _tile(S, 256, 8)
    extra = [] if after is None else [after]

    def body(d_ref, o_ref, *rest):
        out_ref = rest[len(extra)]
        for h in range(H):
            cols = slice(h * HEAD, (h + 1) * HEAD)
            out_ref[:, cols] = jnp.broadcast_to(
                jnp.sum(d_ref[:, cols] * o_ref[:, cols], axis=-1, keepdims=True), (tr, HEAD))

    row = pl.BlockSpec((tr, AW), lambda i: (i, 0))
    return pl.pallas_call(body, name=name, grid=(S // tr,),
                          in_specs=[row, row] + [pl.BlockSpec(memory_space=pl.ANY)] * len(extra), out_specs=row,
                          out_shape=jax.ShapeDtypeStruct((S, AW), F32),
                          compiler_params=_cp(("parallel",)))(dcat, o_attn, *extra)


def _attn_branch_bwd(proj, dcat, lse_tot, delta, bias, dilation, H, name):
    S, NC = proj.shape
    AW = H * HEAD
    r = dilation
    hb = H if r == 1 else 1
    G = Q_BLOCK * r
    ng, nh = S // G, H // hb
    scale = HEAD ** -0.5

    def body(q_ref, kp_ref, kc_ref, vp_ref, vc_ref, do_ref, l_ref, t_ref, b_ref,
             dq_ref, dk_ref, dv_ref, db_ref, ck_ref, cv_ref):
        hblk, n = pl.program_id(0), pl.program_id(1)
        h0 = hblk * hb

        @pl.when(jnp.logical_and(hblk == 0, n == 0))
        def _():
            db_ref[...] = jnp.zeros_like(db_ref)

        @pl.when(n == 0)
        def _():
            ck_ref[...] = jnp.zeros_like(ck_ref)
            cv_ref[...] = jnp.zeros_like(cv_ref)

        @pl.when(n < ng)
        def _():
            pairs = [(pl.ds(rho, Q_BLOCK, stride=r) if r > 1 else slice(None), hh)
                     for rho in range(r) for hh in range(hb)]
            for g0 in range(0, len(pairs), ATTN_PAIRS_PER_STAGE):
                group = pairs[g0:g0 + ATTN_PAIRS_PER_STAGE]
                idx = range(len(group))
                sl = [(rows, slice(hh * HEAD, (hh + 1) * HEAD)) for rows, hh in group]
                q = [q_ref[rw, cl].astype(BF16) for rw, cl in sl]
                kcat = [jnp.concatenate([kp_ref[rw, cl], kc_ref[rw, cl]], axis=0).astype(BF16) for rw, cl in sl]
                vcat = [jnp.concatenate([vp_ref[rw, cl], vc_ref[rw, cl]], axis=0).astype(BF16) for rw, cl in sl]
                doh = [do_ref[rw, cl].astype(BF16) for rw, cl in sl]
                s = [_attn_logits(q[i], kcat[i], b_ref[h0 + group[i][1]], n == 0, scale) for i in idx]
                dp = [lax.dot_general(doh[i], vcat[i], (((1,), (1,)), ((), ())), preferred_element_type=F32) for i in idx]
                p, ds = [], []
                for i, (rw, cl) in enumerate(sl):
                    lt, dl = l_ref[rw, cl], t_ref[rw, cl]
                    p.append(jnp.exp(s[i] - jnp.concatenate([lt, lt], axis=1)))
                    ds.append(p[i] * (dp[i] - jnp.concatenate([dl, dl], axis=1)))
                dsb = [(d * scale).astype(BF16) for d in ds]
                dq = [jnp.dot(dsb[i], kcat[i], preferred_element_type=F32) for i in idx]
                dkc = [lax.dot_general(dsb[i], q[i], (((0,), (0,)), ((), ())), preferred_element_type=F32) for i in idx]
                dvc = [lax.dot_general(p[i].astype(BF16), doh[i], (((0,), (0,)), ((), ())), preferred_element_type=F32)
                       for i in idx]
                for i, (rw, cl) in enumerate(sl):
                    dq_ref[rw, cl] = dq[i]
                    dk_ref[rw, cl] = ck_ref[rw, cl] + dkc[i][:Q_BLOCK]
                    dv_ref[rw, cl] = cv_ref[rw, cl] + dvc[i][:Q_BLOCK]
                    ck_ref[rw, cl] = dkc[i][Q_BLOCK:]
                    cv_ref[rw, cl] = dvc[i][Q_BLOCK:]
                for i in idx:
                    db_ref[h0 + group[i][1]] += ds[i]

        @pl.when(n == ng)
        def _():
            dk_ref[...] = ck_ref[...]
            dv_ref[...] = cv_ref[...]

    last = ng - 1

    def cur(part):
        return pl.BlockSpec((G, hb * HEAD), lambda h, n: (jnp.minimum(n, last), part * nh + h))

    def prev(part):
        return pl.BlockSpec((G, hb * HEAD), lambda h, n: (jnp.clip(n - 1, 0, last), part * nh + h))

    table = pl.BlockSpec((H, Q_BLOCK, 2 * Q_BLOCK), lambda h, n: (0, 0, 0))
    return pl.pallas_call(
        body, name=name, grid=(nh, ng + 1),
        in_specs=[cur(0), prev(1), cur(1), prev(2), cur(2), cur(0), cur(0), cur(0), table],
        out_specs=[cur(0), prev(0), prev(0), table],
        out_shape=[jax.ShapeDtypeStruct((S, AW), F32)] * 3 + [jax.ShapeDtypeStruct((H, Q_BLOCK, 2 * Q_BLOCK), F32)],
        scratch_shapes=[pltpu.VMEM((G, hb * HEAD), F32), pltpu.VMEM((G, hb * HEAD), F32)],
        compiler_params=_cp(("arbitrary", "arbitrary")))(proj, proj, proj, proj, proj, dcat, lse_tot, delta, bias)


def _bias_grad(dbs, onehots, H):
    def body(d0, d1, d2, e0, e1, e2, o_ref):
        acc = jnp.zeros((H, REL_BUCKETS), F32)
        for d, e in ((d0, e0), (d1, e1), (d2, e2)):
            acc = acc + lax.dot_general(d[...], e[...], (((1,), (1,)), ((), ())), preferred_element_type=F32,
                                        precision=lax.Precision.HIGHEST)
        o_ref[...] = acc

    flat = [d.reshape(H, 2 * Q_BLOCK * Q_BLOCK) for d in dbs]
    return pl.pallas_call(body, name="bias_grad", out_shape=jax.ShapeDtypeStruct((H, REL_BUCKETS), F32),
                          compiler_params=pltpu.CompilerParams(vmem_limit_bytes=VMEM_LIMIT))(*flat, *onehots)


def _assemble_dproj(dqkv, dhg, name):
    S, AW = dqkv[0][0].shape
    tr = _tile(S, 256, 8)
    ins = [dqkv[g][c] for g in range(3) for c in range(3)] + list(dhg)

    def body(*refs):
        o_ref = refs[-1]
        j = pl.program_id(1)
        for c in range(3):
            @pl.when(j == c)
            def _(c=c):
                o_ref[...] = (refs[c][...] + refs[3 + c][...] + refs[6 + c][...]).astype(BF16)

        for k in range(4):
            @pl.when(j == 3 + k)
            def _(k=k):
                o_ref[...] = refs[9 + k][...]

    row = pl.BlockSpec((tr, AW), lambda i, j: (i, 0))
    return pl.pallas_call(
        body, name=name, grid=(S // tr, 7), in_specs=[row] * 13,
        out_specs=pl.BlockSpec((tr, AW), lambda i, j: (i, j)),
        out_shape=jax.ShapeDtypeStruct((S, 7 * AW), BF16),
        compiler_params=_cp(("parallel", "arbitrary")))(*ins)


def _tri(n, upper):
    r = lax.broadcasted_iota(jnp.int32, (n, n), 0)
    c = lax.broadcasted_iota(jnp.int32, (n, n), 1)
    return jnp.where(c >= r if upper else c <= r, 1.0, 0.0).astype(F32)


def _hg_gates(fz, qz, lb):
    sig = jax.nn.sigmoid(fz)
    f = lb + (1.0 - lb) * sig
    sq = jax.nn.sigmoid(qz)
    return sig, f, jnp.log(f), 1.0 - f, qz * sq, sq


def _hg_fwd(proj, cat, lb, gain, HH, name):
    S, NC = proj.shape
    W = HH * HEAD
    C = HG_CHUNK
    hb = min(HG_HEADS_PER_STEP, HH)
    nh = HH // hb
    tb = _tile(S, 2048 // hb, C)
    ncb = tb // C
    base = 3 * nh
    cat_cols = cat.shape[1] // (hb * HEAD)

    def body(fz_ref, iv_ref, qz_ref, gz_ref, lb_ref, g_ref, cat_in, cat_ref, o_ref, st_ref, state):
        del cat_in

        @pl.when(pl.program_id(1) == 0)
        def _():
            state[...] = jnp.zeros_like(state)

        tri = _tri(C, False)
        row8 = lax.broadcasted_iota(jnp.int32, (8, 1), 0)

        def chunk(c, carry):
            rows = pl.ds(pl.multiple_of(c * C, C), C)
            heads = range(hb)
            cols = [slice(hh * HEAD, (hh + 1) * HEAD) for hh in heads]
            vv = [iv_ref[rows, cl] for cl in cols]
            gates = [_hg_gates(fz_ref[rows, cl], qz_ref[rows, cl], lb_ref[:, cl]) for cl in cols]
            kk, qq = [gt[3] for gt in gates], [gt[4] for gt in gates]
            b = [jnp.dot(tri, gt[2], preferred_element_type=F32, precision=lax.Precision.HIGHEST) for gt in gates]
            bl = [bh[C - 1:C, :] for bh in b]
            st = [state[hh] for hh in heads]
            stb = [s_.astype(BF16) for s_ in st]
            for hh in heads:
                st_ref[hh, c] = stb[hh]
            o = [lax.dot_general((qq[hh] * jnp.exp(b[hh])).astype(BF16), stb[hh], (((1,), (1,)), ((), ())),
                                 preferred_element_type=F32) for hh in heads]
            upd = [lax.dot_general(vv[hh].astype(BF16), (kk[hh] * jnp.exp(bl[hh] - b[hh])).astype(BF16),
                                   (((0,), (0,)), ((), ())), preferred_element_type=F32) for hh in heads]
            for hh in heads:
                state[hh] = st[hh] * jnp.exp(bl[hh]) + upd[hh]
            wts = {}
            for t0 in range(0, C, 8):
                for s in range(min(C, t0 + 8)):
                    for hh in heads:
                        diff = b[hh][t0:t0 + 8, :] - b[hh][s:s + 1, :]
                        if s >= t0:
                            diff = jnp.minimum(diff, 0.0)
                        a = jnp.sum(qq[hh][t0:t0 + 8, :] * jnp.exp(diff) * kk[hh][s:s + 1, :], axis=-1, keepdims=True)
                        wts[t0, s, hh] = jnp.where(row8 >= s - t0, a, 0.0) if s >= t0 else a
            for hh in heads:
                tiles = []
                for t0 in range(0, C, 8):
                    acc = o[hh][t0:t0 + 8, :]
                    for s in range(min(C, t0 + 8)):
                        acc = acc + wts[t0, s, hh] * vv[hh][s:s + 1, :]
                    tiles.append(acc)
                oh = jnp.concatenate(tiles, axis=0)
                o_ref[rows, cols[hh]] = oh
                n = oh * lax.rsqrt(jnp.mean(oh * oh, axis=-1, keepdims=True) + RMS_EPS)
                gz = gz_ref[rows, cols[hh]]
                cat_ref[rows, cols[hh]] = (n * g_ref[:, cols[hh]] * (gz * jax.nn.sigmoid(gz))).astype(BF16)
            return carry

        lax.fori_loop(0, ncb, chunk, 0)

    def col(k):
        return pl.BlockSpec((tb, hb * HEAD), lambda h, i: (i, base + k * nh + h))

    vec = pl.BlockSpec((1, hb * HEAD), lambda h, i: (0, h))
    cat_out, o_raw, states = pl.pallas_call(
        body, name=name, grid=(nh, S // tb),
        in_specs=[col(0), col(1), col(2), col(3), vec, vec, pl.BlockSpec(memory_space=pl.ANY)],
        out_specs=[pl.BlockSpec((tb, hb * HEAD), lambda h, i: (i, cat_cols - nh + h)),
                   pl.BlockSpec((tb, hb * HEAD), lambda h, i: (i, h)),
                   pl.BlockSpec((hb, ncb, HEAD, HEAD), lambda h, i: (h, i, 0, 0))],
        out_shape=[jax.ShapeDtypeStruct(cat.shape, BF16), jax.ShapeDtypeStruct((S, W), F32),
                   jax.ShapeDtypeStruct((HH, S // C, HEAD, HEAD), BF16)],
        scratch_shapes=[pltpu.VMEM((hb, HEAD, HEAD), F32)],
        input_output_aliases={6: 0},
        compiler_params=_cp(("parallel", "arbitrary")))(proj, proj, proj, proj, lb.reshape(1, W), gain.reshape(1, W), cat)
    return cat_out, o_raw, states


def _hg_bwd(proj, dcat, o_raw, states, lb, gain, after, HH, name):
    S, NC = proj.shape
    W = HH * HEAD
    C = HG_CHUNK
    hb = min(HG_HEADS_PER_STEP, HH)
    nh = HH // hb
    tb = _tile(S, 2048 // hb, C)
    ncb = tb // C
    nt = S // tb
    base = 3 * nh
    dcat_cols = dcat.shape[1] // (hb * HEAD)

    def body(fz_ref, iv_ref, qz_ref, gz_ref, dc_ref, o_ref, st_ref, lb_ref, g_ref, after_ref,
             dfz_ref, div_ref, dqz_ref, dgz_ref, dlb_ref, dg_ref, dstate):
        del after_ref

        @pl.when(pl.program_id(1) == 0)
        def _():
            dstate[...] = jnp.zeros_like(dstate)
            dlb_ref[...] = jnp.zeros_like(dlb_ref)
            dg_ref[...] = jnp.zeros_like(dg_ref)

        tri_lo, tri_up = _tri(C, False), _tri(C, True)
        row = lax.broadcasted_iota(jnp.int32, (C, 1), 0)
        row8 = lax.broadcasted_iota(jnp.int32, (8, 1), 0)

        def chunk(cc, carry):
            c = ncb - 1 - cc
            rows = pl.ds(pl.multiple_of(c * C, C), C)
            heads = range(hb)
            starts = list(range(0, C, 8))
            hd = []
            for hh in heads:
                cols = slice(hh * HEAD, (hh + 1) * HEAD)
                lbv, gv = lb_ref[:, cols], g_ref[:, cols]
                fz, vv, qz, gz = fz_ref[rows, cols], iv_ref[rows, cols], qz_ref[rows, cols], gz_ref[rows, cols]
                sig, f, lf, kk, qq, sq = _hg_gates(fz, qz, lbv)
                b = jnp.dot(tri_lo, lf, preferred_element_type=F32, precision=lax.Precision.HIGHEST)
                bl = b[C - 1:C, :]
                eb, ebl = jnp.exp(b), jnp.exp(bl)
                ekb = jnp.exp(bl - b)
                qh, kh = qq * eb, kk * ekb
                o = o_ref[rows, cols]
                dhg = dc_ref[rows, cols]
                sgz = jax.nn.sigmoid(gz)
                rr = lax.rsqrt(jnp.mean(o * o, axis=-1, keepdims=True) + RMS_EPS)
                nrm = o * rr
                dpre = dhg * (gz * sgz)
                dgz_ref[rows, cols] = (dhg * nrm * gv * (sgz * (1.0 + gz * (1.0 - sgz)))).astype(BF16)
                dg_ref[:, cols] += jnp.sum(dpre * nrm, axis=0, keepdims=True)
                dn = dpre * gv
                do = rr * (dn - nrm * jnp.mean(dn * nrm, axis=-1, keepdims=True))
                dob = do.astype(BF16)
                st0 = st_ref[hh, c]
                dst1 = dstate[hh]
                dst1b = dst1.astype(BF16)
                dqh = jnp.dot(dob, st0, preferred_element_type=F32)
                dv = lax.dot_general(kh.astype(BF16), dst1b, (((1,), (1,)), ((), ())), preferred_element_type=F32)
                dkh = jnp.dot(vv.astype(BF16), dst1b, preferred_element_type=F32)
                dstate[hh] = dst1 * ebl + lax.dot_general(dob, qh.astype(BF16), (((0,), (0,)), ((), ())),
                                                          preferred_element_type=F32)
                extra = (jnp.sum(dkh * kh, axis=0, keepdims=True)
                         + ebl * jnp.sum(st0.astype(F32) * dst1, axis=0, keepdims=True))
                hd.append(dict(cols=cols, lbv=lbv, sig=sig, f=f, kk=kk, qq=qq, sq=sq, qz=qz, vv=vv, b=b, do=do, dv=dv,
                               extra=extra, dq=dqh * eb, dk=dkh * ekb))
            red = {}
            for s in range(C):
                for ti, t0 in enumerate(starts):
                    if t0 + 8 <= s:
                        continue
                    for hh in heads:
                        v = hd[hh]
                        diff = v["b"][t0:t0 + 8, :] - v["b"][s:s + 1, :]
                        if s >= t0:
                            diff = jnp.minimum(diff, 0.0)
                        dec = jnp.exp(diff)
                        da = jnp.sum(v["do"][t0:t0 + 8, :] * v["vv"][s:s + 1, :], axis=-1, keepdims=True)
                        a = jnp.sum(v["qq"][t0:t0 + 8, :] * dec * v["kk"][s:s + 1, :], axis=-1, keepdims=True)
                        if s >= t0:
                            keep = row8 >= s - t0
                            da, a = jnp.where(keep, da, 0.0), jnp.where(keep, a, 0.0)
                        red[s, ti, hh] = (da * dec, a)
            for hh in heads:
                v = hd[hh]
                cols, kk, qq, do = v["cols"], v["kk"], v["qq"], v["do"]
                dq_t = [v["dq"][t0:t0 + 8, :] for t0 in starts]
                dk_rows, dv_rows = [], []
                for s in range(C):
                    dk_s = dv_s = None
                    for ti, t0 in enumerate(starts):
                        if t0 + 8 <= s:
                            continue
                        gd, a = red[s, ti, hh]
                        dq_t[ti] = dq_t[ti] + gd * kk[s:s + 1, :]
                        pk = jnp.sum(gd * qq[t0:t0 + 8, :], axis=0, keepdims=True)
                        pv = jnp.sum(a * do[t0:t0 + 8, :], axis=0, keepdims=True)
                        dk_s = pk if dk_s is None else dk_s + pk
                        dv_s = pv if dv_s is None else dv_s + pv
                    dk_rows.append(dk_s)
                    dv_rows.append(dv_s)
                dq = jnp.concatenate(dq_t, axis=0)
                dk = v["dk"] + jnp.concatenate(dk_rows, axis=0)
                dv = v["dv"] + jnp.concatenate(dv_rows, axis=0)
                db = qq * dq - kk * dk + jnp.where(row == C - 1, v["extra"], 0.0)
                dlf = jnp.dot(tri_up, db, preferred_element_type=F32, precision=lax.Precision.HIGHEST)
                df = dlf / v["f"] - dk
                sig, sq, qz = v["sig"], v["sq"], v["qz"]
                dfz_ref[rows, cols] = (df * (1.0 - v["lbv"]) * sig * (1.0 - sig)).astype(BF16)
                dlb_ref[:, cols] += jnp.sum(df * (1.0 - sig), axis=0, keepdims=True)
                dqz_ref[rows, cols] = (dq * (sq * (1.0 + qz * (1.0 - sq)))).astype(BF16)
                div_ref[rows, cols] = dv.astype(BF16)
            return carry

        lax.fori_loop(0, ncb, chunk, 0)

    def col(k):
        return pl.BlockSpec((tb, hb * HEAD), lambda h, i: (nt - 1 - i, base + k * nh + h))

    ocol = pl.BlockSpec((tb, hb * HEAD), lambda h, i: (nt - 1 - i, h))
    vec = pl.BlockSpec((1, hb * HEAD), lambda h, i: (0, h))
    outs = pl.pallas_call(
        body, name=name, grid=(nh, nt),
        in_specs=[col(0), col(1), col(2), col(3),
                  pl.BlockSpec((tb, hb * HEAD), lambda h, i: (nt - 1 - i, dcat_cols - nh + h)),
                  pl.BlockSpec((tb, hb * HEAD), lambda h, i: (nt - 1 - i, h)),
                  pl.BlockSpec((hb, ncb, HEAD, HEAD), lambda h, i: (h, nt - 1 - i, 0, 0)), vec, vec,
                  pl.BlockSpec(memory_space=pl.ANY)],
        out_specs=[ocol, ocol, ocol, ocol, vec, vec],
        out_shape=[jax.ShapeDtypeStruct((S, W), BF16)] * 4 + [jax.ShapeDtypeStruct((1, W), F32)] * 2,
        scratch_shapes=[pltpu.VMEM((hb, HEAD, HEAD), F32)],
        compiler_params=_cp(("parallel", "arbitrary")))(proj, proj, proj, proj, dcat, o_raw, states,
                                                        lb.reshape(1, W), gain.reshape(1, W), after)
    return outs


def _lb_all(lb_logits):
    p = jax.nn.softmax(lb_logits.astype(F32), axis=0)
    return jnp.cumsum(p, axis=0) - p


def _local_step(x, mem, target, rel_bias, lb_logits, gains, hg_norm, comm):
    S, D = x.shape
    L = gains.shape[0]
    H = (D // 2) // HEAD
    lb = _lb_all(lb_logits)
    tables = [_branch_tables(r) for _, r in BRANCHES]
    onehots = [jnp.where(ok[None], jax.nn.one_hot(bk, REL_BUCKETS, dtype=F32, axis=0), 0.0)
               .reshape(REL_BUCKETS, 2 * Q_BLOCK * Q_BLOCK) for bk, ok in tables]
    bias = [jnp.where(ok[None], jnp.dot(rel_bias.astype(F32).T, oh, precision=lax.Precision.HIGHEST)
                      .reshape(H, Q_BLOCK, 2 * Q_BLOCK), NEG_INF) for oh, (_, ok) in zip(onehots, tables)]

    saved = []
    _, h = _norm_fwd(x, None, None, gains[0, 0], "norm_first")
    xl = x
    for l in range(L):
        g = gains[l]
        w_in, token = comm.weights(l, xl)
        proj = _mm_nn(h, w_in, F32, "mm_in", after=token)
        outs, lses = [], []
        for gi, (_, r) in enumerate(BRANCHES):
            o, ls = _attn_branch_fwd(proj, bias[gi], r, H, f"attn_fwd_d{r}")
            outs.append(o)
            lses.append(ls)
        cat, o_attn, lse_tot = _attn_merge(outs, lses, D, "attn_merge")
        cat, o_raw, states = _hg_fwd(proj, cat, lb[l], hg_norm[l], H, "hgrn_fwd")
        w = dict(comm.rest(l, cat), **{"in": w_in})
        mix = _mm_nn(cat, w["out"], F32, "mm_out")
        x1, hc = _norm_fwd(xl, mix, g[1], g[2], "norm_post_pre")
        _, mn = _norm_fwd(mem, None, None, g[3], "norm_mem")
        cq = _mm_nn(hc, w["cq"], BF16, "mm_cq")
        ckv = _mm_nn(mn, w["ckv"], BF16, "mm_ckv")
        co = _cross_fwd(cq, ckv, "cross_fwd")
        cop = _mm_nn(co, w["co"], F32, "mm_co")
        x2, hf = _norm_fwd(x1, cop, g[4], g[5], "norm_post_pre")
        gu, act = _mm_gate_up(hf, w["gu"], "mm_gu")
        y = _mm_nn(act, w["down"], F32, "mm_down")
        g_next = gains[l + 1, 0] if l + 1 < L else None
        x3, h_next = _norm_fwd(x2, y, g[6], g_next, "norm_post_pre" if l + 1 < L else "norm_post")
        saved.append(dict(x0=xl, h0=h, proj=proj, cat=cat, o_attn=o_attn, lse_tot=lse_tot, o_raw=o_raw, states=states,
                          mix=mix, x1=x1, hc=hc, mn=mn, cq=cq, ckv=ckv, co=co, cop=cop, x2=x2, hf=hf, gu=gu, act=act,
                          y=y, x3=x3, w=w))
        comm.layer_done(l, x3)
        xl, h = x3, h_next

    loss, dres = _loss_and_grad(xl, target)

    dh_next = bwd_token = None
    d_gains = [[None] * 7 for _ in range(L)]
    d_lb = [None] * L
    d_hg = [None] * L
    d_bias = [jnp.zeros((H, Q_BLOCK, 2 * Q_BLOCK), F32) for _ in BRANCHES]
    for l in reversed(range(L)):
        g, sv = gains[l], saved[l]
        w = sv["w"]
        g_next = gains[l + 1, 0] if l + 1 < L else None
        t3, dy, dg_next, dg6 = _norm_bwd(dres, dh_next, sv["x3"], sv["y"], g_next, g[6], "norm_bwd")
        if l + 1 < L:
            d_gains[l + 1][0] = dg_next
        d_gains[l][6] = dg6
        gw = {}
        gw["down"] = _mm_tn(sv["act"], dy, 1, BF16, "mm_dw_down")
        dgu = _mm_dx_down_swiglu(dy, w["down"], sv["gu"], "mm_dx_down", after=bwd_token)
        gw["gu"] = _mm_tn(sv["hf"], dgu, w["gu"].shape[0], BF16, "mm_dw_gu")
        dhf = _mm_nt(dgu, w["gu"], F32, "mm_dx_gu")
        t2, dcop, d_gains[l][5], d_gains[l][4] = _norm_bwd(t3, dhf, sv["x2"], sv["cop"], g[5], g[4], "norm_bwd")
        gw["co"] = _mm_tn(sv["co"], dcop, 1, BF16, "mm_dw_sq")
        dco = _mm_nt(dcop, w["co"], F32, "mm_dx_sq")
        dcq, dckv = _cross_bwd(sv["cq"], sv["ckv"], dco, "cross_bwd")
        gw["cq"] = _mm_tn(sv["hc"], dcq, 1, BF16, "mm_dw_sq")
        dhc = _mm_nt(dcq, w["cq"], F32, "mm_dx_sq")
        gw["ckv"] = _mm_tn(sv["mn"], dckv, w["ckv"].shape[0], BF16, "mm_dw_ckv")
        dmn = _mm_nt(dckv, w["ckv"], F32, "mm_dx_ckv")
        _, _, d_gains[l][3], _ = _norm_bwd(None, dmn, mem, None, g[3], None, "norm_bwd_mem")
        t1, dmix, d_gains[l][2], d_gains[l][1] = _norm_bwd(t2, dhc, sv["x1"], sv["mix"], g[2], g[1], "norm_bwd")
        gw["out"] = _mm_tn(sv["cat"], dmix, 1, BF16, "mm_dw_sq")
        dcat = _mm_nt(dmix, w["out"], F32, "mm_dx_sq")
        delta = _attn_delta(dcat, sv["o_attn"], H, "attn_delta", after=comm.early_grads(l, gw))
        dqkv = []
        for gi, (_, r) in enumerate(BRANCHES):
            dq, dk, dv, db = _attn_branch_bwd(sv["proj"], dcat, sv["lse_tot"], delta, bias[gi], r, H, f"attn_bwd_d{r}")
            dqkv.append((dq, dk, dv))
            d_bias[gi] = d_bias[gi] + db
        dfz, div, dqz, dgz, dlb_l, dhg_l = _hg_bwd(sv["proj"], dcat, sv["o_raw"], sv["states"], lb[l], hg_norm[l],
                                                   delta, H, "hgrn_bwd")
        d_lb[l], d_hg[l] = dlb_l[0], dhg_l[0]
        dproj = _assemble_dproj(dqkv, [dfz, div, dqz, dgz], "assemble_dproj")
        gw["in"] = _mm_tn(sv["h0"], dproj, w["in"].shape[0], BF16, "mm_dw_in")
        dh_next = _mm_nt(dproj, w["in"], F32, "mm_dx_in")
        dres = t1
        bwd_token = comm.grads(l, gw, gw["in"])
    grad_x, _, d_gains[0][0], _ = _norm_bwd(dres, dh_next, x, None, gains[0, 0], None, "norm_bwd_first")

    d_rel_bias = _bias_grad(d_bias, onehots, H).T
    d_gains = jnp.stack([jnp.stack(row) for row in d_gains])
    return loss, grad_x, d_rel_bias, jnp.stack(d_lb), jnp.stack(d_hg), d_gains


def _place():
    return lax.axis_index("x"), lax.axis_index("y"), lax.axis_index("c")


def _all_gather(shards, name):
    n = len(shards)
    HBM = pl.BlockSpec(memory_space=pltpu.HBM)

    def body(*refs):
        ins, outs = refs[:n], refs[n:2 * n]
        send_sems, recv_sems, local_sems = refs[2 * n:]
        x, y, c = _place()
        me, sibling = (x, y, c), (x, y, 1 - c)
        chips = [(1 - x, y), (x, 1 - y), (1 - x, 1 - y)]

        def copy(a, k, block, to, own=False):
            dst = outs[a].at[4 * block[0] + 2 * block[1] + block[2]]
            return pltpu.make_async_remote_copy(
                src_ref=ins[a] if own else dst, dst_ref=dst, send_sem=send_sems.at[7 * a + k],
                recv_sem=recv_sems.at[7 * a + k], device_id=to, device_id_type=MESH)

        mine = [pltpu.make_async_copy(ins[a], outs[a].at[4 * x + 2 * y + c], local_sems.at[a]) for a in range(n)]
        for cp in mine:
            cp.start()
        first = []
        for a in range(n):
            first.append(copy(a, 0, me, sibling, own=True))
            first += [copy(a, 1 + j, me, (*chip, c), own=True) for j, chip in enumerate(chips)]
        for cp in first:
            cp.start()
        passed = []
        for j, chip in enumerate(chips):
            for a in range(n):
                copy(a, 1 + j, (*chip, c), me).wait_recv()
                fwd = copy(a, 4 + j, (*chip, c), sibling)
                fwd.start()
                passed.append(fwd)
        for a in range(n):
            copy(a, 0, sibling, me).wait_recv()
            for j, chip in enumerate(chips):
                copy(a, 4 + j, (*chip, 1 - c), me).wait_recv()
        for cp in first + passed:
            cp.wait_send()
        for cp in mine:
            cp.wait()

    return pl.pallas_call(
        body, name=name, in_specs=[HBM] * n, out_specs=[HBM] * n,
        out_shape=[jax.ShapeDtypeStruct((N_DEV,) + s.shape, s.dtype) for s in shards],
        scratch_shapes=[pltpu.SemaphoreType.DMA((7 * n,)), pltpu.SemaphoreType.DMA((7 * n,)),
                        pltpu.SemaphoreType.DMA((n,))],
    )(*shards)


def _exchange_sibling(grads, name):
    n = len(grads)
    HBM = pl.BlockSpec(memory_space=pltpu.HBM)

    def body(*refs):
        ins, outs = refs[:n], refs[n:2 * n]
        send_sems, recv_sems = refs[2 * n:]
        x, y, c = _place()
        sibling = (x, y, 1 - c)
        for a in range(n):
            for q in range(4):
                pltpu.make_async_remote_copy(
                    src_ref=ins[a].at[2 * q + 1 - c], dst_ref=outs[a].at[q], send_sem=send_sems.at[a],
                    recv_sem=recv_sems.at[a], device_id=sibling, device_id_type=MESH).start()
        for a in range(n):
            pltpu.make_async_remote_copy(
                src_ref=ins[a].at[pl.ds(0, 4)], dst_ref=outs[a], send_sem=send_sems.at[a], recv_sem=recv_sems.at[a],
                device_id=sibling, device_id_type=MESH).wait()

    return pl.pallas_call(
        body, name=name, in_specs=[HBM] * n, out_specs=[HBM] * n,
        out_shape=[jax.ShapeDtypeStruct((4,) + g.shape[1:], g.dtype) for g in grads],
        scratch_shapes=[pltpu.SemaphoreType.DMA((n,)), pltpu.SemaphoreType.DMA((n,))],
    )(*grads)


def _exchange_chips(parts, name):
    n = len(parts)
    HBM = pl.BlockSpec(memory_space=pltpu.HBM)

    def body(*refs):
        ins, outs = refs[:n], refs[n:2 * n]
        send_sems, recv_sems = refs[2 * n:]
        x, y, c = _place()
        chips = [(1 - x, y), (x, 1 - y), (1 - x, 1 - y)]
        copies = []
        for a in range(n):
            for j, chip in enumerate(chips):
                cp = pltpu.make_async_remote_copy(
                    src_ref=ins[a].at[2 * chip[0] + chip[1]], dst_ref=outs[a].at[j], send_sem=send_sems.at[3 * a + j],
                    recv_sem=recv_sems.at[3 * a + j], device_id=(*chip, c), device_id_type=MESH)
                cp.start()
                copies.append(cp)
        for cp in copies:
            cp.wait()

    return pl.pallas_call(
        body, name=name, in_specs=[HBM] * n, out_specs=[HBM] * n,
        out_shape=[jax.ShapeDtypeStruct((3,) + p.shape[1:], p.dtype) for p in parts],
        scratch_shapes=[pltpu.SemaphoreType.DMA((3 * n,)), pltpu.SemaphoreType.DMA((3 * n,))],
    )(*parts)


def _others(x, y, c):
    out = []
    for k in range(1, N_DEV):
        px = 1 - x if (k >> 2) & 1 else x
        py = 1 - y if (k >> 1) & 1 else y
        pc = 1 - c if k & 1 else c
        out.append(((px, py, pc), 4 * px + 2 * py + pc))
    return out


_HBM_SPEC = pl.BlockSpec(memory_space=pltpu.HBM)
_SEM_SPEC = pl.BlockSpec(memory_space=pltpu.SEMAPHORE)
_EFFECT = pltpu.SideEffectType.DATAFLOW_SIDE_EFFECTING


def _copies_start(srcs, lands, after, src_block, name):
    n = len(srcs)

    def body(*refs):
        src_refs, land_refs = refs[:n], refs[n:2 * n]
        send_sems, recv_sems = refs[2 * n + 1], refs[2 * n + 2]
        token = refs[-1]
        x, y, c = _place()
        mine = 4 * x + 2 * y + c
        for a in range(n):
            for k, (peer, block) in enumerate(_others(x, y, c)):
                pltpu.make_async_remote_copy(
                    src_ref=src_refs[a].at[block] if src_block else src_refs[a],
                    dst_ref=land_refs[a].at[k] if src_block else land_refs[a].at[mine],
                    send_sem=send_sems.at[7 * a + k], recv_sem=recv_sems.at[7 * a + k],
                    device_id=peer, device_id_type=MESH).start()
        token[...] = jnp.zeros_like(token)

    outs = pl.pallas_call(
        body, name=name,
        out_shape=(pltpu.SemaphoreType.DMA((7 * n,)), pltpu.SemaphoreType.DMA((7 * n,)),
                   *[pltpu.HBM(s.shape, s.dtype) for s in srcs], *[pltpu.HBM(t.shape, t.dtype) for t in lands],
                   jax.ShapeDtypeStruct((8, 128), F32)),
        in_specs=[_HBM_SPEC] * (2 * n) + [pl.BlockSpec(memory_space=pl.ANY)],
        out_specs=(_SEM_SPEC, _SEM_SPEC, *[_HBM_SPEC] * (2 * n), pl.BlockSpec(memory_space=pltpu.VMEM)),
        input_output_aliases={i: 2 + i for i in range(2 * n)},
        compiler_params=pltpu.CompilerParams(has_side_effects=_EFFECT),
    )(*[pltpu.with_memory_space_constraint(s, pltpu.HBM) for s in srcs],
      *[pltpu.with_memory_space_constraint(t, pltpu.HBM) for t in lands], after)
    return outs[0], outs[1], list(outs[2:2 + n]), list(outs[2 + n:2 + 2 * n]), outs[-1]


def _copies_wait(send_sems, recv_sems, srcs, lands, after, src_block, name):
    n = len(srcs)

    def body(*refs):
        src_refs, land_refs = refs[:n], refs[n:2 * n]
        send_sems, recv_sems = refs[2 * n], refs[2 * n + 1]
        x, y, c = _place()
        for a in range(n):
            for k, (peer, block) in enumerate(_others(x, y, c)):
                cp = pltpu.make_async_remote_copy(
                    src_ref=src_refs[a].at[block] if src_block else src_refs[a],
                    dst_ref=land_refs[a].at[k] if src_block else land_refs[a].at[block],
                    send_sem=send_sems.at[7 * a + k], recv_sem=recv_sems.at[7 * a + k],
                    device_id=peer, device_id_type=MESH)
                cp.wait_send()
                cp.wait_recv()

    outs = pl.pallas_call(
        body, name=name,
        out_shape=(*[pltpu.HBM(s.shape, s.dtype) for s in srcs], *[pltpu.HBM(t.shape, t.dtype) for t in lands]),
        in_specs=[_HBM_SPEC] * (2 * n) + [_SEM_SPEC, _SEM_SPEC, pl.BlockSpec(memory_space=pl.ANY)],
        out_specs=tuple([_HBM_SPEC] * (2 * n)),
        input_output_aliases={i: i for i in range(2 * n)},
        compiler_params=pltpu.CompilerParams(has_side_effects=_EFFECT),
    )(*srcs, *lands, send_sems, recv_sems, after)
    return list(outs[:n]), list(outs[n:])


def _pair_sum(grad, recv, c_idx, name):
    _, R, C = grad.shape
    tr = _tile(R, 512, 16)

    def body(c_ref, g_ref, p_ref, o_ref):
        del c_ref
        o_ref[...] = (g_ref[...].astype(F32) + p_ref[...].astype(F32)).astype(BF16)

    spec = pltpu.PrefetchScalarGridSpec(
        num_scalar_prefetch=1, grid=(4, R // tr),
        in_specs=[pl.BlockSpec((None, tr, C), lambda q, i, c: (2 * q + c[0], i, 0)),
                  pl.BlockSpec((None, tr, C), lambda q, i, c: (q, i, 0))],
        out_specs=pl.BlockSpec((None, tr, C), lambda q, i, c: (q, i, 0)))
    return pl.pallas_call(body, name=name, grid_spec=spec, out_shape=jax.ShapeDtypeStruct((4, R, C), BF16),
                          compiler_params=_cp(("parallel", "parallel")))(c_idx, grad, recv)


def _adam(w, g, m, v):
    m2 = ADAM_B1 * m + (1.0 - ADAM_B1) * g
    v2 = ADAM_B2 * v + (1.0 - ADAM_B2) * (g * g)
    m_hat = m2 / (1.0 - ADAM_B1 ** ADAM_STEP)
    v_hat = v2 / (1.0 - ADAM_B2 ** ADAM_STEP)
    return -ADAM_LR * (m_hat / (jnp.sqrt(v_hat) + ADAM_EPS) + ADAM_WD * w), m2, v2


def _reduce_update(part, recv, own_idx, w, m, v, layer, prev, name):
    L, R, C = w.shape
    nr = recv.shape[0]
    tr = _tile(R, 128, 16)
    has_prev = prev is not None

    def body(idx_ref, q_ref, *rest):
        del idx_ref
        r_refs, (w_ref, m_ref, v_ref) = rest[:nr], rest[nr:nr + 3]
        g_ref, d_ref, m2_ref, v2_ref = rest[-4:]
        g = q_ref[...].astype(F32)
        for r_ref in r_refs:
            g = g + r_ref[...].astype(F32)
        d, m2, v2 = _adam(w_ref[...], g, m_ref[...], v_ref[...])
        g_ref[...] = g
        d_ref[...] = d
        m2_ref[...] = m2
        v2_ref[...] = v2

    def rblk(j):
        return pl.BlockSpec((None, tr, C), lambda i, k: (j, i, 0))

    lay = pl.BlockSpec((None, tr, C), lambda i, k: (layer, i, 0))
    in_specs = [pl.BlockSpec((None, tr, C), lambda i, k: (k[0], i, 0))] + [rblk(j) for j in range(nr)] + [lay, lay, lay]
    ins = [part] + [recv] * nr + [w, m, v]
    aliases = {}
    if has_prev:
        in_specs += [pl.BlockSpec(memory_space=pl.ANY)] * 4
        ins += list(prev)
        aliases = {1 + len(ins) - 4 + t: t for t in range(4)}
    spec = pltpu.PrefetchScalarGridSpec(num_scalar_prefetch=1, grid=(R // tr,), in_specs=in_specs,
                                        out_specs=[lay, lay, lay, lay])
    return pl.pallas_call(body, name=name, grid_spec=spec, out_shape=[jax.ShapeDtypeStruct((L, R, C), F32)] * 4,
                          input_output_aliases=aliases, compiler_params=_cp(("parallel",)))(own_idx, *ins)


def _sum_devices(gathered):
    n, R, C = gathered.shape

    def body(g_ref, o_ref):
        acc = g_ref[0]
        for d in range(1, n):
            acc = acc + g_ref[d]
        o_ref[...] = acc

    return pl.pallas_call(body, name="sum_devices", out_shape=jax.ShapeDtypeStruct((R, C), F32))(gathered)


def _adam_small(w, g, m, v, name):
    shape = w.shape
    two = (-1, shape[-1])

    def body(w_ref, g_ref, m_ref, v_ref, d_ref, m2_ref, v2_ref):
        d, m2, v2 = _adam(w_ref[...], g_ref[...], m_ref[...], v_ref[...])
        d_ref[...] = d
        m2_ref[...] = m2
        v2_ref[...] = v2

    outs = pl.pallas_call(body, name=name, out_shape=[jax.ShapeDtypeStruct(w.reshape(two).shape, F32)] * 3)(
        w.reshape(two), g.reshape(two), m.reshape(two), v.reshape(two))
    return [o.reshape(shape) for o in outs]


_SHARDED = ("in", "out", "cq", "ckv", "co", "gu", "down")
_COLUMN_SHARDED = ("in", "ckv", "gu")


def kernel(x, mem, rel_bias, lb_logits, norm_gains, w_in, hg_norm, w_out, w_cq, w_ckv, w_co, w_gate_up, w_down, loss_target, m_rel_bias, m_lb_logits, m_norm_gains, m_w_in, m_hg_norm, m_w_out, m_w_cq, m_w_ckv, m_w_co, m_w_gate_up, m_w_down, v_rel_bias, v_lb_logits, v_norm_gains, v_w_in, v_hg_norm, v_w_out, v_w_cq, v_w_ckv, v_w_co, v_w_gate_up, v_w_down):
    L = w_in.shape[0]
    D = x.shape[-1]
    ws = dict(zip(_SHARDED, (w_in, w_out, w_cq, w_ckv, w_co, w_gate_up, w_down)))
    ms = dict(zip(_SHARDED, (m_w_in, m_w_out, m_w_cq, m_w_ckv, m_w_co, m_w_gate_up, m_w_down)))
    vs = dict(zip(_SHARDED, (v_w_in, v_w_out, v_w_cq, v_w_ckv, v_w_co, v_w_gate_up, v_w_down)))
    px, py, pc = _place()
    dev = 4 * px + 2 * py + pc
    c_idx = jnp.reshape(pc, (1,)).astype(jnp.int32)
    chip_idx = jnp.reshape(2 * px + py, (1,)).astype(jnp.int32)
    dev_idx = jnp.reshape(dev, (1,)).astype(jnp.int32)

    rest_keys = _SHARDED[1:]

    def shards_of(l, keys):
        return [ws[k][l].astype(BF16) for k in keys]

    def as_weights(keys, gathered):
        return {k: g if k in _COLUMN_SHARDED else g.reshape(1, N_DEV * g.shape[1], g.shape[2])
                for k, g in zip(keys, gathered)}

    def start_gather(shards, after, name):
        lands = [lax.empty((N_DEV,) + s.shape, s.dtype) for s in shards]
        return _copies_start(shards, lands, after, False, name)

    def wait_gather(pending, keys, after, name):
        send_sems, recv_sems, shards, lands = pending
        shards, lands = _copies_wait(send_sems, recv_sems, shards, lands, after, False, name)
        full = [lax.dynamic_update_slice_in_dim(t, s[None], dev, axis=0) for s, t in zip(shards, lands)]
        return as_weights(keys, full)

    def full_grads(gw, keys):
        return [gw[k].reshape((N_DEV,) + ws[k].shape[1:]) for k in keys]

    def start_scatter(full, after, name):
        lands = [lax.empty((N_DEV - 1,) + g.shape[1:], g.dtype) for g in full]
        return _copies_start(full, lands, after, True, name)

    class Comm:
        def __init__(self):
            first, gains = _all_gather(shards_of(0, _SHARDED[:1]) + [norm_gains], "gather_first")
            self.gains = jnp.transpose(gains, (1, 2, 0, 3)).reshape(L, 7, D)
            self.ready = {0: as_weights(_SHARDED[:1], [first])}
            *self.rest0, self.order = start_gather(shards_of(0, rest_keys), first, "gather_start_rest")
            self.gather = None
            self.scatter = None
            self.early = None
            self.results = {k: None for k in _SHARDED}

        def weights(self, l, x_in):
            del x_in
            token = None
            if l + 1 < L:
                after = self.order if l == 0 else self.ready[l]["in"]
                *self.gather, token = start_gather(shards_of(l + 1, _SHARDED), after, "gather_start")
            return self.ready[l]["in"], token

        def rest(self, l, after):
            if l == 0:
                self.ready[0].update(wait_gather(self.rest0, rest_keys, after, "gather_wait_rest"))
            mine = self.ready.pop(l)
            return {k: mine[k] for k in rest_keys}

        def layer_done(self, l, x_out):
            if l + 1 < L:
                self.ready[l + 1] = wait_gather(self.gather, _SHARDED, x_out, "gather_wait")

        def update(self, l, keys, parts, recvs, own_idx):
            for k, part, recv in zip(keys, parts, recvs):
                self.results[k] = _reduce_update(part, recv, own_idx, ws[k], ms[k], vs[k], l, self.results[k],
                                                 "reduce_update")

        def finish_scatter(self, after):
            lp, send_sems, recv_sems, srcs, lands = self.scatter
            srcs, lands = _copies_wait(send_sems, recv_sems, srcs, lands, after, True, "scatter_wait")
            self.update(lp, _SHARDED, srcs, lands, dev_idx)
            self.scatter = None
            return lands[0]

        def early_grads(self, l, gw):
            if l > 0 or L == 1:
                return None
            order = self.finish_scatter(gw["out"])
            *self.early, token = start_scatter(full_grads(gw, rest_keys), order, "scatter_start_rest")
            return token

        def grads(self, l, gw, after):
            if l > 0:
                order = self.finish_scatter(after) if self.scatter is not None else dev_idx
                send_sems, recv_sems, srcs, lands, token = start_scatter(full_grads(gw, _SHARDED), order,
                                                                         "scatter_start")
                self.scatter = (l, send_sems, recv_sems, srcs, lands)
                return token
            last_keys = _SHARDED
            if self.early is not None:
                send_sems, recv_sems, srcs, lands = self.early
                srcs, lands = _copies_wait(send_sems, recv_sems, srcs, lands, after, True, "scatter_wait_rest")
                self.update(0, rest_keys, srcs, lands, dev_idx)
                last_keys = _SHARDED[:1]
            full = full_grads(gw, last_keys)
            from_sibling = _exchange_sibling(full, "scatter_sibling")
            parts = [_pair_sum(g, r, c_idx, "pair_sum") for g, r in zip(full, from_sibling)]
            self.update(0, last_keys, parts, _exchange_chips(parts, "scatter_chips"), chip_idx)
            return None

    comm = Comm()
    loss, grad_x, d_rel_bias, d_lb, d_hg, d_gains = _local_step(
        x[0], mem[0], loss_target[0], rel_bias, lb_logits, comm.gains, hg_norm, comm)
    results = comm.results

    pieces = [jnp.reshape(loss, (1,)), d_rel_bias.reshape(-1), d_lb.reshape(-1), d_hg.reshape(-1), d_gains.reshape(-1)]
    sizes = [p.shape[0] for p in pieces]
    total = sum(sizes)
    rows = -(-total // 1024) * 8
    pack = jnp.pad(jnp.concatenate(pieces), (0, rows * 128 - total)).reshape(rows, 128)
    summed = _sum_devices(_all_gather([pack], "gather_small")[0]).reshape(-1)
    offs = [sum(sizes[:i]) for i in range(len(sizes))]
    loss = summed[0]
    g_rel_bias = summed[offs[1]:offs[1] + sizes[1]].reshape(rel_bias.shape)
    g_lb_all = summed[offs[2]:offs[2] + sizes[2]].reshape(lb_logits.shape)
    g_hg = summed[offs[3]:offs[3] + sizes[3]].reshape(hg_norm.shape)
    g_gains_full = summed[offs[4]:offs[4] + sizes[4]].reshape(L, 7, D)
    g_gains = lax.dynamic_slice_in_dim(g_gains_full, dev * (D // N_DEV), D // N_DEV, axis=2)
    g_lb = jax.vjp(_lb_all, lb_logits)[1](g_lb_all)[0]

    small = [(rel_bias, g_rel_bias, m_rel_bias, v_rel_bias), (lb_logits, g_lb, m_lb_logits, v_lb_logits),
             (norm_gains, g_gains, m_norm_gains, v_norm_gains), (hg_norm, g_hg, m_hg_norm, v_hg_norm)]
    upd = [_adam_small(w, g, m, v, "adam_small") for w, g, m, v in small]

    def ordered(small_vals, big_vals):
        return [small_vals[0], small_vals[1], small_vals[2], big_vals["in"], small_vals[3], big_vals["out"],
                big_vals["cq"], big_vals["ckv"], big_vals["co"], big_vals["gu"], big_vals["down"]]

    grads = ordered([s[1] for s in small], {k: results[k][0] for k in _SHARDED})
    deltas = ordered([u[0] for u in upd], {k: results[k][1] for k in _SHARDED})
    new_m = ordered([u[1] for u in upd], {k: results[k][2] for k in _SHARDED})
    new_v = ordered([u[2] for u in upd], {k: results[k][3] for k in _SHARDED})
    return (loss, grad_x[None], *grads, *deltas, *new_m, *new_v)
```

```python
import functools
import math

import jax
import jax.numpy as jnp
from jax import lax
from jax.experimental import pallas as pl
from jax.experimental.pallas import tpu as pltpu

F32 = jnp.float32
BF16 = jnp.bfloat16
MESH = pl.DeviceIdType.MESH

N_DEV = 8
HEAD = 128
CROSS_HEADS = 4
Q_BLOCK = 128
BRANCHES = ((128, 1), (512, 4), (2048, 16))
REL_BUCKETS = 32
REL_MAX_DIST = 2048
HG_CHUNK = 16
HG_HEADS_PER_STEP = 8
ATTN_PAIRS_PER_STAGE = 8
RMS_EPS = 1e-6
NEG_INF = -1e30
ADAM_LR, ADAM_B1, ADAM_B2, ADAM_EPS, ADAM_WD, ADAM_STEP = 0.001, 0.9, 0.999, 1e-08, 0.01, 10
VMEM_LIMIT = 48 * 1024 * 1024
MXU_WIDTH = 256
NT_K_PER_STEP = 3584


def _tile(n, target, mult):
    best = None
    t = mult
    while t <= min(n, target):
        if n % t == 0:
            best = t
        t += mult
    return best if best is not None else n


def _cp(sem, vmem=VMEM_LIMIT):
    return pltpu.CompilerParams(dimension_semantics=sem, vmem_limit_bytes=vmem)


def _mm_nn(a, b3, out_dtype, name, after=None):
    M, K = a.shape
    J, K2, Nb = b3.shape
    assert K == K2
    tm, tn, tk = _tile(M, 1024, 8), _tile(Nb, 1408, 128), _tile(K, 2816, 128)
    nn, nk = Nb // tn, K // tk
    extra = [] if after is None else [after]
    jb = 2 if nn == 1 and tn % MXU_WIDTH and J % 2 == 0 else 1

    def body(a_ref, b_ref, *rest):
        o_ref, acc = rest[len(extra)], rest[len(extra) + 1:]
        bv = b_ref[0] if jb == 1 else jnp.concatenate([b_ref[j] for j in range(jb)], axis=1)
        prod = jnp.dot(a_ref[...].astype(BF16), bv.astype(BF16), preferred_element_type=F32)
        if nk == 1:
            o_ref[...] = prod.astype(o_ref.dtype)
        else:
            k = pl.program_id(2)

            @pl.when(k == 0)
            def _():
                acc[0][...] = prod

            @pl.when(k > 0)
            def _():
                acc[0][...] += prod

            @pl.when(k == nk - 1)
            def _():
                o_ref[...] = acc[0][...].astype(o_ref.dtype)

    return pl.pallas_call(
        body, name=name, grid=(M // tm, J * nn // jb, nk),
        in_specs=[pl.BlockSpec((tm, tk), lambda i, n, k: (i, k)),
                  pl.BlockSpec((jb, tk, tn), lambda i, n, k: (n // nn, k, n % nn))]
                 + [pl.BlockSpec(memory_space=pl.ANY)] * len(extra),
        out_specs=pl.BlockSpec((tm, jb * tn), lambda i, n, k: (i, n)),
        out_shape=jax.ShapeDtypeStruct((M, J * Nb), out_dtype),
        scratch_shapes=[] if nk == 1 else [pltpu.VMEM((tm, jb * tn), F32)],
        compiler_params=_cp(("parallel", "parallel", "arbitrary")),
    )(a, b3, *extra)


def _mm_nt(a, b3, out_dtype, name, after=None):
    extra = [] if after is None else [after]
    halves = a.ndim == 3
    M, Kt = (a.shape[1], 2 * a.shape[2]) if halves else a.shape
    J, N, Kb = b3.shape
    assert Kt == J * Kb
    tm, tn = _tile(M, 1024, 8), _tile(N, 1408, 128)
    tk = _tile(math.gcd(Kb, Kt // 2) if halves else Kb, 2048, 128)
    nkb = Kb // tk
    groups = J // 2 if halves and J > 1 else J
    jb = max(j for j in range(1, groups + 1) if groups % j == 0 and j * Kb <= NT_K_PER_STEP) if nkb == 1 else 1
    nk = (J // jb) * nkb

    def body(a_ref, b_ref, *rest):
        o_ref, acc = rest[len(extra)], rest[len(extra) + 1:]
        bv = b_ref[0] if jb == 1 else jnp.concatenate([b_ref[j] for j in range(jb)], axis=1)
        prod = lax.dot_general(a_ref[...].astype(BF16), bv.astype(BF16), (((1,), (1,)), ((), ())),
                               preferred_element_type=F32)
        if nk == 1:
            o_ref[...] = prod.astype(o_ref.dtype)
        else:
            k = pl.program_id(2)

            @pl.when(k == 0)
            def _():
                acc[0][...] = prod

            @pl.when(k > 0)
            def _():
                acc[0][...] += prod

            @pl.when(k == nk - 1)
            def _():
                o_ref[...] = acc[0][...].astype(o_ref.dtype)

    return pl.pallas_call(
        body, name=name, grid=(M // tm, N // tn, nk),
        in_specs=[pl.BlockSpec((None, tm, jb * tk), lambda i, n, k: (k // (nk // 2), i, k % (nk // 2))) if halves
                  else pl.BlockSpec((tm, jb * tk), lambda i, n, k: (i, k)),
                  pl.BlockSpec((jb, tn, tk), lambda i, n, k: (k // nkb, n, k % nkb))]
                 + [pl.BlockSpec(memory_space=pl.ANY)] * len(extra),
        out_specs=pl.BlockSpec((tm, tn), lambda i, n, k: (i, n)),
        out_shape=jax.ShapeDtypeStruct((M, N), out_dtype),
        scratch_shapes=[] if nk == 1 else [pltpu.VMEM((tm, tn), F32)],
        compiler_params=_cp(("parallel", "parallel", "arbitrary")),
    )(a, b3, *extra)


def _mm_tn(a, b, J, out_dtype, name):
    S, M = a.shape
    halves = b.ndim == 3
    S2, Nt = (b.shape[1], 2 * b.shape[2]) if halves else b.shape
    assert S == S2 and Nt % J == 0
    Nb = Nt // J
    tm, tk = _tile(M, 1408, 128), _tile(S, 2048, 8)
    tn = _tile(math.gcd(Nb, Nt // 2) if halves else Nb, 1408, 128)
    nn, nk = Nb // tn, S // tk
    jb = 2 if nn == 1 and tn % MXU_WIDTH and J % (4 if halves else 2) == 0 else 1
    if jb * tn > 2048:
        tm = _tile(M, 512, 128)
    per_half = J * nn // 2 // jb
    b_spec = (pl.BlockSpec((None, tk, jb * tn), lambda i, n, k: (n // per_half, k, n % per_half)) if halves
              else pl.BlockSpec((tk, jb * tn), lambda i, n, k: (k, n)))

    def body(a_ref, b_ref, o_ref, *acc):
        prod = lax.dot_general(a_ref[...].astype(BF16), b_ref[...].astype(BF16), (((0,), (0,)), ((), ())),
                               preferred_element_type=F32)

        def write(val):
            for j in range(jb):
                o_ref[j] = val[:, j * tn:(j + 1) * tn].astype(o_ref.dtype)

        if nk == 1:
            write(prod)
        else:
            k = pl.program_id(2)

            @pl.when(k == 0)
            def _():
                acc[0][...] = prod

            @pl.when(k > 0)
            def _():
                acc[0][...] += prod

            @pl.when(k == nk - 1)
            def _():
                write(acc[0][...])

    return pl.pallas_call(
        body, name=name, grid=(M // tm, J * nn // jb, nk),
        in_specs=[pl.BlockSpec((tk, tm), lambda i, n, k: (k, i)), b_spec],
        out_specs=pl.BlockSpec((jb, tm, tn), lambda i, n, k: (n // nn, i, n % nn)),
        out_shape=jax.ShapeDtypeStruct((J, M, Nb), out_dtype),
        scratch_shapes=[] if nk == 1 else [pltpu.VMEM((tm, jb * tn), F32)],
        compiler_params=_cp(("parallel", "parallel", "arbitrary")),
    )(a, b)


def _rms_scale(v):
    return lax.rsqrt(jnp.mean(v * v, axis=-1, keepdims=True) + RMS_EPS)


def _rms_bwd(dy, v, g):
    r = _rms_scale(v)
    vh = v * r
    u = dy * g
    dv = r * (u - vh * jnp.mean(u * vh, axis=-1, keepdims=True))
    return dv, dy * vh


def _fold8(t):
    R, D = t.shape
    return jnp.sum(t.reshape(R // 8, 8, D), axis=0)


def _norm_fwd(x, z, g_post, g_pre, name):
    S, D = x.shape
    tr = _tile(S, 256, 8)
    has_z, has_pre = z is not None, g_pre is not None

    def body(*refs):
        it = iter(refs)
        x_ref = next(it)
        z_ref = next(it) if has_z else None
        gpost_ref = next(it) if has_z else None
        gpre_ref = next(it) if has_pre else None
        xn_ref = next(it) if has_z else None
        h_ref = next(it) if has_pre else None
        xn = x_ref[...]
        if has_z:
            zz = z_ref[...]
            xn = xn + zz * _rms_scale(zz) * gpost_ref[...]
            xn_ref[...] = xn
        if has_pre:
            h_ref[...] = (xn * _rms_scale(xn) * gpre_ref[...]).astype(BF16)

    row = pl.BlockSpec((tr, D), lambda i: (i, 0))
    gain = pl.BlockSpec((1, D), lambda i: (0, 0))
    ins, specs, outs, ospecs = [x], [row], [], []
    if has_z:
        ins += [z, g_post.reshape(1, D)]
        specs += [row, gain]
        outs.append(jax.ShapeDtypeStruct((S, D), F32))
        ospecs.append(row)
    if has_pre:
        ins.append(g_pre.reshape(1, D))
        specs.append(gain)
        outs.append(jax.ShapeDtypeStruct((S, D), BF16))
        ospecs.append(row)
    res = pl.pallas_call(body, name=name, grid=(S // tr,), in_specs=specs, out_specs=ospecs, out_shape=outs,
                         compiler_params=_cp(("parallel",)))(*ins)
    res = list(res)
    xn = res.pop(0) if has_z else x
    h = res.pop(0) if has_pre else None
    return xn, h


def _norm_bwd(dres, dh, xn, z, g_pre, g_post, name):
    S, D = xn.shape
    tr = _tile(S, 256, 8)
    has_res, has_pre, has_z = dres is not None, dh is not None, z is not None

    def body(*refs):
        it = iter(refs)
        dres_ref = next(it) if has_res else None
        dh_ref = next(it) if has_pre else None
        xn_ref = next(it) if has_pre else None
        gpre_ref = next(it) if has_pre else None
        z_ref = next(it) if has_z else None
        gpost_ref = next(it) if has_z else None
        t_ref = next(it)
        dz_ref = next(it) if has_z else None
        dgpre_ref = next(it) if has_pre else None
        dgpost_ref = next(it) if has_z else None
        first = pl.program_id(0) == 0
        t = dres_ref[...] if has_res else None
        if has_pre:
            dv, dg = _rms_bwd(dh_ref[...].astype(F32), xn_ref[...], gpre_ref[...])
            t = dv if t is None else t + dv
            part = _fold8(dg)

            @pl.when(first)
            def _():
                dgpre_ref[...] = part

            @pl.when(jnp.logical_not(first))
            def _():
                dgpre_ref[...] += part
        t_ref[...] = t
        if has_z:
            dv, dg = _rms_bwd(t, z_ref[...], gpost_ref[...])
            dz_ref[...] = dv.astype(BF16)
            part2 = _fold8(dg)

            @pl.when(first)
            def _():
                dgpost_ref[...] = part2

            @pl.when(jnp.logical_not(first))
            def _():
                dgpost_ref[...] += part2

    row = pl.BlockSpec((tr, D), lambda i: (i, 0))
    gain = pl.BlockSpec((1, D), lambda i: (0, 0))
    accs = pl.BlockSpec((8, D), lambda i: (0, 0))
    ins, specs = [], []
    if has_res:
        ins.append(dres)
        specs.append(row)
    if has_pre:
        ins += [dh, xn, g_pre.reshape(1, D)]
        specs += [row, row, gain]
    if has_z:
        ins += [z, g_post.reshape(1, D)]
        specs += [row, gain]
    outs, ospecs = [jax.ShapeDtypeStruct((S, D), F32)], [row]
    if has_z:
        outs.append(jax.ShapeDtypeStruct((S, D), BF16))
        ospecs.append(row)
    if has_pre:
        outs.append(jax.ShapeDtypeStruct((8, D), F32))
        ospecs.append(accs)
    if has_z:
        outs.append(jax.ShapeDtypeStruct((8, D), F32))
        ospecs.append(accs)
    res = list(pl.pallas_call(body, name=name, grid=(S // tr,), in_specs=specs, out_specs=ospecs, out_shape=outs,
                              compiler_params=_cp(("arbitrary",)))(*ins))
    t = res.pop(0)
    dz = res.pop(0) if has_z else None
    dgpre = jnp.sum(res.pop(0), axis=0) if has_pre else None
    dgpost = jnp.sum(res.pop(0), axis=0) if has_z else None
    return t, dz, dgpre, dgpost


def _loss_and_grad(y, target):
    S, D = y.shape
    tr = _tile(S, 256, 8)

    def body(y_ref, t_ref, dy_ref, l_ref):
        err = y_ref[...] - t_ref[...]
        dy_ref[...] = err * (1.0 / D)
        part = 0.5 * jnp.sum(jnp.mean(err * err, axis=-1, keepdims=True), axis=0, keepdims=True)
        first = pl.program_id(0) == 0

        @pl.when(first)
        def _():
            l_ref[...] = jnp.broadcast_to(part, l_ref.shape)

        @pl.when(jnp.logical_not(first))
        def _():
            l_ref[...] += jnp.broadcast_to(part, l_ref.shape)

    row = pl.BlockSpec((tr, D), lambda i: (i, 0))
    dy, l = pl.pallas_call(
        body, name="loss_grad", grid=(S // tr,), in_specs=[row, row],
        out_specs=[row, pl.BlockSpec((8, 128), lambda i: (0, 0))],
        out_shape=[jax.ShapeDtypeStruct((S, D), F32), jax.ShapeDtypeStruct((8, 128), F32)],
        compiler_params=_cp(("arbitrary",)))(y, target)
    return l[0, 0], dy


def _mm_gate_up(a, w3, name):
    S, D = a.shape
    J, D2, Nb = w3.shape
    F = J * Nb // 2
    assert D == D2
    tm, tn = _tile(S, 512, 8), _tile(math.gcd(Nb, F), 1408, 128)
    nn, nf = Nb // tn, F // tn

    def body(a_ref, wg_ref, wu_ref, gu_ref, act_ref):
        both = jnp.dot(a_ref[...], jnp.concatenate([wg_ref[...], wu_ref[...]], axis=1), preferred_element_type=F32)
        g, u = both[:, :tn], both[:, tn:]
        gu_ref[0] = g
        gu_ref[1] = u
        act_ref[...] = (g * jax.nn.sigmoid(g) * u).astype(BF16)

    return pl.pallas_call(
        body, name=name, grid=(nf, S // tm),
        in_specs=[pl.BlockSpec((tm, D), lambda n, i: (i, 0)),
                  pl.BlockSpec((None, D, tn), lambda n, i: (n // nn, 0, n % nn)),
                  pl.BlockSpec((None, D, tn), lambda n, i: ((n + nf) // nn, 0, (n + nf) % nn))],
        out_specs=[pl.BlockSpec((2, tm, tn), lambda n, i: (0, i, n)), pl.BlockSpec((tm, tn), lambda n, i: (i, n))],
        out_shape=[jax.ShapeDtypeStruct((2, S, F), F32), jax.ShapeDtypeStruct((S, F), BF16)],
        compiler_params=_cp(("parallel", "parallel")))(a, w3, w3)


def _mm_dx_down_swiglu(dy, wdown3, gu, name, after=None):
    S, D = dy.shape
    _, F, D2 = wdown3.shape
    assert D == D2
    tm, tn = _tile(S, 1024, 8), _tile(F, 512, 128)
    extra = [] if after is None else [after]

    def body(a_ref, b_ref, gu_ref, *rest):
        o_ref = rest[len(extra)]
        d = lax.dot_general(a_ref[...], b_ref[...], (((1,), (1,)), ((), ())), preferred_element_type=F32)
        g, u = gu_ref[0], gu_ref[1]
        sg = jax.nn.sigmoid(g)
        o_ref[0] = (d * u * (sg * (1.0 + g * (1.0 - sg)))).astype(BF16)
        o_ref[1] = (d * g * sg).astype(BF16)

    return pl.pallas_call(
        body, name=name, grid=(S // tm, F // tn),
        in_specs=[pl.BlockSpec((tm, D), lambda i, n: (i, 0)), pl.BlockSpec((None, tn, D), lambda i, n: (0, n, 0)),
                  pl.BlockSpec((2, tm, tn), lambda i, n: (0, i, n))] + [pl.BlockSpec(memory_space=pl.ANY)] * len(extra),
        out_specs=pl.BlockSpec((2, tm, tn), lambda i, n: (0, i, n)),
        out_shape=jax.ShapeDtypeStruct((2, S, F), BF16),
        compiler_params=_cp(("parallel", "parallel")))(dy, wdown3, gu, *extra)


def _cross_scores(q, k, scale):
    s = lax.dot_general(q, k, (((1,), (1,)), ((), ())), preferred_element_type=F32) * scale
    m = jnp.max(s, axis=-1, keepdims=True)
    e = jnp.exp(s - m)
    return e / jnp.sum(e, axis=-1, keepdims=True)


def _cross_fwd(cq, ckv, name):
    S, D = cq.shape
    Nm = ckv.shape[0]
    dh = D // CROSS_HEADS
    tq = _tile(S, 512, 8)
    scale = dh ** -0.5

    def body(q_ref, kv_ref, o_ref):
        for h in range(CROSS_HEADS):
            cols = slice(h * dh, (h + 1) * dh)
            p = _cross_scores(q_ref[:, cols], kv_ref[:, cols], scale)
            o_ref[:, cols] = jnp.dot(p.astype(BF16), kv_ref[:, D + h * dh:D + (h + 1) * dh],
                                     preferred_element_type=F32).astype(BF16)

    return pl.pallas_call(
        body, name=name, grid=(S // tq,),
        in_specs=[pl.BlockSpec((tq, D), lambda i: (i, 0)), pl.BlockSpec((Nm, 2 * D), lambda i: (0, 0))],
        out_specs=pl.BlockSpec((tq, D), lambda i: (i, 0)),
        out_shape=jax.ShapeDtypeStruct((S, D), BF16), compiler_params=_cp(("parallel",)))(cq, ckv)


def _cross_bwd(cq, ckv, do, name):
    S, D = cq.shape
    Nm = ckv.shape[0]
    dh = D // CROSS_HEADS
    tq = _tile(S, 512, 8)
    scale = dh ** -0.5

    def body(q_ref, kv_ref, do_ref, dq_ref, dkv_ref):
        first = pl.program_id(0) == 0

        @pl.when(first)
        def _():
            dkv_ref[...] = jnp.zeros_like(dkv_ref)

        for h in range(CROSS_HEADS):
            cols = slice(h * dh, (h + 1) * dh)
            vcols = slice(D + h * dh, D + (h + 1) * dh)
            q, k, v = q_ref[:, cols], kv_ref[:, cols], kv_ref[:, vcols]
            doh = do_ref[:, cols].astype(BF16)
            p = _cross_scores(q, k, scale)
            dp = lax.dot_general(doh, v, (((1,), (1,)), ((), ())), preferred_element_type=F32)
            ds = (p * (dp - jnp.sum(p * dp, axis=-1, keepdims=True)) * scale).astype(BF16)
            dq_ref[:, cols] = jnp.dot(ds, k, preferred_element_type=F32).astype(BF16)
            dkv_ref[:, cols] += lax.dot_general(ds, q, (((0,), (0,)), ((), ())), preferred_element_type=F32)
            dkv_ref[:, vcols] += lax.dot_general(p.astype(BF16), doh, (((0,), (0,)), ((), ())),
                                                 preferred_element_type=F32)

    return pl.pallas_call(
        body, name=name, grid=(S // tq,),
        in_specs=[pl.BlockSpec((tq, D), lambda i: (i, 0)), pl.BlockSpec((Nm, 2 * D), lambda i: (0, 0)),
                  pl.BlockSpec((tq, D), lambda i: (i, 0))],
        out_specs=[pl.BlockSpec((tq, D), lambda i: (i, 0)), pl.BlockSpec((Nm, 2 * D), lambda i: (0, 0))],
        out_shape=[jax.ShapeDtypeStruct((S, D), BF16), jax.ShapeDtypeStruct((Nm, 2 * D), F32)],
        compiler_params=_cp(("arbitrary",)))(cq, ckv, do)


def _rel_bucket(dist):
    max_exact = REL_BUCKETS // 2
    d_f = jnp.maximum(dist, 1).astype(F32)
    large = max_exact + (jnp.log(d_f / max_exact) / math.log(REL_MAX_DIST / max_exact)
                         * (REL_BUCKETS - max_exact)).astype(jnp.int32)
    large = jnp.minimum(large, REL_BUCKETS - 1)
    return jnp.where(dist < max_exact, dist, large)


def _branch_tables(dilation):
    qi = jnp.arange(Q_BLOCK)[:, None]
    kj = jnp.arange(2 * Q_BLOCK)[None, :]
    m = qi - kj + Q_BLOCK
    return _rel_bucket(jnp.maximum(m, 0) * dilation), (m >= 0) & (m <= Q_BLOCK)


def _attn_logits(q, kcat, bias, first_block, scale):
    s = lax.dot_general(q, kcat, (((1,), (1,)), ((), ())), preferred_element_type=F32) * scale + bias
    col = lax.broadcasted_iota(jnp.int32, s.shape, 1)
    return jnp.where(jnp.logical_and(first_block, col < Q_BLOCK), NEG_INF, s)


def _attn_branch_fwd(proj, bias, dilation, H, name):
    S, NC = proj.shape
    AW = H * HEAD
    r = dilation
    hb = H if r == 1 else 1
    G = Q_BLOCK * r
    ng, nh = S // G, H // hb
    scale = HEAD ** -0.5

    def body(q_ref, kp_ref, kc_ref, vp_ref, vc_ref, b_ref, o_ref, l_ref):
        first_block = pl.program_id(0) == 0
        h0 = pl.program_id(1) * hb
        pairs = [(pl.ds(rho, Q_BLOCK, stride=r) if r > 1 else slice(None), hh) for rho in range(r) for hh in range(hb)]
        for g0 in range(0, len(pairs), ATTN_PAIRS_PER_STAGE):
            group = pairs[g0:g0 + ATTN_PAIRS_PER_STAGE]
            sl = [(rows, slice(hh * HEAD, (hh + 1) * HEAD)) for rows, hh in group]
            q = [q_ref[rw, cl].astype(BF16) for rw, cl in sl]
            kcat = [jnp.concatenate([kp_ref[rw, cl], kc_ref[rw, cl]], axis=0).astype(BF16) for rw, cl in sl]
            vcat = [jnp.concatenate([vp_ref[rw, cl], vc_ref[rw, cl]], axis=0).astype(BF16) for rw, cl in sl]
            s = [_attn_logits(q[i], kcat[i], b_ref[h0 + hh], first_block, scale) for i, (_, hh) in enumerate(group)]
            m = [jnp.max(si, axis=-1, keepdims=True) for si in s]
            e = [jnp.exp(si - mi) for si, mi in zip(s, m)]
            den = [jnp.sum(ei, axis=-1, keepdims=True) for ei in e]
            o = [jnp.dot(ei.astype(BF16), vi, preferred_element_type=F32) for ei, vi in zip(e, vcat)]
            for i, (rw, cl) in enumerate(sl):
                o_ref[rw, cl] = o[i] / den[i]
                l_ref[rw, cl] = jnp.broadcast_to(m[i] + jnp.log(den[i]), (Q_BLOCK, HEAD))

    def blk(part, prev):
        if prev:
            return pl.BlockSpec((G, hb * HEAD), lambda n, h: (jnp.maximum(n - 1, 0), part * nh + h))
        return pl.BlockSpec((G, hb * HEAD), lambda n, h: (n, part * nh + h))

    out = pl.BlockSpec((G, hb * HEAD), lambda n, h: (n, h))
    return pl.pallas_call(
        body, name=name, grid=(ng, nh),
        in_specs=[blk(0, False), blk(1, True), blk(1, False), blk(2, True), blk(2, False),
                  pl.BlockSpec((H, Q_BLOCK, 2 * Q_BLOCK), lambda n, h: (0, 0, 0))],
        out_specs=[out, out],
        out_shape=[jax.ShapeDtypeStruct((S, AW), F32)] * 2,
        compiler_params=_cp(("parallel", "parallel")))(proj, proj, proj, proj, proj, bias)


def _attn_merge(outs, lses, width_out, name):
    S, AW = outs[0].shape
    tr = _tile(S, 256, 8)

    def body(o0, o1, o2, l0, l1, l2, cat_ref, om_ref, lt_ref):
        a, b, c = l0[...], l1[...], l2[...]
        m = jnp.maximum(jnp.maximum(a, b), c)
        ea, eb, ec = jnp.exp(a - m), jnp.exp(b - m), jnp.exp(c - m)
        tot = ea + eb + ec
        o = (ea * o0[...] + eb * o1[...] + ec * o2[...]) / tot
        om_ref[...] = o
        cat_ref[...] = o.astype(BF16)
        lt_ref[...] = m + jnp.log(tot)

    row = pl.BlockSpec((tr, AW), lambda i: (i, 0))
    return pl.pallas_call(
        body, name=name, grid=(S // tr,), in_specs=[row] * 6, out_specs=[row, row, row],
        out_shape=[jax.ShapeDtypeStruct((S, width_out), BF16), jax.ShapeDtypeStruct((S, AW), F32),
                   jax.ShapeDtypeStruct((S, AW), F32)],
        compiler_params=_cp(("parallel",)))(*outs, *lses)


def _attn_delta(dcat, o_attn, H, name, after=None):
    S, AW = o_attn.shape
    tr = _tile(S, 256, 8)
    extra = [] if after is None else [after]

    def body(d_ref, o_ref, *rest):
        out_ref = rest[len(extra)]
        for h in range(H):
            cols = slice(h * HEAD, (h + 1) * HEAD)
            out_ref[:, cols] = jnp.broadcast_to(
                jnp.sum(d_ref[:, cols] * o_ref[:, cols], axis=-1, keepdims=True), (tr, HEAD))

    row = pl.BlockSpec((tr, AW), lambda i: (i, 0))
    return pl.pallas_call(body, name=name, grid=(S // tr,),
                          in_specs=[row, row] + [pl.BlockSpec(memory_space=pl.ANY)] * len(extra), out_specs=row,
                          out_shape=jax.ShapeDtypeStruct((S, AW), F32),
                          compiler_params=_cp(("parallel",)))(dcat, o_attn, *extra)


def _attn_branch_bwd(proj, dcat, lse_tot, delta, bias, dilation, H, name):
    S, NC = proj.shape
    AW = H * HEAD
    r = dilation
    hb = H if r == 1 else 1
    G = Q_BLOCK * r
    ng, nh = S // G, H // hb
    scale = HEAD ** -0.5

    def body(q_ref, kp_ref, kc_ref, vp_ref, vc_ref, do_ref, l_ref, t_ref, b_ref,
             dq_ref, dk_ref, dv_ref, db_ref, ck_ref, cv_ref):
        hblk, n = pl.program_id(0), pl.program_id(1)
        h0 = hblk * hb

        @pl.when(jnp.logical_and(hblk == 0, n == 0))
        def _():
            db_ref[...] = jnp.zeros_like(db_ref)

        @pl.when(n == 0)
        def _():
            ck_ref[...] = jnp.zeros_like(ck_ref)
            cv_ref[...] = jnp.zeros_like(cv_ref)

        @pl.when(n < ng)
        def _():
            pairs = [(pl.ds(rho, Q_BLOCK, stride=r) if r > 1 else slice(None), hh)
                     for rho in range(r) for hh in range(hb)]
            for g0 in range(0, len(pairs), ATTN_PAIRS_PER_STAGE):
                group = pairs[g0:g0 + ATTN_PAIRS_PER_STAGE]
                idx = range(len(group))
                sl = [(rows, slice(hh * HEAD, (hh + 1) * HEAD)) for rows, hh in group]
                q = [q_ref[rw, cl].astype(BF16) for rw, cl in sl]
                kcat = [jnp.concatenate([kp_ref[rw, cl], kc_ref[rw, cl]], axis=0).astype(BF16) for rw, cl in sl]
                vcat = [jnp.concatenate([vp_ref[rw, cl], vc_ref[rw, cl]], axis=0).astype(BF16) for rw, cl in sl]
                doh = [do_ref[rw, cl].astype(BF16) for rw, cl in sl]
                s = [_attn_logits(q[i], kcat[i], b_ref[h0 + group[i][1]], n == 0, scale) for i in idx]
                dp = [lax.dot_general(doh[i], vcat[i], (((1,), (1,)), ((), ())), preferred_element_type=F32) for i in idx]
                p, ds = [], []
                for i, (rw, cl) in enumerate(sl):
                    lt, dl = l_ref[rw, cl], t_ref[rw, cl]
                    p.append(jnp.exp(s[i] - jnp.concatenate([lt, lt], axis=1)))
                    ds.append(p[i] * (dp[i] - jnp.concatenate([dl, dl], axis=1)))
                dsb = [(d * scale).astype(BF16) for d in ds]
                dq = [jnp.dot(dsb[i], kcat[i], preferred_element_type=F32) for i in idx]
                dkc = [lax.dot_general(dsb[i], q[i], (((0,), (0,)), ((), ())), preferred_element_type=F32) for i in idx]
                dvc = [lax.dot_general(p[i].astype(BF16), doh[i], (((0,), (0,)), ((), ())), preferred_element_type=F32)
                       for i in idx]
                for i, (rw, cl) in enumerate(sl):
                    dq_ref[rw, cl] = dq[i]
                    dk_ref[rw, cl] = ck_ref[rw, cl] + dkc[i][:Q_BLOCK]
                    dv_ref[rw, cl] = cv_ref[rw, cl] + dvc[i][:Q_BLOCK]
                    ck_ref[rw, cl] = dkc[i][Q_BLOCK:]
                    cv_ref[rw, cl] = dvc[i][Q_BLOCK:]
                for i in idx:
                    db_ref[h0 + group[i][1]] += ds[i]

        @pl.when(n == ng)
        def _():
            dk_ref[...] = ck_ref[...]
            dv_ref[...] = cv_ref[...]

    last = ng - 1

    def cur(part):
        return pl.BlockSpec((G, hb * HEAD), lambda h, n: (jnp.minimum(n, last), part * nh + h))

    def prev(part):
        return pl.BlockSpec((G, hb * HEAD), lambda h, n: (jnp.clip(n - 1, 0, last), part * nh + h))

    table = pl.BlockSpec((H, Q_BLOCK, 2 * Q_BLOCK), lambda h, n: (0, 0, 0))
    return pl.pallas_call(
        body, name=name, grid=(nh, ng + 1),
        in_specs=[cur(0), prev(1), cur(1), prev(2), cur(2), cur(0), cur(0), cur(0), table],
        out_specs=[cur(0), prev(0), prev(0), table],
        out_shape=[jax.ShapeDtypeStruct((S, AW), F32)] * 3 + [jax.ShapeDtypeStruct((H, Q_BLOCK, 2 * Q_BLOCK), F32)],
        scratch_shapes=[pltpu.VMEM((G, hb * HEAD), F32), pltpu.VMEM((G, hb * HEAD), F32)],
        compiler_params=_cp(("arbitrary", "arbitrary")))(proj, proj, proj, proj, proj, dcat, lse_tot, delta, bias)


def _bias_grad(dbs, onehots, H):
    def body(d0, d1, d2, e0, e1, e2, o_ref):
        acc = jnp.zeros((H, REL_BUCKETS), F32)
        for d, e in ((d0, e0), (d1, e1), (d2, e2)):
            acc = acc + lax.dot_general(d[...], e[...], (((1,), (1,)), ((), ())), preferred_element_type=F32,
                                        precision=lax.Precision.HIGHEST)
        o_ref[...] = acc

    flat = [d.reshape(H, 2 * Q_BLOCK * Q_BLOCK) for d in dbs]
    return pl.pallas_call(body, name="bias_grad", out_shape=jax.ShapeDtypeStruct((H, REL_BUCKETS), F32),
                          compiler_params=pltpu.CompilerParams(vmem_limit_bytes=VMEM_LIMIT))(*flat, *onehots)


def _assemble_dproj(dqkv, dhg, name):
    S, AW = dqkv[0][0].shape
    tr = _tile(S, 256, 8)
    ins = [dqkv[g][c] for g in range(3) for c in range(3)] + list(dhg)

    def body(*refs):
        o_ref = refs[-1]
        j = pl.program_id(1)
        for c in range(3):
            @pl.when(j == c)
            def _(c=c):
                o_ref[...] = (refs[c][...] + refs[3 + c][...] + refs[6 + c][...]).astype(BF16)

        for k in range(4):
            @pl.when(j == 3 + k)
            def _(k=k):
                o_ref[...] = refs[9 + k][...]

    row = pl.BlockSpec((tr, AW), lambda i, j: (i, 0))
    return pl.pallas_call(
        body, name=name, grid=(S // tr, 7), in_specs=[row] * 13,
        out_specs=pl.BlockSpec((tr, AW), lambda i, j: (i, j)),
        out_shape=jax.ShapeDtypeStruct((S, 7 * AW), BF16),
        compiler_params=_cp(("parallel", "arbitrary")))(*ins)


def _tri(n, upper):
    r = lax.broadcasted_iota(jnp.int32, (n, n), 0)
    c = lax.broadcasted_iota(jnp.int32, (n, n), 1)
    return jnp.where(c >= r if upper else c <= r, 1.0, 0.0).astype(F32)


def _hg_gates(fz, qz, lb):
    sig = jax.nn.sigmoid(fz)
    f = lb + (1.0 - lb) * sig
    sq = jax.nn.sigmoid(qz)
    return sig, f, jnp.log(f), 1.0 - f, qz * sq, sq


def _hg_fwd(proj, cat, lb, gain, HH, name):
    S, NC = proj.shape
    W = HH * HEAD
    C = HG_CHUNK
    hb = min(HG_HEADS_PER_STEP, HH)
    nh = HH // hb
    tb = _tile(S, 2048 // hb, C)
    ncb = tb // C
    base = 3 * nh
    cat_cols = cat.shape[1] // (hb * HEAD)

    def body(fz_ref, iv_ref, qz_ref, gz_ref, lb_ref, g_ref, cat_in, cat_ref, o_ref, st_ref, state):
        del cat_in

        @pl.when(pl.program_id(1) == 0)
        def _():
            state[...] = jnp.zeros_like(state)

        tri = _tri(C, False)
        row8 = lax.broadcasted_iota(jnp.int32, (8, 1), 0)

        def chunk(c, carry):
            rows = pl.ds(pl.multiple_of(c * C, C), C)
            heads = range(hb)
            cols = [slice(hh * HEAD, (hh + 1) * HEAD) for hh in heads]
            vv = [iv_ref[rows, cl] for cl in cols]
            gates = [_hg_gates(fz_ref[rows, cl], qz_ref[rows, cl], lb_ref[:, cl]) for cl in cols]
            kk, qq = [gt[3] for gt in gates], [gt[4] for gt in gates]
            b = [jnp.dot(tri, gt[2], preferred_element_type=F32, precision=lax.Precision.HIGHEST) for gt in gates]
            bl = [bh[C - 1:C, :] for bh in b]
            st = [state[hh] for hh in heads]
            stb = [s_.astype(BF16) for s_ in st]
            for hh in heads:
                st_ref[hh, c] = stb[hh]
            o = [lax.dot_general((qq[hh] * jnp.exp(b[hh])).astype(BF16), stb[hh], (((1,), (1,)), ((), ())),
                                 preferred_element_type=F32) for hh in heads]
            upd = [lax.dot_general(vv[hh].astype(BF16), (kk[hh] * jnp.exp(bl[hh] - b[hh])).astype(BF16),
                                   (((0,), (0,)), ((), ())), preferred_element_type=F32) for hh in heads]
            for hh in heads:
                state[hh] = st[hh] * jnp.exp(bl[hh]) + upd[hh]
            wts = {}
            for t0 in range(0, C, 8):
                for s in range(min(C, t0 + 8)):
                    for hh in heads:
                        diff = b[hh][t0:t0 + 8, :] - b[hh][s:s + 1, :]
                        if s >= t0:
                            diff = jnp.minimum(diff, 0.0)
                        a = jnp.sum(qq[hh][t0:t0 + 8, :] * jnp.exp(diff) * kk[hh][s:s + 1, :], axis=-1, keepdims=True)
                        wts[t0, s, hh] = jnp.where(row8 >= s - t0, a, 0.0) if s >= t0 else a
            for hh in heads:
                tiles = []
                for t0 in range(0, C, 8):
                    acc = o[hh][t0:t0 + 8, :]
                    for s in range(min(C, t0 + 8)):
                        acc = acc + wts[t0, s, hh] * vv[hh][s:s + 1, :]
                    tiles.append(acc)
                oh = jnp.concatenate(tiles, axis=0)
                o_ref[rows, cols[hh]] = oh
                n = oh * lax.rsqrt(jnp.mean(oh * oh, axis=-1, keepdims=True) + RMS_EPS)
                gz = gz_ref[rows, cols[hh]]
                cat_ref[rows, cols[hh]] = (n * g_ref[:, cols[hh]] * (gz * jax.nn.sigmoid(gz))).astype(BF16)
            return carry

        lax.fori_loop(0, ncb, chunk, 0)

    def col(k):
        return pl.BlockSpec((tb, hb * HEAD), lambda h, i: (i, base + k * nh + h))

    vec = pl.BlockSpec((1, hb * HEAD), lambda h, i: (0, h))
    cat_out, o_raw, states = pl.pallas_call(
        body, name=name, grid=(nh, S // tb),
        in_specs=[col(0), col(1), col(2), col(3), vec, vec, pl.BlockSpec(memory_space=pl.ANY)],
        out_specs=[pl.BlockSpec((tb, hb * HEAD), lambda h, i: (i, cat_cols - nh + h)),
                   pl.BlockSpec((tb, hb * HEAD), lambda h, i: (i, h)),
                   pl.BlockSpec((hb, ncb, HEAD, HEAD), lambda h, i: (h, i, 0, 0))],
        out_shape=[jax.ShapeDtypeStruct(cat.shape, BF16), jax.ShapeDtypeStruct((S, W), F32),
                   jax.ShapeDtypeStruct((HH, S // C, HEAD, HEAD), BF16)],
        scratch_shapes=[pltpu.VMEM((hb, HEAD, HEAD), F32)],
        input_output_aliases={6: 0},
        compiler_params=_cp(("parallel", "arbitrary")))(proj, proj, proj, proj, lb.reshape(1, W), gain.reshape(1, W), cat)
    return cat_out, o_raw, states


def _hg_bwd(proj, dcat, o_raw, states, lb, gain, after, HH, name):
    S, NC = proj.shape
    W = HH * HEAD
    C = HG_CHUNK
    hb = min(HG_HEADS_PER_STEP, HH)
    nh = HH // hb
    tb = _tile(S, 2048 // hb, C)
    ncb = tb // C
    nt = S // tb
    base = 3 * nh
    dcat_cols = dcat.shape[1] // (hb * HEAD)

    def body(fz_ref, iv_ref, qz_ref, gz_ref, dc_ref, o_ref, st_ref, lb_ref, g_ref, after_ref,
             dfz_ref, div_ref, dqz_ref, dgz_ref, dlb_ref, dg_ref, dstate):
        del after_ref

        @pl.when(pl.program_id(1) == 0)
        def _():
            dstate[...] = jnp.zeros_like(dstate)
            dlb_ref[...] = jnp.zeros_like(dlb_ref)
            dg_ref[...] = jnp.zeros_like(dg_ref)

        tri_lo, tri_up = _tri(C, False), _tri(C, True)
        row = lax.broadcasted_iota(jnp.int32, (C, 1), 0)
        row8 = lax.broadcasted_iota(jnp.int32, (8, 1), 0)

        def chunk(cc, carry):
            c = ncb - 1 - cc
            rows = pl.ds(pl.multiple_of(c * C, C), C)
            heads = range(hb)
            starts = list(range(0, C, 8))
            hd = []
            for hh in heads:
                cols = slice(hh * HEAD, (hh + 1) * HEAD)
                lbv, gv = lb_ref[:, cols], g_ref[:, cols]
                fz, vv, qz, gz = fz_ref[rows, cols], iv_ref[rows, cols], qz_ref[rows, cols], gz_ref[rows, cols]
                sig, f, lf, kk, qq, sq = _hg_gates(fz, qz, lbv)
                b = jnp.dot(tri_lo, lf, preferred_element_type=F32, precision=lax.Precision.HIGHEST)
                bl = b[C - 1:C, :]
                eb, ebl = jnp.exp(b), jnp.exp(bl)
                ekb = jnp.exp(bl - b)
                qh, kh = qq * eb, kk * ekb
                o = o_ref[rows, cols]
                dhg = dc_ref[rows, cols]
                sgz = jax.nn.sigmoid(gz)
                rr = lax.rsqrt(jnp.mean(o * o, axis=-1, keepdims=True) + RMS_EPS)
                nrm = o * rr
                dpre = dhg * (gz * sgz)
                dgz_ref[rows, cols] = (dhg * nrm * gv * (sgz * (1.0 + gz * (1.0 - sgz)))).astype(BF16)
                dg_ref[:, cols] += jnp.sum(dpre * nrm, axis=0, keepdims=True)
                dn = dpre * gv
                do = rr * (dn - nrm * jnp.mean(dn * nrm, axis=-1, keepdims=True))
                dob = do.astype(BF16)
                st0 = st_ref[hh, c]
                dst1 = dstate[hh]
                dst1b = dst1.astype(BF16)
                dqh = jnp.dot(dob, st0, preferred_element_type=F32)
                dv = lax.dot_general(kh.astype(BF16), dst1b, (((1,), (1,)), ((), ())), preferred_element_type=F32)
                dkh = jnp.dot(vv.astype(BF16), dst1b, preferred_element_type=F32)
                dstate[hh] = dst1 * ebl + lax.dot_general(dob, qh.astype(BF16), (((0,), (0,)), ((), ())),
                                                          preferred_element_type=F32)
                extra = (jnp.sum(dkh * kh, axis=0, keepdims=True)
                         + ebl * jnp.sum(st0.astype(F32) * dst1, axis=0, keepdims=True))
                hd.append(dict(cols=cols, lbv=lbv, sig=sig, f=f, kk=kk, qq=qq, sq=sq, qz=qz, vv=vv, b=b, do=do, dv=dv,
                               extra=extra, dq=dqh * eb, dk=dkh * ekb))
            red = {}
            for s in range(C):
                for ti, t0 in enumerate(starts):
                    if t0 + 8 <= s:
                        continue
                    for hh in heads:
                        v = hd[hh]
                        diff = v["b"][t0:t0 + 8, :] - v["b"][s:s + 1, :]
                        if s >= t0:
                            diff = jnp.minimum(diff, 0.0)
                        dec = jnp.exp(diff)
                        da = jnp.sum(v["do"][t0:t0 + 8, :] * v["vv"][s:s + 1, :], axis=-1, keepdims=True)
                        a = jnp.sum(v["qq"][t0:t0 + 8, :] * dec * v["kk"][s:s + 1, :], axis=-1, keepdims=True)
                        if s >= t0:
                            keep = row8 >= s - t0
                            da, a = jnp.where(keep, da, 0.0), jnp.where(keep, a, 0.0)
                        red[s, ti, hh] = (da * dec, a)
            for hh in heads:
                v = hd[hh]
                cols, kk, qq, do = v["cols"], v["kk"], v["qq"], v["do"]
                dq_t = [v["dq"][t0:t0 + 8, :] for t0 in starts]
                dk_rows, dv_rows = [], []
                for s in range(C):
                    dk_s = dv_s = None
                    for ti, t0 in enumerate(starts):
                        if t0 + 8 <= s:
                            continue
                        gd, a = red[s, ti, hh]
                        dq_t[ti] = dq_t[ti] + gd * kk[s:s + 1, :]
                        pk = jnp.sum(gd * qq[t0:t0 + 8, :], axis=0, keepdims=True)
                        pv = jnp.sum(a * do[t0:t0 + 8, :], axis=0, keepdims=True)
                        dk_s = pk if dk_s is None else dk_s + pk
                        dv_s = pv if dv_s is None else dv_s + pv
                    dk_rows.append(dk_s)
                    dv_rows.append(dv_s)
                dq = jnp.concatenate(dq_t, axis=0)
                dk = v["dk"] + jnp.concatenate(dk_rows, axis=0)
                dv = v["dv"] + jnp.concatenate(dv_rows, axis=0)
                db = qq * dq - kk * dk + jnp.where(row == C - 1, v["extra"], 0.0)
                dlf = jnp.dot(tri_up, db, preferred_element_type=F32, precision=lax.Precision.HIGHEST)
                df = dlf / v["f"] - dk
                sig, sq, qz = v["sig"], v["sq"], v["qz"]
                dfz_ref[rows, cols] = (df * (1.0 - v["lbv"]) * sig * (1.0 - sig)).astype(BF16)
                dlb_ref[:, cols] += jnp.sum(df * (1.0 - sig), axis=0, keepdims=True)
                dqz_ref[rows, cols] = (dq * (sq * (1.0 + qz * (1.0 - sq)))).astype(BF16)
                div_ref[rows, cols] = dv.astype(BF16)
            return carry

        lax.fori_loop(0, ncb, chunk, 0)

    def col(k):
        return pl.BlockSpec((tb, hb * HEAD), lambda h, i: (nt - 1 - i, base + k * nh + h))

    ocol = pl.BlockSpec((tb, hb * HEAD), lambda h, i: (nt - 1 - i, h))
    vec = pl.BlockSpec((1, hb * HEAD), lambda h, i: (0, h))
    outs = pl.pallas_call(
        body, name=name, grid=(nh, nt),
        in_specs=[col(0), col(1), col(2), col(3),
                  pl.BlockSpec((tb, hb * HEAD), lambda h, i: (nt - 1 - i, dcat_cols - nh + h)),
                  pl.BlockSpec((tb, hb * HEAD), lambda h, i: (nt - 1 - i, h)),
                  pl.BlockSpec((hb, ncb, HEAD, HEAD), lambda h, i: (h, nt - 1 - i, 0, 0)), vec, vec,
                  pl.BlockSpec(memory_space=pl.ANY)],
        out_specs=[ocol, ocol, ocol, ocol, vec, vec],
        out_shape=[jax.ShapeDtypeStruct((S, W), BF16)] * 4 + [jax.ShapeDtypeStruct((1, W), F32)] * 2,
        scratch_shapes=[pltpu.VMEM((hb, HEAD, HEAD), F32)],
        compiler_params=_cp(("parallel", "arbitrary")))(proj, proj, proj, proj, dcat, o_raw, states,
                                                        lb.reshape(1, W), gain.reshape(1, W), after)
    return outs


def _lb_all(lb_logits):
    p = jax.nn.softmax(lb_logits.astype(F32), axis=0)
    return jnp.cumsum(p, axis=0) - p


def _local_step(x, mem, target, rel_bias, lb_logits, gains, hg_norm, comm):
    S, D = x.shape
    L = gains.shape[0]
    H = (D // 2) // HEAD
    lb = _lb_all(lb_logits)
    tables = [_branch_tables(r) for _, r in BRANCHES]
    onehots = [jnp.where(ok[None], jax.nn.one_hot(bk, REL_BUCKETS, dtype=F32, axis=0), 0.0)
               .reshape(REL_BUCKETS, 2 * Q_BLOCK * Q_BLOCK) for bk, ok in tables]
    bias = [jnp.where(ok[None], jnp.dot(rel_bias.astype(F32).T, oh, precision=lax.Precision.HIGHEST)
                      .reshape(H, Q_BLOCK, 2 * Q_BLOCK), NEG_INF) for oh, (_, ok) in zip(onehots, tables)]

    saved = []
    _, h = _norm_fwd(x, None, None, gains[0, 0], "norm_first")
    xl = x
    for l in range(L):
        g = gains[l]
        w_in, token = comm.weights(l, xl)
        proj = _mm_nn(h, w_in, F32, "mm_in", after=token)
        outs, lses = [], []
        for gi, (_, r) in enumerate(BRANCHES):
            o, ls = _attn_branch_fwd(proj, bias[gi], r, H, f"attn_fwd_d{r}")
            outs.append(o)
            lses.append(ls)
        cat, o_attn, lse_tot = _attn_merge(outs, lses, D, "attn_merge")
        cat, o_raw, states = _hg_fwd(proj, cat, lb[l], hg_norm[l], H, "hgrn_fwd")
        w = dict(comm.rest(l, cat), **{"in": w_in})
        mix = _mm_nn(cat, w["out"], F32, "mm_out")
        x1, hc = _norm_fwd(xl, mix, g[1], g[2], "norm_post_pre")
        _, mn = _norm_fwd(mem, None, None, g[3], "norm_mem")
        cq = _mm_nn(hc, w["cq"], BF16, "mm_cq")
        ckv = _mm_nn(mn, w["ckv"], BF16, "mm_ckv")
        co = _cross_fwd(cq, ckv, "cross_fwd")
        cop = _mm_nn(co, w["co"], F32, "mm_co")
        x2, hf = _norm_fwd(x1, cop, g[4], g[5], "norm_post_pre")
        gu, act = _mm_gate_up(hf, w["gu"], "mm_gu")
        y = _mm_nn(act, w["down"], F32, "mm_down")
        g_next = gains[l + 1, 0] if l + 1 < L else None
        x3, h_next = _norm_fwd(x2, y, g[6], g_next, "norm_post_pre" if l + 1 < L else "norm_post")
        saved.append(dict(x0=xl, h0=h, proj=proj, cat=cat, o_attn=o_attn, lse_tot=lse_tot, o_raw=o_raw, states=states,
                          mix=mix, x1=x1, hc=hc, mn=mn, cq=cq, ckv=ckv, co=co, cop=cop, x2=x2, hf=hf, gu=gu, act=act,
                          y=y, x3=x3, w=w))
        comm.layer_done(l, x3)
        xl, h = x3, h_next

    loss, dres = _loss_and_grad(xl, target)

    dh_next = bwd_token = None
    d_gains = [[None] * 7 for _ in range(L)]
    d_lb = [None] * L
    d_hg = [None] * L
    d_bias = [jnp.zeros((H, Q_BLOCK, 2 * Q_BLOCK), F32) for _ in BRANCHES]
    for l in reversed(range(L)):
        g, sv = gains[l], saved[l]
        w = sv["w"]
        g_next = gains[l + 1, 0] if l + 1 < L else None
        t3, dy, dg_next, dg6 = _norm_bwd(dres, dh_next, sv["x3"], sv["y"], g_next, g[6], "norm_bwd")
        if l + 1 < L:
            d_gains[l + 1][0] = dg_next
        d_gains[l][6] = dg6
        gw = {}
        gw["down"] = _mm_tn(sv["act"], dy, 1, BF16, "mm_dw_down")
        dgu = _mm_dx_down_swiglu(dy, w["down"], sv["gu"], "mm_dx_down", after=bwd_token)
        gw["gu"] = _mm_tn(sv["hf"], dgu, w["gu"].shape[0], BF16, "mm_dw_gu")
        dhf = _mm_nt(dgu, w["gu"], F32, "mm_dx_gu")
        t2, dcop, d_gains[l][5], d_gains[l][4] = _norm_bwd(t3, dhf, sv["x2"], sv["cop"], g[5], g[4], "norm_bwd")
        gw["co"] = _mm_tn(sv["co"], dcop, 1, BF16, "mm_dw_sq")
        dco = _mm_nt(dcop, w["co"], F32, "mm_dx_sq")
        dcq, dckv = _cross_bwd(sv["cq"], sv["ckv"], dco, "cross_bwd")
        gw["cq"] = _mm_tn(sv["hc"], dcq, 1, BF16, "mm_dw_sq")
        dhc = _mm_nt(dcq, w["cq"], F32, "mm_dx_sq")
        gw["ckv"] = _mm_tn(sv["mn"], dckv, w["ckv"].shape[0], BF16, "mm_dw_ckv")
        dmn = _mm_nt(dckv, w["ckv"], F32, "mm_dx_ckv")
        _, _, d_gains[l][3], _ = _norm_bwd(None, dmn, mem, None, g[3], None, "norm_bwd_mem")
        t1, dmix, d_gains[l][2], d_gains[l][1] = _norm_bwd(t2, dhc, sv["x1"], sv["mix"], g[2], g[1], "norm_bwd")
        gw["out"] = _mm_tn(sv["cat"], dmix, 1, BF16, "mm_dw_sq")
        dcat = _mm_nt(dmix, w["out"], F32, "mm_dx_sq")
        delta = _attn_delta(dcat, sv["o_attn"], H, "attn_delta", after=comm.early_grads(l, gw))
        dqkv = []
        for gi, (_, r) in enumerate(BRANCHES):
            dq, dk, dv, db = _attn_branch_bwd(sv["proj"], dcat, sv["lse_tot"], delta, bias[gi], r, H, f"attn_bwd_d{r}")
            dqkv.append((dq, dk, dv))
            d_bias[gi] = d_bias[gi] + db
        dfz, div, dqz, dgz, dlb_l, dhg_l = _hg_bwd(sv["proj"], dcat, sv["o_raw"], sv["states"], lb[l], hg_norm[l],
                                                   delta, H, "hgrn_bwd")
        d_lb[l], d_hg[l] = dlb_l[0], dhg_l[0]
        dproj = _assemble_dproj(dqkv, [dfz, div, dqz, dgz], "assemble_dproj")
        gw["in"] = _mm_tn(sv["h0"], dproj, w["in"].shape[0], BF16, "mm_dw_in")
        dh_next = _mm_nt(dproj, w["in"], F32, "mm_dx_in")
        dres = t1
        bwd_token = comm.grads(l, gw, gw["in"])
    grad_x, _, d_gains[0][0], _ = _norm_bwd(dres, dh_next, x, None, gains[0, 0], None, "norm_bwd_first")

    d_rel_bias = _bias_grad(d_bias, onehots, H).T
    d_gains = jnp.stack([jnp.stack(row) for row in d_gains])
    return loss, grad_x, d_rel_bias, jnp.stack(d_lb), jnp.stack(d_hg), d_gains


def _place():
    return lax.axis_index("x"), lax.axis_index("y"), lax.axis_index("c")


def _all_gather(shards, name):
    n = len(shards)
    HBM = pl.BlockSpec(memory_space=pltpu.HBM)

    def body(*refs):
        ins, outs = refs[:n], refs[n:2 * n]
        send_sems, recv_sems, local_sems = refs[2 * n:]
        x, y, c = _place()
        me, sibling = (x, y, c), (x, y, 1 - c)
        chips = [(1 - x, y), (x, 1 - y), (1 - x, 1 - y)]

        def copy(a, k, block, to, own=False):
            dst = outs[a].at[4 * block[0] + 2 * block[1] + block[2]]
            return pltpu.make_async_remote_copy(
                src_ref=ins[a] if own else dst, dst_ref=dst, send_sem=send_sems.at[7 * a + k],
                recv_sem=recv_sems.at[7 * a + k], device_id=to, device_id_type=MESH)

        mine = [pltpu.make_async_copy(ins[a], outs[a].at[4 * x + 2 * y + c], local_sems.at[a]) for a in range(n)]
        for cp in mine:
            cp.start()
        first = []
        for a in range(n):
            first.append(copy(a, 0, me, sibling, own=True))
            first += [copy(a, 1 + j, me, (*chip, c), own=True) for j, chip in enumerate(chips)]
        for cp in first:
            cp.start()
        passed = []
        for j, chip in enumerate(chips):
            for a in range(n):
                copy(a, 1 + j, (*chip, c), me).wait_recv()
                fwd = copy(a, 4 + j, (*chip, c), sibling)
                fwd.start()
                passed.append(fwd)
        for a in range(n):
            copy(a, 0, sibling, me).wait_recv()
            for j, chip in enumerate(chips):
                copy(a, 4 + j, (*chip, 1 - c), me).wait_recv()
        for cp in first + passed:
            cp.wait_send()
        for cp in mine:
            cp.wait()

    return pl.pallas_call(
        body, name=name, in_specs=[HBM] * n, out_specs=[HBM] * n,
        out_shape=[jax.ShapeDtypeStruct((N_DEV,) + s.shape, s.dtype) for s in shards],
        scratch_shapes=[pltpu.SemaphoreType.DMA((7 * n,)), pltpu.SemaphoreType.DMA((7 * n,)),
                        pltpu.SemaphoreType.DMA((n,))],
    )(*shards)


def _exchange_sibling(grads, name):
    n = len(grads)
    HBM = pl.BlockSpec(memory_space=pltpu.HBM)

    def body(*refs):
        ins, outs = refs[:n], refs[n:2 * n]
        send_sems, recv_sems = refs[2 * n:]
        x, y, c = _place()
        sibling = (x, y, 1 - c)
        for a in range(n):
            for q in range(4):
                pltpu.make_async_remote_copy(
                    src_ref=ins[a].at[2 * q + 1 - c], dst_ref=outs[a].at[q], send_sem=send_sems.at[a],
                    recv_sem=recv_sems.at[a], device_id=sibling, device_id_type=MESH).start()
        for a in range(n):
            pltpu.make_async_remote_copy(
                src_ref=ins[a].at[pl.ds(0, 4)], dst_ref=outs[a], send_sem=send_sems.at[a], recv_sem=recv_sems.at[a],
                device_id=sibling, device_id_type=MESH).wait()

    return pl.pallas_call(
        body, name=name, in_specs=[HBM] * n, out_specs=[HBM] * n,
        out_shape=[jax.ShapeDtypeStruct((4,) + g.shape[1:], g.dtype) for g in grads],
        scratch_shapes=[pltpu.SemaphoreType.DMA((n,)), pltpu.SemaphoreType.DMA((n,))],
    )(*grads)


def _exchange_chips(parts, name):
    n = len(parts)
    HBM = pl.BlockSpec(memory_space=pltpu.HBM)

    def body(*refs):
        ins, outs = refs[:n], refs[n:2 * n]
        send_sems, recv_sems = refs[2 * n:]
        x, y, c = _place()
        chips = [(1 - x, y), (x, 1 - y), (1 - x, 1 - y)]
        copies = []
        for a in range(n):
            for j, chip in enumerate(chips):
                cp = pltpu.make_async_remote_copy(
                    src_ref=ins[a].at[2 * chip[0] + chip[1]], dst_ref=outs[a].at[j], send_sem=send_sems.at[3 * a + j],
                    recv_sem=recv_sems.at[3 * a + j], device_id=(*chip, c), device_id_type=MESH)
                cp.start()
                copies.append(cp)
        for cp in copies:
            cp.wait()

    return pl.pallas_call(
        body, name=name, in_specs=[HBM] * n, out_specs=[HBM] * n,
        out_shape=[jax.ShapeDtypeStruct((3,) + p.shape[1:], p.dtype) for p in parts],
        scratch_shapes=[pltpu.SemaphoreType.DMA((3 * n,)), pltpu.SemaphoreType.DMA((3 * n,))],
    )(*parts)


def _others(x, y, c):
    out = []
    for k in range(1, N_DEV):
        px = 1 - x if (k >> 2) & 1 else x
        py = 1 - y if (k >> 1) & 1 else y
        pc = 1 - c if k & 1 else c
        out.append(((px, py, pc), 4 * px + 2 * py + pc))
    return out


_HBM_SPEC = pl.BlockSpec(memory_space=pltpu.HBM)
_SEM_SPEC = pl.BlockSpec(memory_space=pltpu.SEMAPHORE)
_EFFECT = pltpu.SideEffectType.DATAFLOW_SIDE_EFFECTING


def _copies_start(srcs, lands, after, src_block, name):
    n = len(srcs)

    def body(*refs):
        src_refs, land_refs = refs[:n], refs[n:2 * n]
        send_sems, recv_sems = refs[2 * n + 1], refs[2 * n + 2]
        token = refs[-1]
        x, y, c = _place()
        mine = 4 * x + 2 * y + c
        for a in range(n):
            for k, (peer, block) in enumerate(_others(x, y, c)):
                pltpu.make_async_remote_copy(
                    src_ref=src_refs[a].at[block] if src_block else src_refs[a],
                    dst_ref=land_refs[a].at[k] if src_block else land_refs[a].at[mine],
                    send_sem=send_sems.at[7 * a + k], recv_sem=recv_sems.at[7 * a + k],
                    device_id=peer, device_id_type=MESH).start()
        token[...] = jnp.zeros_like(token)

    outs = pl.pallas_call(
        body, name=name,
        out_shape=(pltpu.SemaphoreType.DMA((7 * n,)), pltpu.SemaphoreType.DMA((7 * n,)),
                   *[pltpu.HBM(s.shape, s.dtype) for s in srcs], *[pltpu.HBM(t.shape, t.dtype) for t in lands],
                   jax.ShapeDtypeStruct((8, 128), F32)),
        in_specs=[_HBM_SPEC] * (2 * n) + [pl.BlockSpec(memory_space=pl.ANY)],
        out_specs=(_SEM_SPEC, _SEM_SPEC, *[_HBM_SPEC] * (2 * n), pl.BlockSpec(memory_space=pltpu.VMEM)),
        input_output_aliases={i: 2 + i for i in range(2 * n)},
        compiler_params=pltpu.CompilerParams(has_side_effects=_EFFECT),
    )(*[pltpu.with_memory_space_constraint(s, pltpu.HBM) for s in srcs],
      *[pltpu.with_memory_space_constraint(t, pltpu.HBM) for t in lands], after)
    return outs[0], outs[1], list(outs[2:2 + n]), list(outs[2 + n:2 + 2 * n]), outs[-1]


def _copies_wait(send_sems, recv_sems, srcs, lands, after, src_block, name):
    n = len(srcs)

    def body(*refs):
        src_refs, land_refs = refs[:n], refs[n:2 * n]
        send_sems, recv_sems = refs[2 * n], refs[2 * n + 1]
        x, y, c = _place()
        for a in range(n):
            for k, (peer, block) in enumerate(_others(x, y, c)):
                cp = pltpu.make_async_remote_copy(
                    src_ref=src_refs[a].at[block] if src_block else src_refs[a],
                    dst_ref=land_refs[a].at[k] if src_block else land_refs[a].at[block],
                    send_sem=send_sems.at[7 * a + k], recv_sem=recv_sems.at[7 * a + k],
                    device_id=peer, device_id_type=MESH)
                cp.wait_send()
                cp.wait_recv()

    outs = pl.pallas_call(
        body, name=name,
        out_shape=(*[pltpu.HBM(s.shape, s.dtype) for s in srcs], *[pltpu.HBM(t.shape, t.dtype) for t in lands]),
        in_specs=[_HBM_SPEC] * (2 * n) + [_SEM_SPEC, _SEM_SPEC, pl.BlockSpec(memory_space=pl.ANY)],
        out_specs=tuple([_HBM_SPEC] * (2 * n)),
        input_output_aliases={i: i for i in range(2 * n)},
        compiler_params=pltpu.CompilerParams(has_side_effects=_EFFECT),
    )(*srcs, *lands, send_sems, recv_sems, after)
    return list(outs[:n]), list(outs[n:])


def _pair_sum(grad, recv, c_idx, name):
    _, R, C = grad.shape
    tr = _tile(R, 512, 16)

    def body(c_ref, g_ref, p_ref, o_ref):
        del c_ref
        o_ref[...] = (g_ref[...].astype(F32) + p_ref[...].astype(F32)).astype(BF16)

    spec = pltpu.PrefetchScalarGridSpec(
        num_scalar_prefetch=1, grid=(4, R // tr),
        in_specs=[pl.BlockSpec((None, tr, C), lambda q, i, c: (2 * q + c[0], i, 0)),
                  pl.BlockSpec((None, tr, C), lambda q, i, c: (q, i, 0))],
        out_specs=pl.BlockSpec((None, tr, C), lambda q, i, c: (q, i, 0)))
    return pl.pallas_call(body, name=name, grid_spec=spec, out_shape=jax.ShapeDtypeStruct((4, R, C), BF16),
                          compiler_params=_cp(("parallel", "parallel")))(c_idx, grad, recv)


def _adam(w, g, m, v):
    m2 = ADAM_B1 * m + (1.0 - ADAM_B1) * g
    v2 = ADAM_B2 * v + (1.0 - ADAM_B2) * (g * g)
    m_hat = m2 / (1.0 - ADAM_B1 ** ADAM_STEP)
    v_hat = v2 / (1.0 - ADAM_B2 ** ADAM_STEP)
    return -ADAM_LR * (m_hat / (jnp.sqrt(v_hat) + ADAM_EPS) + ADAM_WD * w), m2, v2


def _reduce_update(part, recv, own_idx, w, m, v, layer, prev, name):
    L, R, C = w.shape
    nr = recv.shape[0]
    tr = _tile(R, 128, 16)
    has_prev = prev is not None

    def body(idx_ref, q_ref, *rest):
        del idx_ref
        r_refs, (w_ref, m_ref, v_ref) = rest[:nr], rest[nr:nr + 3]
        g_ref, d_ref, m2_ref, v2_ref = rest[-4:]
        g = q_ref[...].astype(F32)
        for r_ref in r_refs:
            g = g + r_ref[...].astype(F32)
        d, m2, v2 = _adam(w_ref[...], g, m_ref[...], v_ref[...])
        g_ref[...] = g
        d_ref[...] = d
        m2_ref[...] = m2
        v2_ref[...] = v2

    def rblk(j):
        return pl.BlockSpec((None, tr, C), lambda i, k: (j, i, 0))

    lay = pl.BlockSpec((None, tr, C), lambda i, k: (layer, i, 0))
    in_specs = [pl.BlockSpec((None, tr, C), lambda i, k: (k[0], i, 0))] + [rblk(j) for j in range(nr)] + [lay, lay, lay]
    ins = [part] + [recv] * nr + [w, m, v]
    aliases = {}
    if has_prev:
        in_specs += [pl.BlockSpec(memory_space=pl.ANY)] * 4
        ins += list(prev)
        aliases = {1 + len(ins) - 4 + t: t for t in range(4)}
    spec = pltpu.PrefetchScalarGridSpec(num_scalar_prefetch=1, grid=(R // tr,), in_specs=in_specs,
                                        out_specs=[lay, lay, lay, lay])
    return pl.pallas_call(body, name=name, grid_spec=spec, out_shape=[jax.ShapeDtypeStruct((L, R, C), F32)] * 4,
                          input_output_aliases=aliases, compiler_params=_cp(("parallel",)))(own_idx, *ins)


def _sum_devices(gathered):
    n, R, C = gathered.shape

    def body(g_ref, o_ref):
        acc = g_ref[0]
        for d in range(1, n):
            acc = acc + g_ref[d]
        o_ref[...] = acc

    return pl.pallas_call(body, name="sum_devices", out_shape=jax.ShapeDtypeStruct((R, C), F32))(gathered)


def _adam_small(w, g, m, v, name):
    shape = w.shape
    two = (-1, shape[-1])

    def body(w_ref, g_ref, m_ref, v_ref, d_ref, m2_ref, v2_ref):
        d, m2, v2 = _adam(w_ref[...], g_ref[...], m_ref[...], v_ref[...])
        d_ref[...] = d
        m2_ref[...] = m2
        v2_ref[...] = v2

    outs = pl.pallas_call(body, name=name, out_shape=[jax.ShapeDtypeStruct(w.reshape(two).shape, F32)] * 3)(
        w.reshape(two), g.reshape(two), m.reshape(two), v.reshape(two))
    return [o.reshape(shape) for o in outs]


_SHARDED = ("in", "out", "cq", "ckv", "co", "gu", "down")
_COLUMN_SHARDED = ("in", "ckv", "gu")


def kernel(x, mem, rel_bias, lb_logits, norm_gains, w_in, hg_norm, w_out, w_cq, w_ckv, w_co, w_gate_up, w_down, loss_target, m_rel_bias, m_lb_logits, m_norm_gains, m_w_in, m_hg_norm, m_w_out, m_w_cq, m_w_ckv, m_w_co, m_w_gate_up, m_w_down, v_rel_bias, v_lb_logits, v_norm_gains, v_w_in, v_hg_norm, v_w_out, v_w_cq, v_w_ckv, v_w_co, v_w_gate_up, v_w_down):
    L = w_in.shape[0]
    D = x.shape[-1]
    ws = dict(zip(_SHARDED, (w_in, w_out, w_cq, w_ckv, w_co, w_gate_up, w_down)))
    ms = dict(zip(_SHARDED, (m_w_in, m_w_out, m_w_cq, m_w_ckv, m_w_co, m_w_gate_up, m_w_down)))
    vs = dict(zip(_SHARDED, (v_w_in, v_w_out, v_w_cq, v_w_ckv, v_w_co, v_w_gate_up, v_w_down)))
    px, py, pc = _place()
    dev = 4 * px + 2 * py + pc
    c_idx = jnp.reshape(pc, (1,)).astype(jnp.int32)
    chip_idx = jnp.reshape(2 * px + py, (1,)).astype(jnp.int32)
    dev_idx = jnp.reshape(dev, (1,)).astype(jnp.int32)

    rest_keys = _SHARDED[1:]

    def shards_of(l, keys):
        return [ws[k][l].astype(BF16) for k in keys]

    def as_weights(keys, gathered):
        return {k: g if k in _COLUMN_SHARDED else g.reshape(1, N_DEV * g.shape[1], g.shape[2])
                for k, g in zip(keys, gathered)}

    def start_gather(shards, after, name):
        lands = [lax.empty((N_DEV,) + s.shape, s.dtype) for s in shards]
        return _copies_start(shards, lands, after, False, name)

    def wait_gather(pending, keys, after, name):
        send_sems, recv_sems, shards, lands = pending
        shards, lands = _copies_wait(send_sems, recv_sems, shards, lands, after, False, name)
        full = [lax.dynamic_update_slice_in_dim(t, s[None], dev, axis=0) for s, t in zip(shards, lands)]
        return as_weights(keys, full)

    def full_grads(gw, keys):
        return [gw[k].reshape((N_DEV,) + ws[k].shape[1:]) for k in keys]

    def start_scatter(full, after, name):
        lands = [lax.empty((N_DEV - 1,) + g.shape[1:], g.dtype) for g in full]
        return _copies_start(full, lands, after, True, name)

    class Comm:
        def __init__(self):
            first, gains = _all_gather(shards_of(0, _SHARDED[:1]) + [norm_gains], "gather_first")
            self.gains = jnp.transpose(gains, (1, 2, 0, 3)).reshape(L, 7, D)
            self.ready = {0: as_weights(_SHARDED[:1], [first])}
            *self.rest0, self.order = start_gather(shards_of(0, rest_keys), first, "gather_start_rest")
            self.gather = None
            self.scatter = None
            self.early = None
            self.results = {k: None for k in _SHARDED}

        def weights(self, l, x_in):
            del x_in
            token = None
            if l + 1 < L:
                after = self.order if l == 0 else self.ready[l]["in"]
                *self.gather, token = start_gather(shards_of(l + 1, _SHARDED), after, "gather_start")
            return self.ready[l]["in"], token

        def rest(self, l, after):
            if l == 0:
                self.ready[0].update(wait_gather(self.rest0, rest_keys, after, "gather_wait_rest"))
            mine = self.ready.pop(l)
            return {k: mine[k] for k in rest_keys}

        def layer_done(self, l, x_out):
            if l + 1 < L:
                self.ready[l + 1] = wait_gather(self.gather, _SHARDED, x_out, "gather_wait")

        def update(self, l, keys, parts, recvs, own_idx):
            for k, part, recv in zip(keys, parts, recvs):
                self.results[k] = _reduce_update(part, recv, own_idx, ws[k], ms[k], vs[k], l, self.results[k],
                                                 "reduce_update")

        def finish_scatter(self, after):
            lp, send_sems, recv_sems, srcs, lands = self.scatter
            srcs, lands = _copies_wait(send_sems, recv_sems, srcs, lands, after, True, "scatter_wait")
            self.update(lp, _SHARDED, srcs, lands, dev_idx)
            self.scatter = None
            return lands[0]

        def early_grads(self, l, gw):
            if l > 0 or L == 1:
                return None
            order = self.finish_scatter(gw["out"])
            *self.early, token = start_scatter(full_grads(gw, rest_keys), order, "scatter_start_rest")
            return token

        def grads(self, l, gw, after):
            if l > 0:
                order = self.finish_scatter(after) if self.scatter is not None else dev_idx
                send_sems, recv_sems, srcs, lands, token = start_scatter(full_grads(gw, _SHARDED), order,
                                                                         "scatter_start")
                self.scatter = (l, send_sems, recv_sems, srcs, lands)
                return token
            last_keys = _SHARDED
            if self.early is not None:
                send_sems, recv_sems, srcs, lands = self.early
                srcs, lands = _copies_wait(send_sems, recv_sems, srcs, lands, after, True, "scatter_wait_rest")
                self.update(0, rest_keys, srcs, lands, dev_idx)
                last_keys = _SHARDED[:1]
            full = full_grads(gw, last_keys)
            from_sibling = _exchange_sibling(full, "scatter_sibling")
            parts = [_pair_sum(g, r, c_idx, "pair_sum") for g, r in zip(full, from_sibling)]
            self.update(0, last_keys, parts, _exchange_chips(parts, "scatter_chips"), chip_idx)
            return None

    comm = Comm()
    loss, grad_x, d_rel_bias, d_lb, d_hg, d_gains = _local_step(
        x[0], mem[0], loss_target[0], rel_bias, lb_logits, comm.gains, hg_norm, comm)
    results = comm.results

    pieces = [jnp.reshape(loss, (1,)), d_rel_bias.reshape(-1), d_lb.reshape(-1), d_hg.reshape(-1), d_gains.reshape(-1)]
    sizes = [p.shape[0] for p in pieces]
    total = sum(sizes)
    rows = -(-total // 1024) * 8
    pack = jnp.pad(jnp.concatenate(pieces), (0, rows * 128 - total)).reshape(rows, 128)
    summed = _sum_devices(_all_gather([pack], "gather_small")[0]).reshape(-1)
    offs = [sum(sizes[:i]) for i in range(len(sizes))]
    loss = summed[0]
    g_rel_bias = summed[offs[1]:offs[1] + sizes[1]].reshape(rel_bias.shape)
    g_lb_all = summed[offs[2]:offs[2] + sizes[2]].reshape(lb_logits.shape)
    g_hg = summed[offs[3]:offs[3] + sizes[3]].reshape(hg_norm.shape)
    g_gains_full = summed[offs[4]:offs[4] + sizes[4]].reshape(L, 7, D)
    g_gains = lax.dynamic_slice_in_dim(g_gains_full, dev * (D // N_DEV), D // N_DEV, axis=2)
    g_lb = jax.vjp(_lb_all, lb_logits)[1](g_lb_all)[0]

    small = [(rel_bias, g_rel_bias, m_rel_bias, v_rel_bias), (lb_logits, g_lb, m_lb_logits, v_lb_logits),
             (norm_gains, g_gains, m_norm_gains, v_norm_gains), (hg_norm, g_hg, m_hg_norm, v_hg_norm)]
    upd = [_adam_small(w, g, m, v, "adam_small") for w, g, m, v in small]

    def ordered(small_vals, big_vals):
        return [small_vals[0], small_vals[1], small_vals[2], big_vals["in"], small_vals[3], big_vals["out"],
                big_vals["cq"], big_vals["ckv"], big_vals["co"], big_vals["gu"], big_vals["down"]]

    grads = ordered([s[1] for s in small], {k: results[k][0] for k in _SHARDED})
    deltas = ordered([u[0] for u in upd], {k: results[k][1] for k in _SHARDED})
    new_m = ordered([u[1] for u in upd], {k: results[k][2] for k in _SHARDED})
    new_v = ordered([u[2] for u in upd], {k: results[k][3] for k in _SHARDED})
    return (loss, grad_x[None], *grads, *deltas, *new_m, *new_v)
```

```python
import functools
import math

import jax
import jax.numpy as jnp
from jax import lax
from jax.experimental import pallas as pl
from jax.experimental.pallas import tpu as pltpu

F32 = jnp.float32
BF16 = jnp.bfloat16
MESH = pl.DeviceIdType.MESH

N_DEV = 8
HEAD = 128
CROSS_HEADS = 4
Q_BLOCK = 128
BRANCHES = ((128, 1), (512, 4), (2048, 16))
REL_BUCKETS = 32
REL_MAX_DIST = 2048
HG_CHUNK = 16
HG_HEADS_PER_STEP = 8
ATTN_PAIRS_PER_STAGE = 8
RMS_EPS = 1e-6
NEG_INF = -1e30
ADAM_LR, ADAM_B1, ADAM_B2, ADAM_EPS, ADAM_WD, ADAM_STEP = 0.001, 0.9, 0.999, 1e-08, 0.01, 10
VMEM_LIMIT = 48 * 1024 * 1024
MXU_WIDTH = 256
NT_K_PER_STEP = 3584


def _tile(n, target, mult):
    best = None
    t = mult
    while t <= min(n, target):
        if n % t == 0:
            best = t
        t += mult
    return best if best is not None else n


def _cp(sem, vmem=VMEM_LIMIT):
    return pltpu.CompilerParams(dimension_semantics=sem, vmem_limit_bytes=vmem)


def _mm_nn(a, b3, out_dtype, name, after=None):
    M, K = a.shape
    J, K2, Nb = b3.shape
    assert K == K2
    tm, tn, tk = _tile(M, 1024, 8), _tile(Nb, 1408, 128), _tile(K, 2816, 128)
    nn, nk = Nb // tn, K // tk
    extra = [] if after is None else [after]
    jb = 2 if nn == 1 and tn % MXU_WIDTH and J % 2 == 0 else 1

    def body(a_ref, b_ref, *rest):
        o_ref, acc = rest[len(extra)], rest[len(extra) + 1:]
        bv = b_ref[0] if jb == 1 else jnp.concatenate([b_ref[j] for j in range(jb)], axis=1)
        prod = jnp.dot(a_ref[...].astype(BF16), bv.astype(BF16), preferred_element_type=F32)
        if nk == 1:
            o_ref[...] = prod.astype(o_ref.dtype)
        else:
            k = pl.program_id(2)

            @pl.when(k == 0)
            def _():
                acc[0][...] = prod

            @pl.when(k > 0)
            def _():
                acc[0][...] += prod

            @pl.when(k == nk - 1)
            def _():
                o_ref[...] = acc[0][...].astype(o_ref.dtype)

    return pl.pallas_call(
        body, name=name, grid=(M // tm, J * nn // jb, nk),
        in_specs=[pl.BlockSpec((tm, tk), lambda i, n, k: (i, k)),
                  pl.BlockSpec((jb, tk, tn), lambda i, n, k: (n // nn, k, n % nn))]
                 + [pl.BlockSpec(memory_space=pl.ANY)] * len(extra),
        out_specs=pl.BlockSpec((tm, jb * tn), lambda i, n, k: (i, n)),
        out_shape=jax.ShapeDtypeStruct((M, J * Nb), out_dtype),
        scratch_shapes=[] if nk == 1 else [pltpu.VMEM((tm, jb * tn), F32)],
        compiler_params=_cp(("parallel", "parallel", "arbitrary")),
    )(a, b3, *extra)


def _mm_nt(a, b3, out_dtype, name, after=None):
    extra = [] if after is None else [after]
    halves = a.ndim == 3
    M, Kt = (a.shape[1], 2 * a.shape[2]) if halves else a.shape
    J, N, Kb = b3.shape
    assert Kt == J * Kb
    tm, tn = _tile(M, 1024, 8), _tile(N, 1408, 128)
    tk = _tile(math.gcd(Kb, Kt // 2) if halves else Kb, 2048, 128)
    nkb = Kb // tk
    groups = J // 2 if halves and J > 1 else J
    jb = max(j for j in range(1, groups + 1) if groups % j == 0 and j * Kb <= NT_K_PER_STEP) if nkb == 1 else 1
    nk = (J // jb) * nkb

    def body(a_ref, b_ref, *rest):
        o_ref, acc = rest[len(extra)], rest[len(extra) + 1:]
        bv = b_ref[0] if jb == 1 else jnp.concatenate([b_ref[j] for j in range(jb)], axis=1)
        prod = lax.dot_general(a_ref[...].astype(BF16), bv.astype(BF16), (((1,), (1,)), ((), ())),
                               preferred_element_type=F32)
        if nk == 1:
            o_ref[...] = prod.astype(o_ref.dtype)
        else:
            k = pl.program_id(2)

            @pl.when(k == 0)
            def _():
                acc[0][...] = prod

            @pl.when(k > 0)
            def _():
                acc[0][...] += prod

            @pl.when(k == nk - 1)
            def _():
                o_ref[...] = acc[0][...].astype(o_ref.dtype)

    return pl.pallas_call(
        body, name=name, grid=(M // tm, N // tn, nk),
        in_specs=[pl.BlockSpec((None, tm, jb * tk), lambda i, n, k: (k // (nk // 2), i, k % (nk // 2))) if halves
                  else pl.BlockSpec((tm, jb * tk), lambda i, n, k: (i, k)),
                  pl.BlockSpec((jb, tn, tk), lambda i, n, k: (k // nkb, n, k % nkb))]
                 + [pl.BlockSpec(memory_space=pl.ANY)] * len(extra),
        out_specs=pl.BlockSpec((tm, tn), lambda i, n, k: (i, n)),
        out_shape=jax.ShapeDtypeStruct((M, N), out_dtype),
        scratch_shapes=[] if nk == 1 else [pltpu.VMEM((tm, tn), F32)],
        compiler_params=_cp(("parallel", "parallel", "arbitrary")),
    )(a, b3, *extra)


def _mm_tn(a, b, J, out_dtype, name):
    S, M = a.shape
    halves = b.ndim == 3
    S2, Nt = (b.shape[1], 2 * b.shape[2]) if halves else b.shape
    assert S == S2 and Nt % J == 0
    Nb = Nt // J
    tm, tk = _tile(M, 1408, 128), _tile(S, 2048, 8)
    tn = _tile(math.gcd(Nb, Nt // 2) if halves else Nb, 1408, 128)
    nn, nk = Nb // tn, S // tk
    jb = 2 if nn == 1 and tn % MXU_WIDTH and J % (4 if halves else 2) == 0 else 1
    if jb * tn > 2048:
        tm = _tile(M, 512, 128)
    per_half = J * nn // 2 // jb
    b_spec = (pl.BlockSpec((None, tk, jb * tn), lambda i, n, k: (n // per_half, k, n % per_half)) if halves
              else pl.BlockSpec((tk, jb * tn), lambda i, n, k: (k, n)))

    def body(a_ref, b_ref, o_ref, *acc):
        prod = lax.dot_general(a_ref[...].astype(BF16), b_ref[...].astype(BF16), (((0,), (0,)), ((), ())),
                               preferred_element_type=F32)

        def write(val):
            for j in range(jb):
                o_ref[j] = val[:, j * tn:(j + 1) * tn].astype(o_ref.dtype)

        if nk == 1:
            write(prod)
        else:
            k = pl.program_id(2)

            @pl.when(k == 0)
            def _():
                acc[0][...] = prod

            @pl.when(k > 0)
            def _():
                acc[0][...] += prod

            @pl.when(k == nk - 1)
            def _():
                write(acc[0][...])

    return pl.pallas_call(
        body, name=name, grid=(M // tm, J * nn // jb, nk),
        in_specs=[pl.BlockSpec((tk, tm), lambda i, n, k: (k, i)), b_spec],
        out_specs=pl.BlockSpec((jb, tm, tn), lambda i, n, k: (n // nn, i, n % nn)),
        out_shape=jax.ShapeDtypeStruct((J, M, Nb), out_dtype),
        scratch_shapes=[] if nk == 1 else [pltpu.VMEM((tm, jb * tn), F32)],
        compiler_params=_cp(("parallel", "parallel", "arbitrary")),
    )(a, b)


def _rms_scale(v):
    return lax.rsqrt(jnp.mean(v * v, axis=-1, keepdims=True) + RMS_EPS)


def _rms_bwd(dy, v, g):
    r = _rms_scale(v)
    vh = v * r
    u = dy * g
    dv = r * (u - vh * jnp.mean(u * vh, axis=-1, keepdims=True))
    return dv, dy * vh


def _fold8(t):
    R, D = t.shape
    return jnp.sum(t.reshape(R // 8, 8, D), axis=0)


def _norm_fwd(x, z, g_post, g_pre, name):
    S, D = x.shape
    tr = _tile(S, 256, 8)
    has_z, has_pre = z is not None, g_pre is not None

    def body(*refs):
        it = iter(refs)
        x_ref = next(it)
        z_ref = next(it) if has_z else None
        gpost_ref = next(it) if has_z else None
        gpre_ref = next(it) if has_pre else None
        xn_ref = next(it) if has_z else None
        h_ref = next(it) if has_pre else None
        xn = x_ref[...]
        if has_z:
            zz = z_ref[...]
            xn = xn + zz * _rms_scale(zz) * gpost_ref[...]
            xn_ref[...] = xn
        if has_pre:
            h_ref[...] = (xn * _rms_scale(xn) * gpre_ref[...]).astype(BF16)

    row = pl.BlockSpec((tr, D), lambda i: (i, 0))
    gain = pl.BlockSpec((1, D), lambda i: (0, 0))
    ins, specs, outs, ospecs = [x], [row], [], []
    if has_z:
        ins += [z, g_post.reshape(1, D)]
        specs += [row, gain]
        outs.append(jax.ShapeDtypeStruct((S, D), F32))
        ospecs.append(row)
    if has_pre:
        ins.append(g_pre.reshape(1, D))
        specs.append(gain)
        outs.append(jax.ShapeDtypeStruct((S, D), BF16))
        ospecs.append(row)
    res = pl.pallas_call(body, name=name, grid=(S // tr,), in_specs=specs, out_specs=ospecs, out_shape=outs,
                         compiler_params=_cp(("parallel",)))(*ins)
    res = list(res)
    xn = res.pop(0) if has_z else x
    h = res.pop(0) if has_pre else None
    return xn, h


def _norm_bwd(dres, dh, xn, z, g_pre, g_post, name):
    S, D = xn.shape
    tr = _tile(S, 256, 8)
    has_res, has_pre, has_z = dres is not None, dh is not None, z is not None

    def body(*refs):
        it = iter(refs)
        dres_ref = next(it) if has_res else None
        dh_ref = next(it) if has_pre else None
        xn_ref = next(it) if has_pre else None
        gpre_ref = next(it) if has_pre else None
        z_ref = next(it) if has_z else None
        gpost_ref = next(it) if has_z else None
        t_ref = next(it)
        dz_ref = next(it) if has_z else None
        dgpre_ref = next(it) if has_pre else None
        dgpost_ref = next(it) if has_z else None
        first = pl.program_id(0) == 0
        t = dres_ref[...] if has_res else None
        if has_pre:
            dv, dg = _rms_bwd(dh_ref[...].astype(F32), xn_ref[...], gpre_ref[...])
            t = dv if t is None else t + dv
            part = _fold8(dg)

            @pl.when(first)
            def _():
                dgpre_ref[...] = part

            @pl.when(jnp.logical_not(first))
            def _():
                dgpre_ref[...] += part
        t_ref[...] = t
        if has_z:
            dv, dg = _rms_bwd(t, z_ref[...], gpost_ref[...])
            dz_ref[...] = dv.astype(BF16)
            part2 = _fold8(dg)

            @pl.when(first)
            def _():
                dgpost_ref[...] = part2

            @pl.when(jnp.logical_not(first))
            def _():
                dgpost_ref[...] += part2

    row = pl.BlockSpec((tr, D), lambda i: (i, 0))
    gain = pl.BlockSpec((1, D), lambda i: (0, 0))
    accs = pl.BlockSpec((8, D), lambda i: (0, 0))
    ins, specs = [], []
    if has_res:
        ins.append(dres)
        specs.append(row)
    if has_pre:
        ins += [dh, xn, g_pre.reshape(1, D)]
        specs += [row, row, gain]
    if has_z:
        ins += [z, g_post.reshape(1, D)]
        specs += [row, gain]
    outs, ospecs = [jax.ShapeDtypeStruct((S, D), F32)], [row]
    if has_z:
        outs.append(jax.ShapeDtypeStruct((S, D), BF16))
        ospecs.append(row)
    if has_pre:
        outs.append(jax.ShapeDtypeStruct((8, D), F32))
        ospecs.append(accs)
    if has_z:
        outs.append(jax.ShapeDtypeStruct((8, D), F32))
        ospecs.append(accs)
    res = list(pl.pallas_call(body, name=name, grid=(S // tr,), in_specs=specs, out_specs=ospecs, out_shape=outs,
                              compiler_params=_cp(("arbitrary",)))(*ins))
    t = res.pop(0)
    dz = res.pop(0) if has_z else None
    dgpre = jnp.sum(res.pop(0), axis=0) if has_pre else None
    dgpost = jnp.sum(res.pop(0), axis=0) if has_z else None
    return t, dz, dgpre, dgpost


def _loss_and_grad(y, target):
    S, D = y.shape
    tr = _tile(S, 256, 8)

    def body(y_ref, t_ref, dy_ref, l_ref):
        err = y_ref[...] - t_ref[...]
        dy_ref[...] = err * (1.0 / D)
        part = 0.5 * jnp.sum(jnp.mean(err * err, axis=-1, keepdims=True), axis=0, keepdims=True)
        first = pl.program_id(0) == 0

        @pl.when(first)
        def _():
            l_ref[...] = jnp.broadcast_to(part, l_ref.shape)

        @pl.when(jnp.logical_not(first))
        def _():
            l_ref[...] += jnp.broadcast_to(part, l_ref.shape)

    row = pl.BlockSpec((tr, D), lambda i: (i, 0))
    dy, l = pl.pallas_call(
        body, name="loss_grad", grid=(S // tr,), in_specs=[row, row],
        out_specs=[row, pl.BlockSpec((8, 128), lambda i: (0, 0))],
        out_shape=[jax.ShapeDtypeStruct((S, D), F32), jax.ShapeDtypeStruct((8, 128), F32)],
        compiler_params=_cp(("arbitrary",)))(y, target)
    return l[0, 0], dy


def _mm_gate_up(a, w3, name):
    S, D = a.shape
    J, D2, Nb = w3.shape
    F = J * Nb // 2
    assert D == D2
    tm, tn = _tile(S, 512, 8), _tile(math.gcd(Nb, F), 1408, 128)
    nn, nf = Nb // tn, F // tn

    def body(a_ref, wg_ref, wu_ref, gu_ref, act_ref):
        both = jnp.dot(a_ref[...], jnp.concatenate([wg_ref[...], wu_ref[...]], axis=1), preferred_element_type=F32)
        g, u = both[:, :tn], both[:, tn:]
        gu_ref[0] = g
        gu_ref[1] = u
        act_ref[...] = (g * jax.nn.sigmoid(g) * u).astype(BF16)

    return pl.pallas_call(
        body, name=name, grid=(nf, S // tm),
        in_specs=[pl.BlockSpec((tm, D), lambda n, i: (i, 0)),
                  pl.BlockSpec((None, D, tn), lambda n, i: (n // nn, 0, n % nn)),
                  pl.BlockSpec((None, D, tn), lambda n, i: ((n + nf) // nn, 0, (n + nf) % nn))],
        out_specs=[pl.BlockSpec((2, tm, tn), lambda n, i: (0, i, n)), pl.BlockSpec((tm, tn), lambda n, i: (i, n))],
        out_shape=[jax.ShapeDtypeStruct((2, S, F), F32), jax.ShapeDtypeStruct((S, F), BF16)],
        compiler_params=_cp(("parallel", "parallel")))(a, w3, w3)


def _mm_dx_down_swiglu(dy, wdown3, gu, name, after=None):
    S, D = dy.shape
    _, F, D2 = wdown3.shape
    assert D == D2
    tm, tn = _tile(S, 1024, 8), _tile(F, 512, 128)
    extra = [] if after is None else [after]

    def body(a_ref, b_ref, gu_ref, *rest):
        o_ref = rest[len(extra)]
        d = lax.dot_general(a_ref[...], b_ref[...], (((1,), (1,)), ((), ())), preferred_element_type=F32)
        g, u = gu_ref[0], gu_ref[1]
        sg = jax.nn.sigmoid(g)
        o_ref[0] = (d * u * (sg * (1.0 + g * (1.0 - sg)))).astype(BF16)
        o_ref[1] = (d * g * sg).astype(BF16)

    return pl.pallas_call(
        body, name=name, grid=(S // tm, F // tn),
        in_specs=[pl.BlockSpec((tm, D), lambda i, n: (i, 0)), pl.BlockSpec((None, tn, D), lambda i, n: (0, n, 0)),
                  pl.BlockSpec((2, tm, tn), lambda i, n: (0, i, n))] + [pl.BlockSpec(memory_space=pl.ANY)] * len(extra),
        out_specs=pl.BlockSpec((2, tm, tn), lambda i, n: (0, i, n)),
        out_shape=jax.ShapeDtypeStruct((2, S, F), BF16),
        compiler_params=_cp(("parallel", "parallel")))(dy, wdown3, gu, *extra)


def _cross_scores(q, k, scale):
    s = lax.dot_general(q, k, (((1,), (1,)), ((), ())), preferred_element_type=F32) * scale
    m = jnp.max(s, axis=-1, keepdims=True)
    e = jnp.exp(s - m)
    return e / jnp.sum(e, axis=-1, keepdims=True)


def _cross_fwd(cq, ckv, name):
    S, D = cq.shape
    Nm = ckv.shape[0]
    dh = D // CROSS_HEADS
    tq = _tile(S, 512, 8)
    scale = dh ** -0.5

    def body(q_ref, kv_ref, o_ref):
        for h in range(CROSS_HEADS):
            cols = slice(h * dh, (h + 1) * dh)
            p = _cross_scores(q_ref[:, cols], kv_ref[:, cols], scale)
            o_ref[:, cols] = jnp.dot(p.astype(BF16), kv_ref[:, D + h * dh:D + (h + 1) * dh],
                                     preferred_element_type=F32).astype(BF16)

    return pl.pallas_call(
        body, name=name, grid=(S // tq,),
        in_specs=[pl.BlockSpec((tq, D), lambda i: (i, 0)), pl.BlockSpec((Nm, 2 * D), lambda i: (0, 0))],
        out_specs=pl.BlockSpec((tq, D), lambda i: (i, 0)),
        out_shape=jax.ShapeDtypeStruct((S, D), BF16), compiler_params=_cp(("parallel",)))(cq, ckv)


def _cross_bwd(cq, ckv, do, name):
    S, D = cq.shape
    Nm = ckv.shape[0]
    dh = D // CROSS_HEADS
    tq = _tile(S, 512, 8)
    scale = dh ** -0.5

    def body(q_ref, kv_ref, do_ref, dq_ref, dkv_ref):
        first = pl.program_id(0) == 0

        @pl.when(first)
        def _():
            dkv_ref[...] = jnp.zeros_like(dkv_ref)

        for h in range(CROSS_HEADS):
            cols = slice(h * dh, (h + 1) * dh)
            vcols = slice(D + h * dh, D + (h + 1) * dh)
            q, k, v = q_ref[:, cols], kv_ref[:, cols], kv_ref[:, vcols]
            doh = do_ref[:, cols].astype(BF16)
            p = _cross_scores(q, k, scale)
            dp = lax.dot_general(doh, v, (((1,), (1,)), ((), ())), preferred_element_type=F32)
            ds = (p * (dp - jnp.sum(p * dp, axis=-1, keepdims=True)) * scale).astype(BF16)
            dq_ref[:, cols] = jnp.dot(ds, k, preferred_element_type=F32).astype(BF16)
            dkv_ref[:, cols] += lax.dot_general(ds, q, (((0,), (0,)), ((), ())), preferred_element_type=F32)
            dkv_ref[:, vcols] += lax.dot_general(p.astype(BF16), doh, (((0,), (0,)), ((), ())),
                                                 preferred_element_type=F32)

    return pl.pallas_call(
        body, name=name, grid=(S // tq,),
        in_specs=[pl.BlockSpec((tq, D), lambda i: (i, 0)), pl.BlockSpec((Nm, 2 * D), lambda i: (0, 0)),
                  pl.BlockSpec((tq, D), lambda i: (i, 0))],
        out_specs=[pl.BlockSpec((tq, D), lambda i: (i, 0)), pl.BlockSpec((Nm, 2 * D), lambda i: (0, 0))],
        out_shape=[jax.ShapeDtypeStruct((S, D), BF16), jax.ShapeDtypeStruct((Nm, 2 * D), F32)],
        compiler_params=_cp(("arbitrary",)))(cq, ckv, do)


def _rel_bucket(dist):
    max_exact = REL_BUCKETS // 2
    d_f = jnp.maximum(dist, 1).astype(F32)
    large = max_exact + (jnp.log(d_f / max_exact) / math.log(REL_MAX_DIST / max_exact)
                         * (REL_BUCKETS - max_exact)).astype(jnp.int32)
    large = jnp.minimum(large, REL_BUCKETS - 1)
    return jnp.where(dist < max_exact, dist, large)


def _branch_tables(dilation):
    qi = jnp.arange(Q_BLOCK)[:, None]
    kj = jnp.arange(2 * Q_BLOCK)[None, :]
    m = qi - kj + Q_BLOCK
    return _rel_bucket(jnp.maximum(m, 0) * dilation), (m >= 0) & (m <= Q_BLOCK)


def _attn_logits(q, kcat, bias, first_block, scale):
    s = lax.dot_general(q, kcat, (((1,), (1,)), ((), ())), preferred_element_type=F32) * scale + bias
    col = lax.broadcasted_iota(jnp.int32, s.shape, 1)
    return jnp.where(jnp.logical_and(first_block, col < Q_BLOCK), NEG_INF, s)


def _attn_branch_fwd(proj, bias, dilation, H, name):
    S, NC = proj.shape
    AW = H * HEAD
    r = dilation
    hb = H if r == 1 else 1
    G = Q_BLOCK * r
    ng, nh = S // G, H // hb
    scale = HEAD ** -0.5

    def body(q_ref, kp_ref, kc_ref, vp_ref, vc_ref, b_ref, o_ref, l_ref):
        first_block = pl.program_id(0) == 0
        h0 = pl.program_id(1) * hb
        pairs = [(pl.ds(rho, Q_BLOCK, stride=r) if r > 1 else slice(None), hh) for rho in range(r) for hh in range(hb)]
        for g0 in range(0, len(pairs), ATTN_PAIRS_PER_STAGE):
            group = pairs[g0:g0 + ATTN_PAIRS_PER_STAGE]
            sl = [(rows, slice(hh * HEAD, (hh + 1) * HEAD)) for rows, hh in group]
            q = [q_ref[rw, cl].astype(BF16) for rw, cl in sl]
            kcat = [jnp.concatenate([kp_ref[rw, cl], kc_ref[rw, cl]], axis=0).astype(BF16) for rw, cl in sl]
            vcat = [jnp.concatenate([vp_ref[rw, cl], vc_ref[rw, cl]], axis=0).astype(BF16) for rw, cl in sl]
            s = [_attn_logits(q[i], kcat[i], b_ref[h0 + hh], first_block, scale) for i, (_, hh) in enumerate(group)]
            m = [jnp.max(si, axis=-1, keepdims=True) for si in s]
            e = [jnp.exp(si - mi) for si, mi in zip(s, m)]
            den = [jnp.sum(ei, axis=-1, keepdims=True) for ei in e]
            o = [jnp.dot(ei.astype(BF16), vi, preferred_element_type=F32) for ei, vi in zip(e, vcat)]
            for i, (rw, cl) in enumerate(sl):
                o_ref[rw, cl] = o[i] / den[i]
                l_ref[rw, cl] = jnp.broadcast_to(m[i] + jnp.log(den[i]), (Q_BLOCK, HEAD))

    def blk(part, prev):
        if prev:
            return pl.BlockSpec((G, hb * HEAD), lambda n, h: (jnp.maximum(n - 1, 0), part * nh + h))
        return pl.BlockSpec((G, hb * HEAD), lambda n, h: (n, part * nh + h))

    out = pl.BlockSpec((G, hb * HEAD), lambda n, h: (n, h))
    return pl.pallas_call(
        body, name=name, grid=(ng, nh),
        in_specs=[blk(0, False), blk(1, True), blk(1, False), blk(2, True), blk(2, False),
                  pl.BlockSpec((H, Q_BLOCK, 2 * Q_BLOCK), lambda n, h: (0, 0, 0))],
        out_specs=[out, out],
        out_shape=[jax.ShapeDtypeStruct((S, AW), F32)] * 2,
        compiler_params=_cp(("parallel", "parallel")))(proj, proj, proj, proj, proj, bias)


def _attn_merge(outs, lses, width_out, name):
    S, AW = outs[0].shape
    tr = _tile(S, 256, 8)

    def body(o0, o1, o2, l0, l1, l2, cat_ref, om_ref, lt_ref):
        a, b, c = l0[...], l1[...], l2[...]
        m = jnp.maximum(jnp.maximum(a, b), c)
        ea, eb, ec = jnp.exp(a - m), jnp.exp(b - m), jnp.exp(c - m)
        tot = ea + eb + ec
        o = (ea * o0[...] + eb * o1[...] + ec * o2[...]) / tot
        om_ref[...] = o
        cat_ref[...] = o.astype(BF16)
        lt_ref[...] = m + jnp.log(tot)

    row = pl.BlockSpec((tr, AW), lambda i: (i, 0))
    return pl.pallas_call(
        body, name=name, grid=(S // tr,), in_specs=[row] * 6, out_specs=[row, row, row],
        out_shape=[jax.ShapeDtypeStruct((S, width_out), BF16), jax.ShapeDtypeStruct((S, AW), F32),
                   jax.ShapeDtypeStruct((S, AW), F32)],
        compiler_params=_cp(("parallel",)))(*outs, *lses)


def _attn_delta(dcat, o_attn, H, name, after=None):
    S, AW = o_attn.shape
    tr = _tile(S, 256, 8)
    extra = [] if after is None else [after]

    def body(d_ref, o_ref, *rest):
        out_ref = rest[len(extra)]
        for h in range(H):
            cols = slice(h * HEAD, (h + 1) * HEAD)
            out_ref[:, cols] = jnp.broadcast_to(
                jnp.sum(d_ref[:, cols] * o_ref[:, cols], axis=-1, keepdims=True), (tr, HEAD))

    row = pl.BlockSpec((tr, AW), lambda i: (i, 0))
    return pl.pallas_call(body, name=name, grid=(S // tr,),
                          in_specs=[row, row] + [pl.BlockSpec(memory_space=pl.ANY)] * len(extra), out_specs=row,
                          out_shape=jax.ShapeDtypeStruct((S, AW), F32),
                          compiler_params=_cp(("parallel",)))(dcat, o_attn, *extra)


def _attn_branch_bwd(proj, dcat, lse_tot, delta, bias, dilation, H, name):
    S, NC = proj.shape
    AW = H * HEAD
    r = dilation
    hb = H if r == 1 else 1
    G = Q_BLOCK * r
    ng, nh = S // G, H // hb
    scale = HEAD ** -0.5

    def body(q_ref, kp_ref, kc_ref, vp_ref, vc_ref, do_ref, l_ref, t_ref, b_ref,
             dq_ref, dk_ref, dv_ref, db_ref, ck_ref, cv_ref):
        hblk, n = pl.program_id(0), pl.program_id(1)
        h0 = hblk * hb

        @pl.when(jnp.logical_and(hblk == 0, n == 0))
        def _():
            db_ref[...] = jnp.zeros_like(db_ref)

        @pl.when(n == 0)
        def _():
            ck_ref[...] = jnp.zeros_like(ck_ref)
            cv_ref[...] = jnp.zeros_like(cv_ref)

        @pl.when(n < ng)
        def _():
            pairs = [(pl.ds(rho, Q_BLOCK, stride=r) if r > 1 else slice(None), hh)
                     for rho in range(r) for hh in range(hb)]
            for g0 in range(0, len(pairs), ATTN_PAIRS_PER_STAGE):
                group = pairs[g0:g0 + ATTN_PAIRS_PER_STAGE]
                idx = range(len(group))
                sl = [(rows, slice(hh * HEAD, (hh + 1) * HEAD)) for rows, hh in group]
                q = [q_ref[rw, cl].astype(BF16) for rw, cl in sl]
                kcat = [jnp.concatenate([kp_ref[rw, cl], kc_ref[rw, cl]], axis=0).astype(BF16) for rw, cl in sl]
                vcat = [jnp.concatenate([vp_ref[rw, cl], vc_ref[rw, cl]], axis=0).astype(BF16) for rw, cl in sl]
                doh = [do_ref[rw, cl].astype(BF16) for rw, cl in sl]
                s = [_attn_logits(q[i], kcat[i], b_ref[h0 + group[i][1]], n == 0, scale) for i in idx]
                dp = [lax.dot_general(doh[i], vcat[i], (((1,), (1,)), ((), ())), preferred_element_type=F32) for i in idx]
                p, ds = [], []
                for i, (rw, cl) in enumerate(sl):
                    lt, dl = l_ref[rw, cl], t_ref[rw, cl]
                    p.append(jnp.exp(s[i] - jnp.concatenate([lt, lt], axis=1)))
                    ds.append(p[i] * (dp[i] - jnp.concatenate([dl, dl], axis=1)))
                dsb = [(d * scale).astype(BF16) for d in ds]
                dq = [jnp.dot(dsb[i], kcat[i], preferred_element_type=F32) for i in idx]
                dkc = [lax.dot_general(dsb[i], q[i], (((0,), (0,)), ((), ())), preferred_element_type=F32) for i in idx]
                dvc = [lax.dot_general(p[i].astype(BF16), doh[i], (((0,), (0,)), ((), ())), preferred_element_type=F32)
                       for i in idx]
                for i, (rw, cl) in enumerate(sl):
                    dq_ref[rw, cl] = dq[i]
                    dk_ref[rw, cl] = ck_ref[rw, cl] + dkc[i][:Q_BLOCK]
                    dv_ref[rw, cl] = cv_ref[rw, cl] + dvc[i][:Q_BLOCK]
                    ck_ref[rw, cl] = dkc[i][Q_BLOCK:]
                    cv_ref[rw, cl] = dvc[i][Q_BLOCK:]
                for i in idx:
                    db_ref[h0 + group[i][1]] += ds[i]

        @pl.when(n == ng)
        def _():
            dk_ref[...] = ck_ref[...]
            dv_ref[...] = cv_ref[...]

    last = ng - 1

    def cur(part):
        return pl.BlockSpec((G, hb * HEAD), lambda h, n: (jnp.minimum(n, last), part * nh + h))

    def prev(part):
        return pl.BlockSpec((G, hb * HEAD), lambda h, n: (jnp.clip(n - 1, 0, last), part * nh + h))

    table = pl.BlockSpec((H, Q_BLOCK, 2 * Q_BLOCK), lambda h, n: (0, 0, 0))
    return pl.pallas_call(
        body, name=name, grid=(nh, ng + 1),
        in_specs=[cur(0), prev(1), cur(1), prev(2), cur(2), cur(0), cur(0), cur(0), table],
        out_specs=[cur(0), prev(0), prev(0), table],
        out_shape=[jax.ShapeDtypeStruct((S, AW), F32)] * 3 + [jax.ShapeDtypeStruct((H, Q_BLOCK, 2 * Q_BLOCK), F32)],
        scratch_shapes=[pltpu.VMEM((G, hb * HEAD), F32), pltpu.VMEM((G, hb * HEAD), F32)],
        compiler_params=_cp(("arbitrary", "arbitrary")))(proj, proj, proj, proj, proj, dcat, lse_tot, delta, bias)


def _bias_grad(dbs, onehots, H):
    def body(d0, d1, d2, e0, e1, e2, o_ref):
        acc = jnp.zeros((H, REL_BUCKETS), F32)
        for d, e in ((d0, e0), (d1, e1), (d2, e2)):
            acc = acc + lax.dot_general(d[...], e[...], (((1,), (1,)), ((), ())), preferred_element_type=F32,
                                        precision=lax.Precision.HIGHEST)
        o_ref[...] = acc

    flat = [d.reshape(H, 2 * Q_BLOCK * Q_BLOCK) for d in dbs]
    return pl.pallas_call(body, name="bias_grad", out_shape=jax.ShapeDtypeStruct((H, REL_BUCKETS), F32),
                          compiler_params=pltpu.CompilerParams(vmem_limit_bytes=VMEM_LIMIT))(*flat, *onehots)


def _assemble_dproj(dqkv, dhg, name):
    S, AW = dqkv[0][0].shape
    tr = _tile(S, 256, 8)
    ins = [dqkv[g][c] for g in range(3) for c in range(3)] + list(dhg)

    def body(*refs):
        o_ref = refs[-1]
        for c in range(3):
            o_ref[:, c * AW:(c + 1) * AW] = (refs[c][...] + refs[3 + c][...] + refs[6 + c][...]).astype(BF16)
        for k in range(4):
            o_ref[:, (3 + k) * AW:(4 + k) * AW] = refs[9 + k][...]

    row = pl.BlockSpec((tr, AW), lambda i: (i, 0))
    return pl.pallas_call(
        body, name=name, grid=(S // tr,), in_specs=[row] * 13,
        out_specs=pl.BlockSpec((tr, 7 * AW), lambda i: (i, 0)),
        out_shape=jax.ShapeDtypeStruct((S, 7 * AW), BF16),
        compiler_params=_cp(("parallel",)))(*ins)


def _tri(n, upper):
    r = lax.broadcasted_iota(jnp.int32, (n, n), 0)
    c = lax.broadcasted_iota(jnp.int32, (n, n), 1)
    return jnp.where(c >= r if upper else c <= r, 1.0, 0.0).astype(F32)


def _hg_gates(fz, qz, lb):
    sig = jax.nn.sigmoid(fz)
    f = lb + (1.0 - lb) * sig
    sq = jax.nn.sigmoid(qz)
    return sig, f, jnp.log(f), 1.0 - f, qz * sq, sq


def _hg_fwd(proj, cat, lb, gain, HH, name):
    S, NC = proj.shape
    W = HH * HEAD
    C = HG_CHUNK
    hb = min(HG_HEADS_PER_STEP, HH)
    nh = HH // hb
    tb = _tile(S, 2048 // hb, C)
    ncb = tb // C
    base = 3 * nh
    cat_cols = cat.shape[1] // (hb * HEAD)

    def body(fz_ref, iv_ref, qz_ref, gz_ref, lb_ref, g_ref, cat_in, cat_ref, o_ref, st_ref, state):
        del cat_in

        @pl.when(pl.program_id(1) == 0)
        def _():
            state[...] = jnp.zeros_like(state)

        tri = _tri(C, False)
        row8 = lax.broadcasted_iota(jnp.int32, (8, 1), 0)

        def chunk(c, carry):
            rows = pl.ds(pl.multiple_of(c * C, C), C)
            heads = range(hb)
            cols = [slice(hh * HEAD, (hh + 1) * HEAD) for hh in heads]
            vv = [iv_ref[rows, cl] for cl in cols]
            gates = [_hg_gates(fz_ref[rows, cl], qz_ref[rows, cl], lb_ref[:, cl]) for cl in cols]
            kk, qq = [gt[3] for gt in gates], [gt[4] for gt in gates]
            b = [jnp.dot(tri, gt[2], preferred_element_type=F32, precision=lax.Precision.HIGHEST) for gt in gates]
            bl = [bh[C - 1:C, :] for bh in b]
            st = [state[hh] for hh in heads]
            stb = [s_.astype(BF16) for s_ in st]
            for hh in heads:
                st_ref[hh, c] = stb[hh]
            o = [lax.dot_general((qq[hh] * jnp.exp(b[hh])).astype(BF16), stb[hh], (((1,), (1,)), ((), ())),
                                 preferred_element_type=F32) for hh in heads]
            upd = [lax.dot_general(vv[hh].astype(BF16), (kk[hh] * jnp.exp(bl[hh] - b[hh])).astype(BF16),
                                   (((0,), (0,)), ((), ())), preferred_element_type=F32) for hh in heads]
            for hh in heads:
                state[hh] = st[hh] * jnp.exp(bl[hh]) + upd[hh]
            wts = {}
            for t0 in range(0, C, 8):
                for s in range(min(C, t0 + 8)):
                    for hh in heads:
                        diff = b[hh][t0:t0 + 8, :] - b[hh][s:s + 1, :]
                        if s >= t0:
                            diff = jnp.minimum(diff, 0.0)
                        a = jnp.sum(qq[hh][t0:t0 + 8, :] * jnp.exp(diff) * kk[hh][s:s + 1, :], axis=-1, keepdims=True)
                        wts[t0, s, hh] = jnp.where(row8 >= s - t0, a, 0.0) if s >= t0 else a
            for hh in heads:
                tiles = []
                for t0 in range(0, C, 8):
                    acc = o[hh][t0:t0 + 8, :]
                    for s in range(min(C, t0 + 8)):
                        acc = acc + wts[t0, s, hh] * vv[hh][s:s + 1, :]
                    tiles.append(acc)
                oh = jnp.concatenate(tiles, axis=0)
                o_ref[rows, cols[hh]] = oh
                n = oh * lax.rsqrt(jnp.mean(oh * oh, axis=-1, keepdims=True) + RMS_EPS)
                gz = gz_ref[rows, cols[hh]]
                cat_ref[rows, cols[hh]] = (n * g_ref[:, cols[hh]] * (gz * jax.nn.sigmoid(gz))).astype(BF16)
            return carry

        lax.fori_loop(0, ncb, chunk, 0)

    def col(k):
        return pl.BlockSpec((tb, hb * HEAD), lambda h, i: (i, base + k * nh + h))

    vec = pl.BlockSpec((1, hb * HEAD), lambda h, i: (0, h))
    cat_out, o_raw, states = pl.pallas_call(
        body, name=name, grid=(nh, S // tb),
        in_specs=[col(0), col(1), col(2), col(3), vec, vec, pl.BlockSpec(memory_space=pl.ANY)],
        out_specs=[pl.BlockSpec((tb, hb * HEAD), lambda h, i: (i, cat_cols - nh + h)),
                   pl.BlockSpec((tb, hb * HEAD), lambda h, i: (i, h)),
                   pl.BlockSpec((hb, ncb, HEAD, HEAD), lambda h, i: (h, i, 0, 0))],
        out_shape=[jax.ShapeDtypeStruct(cat.shape, BF16), jax.ShapeDtypeStruct((S, W), F32),
                   jax.ShapeDtypeStruct((HH, S // C, HEAD, HEAD), BF16)],
        scratch_shapes=[pltpu.VMEM((hb, HEAD, HEAD), F32)],
        input_output_aliases={6: 0},
        compiler_params=_cp(("parallel", "arbitrary")))(proj, proj, proj, proj, lb.reshape(1, W), gain.reshape(1, W), cat)
    return cat_out, o_raw, states


def _hg_bwd(proj, dcat, o_raw, states, lb, gain, after, HH, name):
    S, NC = proj.shape
    W = HH * HEAD
    C = HG_CHUNK
    hb = min(HG_HEADS_PER_STEP, HH)
    nh = HH // hb
    tb = _tile(S, 2048 // hb, C)
    ncb = tb // C
    nt = S // tb
    base = 3 * nh
    dcat_cols = dcat.shape[1] // (hb * HEAD)

    def body(fz_ref, iv_ref, qz_ref, gz_ref, dc_ref, o_ref, st_ref, lb_ref, g_ref, after_ref,
             dfz_ref, div_ref, dqz_ref, dgz_ref, dlb_ref, dg_ref, dstate):
        del after_ref

        @pl.when(pl.program_id(1) == 0)
        def _():
            dstate[...] = jnp.zeros_like(dstate)
            dlb_ref[...] = jnp.zeros_like(dlb_ref)
            dg_ref[...] = jnp.zeros_like(dg_ref)

        tri_lo, tri_up = _tri(C, False), _tri(C, True)
        row = lax.broadcasted_iota(jnp.int32, (C, 1), 0)
        row8 = lax.broadcasted_iota(jnp.int32, (8, 1), 0)

        def chunk(cc, carry):
            c = ncb - 1 - cc
            rows = pl.ds(pl.multiple_of(c * C, C), C)
            heads = range(hb)
            starts = list(range(0, C, 8))
            hd = []
            for hh in heads:
                cols = slice(hh * HEAD, (hh + 1) * HEAD)
                lbv, gv = lb_ref[:, cols], g_ref[:, cols]
                fz, vv, qz, gz = fz_ref[rows, cols], iv_ref[rows, cols], qz_ref[rows, cols], gz_ref[rows, cols]
                sig, f, lf, kk, qq, sq = _hg_gates(fz, qz, lbv)
                b = jnp.dot(tri_lo, lf, preferred_element_type=F32, precision=lax.Precision.HIGHEST)
                bl = b[C - 1:C, :]
                eb, ebl = jnp.exp(b), jnp.exp(bl)
                ekb = jnp.exp(bl - b)
                qh, kh = qq * eb, kk * ekb
                o = o_ref[rows, cols]
                dhg = dc_ref[rows, cols]
                sgz = jax.nn.sigmoid(gz)
                rr = lax.rsqrt(jnp.mean(o * o, axis=-1, keepdims=True) + RMS_EPS)
                nrm = o * rr
                dpre = dhg * (gz * sgz)
                dgz_ref[rows, cols] = (dhg * nrm * gv * (sgz * (1.0 + gz * (1.0 - sgz)))).astype(BF16)
                dg_ref[:, cols] += jnp.sum(dpre * nrm, axis=0, keepdims=True)
                dn = dpre * gv
                do = rr * (dn - nrm * jnp.mean(dn * nrm, axis=-1, keepdims=True))
                dob = do.astype(BF16)
                st0 = st_ref[hh, c]
                dst1 = dstate[hh]
                dst1b = dst1.astype(BF16)
                dqh = jnp.dot(dob, st0, preferred_element_type=F32)
                dv = lax.dot_general(kh.astype(BF16), dst1b, (((1,), (1,)), ((), ())), preferred_element_type=F32)
                dkh = jnp.dot(vv.astype(BF16), dst1b, preferred_element_type=F32)
                dstate[hh] = dst1 * ebl + lax.dot_general(dob, qh.astype(BF16), (((0,), (0,)), ((), ())),
                                                          preferred_element_type=F32)
                extra = (jnp.sum(dkh * kh, axis=0, keepdims=True)
                         + ebl * jnp.sum(st0.astype(F32) * dst1, axis=0, keepdims=True))
                hd.append(dict(cols=cols, lbv=lbv, sig=sig, f=f, kk=kk, qq=qq, sq=sq, qz=qz, vv=vv, b=b, do=do, dv=dv,
                               extra=extra, dq=dqh * eb, dk=dkh * ekb))
            red = {}
            for s in range(C):
                for ti, t0 in enumerate(starts):
                    if t0 + 8 <= s:
                        continue
                    for hh in heads:
                        v = hd[hh]
                        diff = v["b"][t0:t0 + 8, :] - v["b"][s:s + 1, :]
                        if s >= t0:
                            diff = jnp.minimum(diff, 0.0)
                        dec = jnp.exp(diff)
                        da = jnp.sum(v["do"][t0:t0 + 8, :] * v["vv"][s:s + 1, :], axis=-1, keepdims=True)
                        a = jnp.sum(v["qq"][t0:t0 + 8, :] * dec * v["kk"][s:s + 1, :], axis=-1, keepdims=True)
                        if s >= t0:
                            keep = row8 >= s - t0
                            da, a = jnp.where(keep, da, 0.0), jnp.where(keep, a, 0.0)
                        red[s, ti, hh] = (da * dec, a)
            for hh in heads:
                v = hd[hh]
                cols, kk, qq, do = v["cols"], v["kk"], v["qq"], v["do"]
                dq_t = [v["dq"][t0:t0 + 8, :] for t0 in starts]
                dk_rows, dv_rows = [], []
                for s in range(C):
                    dk_s = dv_s = None
                    for ti, t0 in enumerate(starts):
                        if t0 + 8 <= s:
                            continue
                        gd, a = red[s, ti, hh]
                        dq_t[ti] = dq_t[ti] + gd * kk[s:s + 1, :]
                        pk = jnp.sum(gd * qq[t0:t0 + 8, :], axis=0, keepdims=True)
                        pv = jnp.sum(a * do[t0:t0 + 8, :], axis=0, keepdims=True)
                        dk_s = pk if dk_s is None else dk_s + pk
                        dv_s = pv if dv_s is None else dv_s + pv
                    dk_rows.append(dk_s)
                    dv_rows.append(dv_s)
                dq = jnp.concatenate(dq_t, axis=0)
                dk = v["dk"] + jnp.concatenate(dk_rows, axis=0)
                dv = v["dv"] + jnp.concatenate(dv_rows, axis=0)
                db = qq * dq - kk * dk + jnp.where(row == C - 1, v["extra"], 0.0)
                dlf = jnp.dot(tri_up, db, preferred_element_type=F32, precision=lax.Precision.HIGHEST)
                df = dlf / v["f"] - dk
                sig, sq, qz = v["sig"], v["sq"], v["qz"]
                dfz_ref[rows, cols] = (df * (1.0 - v["lbv"]) * sig * (1.0 - sig)).astype(BF16)
                dlb_ref[:, cols] += jnp.sum(df * (1.0 - sig), axis=0, keepdims=True)
                dqz_ref[rows, cols] = (dq * (sq * (1.0 + qz * (1.0 - sq)))).astype(BF16)
                div_ref[rows, cols] = dv.astype(BF16)
            return carry

        lax.fori_loop(0, ncb, chunk, 0)

    def col(k):
        return pl.BlockSpec((tb, hb * HEAD), lambda h, i: (nt - 1 - i, base + k * nh + h))

    ocol = pl.BlockSpec((tb, hb * HEAD), lambda h, i: (nt - 1 - i, h))
    vec = pl.BlockSpec((1, hb * HEAD), lambda h, i: (0, h))
    outs = pl.pallas_call(
        body, name=name, grid=(nh, nt),
        in_specs=[col(0), col(1), col(2), col(3),
                  pl.BlockSpec((tb, hb * HEAD), lambda h, i: (nt - 1 - i, dcat_cols - nh + h)),
                  pl.BlockSpec((tb, hb * HEAD), lambda h, i: (nt - 1 - i, h)),
                  pl.BlockSpec((hb, ncb, HEAD, HEAD), lambda h, i: (h, nt - 1 - i, 0, 0)), vec, vec,
                  pl.BlockSpec(memory_space=pl.ANY)],
        out_specs=[ocol, ocol, ocol, ocol, vec, vec],
        out_shape=[jax.ShapeDtypeStruct((S, W), BF16)] * 4 + [jax.ShapeDtypeStruct((1, W), F32)] * 2,
        scratch_shapes=[pltpu.VMEM((hb, HEAD, HEAD), F32)],
        compiler_params=_cp(("parallel", "arbitrary")))(proj, proj, proj, proj, dcat, o_raw, states,
                                                        lb.reshape(1, W), gain.reshape(1, W), after)
    return outs


def _lb_all(lb_logits):
    p = jax.nn.softmax(lb_logits.astype(F32), axis=0)
    return jnp.cumsum(p, axis=0) - p


def _local_step(x, mem, target, rel_bias, lb_logits, gains, hg_norm, comm):
    S, D = x.shape
    L = gains.shape[0]
    H = (D // 2) // HEAD
    lb = _lb_all(lb_logits)
    tables = [_branch_tables(r) for _, r in BRANCHES]
    onehots = [jnp.where(ok[None], jax.nn.one_hot(bk, REL_BUCKETS, dtype=F32, axis=0), 0.0)
               .reshape(REL_BUCKETS, 2 * Q_BLOCK * Q_BLOCK) for bk, ok in tables]
    bias = [jnp.where(ok[None], jnp.dot(rel_bias.astype(F32).T, oh, precision=lax.Precision.HIGHEST)
                      .reshape(H, Q_BLOCK, 2 * Q_BLOCK), NEG_INF) for oh, (_, ok) in zip(onehots, tables)]

    saved = []
    _, h = _norm_fwd(x, None, None, gains[0, 0], "norm_first")
    xl = x
    for l in range(L):
        g = gains[l]
        w_in, token = comm.weights(l, xl)
        proj = _mm_nn(h, w_in, F32, "mm_in", after=token)
        outs, lses = [], []
        for gi, (_, r) in enumerate(BRANCHES):
            o, ls = _attn_branch_fwd(proj, bias[gi], r, H, f"attn_fwd_d{r}")
            outs.append(o)
            lses.append(ls)
        cat, o_attn, lse_tot = _attn_merge(outs, lses, D, "attn_merge")
        cat, o_raw, states = _hg_fwd(proj, cat, lb[l], hg_norm[l], H, "hgrn_fwd")
        w = dict(comm.rest(l, cat), **{"in": w_in})
        mix = _mm_nn(cat, w["out"], F32, "mm_out")
        x1, hc = _norm_fwd(xl, mix, g[1], g[2], "norm_post_pre")
        _, mn = _norm_fwd(mem, None, None, g[3], "norm_mem")
        cq = _mm_nn(hc, w["cq"], BF16, "mm_cq")
        ckv = _mm_nn(mn, w["ckv"], BF16, "mm_ckv")
        co = _cross_fwd(cq, ckv, "cross_fwd")
        cop = _mm_nn(co, w["co"], F32, "mm_co")
        x2, hf = _norm_fwd(x1, cop, g[4], g[5], "norm_post_pre")
        gu, act = _mm_gate_up(hf, w["gu"], "mm_gu")
        y = _mm_nn(act, w["down"], F32, "mm_down")
        g_next = gains[l + 1, 0] if l + 1 < L else None
        x3, h_next = _norm_fwd(x2, y, g[6], g_next, "norm_post_pre" if l + 1 < L else "norm_post")
        saved.append(dict(x0=xl, h0=h, proj=proj, cat=cat, o_attn=o_attn, lse_tot=lse_tot, o_raw=o_raw, states=states,
                          mix=mix, x1=x1, hc=hc, mn=mn, cq=cq, ckv=ckv, co=co, cop=cop, x2=x2, hf=hf, gu=gu, act=act,
                          y=y, x3=x3, w=w))
        comm.layer_done(l, x3)
        xl, h = x3, h_next

    loss, dres = _loss_and_grad(xl, target)

    dh_next = bwd_token = None
    d_gains = [[None] * 7 for _ in range(L)]
    d_lb = [None] * L
    d_hg = [None] * L
    d_bias = [jnp.zeros((H, Q_BLOCK, 2 * Q_BLOCK), F32) for _ in BRANCHES]
    for l in reversed(range(L)):
        g, sv = gains[l], saved[l]
        w = sv["w"]
        g_next = gains[l + 1, 0] if l + 1 < L else None
        t3, dy, dg_next, dg6 = _norm_bwd(dres, dh_next, sv["x3"], sv["y"], g_next, g[6], "norm_bwd")
        if l + 1 < L:
            d_gains[l + 1][0] = dg_next
        d_gains[l][6] = dg6
        gw = {}
        gw["down"] = _mm_tn(sv["act"], dy, 1, BF16, "mm_dw_down")
        dgu = _mm_dx_down_swiglu(dy, w["down"], sv["gu"], "mm_dx_down", after=bwd_token)
        gw["gu"] = _mm_tn(sv["hf"], dgu, w["gu"].shape[0], BF16, "mm_dw_gu")
        dhf = _mm_nt(dgu, w["gu"], F32, "mm_dx_gu")
        t2, dcop, d_gains[l][5], d_gains[l][4] = _norm_bwd(t3, dhf, sv["x2"], sv["cop"], g[5], g[4], "norm_bwd")
        gw["co"] = _mm_tn(sv["co"], dcop, 1, BF16, "mm_dw_sq")
        dco = _mm_nt(dcop, w["co"], F32, "mm_dx_sq")
        dcq, dckv = _cross_bwd(sv["cq"], sv["ckv"], dco, "cross_bwd")
        gw["cq"] = _mm_tn(sv["hc"], dcq, 1, BF16, "mm_dw_sq")
        dhc = _mm_nt(dcq, w["cq"], F32, "mm_dx_sq")
        gw["ckv"] = _mm_tn(sv["mn"], dckv, w["ckv"].shape[0], BF16, "mm_dw_ckv")
        dmn = _mm_nt(dckv, w["ckv"], F32, "mm_dx_ckv")
        _, _, d_gains[l][3], _ = _norm_bwd(None, dmn, mem, None, g[3], None, "norm_bwd_mem")
        t1, dmix, d_gains[l][2], d_gains[l][1] = _norm_bwd(t2, dhc, sv["x1"], sv["mix"], g[2], g[1], "norm_bwd")
        gw["out"] = _mm_tn(sv["cat"], dmix, 1, BF16, "mm_dw_sq")
        dcat = _mm_nt(dmix, w["out"], F32, "mm_dx_sq")
        delta = _attn_delta(dcat, sv["o_attn"], H, "attn_delta", after=comm.early_grads(l, gw))
        dqkv = []
        for gi, (_, r) in enumerate(BRANCHES):
            dq, dk, dv, db = _attn_branch_bwd(sv["proj"], dcat, sv["lse_tot"], delta, bias[gi], r, H, f"attn_bwd_d{r}")
            dqkv.append((dq, dk, dv))
            d_bias[gi] = d_bias[gi] + db
        dfz, div, dqz, dgz, dlb_l, dhg_l = _hg_bwd(sv["proj"], dcat, sv["o_raw"], sv["states"], lb[l], hg_norm[l],
                                                   delta, H, "hgrn_bwd")
        d_lb[l], d_hg[l] = dlb_l[0], dhg_l[0]
        dproj = _assemble_dproj(dqkv, [dfz, div, dqz, dgz], "assemble_dproj")
        gw["in"] = _mm_tn(sv["h0"], dproj, w["in"].shape[0], BF16, "mm_dw_in")
        dh_next = _mm_nt(dproj, w["in"], F32, "mm_dx_in")
        dres = t1
        bwd_token = comm.grads(l, gw, gw["in"])
    grad_x, _, d_gains[0][0], _ = _norm_bwd(dres, dh_next, x, None, gains[0, 0], None, "norm_bwd_first")

    d_rel_bias = _bias_grad(d_bias, onehots, H).T
    d_gains = jnp.stack([jnp.stack(row) for row in d_gains])
    return loss, grad_x, d_rel_bias, jnp.stack(d_lb), jnp.stack(d_hg), d_gains


def _place():
    return lax.axis_index("x"), lax.axis_index("y"), lax.axis_index("c")


def _all_gather(shards, name):
    n = len(shards)
    HBM = pl.BlockSpec(memory_space=pltpu.HBM)

    def body(*refs):
        ins, outs = refs[:n], refs[n:2 * n]
        send_sems, recv_sems, local_sems = refs[2 * n:]
        x, y, c = _place()
        me, sibling = (x, y, c), (x, y, 1 - c)
        chips = [(1 - x, y), (x, 1 - y), (1 - x, 1 - y)]

        def copy(a, k, block, to, own=False):
            dst = outs[a].at[4 * block[0] + 2 * block[1] + block[2]]
            return pltpu.make_async_remote_copy(
                src_ref=ins[a] if own else dst, dst_ref=dst, send_sem=send_sems.at[7 * a + k],
                recv_sem=recv_sems.at[7 * a + k], device_id=to, device_id_type=MESH)

        mine = [pltpu.make_async_copy(ins[a], outs[a].at[4 * x + 2 * y + c], local_sems.at[a]) for a in range(n)]
        for cp in mine:
            cp.start()
        first = []
        for a in range(n):
            first.append(copy(a, 0, me, sibling, own=True))
            first += [copy(a, 1 + j, me, (*chip, c), own=True) for j, chip in enumerate(chips)]
        for cp in first:
            cp.start()
        passed = []
        for j, chip in enumerate(chips):
            for a in range(n):
                copy(a, 1 + j, (*chip, c), me).wait_recv()
                fwd = copy(a, 4 + j, (*chip, c), sibling)
                fwd.start()
                passed.append(fwd)
        for a in range(n):
            copy(a, 0, sibling, me).wait_recv()
            for j, chip in enumerate(chips):
                copy(a, 4 + j, (*chip, 1 - c), me).wait_recv()
        for cp in first + passed:
            cp.wait_send()
        for cp in mine:
            cp.wait()

    return pl.pallas_call(
        body, name=name, in_specs=[HBM] * n, out_specs=[HBM] * n,
        out_shape=[jax.ShapeDtypeStruct((N_DEV,) + s.shape, s.dtype) for s in shards],
        scratch_shapes=[pltpu.SemaphoreType.DMA((7 * n,)), pltpu.SemaphoreType.DMA((7 * n,)),
                        pltpu.SemaphoreType.DMA((n,))],
    )(*shards)


def _exchange_sibling(grads, name):
    n = len(grads)
    HBM = pl.BlockSpec(memory_space=pltpu.HBM)

    def body(*refs):
        ins, outs = refs[:n], refs[n:2 * n]
        send_sems, recv_sems = refs[2 * n:]
        x, y, c = _place()
        sibling = (x, y, 1 - c)
        for a in range(n):
            for q in range(4):
                pltpu.make_async_remote_copy(
                    src_ref=ins[a].at[2 * q + 1 - c], dst_ref=outs[a].at[q], send_sem=send_sems.at[a],
                    recv_sem=recv_sems.at[a], device_id=sibling, device_id_type=MESH).start()
        for a in range(n):
            pltpu.make_async_remote_copy(
                src_ref=ins[a].at[pl.ds(0, 4)], dst_ref=outs[a], send_sem=send_sems.at[a], recv_sem=recv_sems.at[a],
                device_id=sibling, device_id_type=MESH).wait()

    return pl.pallas_call(
        body, name=name, in_specs=[HBM] * n, out_specs=[HBM] * n,
        out_shape=[jax.ShapeDtypeStruct((4,) + g.shape[1:], g.dtype) for g in grads],
        scratch_shapes=[pltpu.SemaphoreType.DMA((n,)), pltpu.SemaphoreType.DMA((n,))],
    )(*grads)


def _exchange_chips(parts, name):
    n = len(parts)
    HBM = pl.BlockSpec(memory_space=pltpu.HBM)

    def body(*refs):
        ins, outs = refs[:n], refs[n:2 * n]
        send_sems, recv_sems = refs[2 * n:]
        x, y, c = _place()
        chips = [(1 - x, y), (x, 1 - y), (1 - x, 1 - y)]
        copies = []
        for a in range(n):
            for j, chip in enumerate(chips):
                cp = pltpu.make_async_remote_copy(
                    src_ref=ins[a].at[2 * chip[0] + chip[1]], dst_ref=outs[a].at[j], send_sem=send_sems.at[3 * a + j],
                    recv_sem=recv_sems.at[3 * a + j], device_id=(*chip, c), device_id_type=MESH)
                cp.start()
                copies.append(cp)
        for cp in copies:
            cp.wait()

    return pl.pallas_call(
        body, name=name, in_specs=[HBM] * n, out_specs=[HBM] * n,
        out_shape=[jax.ShapeDtypeStruct((3,) + p.shape[1:], p.dtype) for p in parts],
        scratch_shapes=[pltpu.SemaphoreType.DMA((3 * n,)), pltpu.SemaphoreType.DMA((3 * n,))],
    )(*parts)


def _others(x, y, c):
    out = []
    for k in range(1, N_DEV):
        px = 1 - x if (k >> 2) & 1 else x
        py = 1 - y if (k >> 1) & 1 else y
        pc = 1 - c if k & 1 else c
        out.append(((px, py, pc), 4 * px + 2 * py + pc))
    return out


_HBM_SPEC = pl.BlockSpec(memory_space=pltpu.HBM)
_SEM_SPEC = pl.BlockSpec(memory_space=pltpu.SEMAPHORE)
_EFFECT = pltpu.SideEffectType.DATAFLOW_SIDE_EFFECTING


def _copies_start(srcs, lands, after, src_block, name):
    n = len(srcs)

    def body(*refs):
        src_refs, land_refs = refs[:n], refs[n:2 * n]
        send_sems, recv_sems = refs[2 * n + 1], refs[2 * n + 2]
        token = refs[-1]
        x, y, c = _place()
        mine = 4 * x + 2 * y + c
        for a in range(n):
            for k, (peer, block) in enumerate(_others(x, y, c)):
                pltpu.make_async_remote_copy(
                    src_ref=src_refs[a].at[block] if src_block else src_refs[a],
                    dst_ref=land_refs[a].at[k] if src_block else land_refs[a].at[mine],
                    send_sem=send_sems.at[7 * a + k], recv_sem=recv_sems.at[7 * a + k],
                    device_id=peer, device_id_type=MESH).start()
        token[...] = jnp.zeros_like(token)

    outs = pl.pallas_call(
        body, name=name,
        out_shape=(pltpu.SemaphoreType.DMA((7 * n,)), pltpu.SemaphoreType.DMA((7 * n,)),
                   *[pltpu.HBM(s.shape, s.dtype) for s in srcs], *[pltpu.HBM(t.shape, t.dtype) for t in lands],
                   jax.ShapeDtypeStruct((8, 128), F32)),
        in_specs=[_HBM_SPEC] * (2 * n) + [pl.BlockSpec(memory_space=pl.ANY)],
        out_specs=(_SEM_SPEC, _SEM_SPEC, *[_HBM_SPEC] * (2 * n), pl.BlockSpec(memory_space=pltpu.VMEM)),
        input_output_aliases={i: 2 + i for i in range(2 * n)},
        compiler_params=pltpu.CompilerParams(has_side_effects=_EFFECT),
    )(*[pltpu.with_memory_space_constraint(s, pltpu.HBM) for s in srcs],
      *[pltpu.with_memory_space_constraint(t, pltpu.HBM) for t in lands], after)
    return outs[0], outs[1], list(outs[2:2 + n]), list(outs[2 + n:2 + 2 * n]), outs[-1]


def _copies_wait(send_sems, recv_sems, srcs, lands, after, src_block, name):
    n = len(srcs)

    def body(*refs):
        src_refs, land_refs = refs[:n], refs[n:2 * n]
        send_sems, recv_sems = refs[2 * n], refs[2 * n + 1]
        x, y, c = _place()
        for a in range(n):
            for k, (peer, block) in enumerate(_others(x, y, c)):
                cp = pltpu.make_async_remote_copy(
                    src_ref=src_refs[a].at[block] if src_block else src_refs[a],
                    dst_ref=land_refs[a].at[k] if src_block else land_refs[a].at[block],
                    send_sem=send_sems.at[7 * a + k], recv_sem=recv_sems.at[7 * a + k],
                    device_id=peer, device_id_type=MESH)
                cp.wait_send()
                cp.wait_recv()

    outs = pl.pallas_call(
        body, name=name,
        out_shape=(*[pltpu.HBM(s.shape, s.dtype) for s in srcs], *[pltpu.HBM(t.shape, t.dtype) for t in lands]),
        in_specs=[_HBM_SPEC] * (2 * n) + [_SEM_SPEC, _SEM_SPEC, pl.BlockSpec(memory_space=pl.ANY)],
        out_specs=tuple([_HBM_SPEC] * (2 * n)),
        input_output_aliases={i: i for i in range(2 * n)},
        compiler_params=pltpu.CompilerParams(has_side_effects=_EFFECT),
    )(*srcs, *lands, send_sems, recv_sems, after)
    return list(outs[:n]), list(outs[n:])


def _pair_sum(grad, recv, c_idx, name):
    _, R, C = grad.shape
    tr = _tile(R, 512, 16)

    def body(c_ref, g_ref, p_ref, o_ref):
        del c_ref
        o_ref[...] = (g_ref[...].astype(F32) + p_ref[...].astype(F32)).astype(BF16)

    spec = pltpu.PrefetchScalarGridSpec(
        num_scalar_prefetch=1, grid=(4, R // tr),
        in_specs=[pl.BlockSpec((None, tr, C), lambda q, i, c: (2 * q + c[0], i, 0)),
                  pl.BlockSpec((None, tr, C), lambda q, i, c: (q, i, 0))],
        out_specs=pl.BlockSpec((None, tr, C), lambda q, i, c: (q, i, 0)))
    return pl.pallas_call(body, name=name, grid_spec=spec, out_shape=jax.ShapeDtypeStruct((4, R, C), BF16),
                          compiler_params=_cp(("parallel", "parallel")))(c_idx, grad, recv)


def _adam(w, g, m, v):
    m2 = ADAM_B1 * m + (1.0 - ADAM_B1) * g
    v2 = ADAM_B2 * v + (1.0 - ADAM_B2) * (g * g)
    m_hat = m2 / (1.0 - ADAM_B1 ** ADAM_STEP)
    v_hat = v2 / (1.0 - ADAM_B2 ** ADAM_STEP)
    return -ADAM_LR * (m_hat / (jnp.sqrt(v_hat) + ADAM_EPS) + ADAM_WD * w), m2, v2


def _reduce_update(part, recv, own_idx, w, m, v, layer, prev, name):
    L, R, C = w.shape
    nr = recv.shape[0]
    tr = _tile(R, 128, 16)
    has_prev = prev is not None

    def body(idx_ref, q_ref, *rest):
        del idx_ref
        r_refs, (w_ref, m_ref, v_ref) = rest[:nr], rest[nr:nr + 3]
        g_ref, d_ref, m2_ref, v2_ref = rest[-4:]
        g = q_ref[...].astype(F32)
        for r_ref in r_refs:
            g = g + r_ref[...].astype(F32)
        d, m2, v2 = _adam(w_ref[...], g, m_ref[...], v_ref[...])
        g_ref[...] = g
        d_ref[...] = d
        m2_ref[...] = m2
        v2_ref[...] = v2

    def rblk(j):
        return pl.BlockSpec((None, tr, C), lambda i, k: (j, i, 0))

    lay = pl.BlockSpec((None, tr, C), lambda i, k: (layer, i, 0))
    in_specs = [pl.BlockSpec((None, tr, C), lambda i, k: (k[0], i, 0))] + [rblk(j) for j in range(nr)] + [lay, lay, lay]
    ins = [part] + [recv] * nr + [w, m, v]
    aliases = {}
    if has_prev:
        in_specs += [pl.BlockSpec(memory_space=pl.ANY)] * 4
        ins += list(prev)
        aliases = {1 + len(ins) - 4 + t: t for t in range(4)}
    spec = pltpu.PrefetchScalarGridSpec(num_scalar_prefetch=1, grid=(R // tr,), in_specs=in_specs,
                                        out_specs=[lay, lay, lay, lay])
    return pl.pallas_call(body, name=name, grid_spec=spec, out_shape=[jax.ShapeDtypeStruct((L, R, C), F32)] * 4,
                          input_output_aliases=aliases, compiler_params=_cp(("parallel",)))(own_idx, *ins)


def _sum_devices(gathered):
    n, R, C = gathered.shape

    def body(g_ref, o_ref):
        acc = g_ref[0]
        for d in range(1, n):
            acc = acc + g_ref[d]
        o_ref[...] = acc

    return pl.pallas_call(body, name="sum_devices", out_shape=jax.ShapeDtypeStruct((R, C), F32))(gathered)


def _adam_small(w, g, m, v, name):
    shape = w.shape
    two = (-1, shape[-1])

    def body(w_ref, g_ref, m_ref, v_ref, d_ref, m2_ref, v2_ref):
        d, m2, v2 = _adam(w_ref[...], g_ref[...], m_ref[...], v_ref[...])
        d_ref[...] = d
        m2_ref[...] = m2
        v2_ref[...] = v2

    outs = pl.pallas_call(body, name=name, out_shape=[jax.ShapeDtypeStruct(w.reshape(two).shape, F32)] * 3)(
        w.reshape(two), g.reshape(two), m.reshape(two), v.reshape(two))
    return [o.reshape(shape) for o in outs]


_SHARDED = ("in", "out", "cq", "ckv", "co", "gu", "down")
_COLUMN_SHARDED = ("in", "ckv", "gu")


def kernel(x, mem, rel_bias, lb_logits, norm_gains, w_in, hg_norm, w_out, w_cq, w_ckv, w_co, w_gate_up, w_down, loss_target, m_rel_bias, m_lb_logits, m_norm_gains, m_w_in, m_hg_norm, m_w_out, m_w_cq, m_w_ckv, m_w_co, m_w_gate_up, m_w_down, v_rel_bias, v_lb_logits, v_norm_gains, v_w_in, v_hg_norm, v_w_out, v_w_cq, v_w_ckv, v_w_co, v_w_gate_up, v_w_down):
    L = w_in.shape[0]
    D = x.shape[-1]
    ws = dict(zip(_SHARDED, (w_in, w_out, w_cq, w_ckv, w_co, w_gate_up, w_down)))
    ms = dict(zip(_SHARDED, (m_w_in, m_w_out, m_w_cq, m_w_ckv, m_w_co, m_w_gate_up, m_w_down)))
    vs = dict(zip(_SHARDED, (v_w_in, v_w_out, v_w_cq, v_w_ckv, v_w_co, v_w_gate_up, v_w_down)))
    px, py, pc = _place()
    dev = 4 * px + 2 * py + pc
    c_idx = jnp.reshape(pc, (1,)).astype(jnp.int32)
    chip_idx = jnp.reshape(2 * px + py, (1,)).astype(jnp.int32)
    dev_idx = jnp.reshape(dev, (1,)).astype(jnp.int32)

    rest_keys = _SHARDED[1:]

    def shards_of(l, keys):
        return [ws[k][l].astype(BF16) for k in keys]

    def as_weights(keys, gathered):
        return {k: g if k in _COLUMN_SHARDED else g.reshape(1, N_DEV * g.shape[1], g.shape[2])
                for k, g in zip(keys, gathered)}

    def start_gather(shards, after, name):
        lands = [lax.empty((N_DEV,) + s.shape, s.dtype) for s in shards]
        return _copies_start(shards, lands, after, False, name)

    def wait_gather(pending, keys, after, name):
        send_sems, recv_sems, shards, lands = pending
        shards, lands = _copies_wait(send_sems, recv_sems, shards, lands, after, False, name)
        full = [lax.dynamic_update_slice_in_dim(t, s[None], dev, axis=0) for s, t in zip(shards, lands)]
        return as_weights(keys, full)

    def full_grads(gw, keys):
        return [gw[k].reshape((N_DEV,) + ws[k].shape[1:]) for k in keys]

    def start_scatter(full, after, name):
        lands = [lax.empty((N_DEV - 1,) + g.shape[1:], g.dtype) for g in full]
        return _copies_start(full, lands, after, True, name)

    class Comm:
        def __init__(self):
            first, gains = _all_gather(shards_of(0, _SHARDED[:1]) + [norm_gains], "gather_first")
            self.gains = jnp.transpose(gains, (1, 2, 0, 3)).reshape(L, 7, D)
            self.ready = {0: as_weights(_SHARDED[:1], [first])}
            *self.rest0, self.order = start_gather(shards_of(0, rest_keys), first, "gather_start_rest")
            self.gather = None
            self.scatter = None
            self.early = None
            self.results = {k: None for k in _SHARDED}

        def weights(self, l, x_in):
            del x_in
            token = None
            if l + 1 < L:
                after = self.order if l == 0 else self.ready[l]["in"]
                *self.gather, token = start_gather(shards_of(l + 1, _SHARDED), after, "gather_start")
            return self.ready[l]["in"], token

        def rest(self, l, after):
            if l == 0:
                self.ready[0].update(wait_gather(self.rest0, rest_keys, after, "gather_wait_rest"))
            mine = self.ready.pop(l)
            return {k: mine[k] for k in rest_keys}

        def layer_done(self, l, x_out):
            if l + 1 < L:
                self.ready[l + 1] = wait_gather(self.gather, _SHARDED, x_out, "gather_wait")

        def update(self, l, keys, parts, recvs, own_idx):
            for k, part, recv in zip(keys, parts, recvs):
                self.results[k] = _reduce_update(part, recv, own_idx, ws[k], ms[k], vs[k], l, self.results[k],
                                                 "reduce_update")

        def finish_scatter(self, after):
            lp, send_sems, recv_sems, srcs, lands = self.scatter
            srcs, lands = _copies_wait(send_sems, recv_sems, srcs, lands, after, True, "scatter_wait")
            self.update(lp, _SHARDED, srcs, lands, dev_idx)
            self.scatter = None
            return lands[0]

        def early_grads(self, l, gw):
            if l > 0 or L == 1:
                return None
            order = self.finish_scatter(gw["out"])
            *self.early, token = start_scatter(full_grads(gw, rest_keys), order, "scatter_start_rest")
            return token

        def grads(self, l, gw, after):
            if l > 0:
                order = self.finish_scatter(after) if self.scatter is not None else dev_idx
                send_sems, recv_sems, srcs, lands, token = start_scatter(full_grads(gw, _SHARDED), order,
                                                                         "scatter_start")
                self.scatter = (l, send_sems, recv_sems, srcs, lands)
                return token
            last_keys = _SHARDED
            if self.early is not None:
                send_sems, recv_sems, srcs, lands = self.early
                srcs, lands = _copies_wait(send_sems, recv_sems, srcs, lands, after, True, "scatter_wait_rest")
                self.update(0, rest_keys, srcs, lands, dev_idx)
                last_keys = _SHARDED[:1]
            full = full_grads(gw, last_keys)
            from_sibling = _exchange_sibling(full, "scatter_sibling")
            parts = [_pair_sum(g, r, c_idx, "pair_sum") for g, r in zip(full, from_sibling)]
            self.update(0, last_keys, parts, _exchange_chips(parts, "scatter_chips"), chip_idx)
            return None

    comm = Comm()
    loss, grad_x, d_rel_bias, d_lb, d_hg, d_gains = _local_step(
        x[0], mem[0], loss_target[0], rel_bias, lb_logits, comm.gains, hg_norm, comm)
    results = comm.results

    pieces = [jnp.reshape(loss, (1,)), d_rel_bias.reshape(-1), d_lb.reshape(-1), d_hg.reshape(-1), d_gains.reshape(-1)]
    sizes = [p.shape[0] for p in pieces]
    total = sum(sizes)
    rows = -(-total // 1024) * 8
    pack = jnp.pad(jnp.concatenate(pieces), (0, rows * 128 - total)).reshape(rows, 128)
    summed = _sum_devices(_all_gather([pack], "gather_small")[0]).reshape(-1)
    offs = [sum(sizes[:i]) for i in range(len(sizes))]
    loss = summed[0]
    g_rel_bias = summed[offs[1]:offs[1] + sizes[1]].reshape(rel_bias.shape)
    g_lb_all = summed[offs[2]:offs[2] + sizes[2]].reshape(lb_logits.shape)
    g_hg = summed[offs[3]:offs[3] + sizes[3]].reshape(hg_norm.shape)
    g_gains_full = summed[offs[4]:offs[4] + sizes[4]].reshape(L, 7, D)
    g_gains = lax.dynamic_slice_in_dim(g_gains_full, dev * (D // N_DEV), D // N_DEV, axis=2)
    g_lb = jax.vjp(_lb_all, lb_logits)[1](g_lb_all)[0]

    small = [(rel_bias, g_rel_bias, m_rel_bias, v_rel_bias), (lb_logits, g_lb, m_lb_logits, v_lb_logits),
             (norm_gains, g_gains, m_norm_gains, v_norm_gains), (hg_norm, g_hg, m_hg_norm, v_hg_norm)]
    upd = [_adam_small(w, g, m, v, "adam_small") for w, g, m, v in small]

    def ordered(small_vals, big_vals):
        return [small_vals[0], small_vals[1], small_vals[2], big_vals["in"], small_vals[3], big_vals["out"],
                big_vals["cq"], big_vals["ckv"], big_vals["co"], big_vals["gu"], big_vals["down"]]

    grads = ordered([s[1] for s in small], {k: results[k][0] for k in _SHARDED})
    deltas = ordered([u[0] for u in upd], {k: results[k][1] for k in _SHARDED})
    new_m = ordered([u[1] for u in upd], {k: results[k][2] for k in _SHARDED})
    new_v = ordered([u[2] for u in upd], {k: results[k][3] for k in _SHARDED})
    return (loss, grad_x[None], *grads, *deltas, *new_m, *new_v)
```

```python
import functools
import math

import jax
import jax.numpy as jnp
from jax import lax
from jax.experimental import pallas as pl
from jax.experimental.pallas import tpu as pltpu

F32 = jnp.float32
BF16 = jnp.bfloat16
MESH = pl.DeviceIdType.MESH

N_DEV = 8
HEAD = 128
CROSS_HEADS = 4
Q_BLOCK = 128
BRANCHES = ((128, 1), (512, 4), (2048, 16))
REL_BUCKETS = 32
REL_MAX_DIST = 2048
HG_CHUNK = 16
HG_HEADS_PER_STEP = 8
ATTN_PAIRS_PER_STAGE = 8
RMS_EPS = 1e-6
NEG_INF = -1e30
ADAM_LR, ADAM_B1, ADAM_B2, ADAM_EPS, ADAM_WD, ADAM_STEP = 0.001, 0.9, 0.999, 1e-08, 0.01, 10
VMEM_LIMIT = 48 * 1024 * 1024
MXU_WIDTH = 256
NT_K_PER_STEP = 3584


def _tile(n, target, mult):
    best = None
    t = mult
    while t <= min(n, target):
        if n % t == 0:
            best = t
        t += mult
    return best if best is not None else n


def _cp(sem, vmem=VMEM_LIMIT):
    return pltpu.CompilerParams(dimension_semantics=sem, vmem_limit_bytes=vmem)


def _mm_nn(a, b3, out_dtype, name, after=None):
    M, K = a.shape
    J, K2, Nb = b3.shape
    assert K == K2
    tm, tn, tk = _tile(M, 1024, 8), _tile(Nb, 1408, 128), _tile(K, 2816, 128)
    nn, nk = Nb // tn, K // tk
    extra = [] if after is None else [after]
    jb = 2 if nn == 1 and tn % MXU_WIDTH and J % 2 == 0 else 1

    def body(a_ref, b_ref, *rest):
        o_ref, acc = rest[len(extra)], rest[len(extra) + 1:]
        bv = b_ref[0] if jb == 1 else jnp.concatenate([b_ref[j] for j in range(jb)], axis=1)
        prod = jnp.dot(a_ref[...].astype(BF16), bv.astype(BF16), preferred_element_type=F32)
        if nk == 1:
            o_ref[...] = prod.astype(o_ref.dtype)
        else:
            k = pl.program_id(2)

            @pl.when(k == 0)
            def _():
                acc[0][...] = prod

            @pl.when(k > 0)
            def _():
                acc[0][...] += prod

            @pl.when(k == nk - 1)
            def _():
                o_ref[...] = acc[0][...].astype(o_ref.dtype)

    return pl.pallas_call(
        body, name=name, grid=(M // tm, J * nn // jb, nk),
        in_specs=[pl.BlockSpec((tm, tk), lambda i, n, k: (i, k)),
                  pl.BlockSpec((jb, tk, tn), lambda i, n, k: (n // nn, k, n % nn))]
                 + [pl.BlockSpec(memory_space=pl.ANY)] * len(extra),
        out_specs=pl.BlockSpec((tm, jb * tn), lambda i, n, k: (i, n)),
        out_shape=jax.ShapeDtypeStruct((M, J * Nb), out_dtype),
        scratch_shapes=[] if nk == 1 else [pltpu.VMEM((tm, jb * tn), F32)],
        compiler_params=_cp(("parallel", "parallel", "arbitrary")),
    )(a, b3, *extra)


def _mm_nt(a, b3, out_dtype, name, after=None):
    extra = [] if after is None else [after]
    halves = a.ndim == 3
    M, Kt = (a.shape[1], 2 * a.shape[2]) if halves else a.shape
    J, N, Kb = b3.shape
    assert Kt == J * Kb
    tm, tn = _tile(M, 1024, 8), _tile(N, 1408, 128)
    tk = _tile(math.gcd(Kb, Kt // 2) if halves else Kb, 2048, 128)
    nkb = Kb // tk
    groups = J // 2 if halves and J > 1 else J
    jb = max(j for j in range(1, groups + 1) if groups % j == 0 and j * Kb <= NT_K_PER_STEP) if nkb == 1 else 1
    nk = (J // jb) * nkb

    def body(a_ref, b_ref, *rest):
        o_ref, acc = rest[len(extra)], rest[len(extra) + 1:]
        bv = b_ref[0] if jb == 1 else jnp.concatenate([b_ref[j] for j in range(jb)], axis=1)
        prod = lax.dot_general(a_ref[...].astype(BF16), bv.astype(BF16), (((1,), (1,)), ((), ())),
                               preferred_element_type=F32)
        if nk == 1:
            o_ref[...] = prod.astype(o_ref.dtype)
        else:
            k = pl.program_id(2)

            @pl.when(k == 0)
            def _():
                acc[0][...] = prod

            @pl.when(k > 0)
            def _():
                acc[0][...] += prod

            @pl.when(k == nk - 1)
            def _():
                o_ref[...] = acc[0][...].astype(o_ref.dtype)

    return pl.pallas_call(
        body, name=name, grid=(M // tm, N // tn, nk),
        in_specs=[pl.BlockSpec((None, tm, jb * tk), lambda i, n, k: (k // (nk // 2), i, k % (nk // 2))) if halves
                  else pl.BlockSpec((tm, jb * tk), lambda i, n, k: (i, k)),
                  pl.BlockSpec((jb, tn, tk), lambda i, n, k: (k // nkb, n, k % nkb))]
                 + [pl.BlockSpec(memory_space=pl.ANY)] * len(extra),
        out_specs=pl.BlockSpec((tm, tn), lambda i, n, k: (i, n)),
        out_shape=jax.ShapeDtypeStruct((M, N), out_dtype),
        scratch_shapes=[] if nk == 1 else [pltpu.VMEM((tm, tn), F32)],
        compiler_params=_cp(("parallel", "parallel", "arbitrary")),
    )(a, b3, *extra)


def _mm_tn(a, b, J, out_dtype, name):
    S, M = a.shape
    halves = b.ndim == 3
    S2, Nt = (b.shape[1], 2 * b.shape[2]) if halves else b.shape
    assert S == S2 and Nt % J == 0
    Nb = Nt // J
    tm, tk = _tile(M, 1408, 128), _tile(S, 2048, 8)
    tn = _tile(math.gcd(Nb, Nt // 2) if halves else Nb, 1408, 128)
    nn, nk = Nb // tn, S // tk
    jb = 2 if nn == 1 and tn % MXU_WIDTH and J % (4 if halves else 2) == 0 else 1
    if jb * tn > 2048:
        tm = _tile(M, 512, 128)
    per_half = J * nn // 2 // jb
    b_spec = (pl.BlockSpec((None, tk, jb * tn), lambda i, n, k: (n // per_half, k, n % per_half)) if halves
              else pl.BlockSpec((tk, jb * tn), lambda i, n, k: (k, n)))

    def body(a_ref, b_ref, o_ref, *acc):
        prod = lax.dot_general(a_ref[...].astype(BF16), b_ref[...].astype(BF16), (((0,), (0,)), ((), ())),
                               preferred_element_type=F32)

        def write(val):
            for j in range(jb):
                o_ref[j] = val[:, j * tn:(j + 1) * tn].astype(o_ref.dtype)

        if nk == 1:
            write(prod)
        else:
            k = pl.program_id(2)

            @pl.when(k == 0)
            def _():
                acc[0][...] = prod

            @pl.when(k > 0)
            def _():
                acc[0][...] += prod

            @pl.when(k == nk - 1)
            def _():
                write(acc[0][...])

    return pl.pallas_call(
        body, name=name, grid=(M // tm, J * nn // jb, nk),
        in_specs=[pl.BlockSpec((tk, tm), lambda i, n, k: (k, i)), b_spec],
        out_specs=pl.BlockSpec((jb, tm, tn), lambda i, n, k: (n // nn, i, n % nn)),
        out_shape=jax.ShapeDtypeStruct((J, M, Nb), out_dtype),
        scratch_shapes=[] if nk == 1 else [pltpu.VMEM((tm, jb * tn), F32)],
        compiler_params=_cp(("parallel", "parallel", "arbitrary")),
    )(a, b)


def _rms_scale(v):
    return lax.rsqrt(jnp.mean(v * v, axis=-1, keepdims=True) + RMS_EPS)


def _rms_bwd(dy, v, g):
    r = _rms_scale(v)
    vh = v * r
    u = dy * g
    dv = r * (u - vh * jnp.mean(u * vh, axis=-1, keepdims=True))
    return dv, dy * vh


def _fold8(t):
    R, D = t.shape
    return jnp.sum(t.reshape(R // 8, 8, D), axis=0)


def _norm_fwd(x, z, g_post, g_pre, name):
    S, D = x.shape
    tr = _tile(S, 256, 8)
    has_z, has_pre = z is not None, g_pre is not None

    def body(*refs):
        it = iter(refs)
        x_ref = next(it)
        z_ref = next(it) if has_z else None
        gpost_ref = next(it) if has_z else None
        gpre_ref = next(it) if has_pre else None
        xn_ref = next(it) if has_z else None
        h_ref = next(it) if has_pre else None
        xn = x_ref[...]
        if has_z:
            zz = z_ref[...]
            xn = xn + zz * _rms_scale(zz) * gpost_ref[...]
            xn_ref[...] = xn
        if has_pre:
            h_ref[...] = (xn * _rms_scale(xn) * gpre_ref[...]).astype(BF16)

    row = pl.BlockSpec((tr, D), lambda i: (i, 0))
    gain = pl.BlockSpec((1, D), lambda i: (0, 0))
    ins, specs, outs, ospecs = [x], [row], [], []
    if has_z:
        ins += [z, g_post.reshape(1, D)]
        specs += [row, gain]
        outs.append(jax.ShapeDtypeStruct((S, D), F32))
        ospecs.append(row)
    if has_pre:
        ins.append(g_pre.reshape(1, D))
        specs.append(gain)
        outs.append(jax.ShapeDtypeStruct((S, D), BF16))
        ospecs.append(row)
    res = pl.pallas_call(body, name=name, grid=(S // tr,), in_specs=specs, out_specs=ospecs, out_shape=outs,
                         compiler_params=_cp(("parallel",)))(*ins)
    res = list(res)
    xn = res.pop(0) if has_z else x
    h = res.pop(0) if has_pre else None
    return xn, h


def _norm_bwd(dres, dh, xn, z, g_pre, g_post, name):
    S, D = xn.shape
    tr = _tile(S, 256, 8)
    has_res, has_pre, has_z = dres is not None, dh is not None, z is not None

    def body(*refs):
        it = iter(refs)
        dres_ref = next(it) if has_res else None
        dh_ref = next(it) if has_pre else None
        xn_ref = next(it) if has_pre else None
        gpre_ref = next(it) if has_pre else None
        z_ref = next(it) if has_z else None
        gpost_ref = next(it) if has_z else None
        t_ref = next(it)
        dz_ref = next(it) if has_z else None
        dgpre_ref = next(it) if has_pre else None
        dgpost_ref = next(it) if has_z else None
        first = pl.program_id(0) == 0
        t = dres_ref[...] if has_res else None
        if has_pre:
            dv, dg = _rms_bwd(dh_ref[...].astype(F32), xn_ref[...], gpre_ref[...])
            t = dv if t is None else t + dv
            part = _fold8(dg)

            @pl.when(first)
            def _():
                dgpre_ref[...] = part

            @pl.when(jnp.logical_not(first))
            def _():
                dgpre_ref[...] += part
        t_ref[...] = t
        if has_z:
            dv, dg = _rms_bwd(t, z_ref[...], gpost_ref[...])
            dz_ref[...] = dv.astype(BF16)
            part2 = _fold8(dg)

            @pl.when(first)
            def _():
                dgpost_ref[...] = part2

            @pl.when(jnp.logical_not(first))
            def _():
                dgpost_ref[...] += part2

    row = pl.BlockSpec((tr, D), lambda i: (i, 0))
    gain = pl.BlockSpec((1, D), lambda i: (0, 0))
    accs = pl.BlockSpec((8, D), lambda i: (0, 0))
    ins, specs = [], []
    if has_res:
        ins.append(dres)
        specs.append(row)
    if has_pre:
        ins += [dh, xn, g_pre.reshape(1, D)]
        specs += [row, row, gain]
    if has_z:
        ins += [z, g_post.reshape(1, D)]
        specs += [row, gain]
    outs, ospecs = [jax.ShapeDtypeStruct((S, D), F32)], [row]
    if has_z:
        outs.append(jax.ShapeDtypeStruct((S, D), BF16))
        ospecs.append(row)
    if has_pre:
        outs.append(jax.ShapeDtypeStruct((8, D), F32))
        ospecs.append(accs)
    if has_z:
        outs.append(jax.ShapeDtypeStruct((8, D), F32))
        ospecs.append(accs)
    res = list(pl.pallas_call(body, name=name, grid=(S // tr,), in_specs=specs, out_specs=ospecs, out_shape=outs,
                              compiler_params=_cp(("arbitrary",)))(*ins))
    t = res.pop(0)
    dz = res.pop(0) if has_z else None
    dgpre = jnp.sum(res.pop(0), axis=0) if has_pre else None
    dgpost = jnp.sum(res.pop(0), axis=0) if has_z else None
    return t, dz, dgpre, dgpost


def _loss_and_grad(y, target):
    S, D = y.shape
    tr = _tile(S, 256, 8)

    def body(y_ref, t_ref, dy_ref, l_ref):
        err = y_ref[...] - t_ref[...]
        dy_ref[...] = err * (1.0 / D)
        part = 0.5 * jnp.sum(jnp.mean(err * err, axis=-1, keepdims=True), axis=0, keepdims=True)
        first = pl.program_id(0) == 0

        @pl.when(first)
        def _():
            l_ref[...] = jnp.broadcast_to(part, l_ref.shape)

        @pl.when(jnp.logical_not(first))
        def _():
            l_ref[...] += jnp.broadcast_to(part, l_ref.shape)

    row = pl.BlockSpec((tr, D), lambda i: (i, 0))
    dy, l = pl.pallas_call(
        body, name="loss_grad", grid=(S // tr,), in_specs=[row, row],
        out_specs=[row, pl.BlockSpec((8, 128), lambda i: (0, 0))],
        out_shape=[jax.ShapeDtypeStruct((S, D), F32), jax.ShapeDtypeStruct((8, 128), F32)],
        compiler_params=_cp(("arbitrary",)))(y, target)
    return l[0, 0], dy


def _mm_gate_up(a, w3, name):
    S, D = a.shape
    J, D2, Nb = w3.shape
    F = J * Nb // 2
    assert D == D2
    tm, tn = _tile(S, 512, 8), _tile(math.gcd(Nb, F), 1408, 128)
    nn, nf = Nb // tn, F // tn

    def body(a_ref, wg_ref, wu_ref, gu_ref, act_ref):
        both = jnp.dot(a_ref[...], jnp.concatenate([wg_ref[...], wu_ref[...]], axis=1), preferred_element_type=F32)
        g, u = both[:, :tn], both[:, tn:]
        gu_ref[0] = g
        gu_ref[1] = u
        act_ref[...] = (g * jax.nn.sigmoid(g) * u).astype(BF16)

    return pl.pallas_call(
        body, name=name, grid=(nf, S // tm),
        in_specs=[pl.BlockSpec((tm, D), lambda n, i: (i, 0)),
                  pl.BlockSpec((None, D, tn), lambda n, i: (n // nn, 0, n % nn)),
                  pl.BlockSpec((None, D, tn), lambda n, i: ((n + nf) // nn, 0, (n + nf) % nn))],
        out_specs=[pl.BlockSpec((2, tm, tn), lambda n, i: (0, i, n)), pl.BlockSpec((tm, tn), lambda n, i: (i, n))],
        out_shape=[jax.ShapeDtypeStruct((2, S, F), F32), jax.ShapeDtypeStruct((S, F), BF16)],
        compiler_params=_cp(("parallel", "parallel")))(a, w3, w3)


def _mm_dx_down_swiglu(dy, wdown3, gu, name, after=None):
    S, D = dy.shape
    _, F, D2 = wdown3.shape
    assert D == D2
    tm, tn = _tile(S, 1024, 8), _tile(F, 512, 128)
    extra = [] if after is None else [after]

    def body(a_ref, b_ref, gu_ref, *rest):
        o_ref = rest[len(extra)]
        d = lax.dot_general(a_ref[...], b_ref[...], (((1,), (1,)), ((), ())), preferred_element_type=F32)
        g, u = gu_ref[0], gu_ref[1]
        sg = jax.nn.sigmoid(g)
        o_ref[0] = (d * u * (sg * (1.0 + g * (1.0 - sg)))).astype(BF16)
        o_ref[1] = (d * g * sg).astype(BF16)

    return pl.pallas_call(
        body, name=name, grid=(S // tm, F // tn),
        in_specs=[pl.BlockSpec((tm, D), lambda i, n: (i, 0)), pl.BlockSpec((None, tn, D), lambda i, n: (0, n, 0)),
                  pl.BlockSpec((2, tm, tn), lambda i, n: (0, i, n))] + [pl.BlockSpec(memory_space=pl.ANY)] * len(extra),
        out_specs=pl.BlockSpec((2, tm, tn), lambda i, n: (0, i, n)),
        out_shape=jax.ShapeDtypeStruct((2, S, F), BF16),
        compiler_params=_cp(("parallel", "parallel")))(dy, wdown3, gu, *extra)


def _cross_scores(q, k, scale):
    s = lax.dot_general(q, k, (((1,), (1,)), ((), ())), preferred_element_type=F32) * scale
    m = jnp.max(s, axis=-1, keepdims=True)
    e = jnp.exp(s - m)
    return e / jnp.sum(e, axis=-1, keepdims=True)


def _cross_fwd(cq, ckv, name):
    S, D = cq.shape
    Nm = ckv.shape[0]
    dh = D // CROSS_HEADS
    tq = _tile(S, 512, 8)
    scale = dh ** -0.5

    def body(q_ref, kv_ref, o_ref):
        for h in range(CROSS_HEADS):
            cols = slice(h * dh, (h + 1) * dh)
            p = _cross_scores(q_ref[:, cols], kv_ref[:, cols], scale)
            o_ref[:, cols] = jnp.dot(p.astype(BF16), kv_ref[:, D + h * dh:D + (h + 1) * dh],
                                     preferred_element_type=F32).astype(BF16)

    return pl.pallas_call(
        body, name=name, grid=(S // tq,),
        in_specs=[pl.BlockSpec((tq, D), lambda i: (i, 0)), pl.BlockSpec((Nm, 2 * D), lambda i: (0, 0))],
        out_specs=pl.BlockSpec((tq, D), lambda i: (i, 0)),
        out_shape=jax.ShapeDtypeStruct((S, D), BF16), compiler_params=_cp(("parallel",)))(cq, ckv)


def _cross_bwd(cq, ckv, do, name):
    S, D = cq.shape
    Nm = ckv.shape[0]
    dh = D // CROSS_HEADS
    tq = _tile(S, 512, 8)
    scale = dh ** -0.5

    def body(q_ref, kv_ref, do_ref, dq_ref, dkv_ref):
        first = pl.program_id(0) == 0

        @pl.when(first)
        def _():
            dkv_ref[...] = jnp.zeros_like(dkv_ref)

        for h in range(CROSS_HEADS):
            cols = slice(h * dh, (h + 1) * dh)
            vcols = slice(D + h * dh, D + (h + 1) * dh)
            q, k, v = q_ref[:, cols], kv_ref[:, cols], kv_ref[:, vcols]
            doh = do_ref[:, cols].astype(BF16)
            p = _cross_scores(q, k, scale)
            dp = lax.dot_general(doh, v, (((1,), (1,)), ((), ())), preferred_element_type=F32)
            ds = (p * (dp - jnp.sum(p * dp, axis=-1, keepdims=True)) * scale).astype(BF16)
            dq_ref[:, cols] = jnp.dot(ds, k, preferred_element_type=F32).astype(BF16)
            dkv_ref[:, cols] += lax.dot_general(ds, q, (((0,), (0,)), ((), ())), preferred_element_type=F32)
            dkv_ref[:, vcols] += lax.dot_general(p.astype(BF16), doh, (((0,), (0,)), ((), ())),
                                                 preferred_element_type=F32)

    return pl.pallas_call(
        body, name=name, grid=(S // tq,),
        in_specs=[pl.BlockSpec((tq, D), lambda i: (i, 0)), pl.BlockSpec((Nm, 2 * D), lambda i: (0, 0)),
                  pl.BlockSpec((tq, D), lambda i: (i, 0))],
        out_specs=[pl.BlockSpec((tq, D), lambda i: (i, 0)), pl.BlockSpec((Nm, 2 * D), lambda i: (0, 0))],
        out_shape=[jax.ShapeDtypeStruct((S, D), BF16), jax.ShapeDtypeStruct((Nm, 2 * D), F32)],
        compiler_params=_cp(("arbitrary",)))(cq, ckv, do)


def _rel_bucket(dist):
    max_exact = REL_BUCKETS // 2
    d_f = jnp.maximum(dist, 1).astype(F32)
    large = max_exact + (jnp.log(d_f / max_exact) / math.log(REL_MAX_DIST / max_exact)
                         * (REL_BUCKETS - max_exact)).astype(jnp.int32)
    large = jnp.minimum(large, REL_BUCKETS - 1)
    return jnp.where(dist < max_exact, dist, large)


def _branch_tables(dilation):
    qi = jnp.arange(Q_BLOCK)[:, None]
    kj = jnp.arange(2 * Q_BLOCK)[None, :]
    m = qi - kj + Q_BLOCK
    return _rel_bucket(jnp.maximum(m, 0) * dilation), (m >= 0) & (m <= Q_BLOCK)


def _attn_logits(q, kcat, bias, first_block, scale):
    s = lax.dot_general(q, kcat, (((1,), (1,)), ((), ())), preferred_element_type=F32) * scale + bias
    col = lax.broadcasted_iota(jnp.int32, s.shape, 1)
    return jnp.where(jnp.logical_and(first_block, col < Q_BLOCK), NEG_INF, s)


def _attn_branch_fwd(proj, bias, dilation, H, name):
    S, NC = proj.shape
    AW = H * HEAD
    r = dilation
    hb = H if r == 1 else 1
    G = Q_BLOCK * r
    ng, nh = S // G, H // hb
    scale = HEAD ** -0.5

    def body(q_ref, kc_ref, vc_ref, b_ref, o_ref, l_ref, kp_ref, vp_ref):
        first_block = pl.program_id(1) == 0
        h0 = pl.program_id(0) * hb

        @pl.when(first_block)
        def _():
            kp_ref[...] = jnp.zeros_like(kp_ref)
            vp_ref[...] = jnp.zeros_like(vp_ref)

        pairs = [(pl.ds(rho, Q_BLOCK, stride=r) if r > 1 else slice(None), hh) for rho in range(r) for hh in range(hb)]
        for g0 in range(0, len(pairs), ATTN_PAIRS_PER_STAGE):
            group = pairs[g0:g0 + ATTN_PAIRS_PER_STAGE]
            sl = [(rows, slice(hh * HEAD, (hh + 1) * HEAD)) for rows, hh in group]
            q = [q_ref[rw, cl].astype(BF16) for rw, cl in sl]
            kcat = [jnp.concatenate([kp_ref[rw, cl], kc_ref[rw, cl]], axis=0).astype(BF16) for rw, cl in sl]
            vcat = [jnp.concatenate([vp_ref[rw, cl], vc_ref[rw, cl]], axis=0).astype(BF16) for rw, cl in sl]
            s = [_attn_logits(q[i], kcat[i], b_ref[h0 + hh], first_block, scale) for i, (_, hh) in enumerate(group)]
            m = [jnp.max(si, axis=-1, keepdims=True) for si in s]
            e = [jnp.exp(si - mi) for si, mi in zip(s, m)]
            den = [jnp.sum(ei, axis=-1, keepdims=True) for ei in e]
            o = [jnp.dot(ei.astype(BF16), vi, preferred_element_type=F32) for ei, vi in zip(e, vcat)]
            for i, (rw, cl) in enumerate(sl):
                o_ref[rw, cl] = o[i] / den[i]
                l_ref[rw, cl] = jnp.broadcast_to(m[i] + jnp.log(den[i]), (Q_BLOCK, HEAD))
        kp_ref[...] = kc_ref[...]
        vp_ref[...] = vc_ref[...]

    def blk(part):
        return pl.BlockSpec((G, hb * HEAD), lambda h, n: (n, part * nh + h))

    out = pl.BlockSpec((G, hb * HEAD), lambda h, n: (n, h))
    return pl.pallas_call(
        body, name=name, grid=(nh, ng),
        in_specs=[blk(0), blk(1), blk(2), pl.BlockSpec((H, Q_BLOCK, 2 * Q_BLOCK), lambda h, n: (0, 0, 0))],
        out_specs=[out, out],
        out_shape=[jax.ShapeDtypeStruct((S, AW), F32)] * 2,
        scratch_shapes=[pltpu.VMEM((G, hb * HEAD), F32), pltpu.VMEM((G, hb * HEAD), F32)],
        compiler_params=_cp(("parallel", "arbitrary")))(proj, proj, proj, bias)


def _attn_merge(outs, lses, width_out, name):
    S, AW = outs[0].shape
    tr = _tile(S, 256, 8)

    def body(o0, o1, o2, l0, l1, l2, cat_ref, om_ref, lt_ref):
        a, b, c = l0[...], l1[...], l2[...]
        m = jnp.maximum(jnp.maximum(a, b), c)
        ea, eb, ec = jnp.exp(a - m), jnp.exp(b - m), jnp.exp(c - m)
        tot = ea + eb + ec
        o = (ea * o0[...] + eb * o1[...] + ec * o2[...]) / tot
        om_ref[...] = o
        cat_ref[...] = o.astype(BF16)
        lt_ref[...] = m + jnp.log(tot)

    row = pl.BlockSpec((tr, AW), lambda i: (i, 0))
    return pl.pallas_call(
        body, name=name, grid=(S // tr,), in_specs=[row] * 6, out_specs=[row, row, row],
        out_shape=[jax.ShapeDtypeStruct((S, width_out), BF16), jax.ShapeDtypeStruct((S, AW), F32),
                   jax.ShapeDtypeStruct((S, AW), F32)],
        compiler_params=_cp(("parallel",)))(*outs, *lses)


def _attn_delta(dcat, o_attn, H, name, after=None):
    S, AW = o_attn.shape
    tr = _tile(S, 256, 8)
    extra = [] if after is None else [after]

    def body(d_ref, o_ref, *rest):
        out_ref = rest[len(extra)]
        for h in range(H):
            cols = slice(h * HEAD, (h + 1) * HEAD)
            out_ref[:, cols] = jnp.broadcast_to(
                jnp.sum(d_ref[:, cols] * o_ref[:, cols], axis=-1, keepdims=True), (tr, HEAD))

    row = pl.BlockSpec((tr, AW), lambda i: (i, 0))
    return pl.pallas_call(body, name=name, grid=(S // tr,),
                          in_specs=[row, row] + [pl.BlockSpec(memory_space=pl.ANY)] * len(extra), out_specs=row,
                          out_shape=jax.ShapeDtypeStruct((S, AW), F32),
                          compiler_params=_cp(("parallel",)))(dcat, o_attn, *extra)


def _attn_branch_bwd(proj, dcat, lse_tot, delta, bias, dilation, H, name):
    S, NC = proj.shape
    AW = H * HEAD
    r = dilation
    hb = H if r == 1 else 1
    G = Q_BLOCK * r
    ng, nh = S // G, H // hb
    scale = HEAD ** -0.5

    def body(q_ref, kp_ref, kc_ref, vp_ref, vc_ref, do_ref, l_ref, t_ref, b_ref,
             dq_ref, dk_ref, dv_ref, db_ref, ck_ref, cv_ref):
        hblk, n = pl.program_id(0), pl.program_id(1)
        h0 = hblk * hb

        @pl.when(jnp.logical_and(hblk == 0, n == 0))
        def _():
            db_ref[...] = jnp.zeros_like(db_ref)

        @pl.when(n == 0)
        def _():
            ck_ref[...] = jnp.zeros_like(ck_ref)
            cv_ref[...] = jnp.zeros_like(cv_ref)

        @pl.when(n < ng)
        def _():
            pairs = [(pl.ds(rho, Q_BLOCK, stride=r) if r > 1 else slice(None), hh)
                     for rho in range(r) for hh in range(hb)]
            for g0 in range(0, len(pairs), ATTN_PAIRS_PER_STAGE):
                group = pairs[g0:g0 + ATTN_PAIRS_PER_STAGE]
                idx = range(len(group))
                sl = [(rows, slice(hh * HEAD, (hh + 1) * HEAD)) for rows, hh in group]
                q = [q_ref[rw, cl].astype(BF16) for rw, cl in sl]
                kcat = [jnp.concatenate([kp_ref[rw, cl], kc_ref[rw, cl]], axis=0).astype(BF16) for rw, cl in sl]
                vcat = [jnp.concatenate([vp_ref[rw, cl], vc_ref[rw, cl]], axis=0).astype(BF16) for rw, cl in sl]
                doh = [do_ref[rw, cl].astype(BF16) for rw, cl in sl]
                s = [_attn_logits(q[i], kcat[i], b_ref[h0 + group[i][1]], n == 0, scale) for i in idx]
                dp = [lax.dot_general(doh[i], vcat[i], (((1,), (1,)), ((), ())), preferred_element_type=F32) for i in idx]
                p, ds = [], []
                for i, (rw, cl) in enumerate(sl):
                    lt, dl = l_ref[rw, cl], t_ref[rw, cl]
                    p.append(jnp.exp(s[i] - jnp.concatenate([lt, lt], axis=1)))
                    ds.append(p[i] * (dp[i] - jnp.concatenate([dl, dl], axis=1)))
                dsb = [(d * scale).astype(BF16) for d in ds]
                dq = [jnp.dot(dsb[i], kcat[i], preferred_element_type=F32) for i in idx]
                dkc = [lax.dot_general(dsb[i], q[i], (((0,), (0,)), ((), ())), preferred_element_type=F32) for i in idx]
                dvc = [lax.dot_general(p[i].astype(BF16), doh[i], (((0,), (0,)), ((), ())), preferred_element_type=F32)
                       for i in idx]
                for i, (rw, cl) in enumerate(sl):
                    dq_ref[rw, cl] = dq[i]
                    dk_ref[rw, cl] = ck_ref[rw, cl] + dkc[i][:Q_BLOCK]
                    dv_ref[rw, cl] = cv_ref[rw, cl] + dvc[i][:Q_BLOCK]
                    ck_ref[rw, cl] = dkc[i][Q_BLOCK:]
                    cv_ref[rw, cl] = dvc[i][Q_BLOCK:]
                for i in idx:
                    db_ref[h0 + group[i][1]] += ds[i]

        @pl.when(n == ng)
        def _():
            dk_ref[...] = ck_ref[...]
            dv_ref[...] = cv_ref[...]

    last = ng - 1

    def cur(part):
        return pl.BlockSpec((G, hb * HEAD), lambda h, n: (jnp.minimum(n, last), part * nh + h))

    def prev(part):
        return pl.BlockSpec((G, hb * HEAD), lambda h, n: (jnp.clip(n - 1, 0, last), part * nh + h))

    table = pl.BlockSpec((H, Q_BLOCK, 2 * Q_BLOCK), lambda h, n: (0, 0, 0))
    return pl.pallas_call(
        body, name=name, grid=(nh, ng + 1),
        in_specs=[cur(0), prev(1), cur(1), prev(2), cur(2), cur(0), cur(0), cur(0), table],
        out_specs=[cur(0), prev(0), prev(0), table],
        out_shape=[jax.ShapeDtypeStruct((S, AW), F32)] * 3 + [jax.ShapeDtypeStruct((H, Q_BLOCK, 2 * Q_BLOCK), F32)],
        scratch_shapes=[pltpu.VMEM((G, hb * HEAD), F32), pltpu.VMEM((G, hb * HEAD), F32)],
        compiler_params=_cp(("arbitrary", "arbitrary")))(proj, proj, proj, proj, proj, dcat, lse_tot, delta, bias)


def _bias_grad(dbs, onehots, H):
    def body(d0, d1, d2, e0, e1, e2, o_ref):
        acc = jnp.zeros((H, REL_BUCKETS), F32)
        for d, e in ((d0, e0), (d1, e1), (d2, e2)):
            acc = acc + lax.dot_general(d[...], e[...], (((1,), (1,)), ((), ())), preferred_element_type=F32,
                                        precision=lax.Precision.HIGHEST)
        o_ref[...] = acc

    flat = [d.reshape(H, 2 * Q_BLOCK * Q_BLOCK) for d in dbs]
    return pl.pallas_call(body, name="bias_grad", out_shape=jax.ShapeDtypeStruct((H, REL_BUCKETS), F32),
                          compiler_params=pltpu.CompilerParams(vmem_limit_bytes=VMEM_LIMIT))(*flat, *onehots)


def _assemble_dproj(dqkv, dhg, name):
    S, AW = dqkv[0][0].shape
    tr = _tile(S, 256, 8)
    ins = [dqkv[g][c] for g in range(3) for c in range(3)] + list(dhg)

    def body(*refs):
        o_ref = refs[-1]
        for c in range(3):
            o_ref[:, c * AW:(c + 1) * AW] = (refs[c][...] + refs[3 + c][...] + refs[6 + c][...]).astype(BF16)
        for k in range(4):
            o_ref[:, (3 + k) * AW:(4 + k) * AW] = refs[9 + k][...]

    row = pl.BlockSpec((tr, AW), lambda i: (i, 0))
    return pl.pallas_call(
        body, name=name, grid=(S // tr,), in_specs=[row] * 13,
        out_specs=pl.BlockSpec((tr, 7 * AW), lambda i: (i, 0)),
        out_shape=jax.ShapeDtypeStruct((S, 7 * AW), BF16),
        compiler_params=_cp(("parallel",)))(*ins)


def _tri(n, upper):
    r = lax.broadcasted_iota(jnp.int32, (n, n), 0)
    c = lax.broadcasted_iota(jnp.int32, (n, n), 1)
    return jnp.where(c >= r if upper else c <= r, 1.0, 0.0).astype(F32)


def _hg_gates(fz, qz, lb):
    sig = jax.nn.sigmoid(fz)
    f = lb + (1.0 - lb) * sig
    sq = jax.nn.sigmoid(qz)
    return sig, f, jnp.log(f), 1.0 - f, qz * sq, sq


def _hg_fwd(proj, cat, lb, gain, HH, name):
    S, NC = proj.shape
    W = HH * HEAD
    C = HG_CHUNK
    hb = min(HG_HEADS_PER_STEP, HH)
    nh = HH // hb
    tb = _tile(S, 2048 // hb, C)
    ncb = tb // C
    base = 3 * nh
    cat_cols = cat.shape[1] // (hb * HEAD)

    def body(fz_ref, iv_ref, qz_ref, gz_ref, lb_ref, g_ref, cat_in, cat_ref, o_ref, st_ref, state):
        del cat_in

        @pl.when(pl.program_id(1) == 0)
        def _():
            state[...] = jnp.zeros_like(state)

        tri = _tri(C, False)
        row8 = lax.broadcasted_iota(jnp.int32, (8, 1), 0)

        def chunk(c, carry):
            rows = pl.ds(pl.multiple_of(c * C, C), C)
            heads = range(hb)
            cols = [slice(hh * HEAD, (hh + 1) * HEAD) for hh in heads]
            vv = [iv_ref[rows, cl] for cl in cols]
            gates = [_hg_gates(fz_ref[rows, cl], qz_ref[rows, cl], lb_ref[:, cl]) for cl in cols]
            kk, qq = [gt[3] for gt in gates], [gt[4] for gt in gates]
            b = [jnp.dot(tri, gt[2], preferred_element_type=F32, precision=lax.Precision.HIGHEST) for gt in gates]
            bl = [bh[C - 1:C, :] for bh in b]
            st = [state[hh] for hh in heads]
            stb = [s_.astype(BF16) for s_ in st]
            for hh in heads:
                st_ref[hh, c] = stb[hh]
            o = [lax.dot_general((qq[hh] * jnp.exp(b[hh])).astype(BF16), stb[hh], (((1,), (1,)), ((), ())),
                                 preferred_element_type=F32) for hh in heads]
            upd = [lax.dot_general(vv[hh].astype(BF16), (kk[hh] * jnp.exp(bl[hh] - b[hh])).astype(BF16),
                                   (((0,), (0,)), ((), ())), preferred_element_type=F32) for hh in heads]
            for hh in heads:
                state[hh] = st[hh] * jnp.exp(bl[hh]) + upd[hh]
            wts = {}
            for t0 in range(0, C, 8):
                for s in range(min(C, t0 + 8)):
                    for hh in heads:
                        diff = b[hh][t0:t0 + 8, :] - b[hh][s:s + 1, :]
                        if s >= t0:
                            diff = jnp.minimum(diff, 0.0)
                        a = jnp.sum(qq[hh][t0:t0 + 8, :] * jnp.exp(diff) * kk[hh][s:s + 1, :], axis=-1, keepdims=True)
                        wts[t0, s, hh] = jnp.where(row8 >= s - t0, a, 0.0) if s >= t0 else a
            for hh in heads:
                tiles = []
                for t0 in range(0, C, 8):
                    acc = o[hh][t0:t0 + 8, :]
                    for s in range(min(C, t0 + 8)):
                        acc = acc + wts[t0, s, hh] * vv[hh][s:s + 1, :]
                    tiles.append(acc)
                oh = jnp.concatenate(tiles, axis=0)
                o_ref[rows, cols[hh]] = oh
                n = oh * lax.rsqrt(jnp.mean(oh * oh, axis=-1, keepdims=True) + RMS_EPS)
                gz = gz_ref[rows, cols[hh]]
                cat_ref[rows, cols[hh]] = (n * g_ref[:, cols[hh]] * (gz * jax.nn.sigmoid(gz))).astype(BF16)
            return carry

        lax.fori_loop(0, ncb, chunk, 0)

    def col(k):
        return pl.BlockSpec((tb, hb * HEAD), lambda h, i: (i, base + k * nh + h))

    vec = pl.BlockSpec((1, hb * HEAD), lambda h, i: (0, h))
    cat_out, o_raw, states = pl.pallas_call(
        body, name=name, grid=(nh, S // tb),
        in_specs=[col(0), col(1), col(2), col(3), vec, vec, pl.BlockSpec(memory_space=pl.ANY)],
        out_specs=[pl.BlockSpec((tb, hb * HEAD), lambda h, i: (i, cat_cols - nh + h)),
                   pl.BlockSpec((tb, hb * HEAD), lambda h, i: (i, h)),
                   pl.BlockSpec((hb, ncb, HEAD, HEAD), lambda h, i: (h, i, 0, 0))],
        out_shape=[jax.ShapeDtypeStruct(cat.shape, BF16), jax.ShapeDtypeStruct((S, W), F32),
                   jax.ShapeDtypeStruct((HH, S // C, HEAD, HEAD), BF16)],
        scratch_shapes=[pltpu.VMEM((hb, HEAD, HEAD), F32)],
        input_output_aliases={6: 0},
        compiler_params=_cp(("parallel", "arbitrary")))(proj, proj, proj, proj, lb.reshape(1, W), gain.reshape(1, W), cat)
    return cat_out, o_raw, states


def _hg_bwd(proj, dcat, o_raw, states, lb, gain, after, HH, name):
    S, NC = proj.shape
    W = HH * HEAD
    C = HG_CHUNK
    hb = min(HG_HEADS_PER_STEP, HH)
    nh = HH // hb
    tb = _tile(S, 2048 // hb, C)
    ncb = tb // C
    nt = S // tb
    base = 3 * nh
    dcat_cols = dcat.shape[1] // (hb * HEAD)

    def body(fz_ref, iv_ref, qz_ref, gz_ref, dc_ref, o_ref, st_ref, lb_ref, g_ref, after_ref,
             dfz_ref, div_ref, dqz_ref, dgz_ref, dlb_ref, dg_ref, dstate):
        del after_ref

        @pl.when(pl.program_id(1) == 0)
        def _():
            dstate[...] = jnp.zeros_like(dstate)
            dlb_ref[...] = jnp.zeros_like(dlb_ref)
            dg_ref[...] = jnp.zeros_like(dg_ref)

        tri_lo, tri_up = _tri(C, False), _tri(C, True)
        row = lax.broadcasted_iota(jnp.int32, (C, 1), 0)
        row8 = lax.broadcasted_iota(jnp.int32, (8, 1), 0)

        def chunk(cc, carry):
            c = ncb - 1 - cc
            rows = pl.ds(pl.multiple_of(c * C, C), C)
            heads = range(hb)
            starts = list(range(0, C, 8))
            hd = []
            for hh in heads:
                cols = slice(hh * HEAD, (hh + 1) * HEAD)
                lbv, gv = lb_ref[:, cols], g_ref[:, cols]
                fz, vv, qz, gz = fz_ref[rows, cols], iv_ref[rows, cols], qz_ref[rows, cols], gz_ref[rows, cols]
                sig, f, lf, kk, qq, sq = _hg_gates(fz, qz, lbv)
                b = jnp.dot(tri_lo, lf, preferred_element_type=F32, precision=lax.Precision.HIGHEST)
                bl = b[C - 1:C, :]
                eb, ebl = jnp.exp(b), jnp.exp(bl)
                ekb = jnp.exp(bl - b)
                qh, kh = qq * eb, kk * ekb
                o = o_ref[rows, cols]
                dhg = dc_ref[rows, cols]
                sgz = jax.nn.sigmoid(gz)
                rr = lax.rsqrt(jnp.mean(o * o, axis=-1, keepdims=True) + RMS_EPS)
                nrm = o * rr
                dpre = dhg * (gz * sgz)
                dgz_ref[rows, cols] = (dhg * nrm * gv * (sgz * (1.0 + gz * (1.0 - sgz)))).astype(BF16)
                dg_ref[:, cols] += jnp.sum(dpre * nrm, axis=0, keepdims=True)
                dn = dpre * gv
                do = rr * (dn - nrm * jnp.mean(dn * nrm, axis=-1, keepdims=True))
                dob = do.astype(BF16)
                st0 = st_ref[hh, c]
                dst1 = dstate[hh]
                dst1b = dst1.astype(BF16)
                dqh = jnp.dot(dob, st0, preferred_element_type=F32)
                dv = lax.dot_general(kh.astype(BF16), dst1b, (((1,), (1,)), ((), ())), preferred_element_type=F32)
                dkh = jnp.dot(vv.astype(BF16), dst1b, preferred_element_type=F32)
                dstate[hh] = dst1 * ebl + lax.dot_general(dob, qh.astype(BF16), (((0,), (0,)), ((), ())),
                                                          preferred_element_type=F32)
                extra = (jnp.sum(dkh * kh, axis=0, keepdims=True)
                         + ebl * jnp.sum(st0.astype(F32) * dst1, axis=0, keepdims=True))
                hd.append(dict(cols=cols, lbv=lbv, sig=sig, f=f, kk=kk, qq=qq, sq=sq, qz=qz, vv=vv, b=b, do=do, dv=dv,
                               extra=extra, dq=dqh * eb, dk=dkh * ekb))
            red = {}
            for s in range(C):
                for ti, t0 in enumerate(starts):
                    if t0 + 8 <= s:
                        continue
                    for hh in heads:
                        v = hd[hh]
                        diff = v["b"][t0:t0 + 8, :] - v["b"][s:s + 1, :]
                        if s >= t0:
                            diff = jnp.minimum(diff, 0.0)
                        dec = jnp.exp(diff)
                        da = jnp.sum(v["do"][t0:t0 + 8, :] * v["vv"][s:s + 1, :], axis=-1, keepdims=True)
                        a = jnp.sum(v["qq"][t0:t0 + 8, :] * dec * v["kk"][s:s + 1, :], axis=-1, keepdims=True)
                        if s >= t0:
                            keep = row8 >= s - t0
                            da, a = jnp.where(keep, da, 0.0), jnp.where(keep, a, 0.0)
                        red[s, ti, hh] = (da * dec, a)
            for hh in heads:
                v = hd[hh]
                cols, kk, qq, do = v["cols"], v["kk"], v["qq"], v["do"]
                dq_t = [v["dq"][t0:t0 + 8, :] for t0 in starts]
                dk_rows, dv_rows = [], []
                for s in range(C):
                    dk_s = dv_s = None
                    for ti, t0 in enumerate(starts):
                        if t0 + 8 <= s:
                            continue
                        gd, a = red[s, ti, hh]
                        dq_t[ti] = dq_t[ti] + gd * kk[s:s + 1, :]
                        pk = jnp.sum(gd * qq[t0:t0 + 8, :], axis=0, keepdims=True)
                        pv = jnp.sum(a * do[t0:t0 + 8, :], axis=0, keepdims=True)
                        dk_s = pk if dk_s is None else dk_s + pk
                        dv_s = pv if dv_s is None else dv_s + pv
                    dk_rows.append(dk_s)
                    dv_rows.append(dv_s)
                dq = jnp.concatenate(dq_t, axis=0)
                dk = v["dk"] + jnp.concatenate(dk_rows, axis=0)
                dv = v["dv"] + jnp.concatenate(dv_rows, axis=0)
                db = qq * dq - kk * dk + jnp.where(row == C - 1, v["extra"], 0.0)
                dlf = jnp.dot(tri_up, db, preferred_element_type=F32, precision=lax.Precision.HIGHEST)
                df = dlf / v["f"] - dk
                sig, sq, qz = v["sig"], v["sq"], v["qz"]
                dfz_ref[rows, cols] = (df * (1.0 - v["lbv"]) * sig * (1.0 - sig)).astype(BF16)
                dlb_ref[:, cols] += jnp.sum(df * (1.0 - sig), axis=0, keepdims=True)
                dqz_ref[rows, cols] = (dq * (sq * (1.0 + qz * (1.0 - sq)))).astype(BF16)
                div_ref[rows, cols] = dv.astype(BF16)
            return carry

        lax.fori_loop(0, ncb, chunk, 0)

    def col(k):
        return pl.BlockSpec((tb, hb * HEAD), lambda h, i: (nt - 1 - i, base + k * nh + h))

    ocol = pl.BlockSpec((tb, hb * HEAD), lambda h, i: (nt - 1 - i, h))
    vec = pl.BlockSpec((1, hb * HEAD), lambda h, i: (0, h))
    outs = pl.pallas_call(
        body, name=name, grid=(nh, nt),
        in_specs=[col(0), col(1), col(2), col(3),
                  pl.BlockSpec((tb, hb * HEAD), lambda h, i: (nt - 1 - i, dcat_cols - nh + h)),
                  pl.BlockSpec((tb, hb * HEAD), lambda h, i: (nt - 1 - i, h)),
                  pl.BlockSpec((hb, ncb, HEAD, HEAD), lambda h, i: (h, nt - 1 - i, 0, 0)), vec, vec,
                  pl.BlockSpec(memory_space=pl.ANY)],
        out_specs=[ocol, ocol, ocol, ocol, vec, vec],
        out_shape=[jax.ShapeDtypeStruct((S, W), BF16)] * 4 + [jax.ShapeDtypeStruct((1, W), F32)] * 2,
        scratch_shapes=[pltpu.VMEM((hb, HEAD, HEAD), F32)],
        compiler_params=_cp(("parallel", "arbitrary")))(proj, proj, proj, proj, dcat, o_raw, states,
                                                        lb.reshape(1, W), gain.reshape(1, W), after)
    return outs


def _lb_all(lb_logits):
    p = jax.nn.softmax(lb_logits.astype(F32), axis=0)
    return jnp.cumsum(p, axis=0) - p


def _local_step(x, mem, target, rel_bias, lb_logits, gains, hg_norm, comm):
    S, D = x.shape
    L = gains.shape[0]
    H = (D // 2) // HEAD
    lb = _lb_all(lb_logits)
    tables = [_branch_tables(r) for _, r in BRANCHES]
    onehots = [jnp.where(ok[None], jax.nn.one_hot(bk, REL_BUCKETS, dtype=F32, axis=0), 0.0)
               .reshape(REL_BUCKETS, 2 * Q_BLOCK * Q_BLOCK) for bk, ok in tables]
    bias = [jnp.where(ok[None], jnp.dot(rel_bias.astype(F32).T, oh, precision=lax.Precision.HIGHEST)
                      .reshape(H, Q_BLOCK, 2 * Q_BLOCK), NEG_INF) for oh, (_, ok) in zip(onehots, tables)]

    saved = []
    _, h = _norm_fwd(x, None, None, gains[0, 0], "norm_first")
    xl = x
    for l in range(L):
        g = gains[l]
        w_in, token = comm.weights(l, xl)
        proj = _mm_nn(h, w_in, F32, "mm_in", after=token)
        outs, lses = [], []
        for gi, (_, r) in enumerate(BRANCHES):
            o, ls = _attn_branch_fwd(proj, bias[gi], r, H, f"attn_fwd_d{r}")
            outs.append(o)
            lses.append(ls)
        cat, o_attn, lse_tot = _attn_merge(outs, lses, D, "attn_merge")
        cat, o_raw, states = _hg_fwd(proj, cat, lb[l], hg_norm[l], H, "hgrn_fwd")
        w = dict(comm.rest(l, cat), **{"in": w_in})
        mix = _mm_nn(cat, w["out"], F32, "mm_out")
        x1, hc = _norm_fwd(xl, mix, g[1], g[2], "norm_post_pre")
        _, mn = _norm_fwd(mem, None, None, g[3], "norm_mem")
        cq = _mm_nn(hc, w["cq"], BF16, "mm_cq")
        ckv = _mm_nn(mn, w["ckv"], BF16, "mm_ckv")
        co = _cross_fwd(cq, ckv, "cross_fwd")
        cop = _mm_nn(co, w["co"], F32, "mm_co")
        x2, hf = _norm_fwd(x1, cop, g[4], g[5], "norm_post_pre")
        gu, act = _mm_gate_up(hf, w["gu"], "mm_gu")
        y = _mm_nn(act, w["down"], F32, "mm_down")
        g_next = gains[l + 1, 0] if l + 1 < L else None
        x3, h_next = _norm_fwd(x2, y, g[6], g_next, "norm_post_pre" if l + 1 < L else "norm_post")
        saved.append(dict(x0=xl, h0=h, proj=proj, cat=cat, o_attn=o_attn, lse_tot=lse_tot, o_raw=o_raw, states=states,
                          mix=mix, x1=x1, hc=hc, mn=mn, cq=cq, ckv=ckv, co=co, cop=cop, x2=x2, hf=hf, gu=gu, act=act,
                          y=y, x3=x3, w=w))
        comm.layer_done(l, x3)
        xl, h = x3, h_next

    loss, dres = _loss_and_grad(xl, target)

    dh_next = bwd_token = None
    d_gains = [[None] * 7 for _ in range(L)]
    d_lb = [None] * L
    d_hg = [None] * L
    d_bias = [jnp.zeros((H, Q_BLOCK, 2 * Q_BLOCK), F32) for _ in BRANCHES]
    for l in reversed(range(L)):
        g, sv = gains[l], saved[l]
        w = sv["w"]
        g_next = gains[l + 1, 0] if l + 1 < L else None
        t3, dy, dg_next, dg6 = _norm_bwd(dres, dh_next, sv["x3"], sv["y"], g_next, g[6], "norm_bwd")
        if l + 1 < L:
            d_gains[l + 1][0] = dg_next
        d_gains[l][6] = dg6
        gw = {}
        gw["down"] = _mm_tn(sv["act"], dy, 1, BF16, "mm_dw_down")
        dgu = _mm_dx_down_swiglu(dy, w["down"], sv["gu"], "mm_dx_down", after=bwd_token)
        gw["gu"] = _mm_tn(sv["hf"], dgu, w["gu"].shape[0], BF16, "mm_dw_gu")
        dhf = _mm_nt(dgu, w["gu"], F32, "mm_dx_gu")
        t2, dcop, d_gains[l][5], d_gains[l][4] = _norm_bwd(t3, dhf, sv["x2"], sv["cop"], g[5], g[4], "norm_bwd")
        gw["co"] = _mm_tn(sv["co"], dcop, 1, BF16, "mm_dw_sq")
        dco = _mm_nt(dcop, w["co"], F32, "mm_dx_sq")
        dcq, dckv = _cross_bwd(sv["cq"], sv["ckv"], dco, "cross_bwd")
        gw["cq"] = _mm_tn(sv["hc"], dcq, 1, BF16, "mm_dw_sq")
        dhc = _mm_nt(dcq, w["cq"], F32, "mm_dx_sq")
        gw["ckv"] = _mm_tn(sv["mn"], dckv, w["ckv"].shape[0], BF16, "mm_dw_ckv")
        dmn = _mm_nt(dckv, w["ckv"], F32, "mm_dx_ckv")
        _, _, d_gains[l][3], _ = _norm_bwd(None, dmn, mem, None, g[3], None, "norm_bwd_mem")
        t1, dmix, d_gains[l][2], d_gains[l][1] = _norm_bwd(t2, dhc, sv["x1"], sv["mix"], g[2], g[1], "norm_bwd")
        gw["out"] = _mm_tn(sv["cat"], dmix, 1, BF16, "mm_dw_sq")
        dcat = _mm_nt(dmix, w["out"], F32, "mm_dx_sq")
        delta = _attn_delta(dcat, sv["o_attn"], H, "attn_delta", after=comm.early_grads(l, gw))
        dqkv = []
        for gi, (_, r) in enumerate(BRANCHES):
            dq, dk, dv, db = _attn_branch_bwd(sv["proj"], dcat, sv["lse_tot"], delta, bias[gi], r, H, f"attn_bwd_d{r}")
            dqkv.append((dq, dk, dv))
            d_bias[gi] = d_bias[gi] + db
        dfz, div, dqz, dgz, dlb_l, dhg_l = _hg_bwd(sv["proj"], dcat, sv["o_raw"], sv["states"], lb[l], hg_norm[l],
                                                   delta, H, "hgrn_bwd")
        d_lb[l], d_hg[l] = dlb_l[0], dhg_l[0]
        dproj = _assemble_dproj(dqkv, [dfz, div, dqz, dgz], "assemble_dproj")
        gw["in"] = _mm_tn(sv["h0"], dproj, w["in"].shape[0], BF16, "mm_dw_in")
        dh_next = _mm_nt(dproj, w["in"], F32, "mm_dx_in")
        dres = t1
        bwd_token = comm.grads(l, gw, gw["in"])
    grad_x, _, d_gains[0][0], _ = _norm_bwd(dres, dh_next, x, None, gains[0, 0], None, "norm_bwd_first")

    d_rel_bias = _bias_grad(d_bias, onehots, H).T
    d_gains = jnp.stack([jnp.stack(row) for row in d_gains])
    return loss, grad_x, d_rel_bias, jnp.stack(d_lb), jnp.stack(d_hg), d_gains


def _place():
    return lax.axis_index("x"), lax.axis_index("y"), lax.axis_index("c")


def _all_gather(shards, name):
    n = len(shards)
    HBM = pl.BlockSpec(memory_space=pltpu.HBM)

    def body(*refs):
        ins, outs = refs[:n], refs[n:2 * n]
        send_sems, recv_sems, local_sems = refs[2 * n:]
        x, y, c = _place()
        me, sibling = (x, y, c), (x, y, 1 - c)
        chips = [(1 - x, y), (x, 1 - y), (1 - x, 1 - y)]

        def copy(a, k, block, to, own=False):
            dst = outs[a].at[4 * block[0] + 2 * block[1] + block[2]]
            return pltpu.make_async_remote_copy(
                src_ref=ins[a] if own else dst, dst_ref=dst, send_sem=send_sems.at[7 * a + k],
                recv_sem=recv_sems.at[7 * a + k], device_id=to, device_id_type=MESH)

        mine = [pltpu.make_async_copy(ins[a], outs[a].at[4 * x + 2 * y + c], local_sems.at[a]) for a in range(n)]
        for cp in mine:
            cp.start()
        first = []
        for a in range(n):
            first.append(copy(a, 0, me, sibling, own=True))
            first += [copy(a, 1 + j, me, (*chip, c), own=True) for j, chip in enumerate(chips)]
        for cp in first:
            cp.start()
        passed = []
        for j, chip in enumerate(chips):
            for a in range(n):
                copy(a, 1 + j, (*chip, c), me).wait_recv()
                fwd = copy(a, 4 + j, (*chip, c), sibling)
                fwd.start()
                passed.append(fwd)
        for a in range(n):
            copy(a, 0, sibling, me).wait_recv()
            for j, chip in enumerate(chips):
                copy(a, 4 + j, (*chip, 1 - c), me).wait_recv()
        for cp in first + passed:
            cp.wait_send()
        for cp in mine:
            cp.wait()

    return pl.pallas_call(
        body, name=name, in_specs=[HBM] * n, out_specs=[HBM] * n,
        out_shape=[jax.ShapeDtypeStruct((N_DEV,) + s.shape, s.dtype) for s in shards],
        scratch_shapes=[pltpu.SemaphoreType.DMA((7 * n,)), pltpu.SemaphoreType.DMA((7 * n,)),
                        pltpu.SemaphoreType.DMA((n,))],
    )(*shards)


def _exchange_sibling(grads, name):
    n = len(grads)
    HBM = pl.BlockSpec(memory_space=pltpu.HBM)

    def body(*refs):
        ins, outs = refs[:n], refs[n:2 * n]
        send_sems, recv_sems = refs[2 * n:]
        x, y, c = _place()
        sibling = (x, y, 1 - c)
        for a in range(n):
            for q in range(4):
                pltpu.make_async_remote_copy(
                    src_ref=ins[a].at[2 * q + 1 - c], dst_ref=outs[a].at[q], send_sem=send_sems.at[a],
                    recv_sem=recv_sems.at[a], device_id=sibling, device_id_type=MESH).start()
        for a in range(n):
            pltpu.make_async_remote_copy(
                src_ref=ins[a].at[pl.ds(0, 4)], dst_ref=outs[a], send_sem=send_sems.at[a], recv_sem=recv_sems.at[a],
                device_id=sibling, device_id_type=MESH).wait()

    return pl.pallas_call(
        body, name=name, in_specs=[HBM] * n, out_specs=[HBM] * n,
        out_shape=[jax.ShapeDtypeStruct((4,) + g.shape[1:], g.dtype) for g in grads],
        scratch_shapes=[pltpu.SemaphoreType.DMA((n,)), pltpu.SemaphoreType.DMA((n,))],
    )(*grads)


def _exchange_chips(parts, name):
    n = len(parts)
    HBM = pl.BlockSpec(memory_space=pltpu.HBM)

    def body(*refs):
        ins, outs = refs[:n], refs[n:2 * n]
        send_sems, recv_sems = refs[2 * n:]
        x, y, c = _place()
        chips = [(1 - x, y), (x, 1 - y), (1 - x, 1 - y)]
        copies = []
        for a in range(n):
            for j, chip in enumerate(chips):
                cp = pltpu.make_async_remote_copy(
                    src_ref=ins[a].at[2 * chip[0] + chip[1]], dst_ref=outs[a].at[j], send_sem=send_sems.at[3 * a + j],
                    recv_sem=recv_sems.at[3 * a + j], device_id=(*chip, c), device_id_type=MESH)
                cp.start()
                copies.append(cp)
        for cp in copies:
            cp.wait()

    return pl.pallas_call(
        body, name=name, in_specs=[HBM] * n, out_specs=[HBM] * n,
        out_shape=[jax.ShapeDtypeStruct((3,) + p.shape[1:], p.dtype) for p in parts],
        scratch_shapes=[pltpu.SemaphoreType.DMA((3 * n,)), pltpu.SemaphoreType.DMA((3 * n,))],
    )(*parts)


def _others(x, y, c):
    out = []
    for k in range(1, N_DEV):
        px = 1 - x if (k >> 2) & 1 else x
        py = 1 - y if (k >> 1) & 1 else y
        pc = 1 - c if k & 1 else c
        out.append(((px, py, pc), 4 * px + 2 * py + pc))
    return out


_HBM_SPEC = pl.BlockSpec(memory_space=pltpu.HBM)
_SEM_SPEC = pl.BlockSpec(memory_space=pltpu.SEMAPHORE)
_EFFECT = pltpu.SideEffectType.DATAFLOW_SIDE_EFFECTING


def _copies_start(srcs, lands, after, src_block, name):
    n = len(srcs)

    def body(*refs):
        src_refs, land_refs = refs[:n], refs[n:2 * n]
        send_sems, recv_sems = refs[2 * n + 1], refs[2 * n + 2]
        token = refs[-1]
        x, y, c = _place()
        mine = 4 * x + 2 * y + c
        for a in range(n):
            for k, (peer, block) in enumerate(_others(x, y, c)):
                pltpu.make_async_remote_copy(
                    src_ref=src_refs[a].at[block] if src_block else src_refs[a],
                    dst_ref=land_refs[a].at[k] if src_block else land_refs[a].at[mine],
                    send_sem=send_sems.at[7 * a + k], recv_sem=recv_sems.at[7 * a + k],
                    device_id=peer, device_id_type=MESH).start()
        token[...] = jnp.zeros_like(token)

    outs = pl.pallas_call(
        body, name=name,
        out_shape=(pltpu.SemaphoreType.DMA((7 * n,)), pltpu.SemaphoreType.DMA((7 * n,)),
                   *[pltpu.HBM(s.shape, s.dtype) for s in srcs], *[pltpu.HBM(t.shape, t.dtype) for t in lands],
                   jax.ShapeDtypeStruct((8, 128), F32)),
        in_specs=[_HBM_SPEC] * (2 * n) + [pl.BlockSpec(memory_space=pl.ANY)],
        out_specs=(_SEM_SPEC, _SEM_SPEC, *[_HBM_SPEC] * (2 * n), pl.BlockSpec(memory_space=pltpu.VMEM)),
        input_output_aliases={i: 2 + i for i in range(2 * n)},
        compiler_params=pltpu.CompilerParams(has_side_effects=_EFFECT),
    )(*[pltpu.with_memory_space_constraint(s, pltpu.HBM) for s in srcs],
      *[pltpu.with_memory_space_constraint(t, pltpu.HBM) for t in lands], after)
    return outs[0], outs[1], list(outs[2:2 + n]), list(outs[2 + n:2 + 2 * n]), outs[-1]


def _copies_wait(send_sems, recv_sems, srcs, lands, after, src_block, name):
    n = len(srcs)

    def body(*refs):
        src_refs, land_refs = refs[:n], refs[n:2 * n]
        send_sems, recv_sems = refs[2 * n], refs[2 * n + 1]
        x, y, c = _place()
        for a in range(n):
            for k, (peer, block) in enumerate(_others(x, y, c)):
                cp = pltpu.make_async_remote_copy(
                    src_ref=src_refs[a].at[block] if src_block else src_refs[a],
                    dst_ref=land_refs[a].at[k] if src_block else land_refs[a].at[block],
                    send_sem=send_sems.at[7 * a + k], recv_sem=recv_sems.at[7 * a + k],
                    device_id=peer, device_id_type=MESH)
                cp.wait_send()
                cp.wait_recv()

    outs = pl.pallas_call(
        body, name=name,
        out_shape=(*[pltpu.HBM(s.shape, s.dtype) for s in srcs], *[pltpu.HBM(t.shape, t.dtype) for t in lands]),
        in_specs=[_HBM_SPEC] * (2 * n) + [_SEM_SPEC, _SEM_SPEC, pl.BlockSpec(memory_space=pl.ANY)],
        out_specs=tuple([_HBM_SPEC] * (2 * n)),
        input_output_aliases={i: i for i in range(2 * n)},
        compiler_params=pltpu.CompilerParams(has_side_effects=_EFFECT),
    )(*srcs, *lands, send_sems, recv_sems, after)
    return list(outs[:n]), list(outs[n:])


def _pair_sum(grad, recv, c_idx, name):
    _, R, C = grad.shape
    tr = _tile(R, 512, 16)

    def body(c_ref, g_ref, p_ref, o_ref):
        del c_ref
        o_ref[...] = (g_ref[...].astype(F32) + p_ref[...].astype(F32)).astype(BF16)

    spec = pltpu.PrefetchScalarGridSpec(
        num_scalar_prefetch=1, grid=(4, R // tr),
        in_specs=[pl.BlockSpec((None, tr, C), lambda q, i, c: (2 * q + c[0], i, 0)),
                  pl.BlockSpec((None, tr, C), lambda q, i, c: (q, i, 0))],
        out_specs=pl.BlockSpec((None, tr, C), lambda q, i, c: (q, i, 0)))
    return pl.pallas_call(body, name=name, grid_spec=spec, out_shape=jax.ShapeDtypeStruct((4, R, C), BF16),
                          compiler_params=_cp(("parallel", "parallel")))(c_idx, grad, recv)


def _adam(w, g, m, v):
    m2 = ADAM_B1 * m + (1.0 - ADAM_B1) * g
    v2 = ADAM_B2 * v + (1.0 - ADAM_B2) * (g * g)
    m_hat = m2 / (1.0 - ADAM_B1 ** ADAM_STEP)
    v_hat = v2 / (1.0 - ADAM_B2 ** ADAM_STEP)
    return -ADAM_LR * (m_hat / (jnp.sqrt(v_hat) + ADAM_EPS) + ADAM_WD * w), m2, v2


def _reduce_update(part, recv, own_idx, w, m, v, layer, prev, name):
    L, R, C = w.shape
    nr = recv.shape[0]
    tr = _tile(R, 128, 16)
    has_prev = prev is not None

    def body(idx_ref, q_ref, *rest):
        del idx_ref
        r_refs, (w_ref, m_ref, v_ref) = rest[:nr], rest[nr:nr + 3]
        g_ref, d_ref, m2_ref, v2_ref = rest[-4:]
        g = q_ref[...].astype(F32)
        for r_ref in r_refs:
            g = g + r_ref[...].astype(F32)
        d, m2, v2 = _adam(w_ref[...], g, m_ref[...], v_ref[...])
        g_ref[...] = g
        d_ref[...] = d
        m2_ref[...] = m2
        v2_ref[...] = v2

    def rblk(j):
        return pl.BlockSpec((None, tr, C), lambda i, k: (j, i, 0))

    lay = pl.BlockSpec((None, tr, C), lambda i, k: (layer, i, 0))
    in_specs = [pl.BlockSpec((None, tr, C), lambda i, k: (k[0], i, 0))] + [rblk(j) for j in range(nr)] + [lay, lay, lay]
    ins = [part] + [recv] * nr + [w, m, v]
    aliases = {}
    if has_prev:
        in_specs += [pl.BlockSpec(memory_space=pl.ANY)] * 4
        ins += list(prev)
        aliases = {1 + len(ins) - 4 + t: t for t in range(4)}
    spec = pltpu.PrefetchScalarGridSpec(num_scalar_prefetch=1, grid=(R // tr,), in_specs=in_specs,
                                        out_specs=[lay, lay, lay, lay])
    return pl.pallas_call(body, name=name, grid_spec=spec, out_shape=[jax.ShapeDtypeStruct((L, R, C), F32)] * 4,
                          input_output_aliases=aliases, compiler_params=_cp(("parallel",)))(own_idx, *ins)


def _sum_devices(gathered):
    n, R, C = gathered.shape

    def body(g_ref, o_ref):
        acc = g_ref[0]
        for d in range(1, n):
            acc = acc + g_ref[d]
        o_ref[...] = acc

    return pl.pallas_call(body, name="sum_devices", out_shape=jax.ShapeDtypeStruct((R, C), F32))(gathered)


def _adam_small(w, g, m, v, name):
    shape = w.shape
    two = (-1, shape[-1])

    def body(w_ref, g_ref, m_ref, v_ref, d_ref, m2_ref, v2_ref):
        d, m2, v2 = _adam(w_ref[...], g_ref[...], m_ref[...], v_ref[...])
        d_ref[...] = d
        m2_ref[...] = m2
        v2_ref[...] = v2

    outs = pl.pallas_call(body, name=name, out_shape=[jax.ShapeDtypeStruct(w.reshape(two).shape, F32)] * 3)(
        w.reshape(two), g.reshape(two), m.reshape(two), v.reshape(two))
    return [o.reshape(shape) for o in outs]


_SHARDED = ("in", "out", "cq", "ckv", "co", "gu", "down")
_COLUMN_SHARDED = ("in", "ckv", "gu")


def kernel(x, mem, rel_bias, lb_logits, norm_gains, w_in, hg_norm, w_out, w_cq, w_ckv, w_co, w_gate_up, w_down, loss_target, m_rel_bias, m_lb_logits, m_norm_gains, m_w_in, m_hg_norm, m_w_out, m_w_cq, m_w_ckv, m_w_co, m_w_gate_up, m_w_down, v_rel_bias, v_lb_logits, v_norm_gains, v_w_in, v_hg_norm, v_w_out, v_w_cq, v_w_ckv, v_w_co, v_w_gate_up, v_w_down):
    L = w_in.shape[0]
    D = x.shape[-1]
    ws = dict(zip(_SHARDED, (w_in, w_out, w_cq, w_ckv, w_co, w_gate_up, w_down)))
    ms = dict(zip(_SHARDED, (m_w_in, m_w_out, m_w_cq, m_w_ckv, m_w_co, m_w_gate_up, m_w_down)))
    vs = dict(zip(_SHARDED, (v_w_in, v_w_out, v_w_cq, v_w_ckv, v_w_co, v_w_gate_up, v_w_down)))
    px, py, pc = _place()
    dev = 4 * px + 2 * py + pc
    c_idx = jnp.reshape(pc, (1,)).astype(jnp.int32)
    chip_idx = jnp.reshape(2 * px + py, (1,)).astype(jnp.int32)
    dev_idx = jnp.reshape(dev, (1,)).astype(jnp.int32)

    rest_keys = _SHARDED[1:]

    def shards_of(l, keys):
        return [ws[k][l].astype(BF16) for k in keys]

    def as_weights(keys, gathered):
        return {k: g if k in _COLUMN_SHARDED else g.reshape(1, N_DEV * g.shape[1], g.shape[2])
                for k, g in zip(keys, gathered)}

    def start_gather(shards, after, name):
        lands = [lax.empty((N_DEV,) + s.shape, s.dtype) for s in shards]
        return _copies_start(shards, lands, after, False, name)

    def wait_gather(pending, keys, after, name):
        send_sems, recv_sems, shards, lands = pending
        shards, lands = _copies_wait(send_sems, recv_sems, shards, lands, after, False, name)
        full = [lax.dynamic_update_slice_in_dim(t, s[None], dev, axis=0) for s, t in zip(shards, lands)]
        return as_weights(keys, full)

    def full_grads(gw, keys):
        return [gw[k].reshape((N_DEV,) + ws[k].shape[1:]) for k in keys]

    def start_scatter(full, after, name):
        lands = [lax.empty((N_DEV - 1,) + g.shape[1:], g.dtype) for g in full]
        return _copies_start(full, lands, after, True, name)

    class Comm:
        def __init__(self):
            first, gains = _all_gather(shards_of(0, _SHARDED[:1]) + [norm_gains], "gather_first")
            self.gains = jnp.transpose(gains, (1, 2, 0, 3)).reshape(L, 7, D)
            self.ready = {0: as_weights(_SHARDED[:1], [first])}
            *self.rest0, self.order = start_gather(shards_of(0, rest_keys), first, "gather_start_rest")
            self.gather = None
            self.scatter = None
            self.early = None
            self.results = {k: None for k in _SHARDED}

        def weights(self, l, x_in):
            del x_in
            token = None
            if l + 1 < L:
                after = self.order if l == 0 else self.ready[l]["in"]
                *self.gather, token = start_gather(shards_of(l + 1, _SHARDED), after, "gather_start")
            return self.ready[l]["in"], token

        def rest(self, l, after):
            if l == 0:
                self.ready[0].update(wait_gather(self.rest0, rest_keys, after, "gather_wait_rest"))
            mine = self.ready.pop(l)
            return {k: mine[k] for k in rest_keys}

        def layer_done(self, l, x_out):
            if l + 1 < L:
                self.ready[l + 1] = wait_gather(self.gather, _SHARDED, x_out, "gather_wait")

        def update(self, l, keys, parts, recvs, own_idx):
            for k, part, recv in zip(keys, parts, recvs):
                self.results[k] = _reduce_update(part, recv, own_idx, ws[k], ms[k], vs[k], l, self.results[k],
                                                 "reduce_update")

        def finish_scatter(self, after):
            lp, send_sems, recv_sems, srcs, lands = self.scatter
            srcs, lands = _copies_wait(send_sems, recv_sems, srcs, lands, after, True, "scatter_wait")
            self.update(lp, _SHARDED, srcs, lands, dev_idx)
            self.scatter = None
            return lands[0]

        def early_grads(self, l, gw):
            if l > 0 or L == 1:
                return None
            order = self.finish_scatter(gw["out"])
            *self.early, token = start_scatter(full_grads(gw, rest_keys), order, "scatter_start_rest")
            return token

        def grads(self, l, gw, after):
            if l > 0:
                order = self.finish_scatter(after) if self.scatter is not None else dev_idx
                send_sems, recv_sems, srcs, lands, token = start_scatter(full_grads(gw, _SHARDED), order,
                                                                         "scatter_start")
                self.scatter = (l, send_sems, recv_sems, srcs, lands)
                return token
            last_keys = _SHARDED
            if self.early is not None:
                send_sems, recv_sems, srcs, lands = self.early
                srcs, lands = _copies_wait(send_sems, recv_sems, srcs, lands, after, True, "scatter_wait_rest")
                self.update(0, rest_keys, srcs, lands, dev_idx)
                last_keys = _SHARDED[:1]
            full = full_grads(gw, last_keys)
            from_sibling = _exchange_sibling(full, "scatter_sibling")
            parts = [_pair_sum(g, r, c_idx, "pair_sum") for g, r in zip(full, from_sibling)]
            self.update(0, last_keys, parts, _exchange_chips(parts, "scatter_chips"), chip_idx)
            return None

    comm = Comm()
    loss, grad_x, d_rel_bias, d_lb, d_hg, d_gains = _local_step(
        x[0], mem[0], loss_target[0], rel_bias, lb_logits, comm.gains, hg_norm, comm)
    results = comm.results

    pieces = [jnp.reshape(loss, (1,)), d_rel_bias.reshape(-1), d_lb.reshape(-1), d_hg.reshape(-1), d_gains.reshape(-1)]
    sizes = [p.shape[0] for p in pieces]
    total = sum(sizes)
    rows = -(-total // 1024) * 8
    pack = jnp.pad(jnp.concatenate(pieces), (0, rows * 128 - total)).reshape(rows, 128)
    summed = _sum_devices(_all_gather([pack], "gather_small")[0]).reshape(-1)
    offs = [sum(sizes[:i]) for i in range(len(sizes))]
    loss = summed[0]
    g_rel_bias = summed[offs[1]:offs[1] + sizes[1]].reshape(rel_bias.shape)
    g_lb_all = summed[offs[2]:offs[2] + sizes[2]].reshape(lb_logits.shape)
    g_hg = summed[offs[3]:offs[3] + sizes[3]].reshape(hg_norm.shape)
    g_gains_full = summed[offs[4]:offs[4] + sizes[4]].reshape(L, 7, D)
    g_gains = lax.dynamic_slice_in_dim(g_gains_full, dev * (D // N_DEV), D // N_DEV, axis=2)
    g_lb = jax.vjp(_lb_all, lb_logits)[1](g_lb_all)[0]

    small = [(rel_bias, g_rel_bias, m_rel_bias, v_rel_bias), (lb_logits, g_lb, m_lb_logits, v_lb_logits),
             (norm_gains, g_gains, m_norm_gains, v_norm_gains), (hg_norm, g_hg, m_hg_norm, v_hg_norm)]
    upd = [_adam_small(w, g, m, v, "adam_small") for w, g, m, v in small]

    def ordered(small_vals, big_vals):
        return [small_vals[0], small_vals[1], small_vals[2], big_vals["in"], small_vals[3], big_vals["out"],
                big_vals["cq"], big_vals["ckv"], big_vals["co"], big_vals["gu"], big_vals["down"]]

    grads = ordered([s[1] for s in small], {k: results[k][0] for k in _SHARDED})
    deltas = ordered([u[0] for u in upd], {k: results[k][1] for k in _SHARDED})
    new_m = ordered([u[1] for u in upd], {k: results[k][2] for k in _SHARDED})
    new_v = ordered([u[2] for u in upd], {k: results[k][3] for k in _SHARDED})
    return (loss, grad_x[None], *grads, *deltas, *new_m, *new_v)
```
